```python
import math
import jax, jax.numpy as jnp
from jax import lax
import numpy as np

D_MODEL = 1024
BATCH = 2
SEQ = 16384
DEPTH = 4

A_HEADS = 16
A_HEAD_DIM = 64
A_WIDTH = A_HEADS * A_HEAD_DIM
A_KV_RANK = 128
A_IDX_HEADS = 8
A_IDX_DIM = 64
A_TOPK_MAX = 256
Q_BLOCK = 128
REL_BUCKETS = 32
REL_MAX_DIST = 128
B_HEADS = 8
B_HEAD_DIM = 128
B_WIDTH = B_HEADS * B_HEAD_DIM
CONV_K = 4
B_CHUNK = 64
C_HEADS = 4
C_QK_DIM = 128
C_V_DIM = 256
C_WIDTH = C_HEADS * C_V_DIM
C_CHUNK = 128
ROPE_BASE = 10000.0
D_GROUP = 16
D_STATE = 64
D_WIDTH = 1024
D_GROUPS = D_WIDTH // D_GROUP
D_SCAN_CHUNK = 1024

EPS = 1e-6
N_EVEN = (DEPTH + 1) // 2
N_ODD = DEPTH // 2

EVEN_SPLITS = (A_WIDTH, A_KV_RANK, A_IDX_HEADS * A_IDX_DIM, A_IDX_DIM, A_IDX_HEADS, A_WIDTH,
               3 * B_WIDTH, B_HEADS, B_HEADS, B_WIDTH)
EVEN_IN = sum(EVEN_SPLITS)
ODD_SPLITS = (C_HEADS * C_QK_DIM, C_HEADS * C_QK_DIM, C_WIDTH, C_WIDTH, D_WIDTH, D_WIDTH)
ODD_IN = sum(ODD_SPLITS)
EVEN_MIX = A_WIDTH + B_WIDTH
ODD_MIX = C_WIDTH + D_WIDTH

kernel_name = "hybrid_dsa_deltanet_retention_s5"

F32 = jnp.float32


def _split_cols(h, sizes):
    outs, start = [], 0
    for s in sizes:
        outs.append(h[..., start:start + s])
        start += s
    return outs


def rms_norm(x, gain):
    xf = x.astype(F32)
    y = xf * lax.rsqrt(jnp.mean(xf * xf, axis=-1, keepdims=True) + EPS)
    return (y * gain.astype(F32)).astype(x.dtype)


def l2_norm(x):
    xf = x.astype(F32)
    return xf * lax.rsqrt(jnp.sum(xf * xf, axis=-1, keepdims=True) + EPS)


def t5_bucket(dist):
    exact = REL_BUCKETS // 2
    n = jnp.maximum(dist, 0)
    large = exact + (jnp.log(jnp.maximum(n, 1).astype(F32) / exact)
                     / math.log(REL_MAX_DIST / exact) * (REL_BUCKETS - exact)).astype(jnp.int32)
    large = jnp.minimum(large, REL_BUCKETS - 1)
    return jnp.where(n < exact, n, large)


def dsa_attention(q, k, v, q_idx, k_idx, w_idx, rel_bias):
    bsz, L = q.shape[0], q.shape[1]
    topk = min(A_TOPK_MAX, L // 4)
    nb = L // Q_BLOCK
    kpos = jnp.arange(L, dtype=jnp.int32)
    k_idx32 = k_idx.astype(F32)

    def to_blocks(a):
        return jnp.swapaxes(a.reshape((bsz, nb, Q_BLOCK) + a.shape[2:]), 0, 1)

    def block(args):
        qb, qib, wb, start = args
        tpos = start + jnp.arange(Q_BLOCK, dtype=jnp.int32)
        s = jnp.einsum('bqhd,bsd->bqhs', qib.astype(F32), k_idx32) * (A_IDX_DIM ** -0.5)
        score = jnp.einsum('bqhs,bqh->bqs', jax.nn.relu(s), wb.astype(F32))
        causal = kpos[None, :] <= tpos[:, None]
        score = jnp.where(causal[None], score, -jnp.inf)
        _, idx = lax.top_k(score, topk)
        k_sel = jax.vmap(lambda a, i: a[i])(k, idx)
        v_sel = jax.vmap(lambda a, i: a[i])(v, idx)
        dist = tpos[None, :, None] - idx
        bias = jnp.moveaxis(rel_bias.astype(F32)[t5_bucket(dist)], -1, 2)
        logits = jnp.einsum('bqhd,bqkd->bqhk', qb, k_sel).astype(F32) * (A_HEAD_DIM ** -0.5) + bias
        logits = jnp.where((dist >= 0)[:, :, None, :], logits, -jnp.inf)
        p = jax.nn.softmax(logits, axis=-1)
        return jnp.einsum('bqhk,bqkd->bqhd', p.astype(v.dtype), v_sel)

    starts = jnp.arange(nb, dtype=jnp.int32) * Q_BLOCK
    out = lax.map(block, (to_blocks(q), to_blocks(q_idx), to_blocks(w_idx), starts))
    return jnp.swapaxes(out, 0, 1).reshape(bsz, L, A_HEADS, A_HEAD_DIM)


def causal_conv_silu(x, w):
    y = lax.conv_general_dilated(x, w[:, None, :].astype(x.dtype), window_strides=(1,),
                                 padding=[(CONV_K - 1, 0)],
                                 dimension_numbers=('NWC', 'WIO', 'NWC'),
                                 feature_group_count=x.shape[-1])
    return jax.nn.silu(y)


def gated_delta_rule(q, k, v, g, beta):
    bsz, L, H, dk = q.shape
    dv = v.shape[-1]
    n = L // B_CHUNK

    def chunks(a):
        a = a.reshape((bsz, n, B_CHUNK, H) + a.shape[3:])
        return jnp.moveaxis(a, 3, 1)

    qc = chunks(l2_norm(q) * (dk ** -0.5))
    kc = chunks(l2_norm(k))
    vc = chunks(v.astype(F32))
    bc = chunks(beta.astype(F32))
    gc = jnp.cumsum(chunks(g.astype(F32)), axis=-1)
    pos = jnp.arange(B_CHUNK)
    lower = pos[:, None] >= pos[None, :]
    strict = pos[:, None] > pos[None, :]
    decay = jnp.exp(jnp.where(lower, gc[..., :, None] - gc[..., None, :], -jnp.inf))
    kb = kc * bc[..., None]
    a_mat = jnp.where(strict, jnp.einsum('bhnid,bhnjd->bhnij', kb, kc) * decay, 0.0)
    sys = a_mat + jnp.eye(B_CHUNK, dtype=F32)
    rhs = jnp.concatenate([vc * bc[..., None], kb * jnp.exp(gc)[..., None]], axis=-1)
    sol = lax.linalg.triangular_solve(sys, rhs, left_side=True, lower=True, unit_diagonal=True)
    u, w = sol[..., :dv], sol[..., dv:]
    attn = jnp.where(lower, jnp.einsum('bhnid,bhnjd->bhnij', qc, kc) * decay, 0.0)
    g_last = gc[..., -1]
    k_dec = kc * jnp.exp(g_last[..., None] - gc)[..., None]
    q_dec = qc * jnp.exp(gc)[..., None]

    def step(S, xs):
        u_i, w_i, q_i, k_i, attn_i, gl_i = xs
        v_new = u_i - jnp.einsum('bhck,bhkv->bhcv', w_i, S)
        o = jnp.einsum('bhck,bhkv->bhcv', q_i, S) + jnp.einsum('bhcm,bhmv->bhcv', attn_i, v_new)
        S = S * jnp.exp(gl_i)[..., None, None] + jnp.einsum('bhck,bhcv->bhkv', k_i, v_new)
        return S, o

    xs = tuple(jnp.moveaxis(a, 2, 0) for a in (u, w, q_dec, k_dec, attn, g_last))
    S0 = jnp.zeros((bsz, H, dk, dv), F32)
    _, o = lax.scan(step, S0, xs)
    o = jnp.moveaxis(jnp.moveaxis(o, 0, 2), 1, 3)
    return o.reshape(bsz, L, H, dv)


def rotary(x, pos):
    half = x.shape[-1] // 2
    inv = 1.0 / (ROPE_BASE ** jnp.linspace(0.0, 1.0, half, dtype=F32))
    ang = pos.astype(F32)[:, None] * inv[None, :]
    cos, sin = jnp.cos(ang)[:, None, :], jnp.sin(ang)[:, None, :]
    xf = x.astype(F32)
    x1, x2 = xf[..., :half], xf[..., half:]
    return jnp.concatenate([x1 * cos - x2 * sin, x1 * sin + x2 * cos], axis=-1)


def retention(q, k, v):
    bsz, L, H, dk = q.shape
    dv = v.shape[-1]
    n = L // C_CHUNK
    lg = jnp.log1p(-jnp.exp2(-5.0 - jnp.arange(H, dtype=F32)))

    def chunks(a):
        return jnp.moveaxis(a.reshape(bsz, n, C_CHUNK, H, a.shape[-1]), 3, 1)

    qc, kc, vc = chunks(q), chunks(k), chunks(v.astype(F32))
    i = jnp.arange(C_CHUNK, dtype=F32)
    diff = i[:, None] - i[None, :]
    d_mask = jnp.where(diff >= 0, jnp.exp(jnp.maximum(diff, 0.0)[None] * lg[:, None, None]), 0.0)
    inner = jnp.einsum('bhnij,bhnjv->bhniv',
                       jnp.einsum('bhnid,bhnjd->bhnij', qc, kc) * d_mask[:, None], vc)
    zeta = jnp.exp((C_CHUNK - 1 - i)[None, :] * lg[:, None])
    xi = jnp.exp((i + 1.0)[None, :] * lg[:, None])
    g_chunk = jnp.exp(C_CHUNK * lg)
    kv = jnp.einsum('bhncd,bhncv->bhndv', kc * zeta[:, None, :, None], vc)

    def step(R, xs):
        q_i, kv_i = xs
        cross = jnp.einsum('bhcd,bhdv->bhcv', q_i, R) * xi[None, :, :, None]
        R = R * g_chunk[None, :, None, None] + kv_i
        return R, cross

    R0 = jnp.zeros((bsz, H, dk, dv), F32)
    _, cross = lax.scan(step, R0, (jnp.moveaxis(qc, 2, 0), jnp.moveaxis(kv, 2, 0)))
    o = inner + jnp.moveaxis(cross, 0, 2)
    return jnp.moveaxis(o, 1, 3).reshape(bsz, L, H, dv)


def head_group_norm(o, gain):
    of = o.astype(F32)
    mu = jnp.mean(of, axis=-1, keepdims=True)
    var = jnp.mean(jnp.square(of - mu), axis=-1, keepdims=True)
    y = (of - mu) * lax.rsqrt(var + EPS)
    return y.reshape(o.shape[0], o.shape[1], -1) * gain.astype(F32)


def s5_ssm(u, lam_re, lam_im, log_dt, b_re, b_im, c_re, c_im, d_skip):
    bsz, L, W = u.shape
    uf = u.astype(F32)
    lr = jnp.minimum(lam_re.astype(F32), -1e-4)
    li = lam_im.astype(F32)
    dt = jnp.exp(log_dt.astype(F32))[:, None]
    mag = jnp.exp(lr * dt)
    a_re, a_im = mag * jnp.cos(li * dt), mag * jnp.sin(li * dt)
    den = lr * lr + li * li
    f_re = ((a_re - 1.0) * lr + a_im * li) / den
    f_im = (a_im * lr - (a_re - 1.0) * li) / den
    br32, bi32 = b_re.astype(F32), b_im.astype(F32)
    bb_re = f_re[..., None] * br32 - f_im[..., None] * bi32
    bb_im = f_re[..., None] * bi32 + f_im[..., None] * br32
    cr32, ci32 = c_re.astype(F32), c_im.astype(F32)

    chunk = math.gcd(L, D_SCAN_CHUNK)
    n = L // chunk
    u_c = jnp.moveaxis(uf.reshape(bsz, n, chunk, D_GROUPS, D_GROUP), 1, 0)

    def combine(e1, e2):
        a1r, a1i, b1r, b1i = e1
        a2r, a2i, b2r, b2i = e2
        return (a2r * a1r - a2i * a1i, a2r * a1i + a2i * a1r,
                a2r * b1r - a2i * b1i + b2r, a2r * b1i + a2i * b1r + b2i)

    def chunk_step(carry, u_i):
        h_re, h_im = carry
        bu_re = jnp.einsum('bcgi,gpi->bcgp', u_i, bb_re)
        bu_im = jnp.einsum('bcgi,gpi->bcgp', u_i, bb_im)
        bu_re = bu_re.at[:, 0].add(a_re * h_re - a_im * h_im)
        bu_im = bu_im.at[:, 0].add(a_re * h_im + a_im * h_re)
        ar = jnp.broadcast_to(a_re, bu_re.shape)
        ai = jnp.broadcast_to(a_im, bu_re.shape)
        _, _, s_re, s_im = lax.associative_scan(combine, (ar, ai, bu_re, bu_im), axis=1)
        y = jnp.einsum('bcgp,gip->bcgi', s_re, cr32) - jnp.einsum('bcgp,gip->bcgi', s_im, ci32)
        return (s_re[:, -1], s_im[:, -1]), y

    h0 = jnp.zeros((bsz, D_GROUPS, D_STATE), F32)
    _, y = lax.scan(chunk_step, (h0, h0), u_c)
    y = jnp.moveaxis(y, 0, 1).reshape(bsz, L, W)
    return y + d_skip.astype(F32) * uf


def even_layer(x, norm_g, w_in, a_q_norm, a_kv_norm, w_kv_up, a_k_norm,
               b_conv, b_a_log, b_dt_bias, b_out_norm, w_out, rel_bias):
    bsz, L, _ = x.shape
    h = rms_norm(x, norm_g) @ w_in
    aq, ckv, qi, ki, wi, az, bqkv, ba, bb, bz = _split_cols(h, EVEN_SPLITS)
    q = rms_norm(aq.reshape(bsz, L, A_HEADS, A_HEAD_DIM), a_q_norm)
    kv = rms_norm(ckv, a_kv_norm) @ w_kv_up
    k = rms_norm(kv[..., :A_HEAD_DIM], a_k_norm)
    v = kv[..., A_HEAD_DIM:]
    att = dsa_attention(q, k, v, qi.reshape(bsz, L, A_IDX_HEADS, A_IDX_DIM), ki,
                        wi * (A_IDX_HEADS ** -0.5), rel_bias)
    a_out = att.reshape(bsz, L, A_WIDTH).astype(F32) * jax.nn.silu(az.astype(F32))
    qkv = causal_conv_silu(bqkv, b_conv)
    bq, bk, bv = _split_cols(qkv, (B_WIDTH, B_WIDTH, B_WIDTH))
    shp = (bsz, L, B_HEADS, B_HEAD_DIM)
    beta = jax.nn.sigmoid(ba.astype(F32) * 0.0 + bb.astype(F32)) if False else jax.nn.sigmoid(bb.astype(F32))
    g = -jnp.exp(b_a_log.astype(F32)) * jax.nn.softplus(ba.astype(F32) + b_dt_bias.astype(F32))
    o = gated_delta_rule(bq.reshape(shp), bk.reshape(shp), bv.reshape(shp), g, beta)
    b_out = rms_norm(o, b_out_norm).reshape(bsz, L, B_WIDTH) * jax.nn.silu(bz.astype(F32))
    mix = jnp.concatenate([a_out, b_out], axis=-1).astype(x.dtype)
    return x + (mix @ w_out).astype(x.dtype)


def odd_layer(x, norm_g, w_in, c_out_norm, lam_re, lam_im, log_dt, b_re, b_im,
              c_re, c_im, d_skip, w_glu, b_glu, w_out):
    bsz, L, _ = x.shape
    h = rms_norm(x, norm_g) @ w_in
    cq, ck, cv, cz, du, dz = _split_cols(h, ODD_SPLITS)
    pos = jnp.arange(L, dtype=jnp.int32)
    q = rotary(cq.reshape(bsz, L, C_HEADS, C_QK_DIM), pos)
    k = rotary(ck.reshape(bsz, L, C_HEADS, C_QK_DIM), pos) * (C_QK_DIM ** -0.5)
    ret = retention(q, k, cv.reshape(bsz, L, C_HEADS, C_V_DIM))
    c_out = head_group_norm(ret, c_out_norm) * jax.nn.silu(cz.astype(F32))
    y = jax.nn.gelu(s5_ssm(du, lam_re, lam_im, log_dt, b_re, b_im, c_re, c_im, d_skip))
    glu = y * jax.nn.sigmoid(y @ w_glu.astype(F32) + b_glu.astype(F32))
    d_out = glu * jax.nn.silu(dz.astype(F32))
    mix = jnp.concatenate([c_out, d_out], axis=-1).astype(x.dtype)
    return x + (mix @ w_out).astype(x.dtype)


def setup_inputs(seed: int = 0) -> dict:
    key = jax.random.key(seed)
    ks = iter(jax.random.split(key, 40))

    def nrm(shape, scale):
        return jax.random.normal(next(ks), shape, F32) * scale

    def gain(shape):
        return 1.0 + nrm(shape, 0.01)

    E, O = N_EVEN, N_ODD
    x = nrm((BATCH, SEQ, D_MODEL), 1.0)
    rel_bias = nrm((REL_BUCKETS, A_HEADS), 0.5)
    ev_norm = gain((E, D_MODEL))
    ev_w_in = nrm((E, D_MODEL, EVEN_IN), D_MODEL ** -0.5)
    ev_a_q_norm = gain((E, A_HEAD_DIM))
    ev_a_kv_norm = gain((E, A_KV_RANK))
    ev_w_kv_up = nrm((E, A_KV_RANK, 2 * A_HEAD_DIM), A_KV_RANK ** -0.5)
    ev_a_k_norm = gain((E, A_HEAD_DIM))
    ev_b_conv = nrm((E, CONV_K, 3 * B_WIDTH), CONV_K ** -0.5)
    ev_b_a_log = jnp.log(jax.random.uniform(next(ks), (E, B_HEADS), F32, 1.0, 16.0))
    dt = jnp.exp(jax.random.uniform(next(ks), (E, B_HEADS), F32, math.log(1e-3), math.log(1e-1)))
    ev_b_dt_bias = dt + jnp.log(-jnp.expm1(-dt))
    ev_b_out_norm = gain((E, B_HEAD_DIM))
    ev_w_out = nrm((E, EVEN_MIX, D_MODEL), EVEN_MIX ** -0.5)
    od_norm = gain((O, D_MODEL))
    od_w_in = nrm((O, D_MODEL, ODD_IN), D_MODEL ** -0.5)
    od_c_out_norm = gain((O, C_WIDTH))
    od_lam_re = -0.5 + nrm((O, D_GROUPS, D_STATE), 0.01)
    od_lam_im = math.pi * jnp.arange(D_STATE, dtype=F32) + nrm((O, D_GROUPS, D_STATE), 0.01)
    od_log_dt = jax.random.uniform(next(ks), (O, D_GROUPS), F32, math.log(1e-3), math.log(1e-1))
    od_b_re = nrm((O, D_GROUPS, D_STATE, D_GROUP), (2 * D_GROUP) ** -0.5)
    od_b_im = nrm((O, D_GROUPS, D_STATE, D_GROUP), (2 * D_GROUP) ** -0.5)
    od_c_re = nrm((O, D_GROUPS, D_GROUP, D_STATE), D_STATE ** -0.5)
    od_c_im = nrm((O, D_GROUPS, D_GROUP, D_STATE), D_STATE ** -0.5)
    od_d_skip = nrm((O, D_WIDTH), 1.0)
    od_w_glu = nrm((O, D_WIDTH, D_WIDTH), D_WIDTH ** -0.5)
    od_b_glu = nrm((O, D_WIDTH), 0.01)
    od_w_out = nrm((O, ODD_MIX, D_MODEL), ODD_MIX ** -0.5)
    return {"x": x, "rel_bias": rel_bias,
            "ev_norm": ev_norm, "ev_w_in": ev_w_in, "ev_a_q_norm": ev_a_q_norm,
            "ev_a_kv_norm": ev_a_kv_norm, "ev_w_kv_up": ev_w_kv_up, "ev_a_k_norm": ev_a_k_norm,
            "ev_b_conv": ev_b_conv, "ev_b_a_log": ev_b_a_log, "ev_b_dt_bias": ev_b_dt_bias,
            "ev_b_out_norm": ev_b_out_norm, "ev_w_out": ev_w_out,
            "od_norm": od_norm, "od_w_in": od_w_in, "od_c_out_norm": od_c_out_norm,
            "od_lam_re": od_lam_re, "od_lam_im": od_lam_im, "od_log_dt": od_log_dt,
            "od_b_re": od_b_re, "od_b_im": od_b_im, "od_c_re": od_c_re, "od_c_im": od_c_im,
            "od_d_skip": od_d_skip, "od_w_glu": od_w_glu, "od_b_glu": od_b_glu,
            "od_w_out": od_w_out}


def reference(x, rel_bias, ev_norm, ev_w_in, ev_a_q_norm, ev_a_kv_norm, ev_w_kv_up, ev_a_k_norm,
              ev_b_conv, ev_b_a_log, ev_b_dt_bias, ev_b_out_norm, ev_w_out,
              od_norm, od_w_in, od_c_out_norm, od_lam_re, od_lam_im, od_log_dt,
              od_b_re, od_b_im, od_c_re, od_c_im, od_d_skip, od_w_glu, od_b_glu, od_w_out):
    for layer in range(DEPTH):
        i = layer // 2
        if layer % 2 == 0:
            x = even_layer(x, ev_norm[i], ev_w_in[i], ev_a_q_norm[i], ev_a_kv_norm[i], ev_w_kv_up[i],
                           ev_a_k_norm[i], ev_b_conv[i], ev_b_a_log[i], ev_b_dt_bias[i],
                           ev_b_out_norm[i], ev_w_out[i], rel_bias)
        else:
            x = odd_layer(x, od_norm[i], od_w_in[i], od_c_out_norm[i], od_lam_re[i], od_lam_im[i],
                          od_log_dt[i], od_b_re[i], od_b_im[i], od_c_re[i], od_c_im[i],
                          od_d_skip[i], od_w_glu[i], od_b_glu[i], od_w_out[i])
    return x
```

```python
import functools
import math

import numpy as np
import jax
import jax.numpy as jnp
from jax import lax
from jax.experimental import pallas as pl
from jax.experimental.pallas import tpu as pltpu

F32 = jnp.float32
BF16 = jnp.bfloat16
I32 = jnp.int32

D_MODEL = 1024
EPS = 1e-6
LOG2E = 1.4426950408889634
NEG_INF = float("-inf")
INT_MIN = -(2 ** 31)

A_HEADS, A_HEAD_DIM, A_KV_RANK = 16, 64, 128
A_IDX_HEADS, A_IDX_DIM, A_TOPK_MAX = 8, 64, 256
QB = 128
REL_BUCKETS, REL_MAX_DIST = 32, 128
VT_ROWS = 80
B_HEADS, B_HEAD_DIM, CONV_K, B_CHUNK = 8, 128, 4, 64
C_HEADS, C_QK_DIM, C_V_DIM, C_CHUNK = 4, 128, 256, 128
ROPE_BASE = 10000.0
D_GROUP, D_STATE, D_GROUPS = 16, 64, 64
D_SETS, D_SET_CH, D_SET_ST = 4, 256, 1024

EV_AQ, EV_AZ, EV_BQ, EV_BK, EV_BV, EV_BZ, EV_QI, EV_CKV, EV_MISC = (
    0, 1024, 2048, 3072, 4096, 5120, 6144, 6656, 6784)
EV_WIDTH = 6912
MISC_KI, MISC_WI, MISC_BA, MISC_BB = 0, 64, 72, 80
OD_CQ, OD_CK, OD_CV, OD_CZ, OD_DU, OD_DZ = 0, 512, 1024, 2048, 3072, 4096
OD_WIDTH = 5120

VMEM_LIMIT = 48 * 1024 * 1024


def _cparams(sem):
    return pltpu.CompilerParams(dimension_semantics=sem, vmem_limit_bytes=VMEM_LIMIT)


def _dot(a, b):
    return jnp.dot(a, b, preferred_element_type=F32)


def _dot_nt(a, b):
    return lax.dot_general(a, b, (((1,), (1,)), ((), ())), preferred_element_type=F32)


def _dot_hi(a, b):
    return jnp.dot(a, b, preferred_element_type=F32, precision=lax.Precision.HIGHEST)


def _sigmoid(x):
    return 1.0 / (1.0 + jnp.exp(-x))


def _silu(x):
    return x * _sigmoid(x)


def _norm_proj_kernel(x_ref, g_ref, w_ref, o_ref, xn_ref):
    @pl.when(pl.program_id(1) == 0)
    def _():
        x = x_ref[...]
        ms = jnp.mean(x * x, axis=-1, keepdims=True)
        xn_ref[...] = (x * lax.rsqrt(ms + EPS) * g_ref[...]).astype(BF16)

    o_ref[...] = _dot(xn_ref[...], w_ref[...])


def _norm_proj(x2d, gain, w_bf16, tn, tm=1024):
    t, d = x2d.shape
    n = w_bf16.shape[1]
    tm = min(tm, t)
    return pl.pallas_call(
        _norm_proj_kernel,
        out_shape=jax.ShapeDtypeStruct((t, n), F32),
        grid=(t // tm, n // tn),
        in_specs=[pl.BlockSpec((tm, d), lambda i, j: (i, 0)),
                  pl.BlockSpec((1, d), lambda i, j: (0, 0)),
                  pl.BlockSpec((d, tn), lambda i, j: (0, j))],
        out_specs=pl.BlockSpec((tm, tn), lambda i, j: (i, j)),
        scratch_shapes=[pltpu.VMEM((tm, d), BF16)],
        compiler_params=_cparams(("parallel", "arbitrary")),
        name="norm_proj",
    )(x2d, gain.reshape(1, d), w_bf16)


def _out_proj_kernel(x_ref, a_ref, b_ref, wa_ref, wb_ref, o_ref):
    o_ref[...] = x_ref[...] + _dot(a_ref[...], wa_ref[...]) + _dot(b_ref[...], wb_ref[...])


def _out_proj(x2d, mix_a, mix_b, w_out, tm=512):
    t, d = x2d.shape
    half = mix_a.shape[1]
    tm = min(tm, t)
    wa = w_out[:half].astype(BF16)
    wb = w_out[half:].astype(BF16)
    return pl.pallas_call(
        _out_proj_kernel,
        out_shape=jax.ShapeDtypeStruct((t, d), F32),
        grid=(t // tm,),
        in_specs=[pl.BlockSpec((tm, d), lambda i: (i, 0)),
                  pl.BlockSpec((tm, half), lambda i: (i, 0)),
                  pl.BlockSpec((tm, half), lambda i: (i, 0)),
                  pl.BlockSpec((half, d), lambda i: (0, 0)),
                  pl.BlockSpec((half, d), lambda i: (0, 0))],
        out_specs=pl.BlockSpec((tm, d), lambda i: (i, 0)),
        compiler_params=_cparams(("parallel",)),
        name="out_proj",
    )(x2d, mix_a, mix_b, wa, wb)


def _t5_bucket_starts():
    exact = REL_BUCKETS // 2
    n = np.arange(0, 4 * REL_MAX_DIST, dtype=np.int64)
    ratio = np.maximum(n, 1).astype(np.float32) / np.float32(exact)
    large = exact + (np.log(ratio).astype(np.float32) / np.float32(math.log(REL_MAX_DIST / exact))
                     * np.float32(REL_BUCKETS - exact)).astype(np.int32)
    bucket = np.where(n < exact, n, np.minimum(large, REL_BUCKETS - 1))
    starts = [int(np.argmax(bucket >= b)) for b in range(REL_BUCKETS)]
    assert all(bucket[s] == b for b, s in enumerate(starts)) and np.all(np.diff(bucket) >= 0)
    assert starts[-1] <= QB, "distances beyond one key block must share the last bucket"
    return starts


_BUCKET_STARTS = _t5_bucket_starts()


def _bias_table_kernel(rb_ref, o_ref):
    row = lax.broadcasted_iota(I32, (QB, QB), 0)
    lane = lax.broadcasted_iota(I32, (QB, QB), 1)
    for which in range(2):
        dist = lane - row + which * QB
        for h in range(A_HEADS):
            val = jnp.full((QB, QB), rb_ref[0, h], F32)
            for b in range(1, REL_BUCKETS):
                val = jnp.where(dist >= _BUCKET_STARTS[b], rb_ref[b, h], val)
            val = (val - rb_ref[REL_BUCKETS - 1, h]) * LOG2E
            o_ref[which, :, h * QB:(h + 1) * QB] = jnp.where(dist >= 0, val, 0.0)


def _bias_table(rel_bias):
    return pl.pallas_call(
        _bias_table_kernel,
        out_shape=jax.ShapeDtypeStruct((2, QB, A_HEADS * QB), F32),
        in_specs=[pl.BlockSpec(memory_space=pltpu.SMEM)],
        out_specs=pl.BlockSpec(memory_space=pltpu.VMEM),
        name="dsa_bias_table",
    )(rel_bias)


def _dsa_prep_kernel(aq_ref, qi_ref, ckv_ref, misc_ref, gq_ref, gkv_ref, wkv_ref, gk_ref,
                     qh_ref, qih_ref, wt_ref, kcat_ref, vt_ref):
    zeros = jnp.zeros((QB, A_HEAD_DIM), F32)
    aq = aq_ref[...]
    gq = gq_ref[...]
    for h in range(A_HEADS):
        q = aq[:, h * A_HEAD_DIM:(h + 1) * A_HEAD_DIM]
        ms = jnp.mean(q * q, axis=-1, keepdims=True)
        qn = q * lax.rsqrt(ms + EPS) * gq * (A_HEAD_DIM ** -0.5 * LOG2E)
        qh_ref[h * QB:(h + 1) * QB, :] = jnp.concatenate([zeros, qn], axis=1).astype(BF16)
    qi = qi_ref[...]
    for h in range(A_IDX_HEADS):
        qih = qi[:, h * A_IDX_DIM:(h + 1) * A_IDX_DIM]
        qih_ref[h * QB:(h + 1) * QB, :] = jnp.concatenate([qih, zeros], axis=1).astype(BF16)
    misc = misc_ref[...]
    wt_ref[...] = misc.T[MISC_WI:MISC_WI + A_IDX_HEADS, :] * (A_IDX_HEADS ** -0.5 * A_IDX_DIM ** -0.5)
    c = ckv_ref[...]
    cn = c * lax.rsqrt(jnp.mean(c * c, axis=-1, keepdims=True) + EPS) * gkv_ref[...]
    kv = _dot(cn.astype(BF16), wkv_ref[...])
    k = kv[:, :A_HEAD_DIM]
    kn = k * lax.rsqrt(jnp.mean(k * k, axis=-1, keepdims=True) + EPS) * gk_ref[...]
    kcat_ref[...] = jnp.concatenate([misc[:, MISC_KI:MISC_KI + A_IDX_DIM], kn], axis=1).astype(BF16)
    kvt = kv.T
    vt_ref[0:A_HEAD_DIM, :] = kvt[A_HEAD_DIM:, :].astype(BF16)
    ones_row = lax.broadcasted_iota(I32, (VT_ROWS - A_HEAD_DIM, QB), 0) == 0
    vt_ref[A_HEAD_DIM:, :] = jnp.where(ones_row, 1.0, 0.0).astype(BF16)


def _dsa_prep(h, nblk, a_q_norm, a_kv_norm, w_kv_up, a_k_norm):
    cb = lambda width, off: off // width
    return pl.pallas_call(
        _dsa_prep_kernel,
        out_shape=(jax.ShapeDtypeStruct((nblk, A_HEADS * QB, 128), BF16),
                   jax.ShapeDtypeStruct((nblk, A_IDX_HEADS * QB, 128), BF16),
                   jax.ShapeDtypeStruct((nblk, A_IDX_HEADS, QB), F32),
                   jax.ShapeDtypeStruct((nblk, QB, 128), BF16),
                   jax.ShapeDtypeStruct((nblk, VT_ROWS, QB), BF16)),
        grid=(nblk,),
        in_specs=[pl.BlockSpec((QB, 1024), lambda i: (i, cb(1024, EV_AQ))),
                  pl.BlockSpec((QB, 512), lambda i: (i, cb(512, EV_QI))),
                  pl.BlockSpec((QB, 128), lambda i: (i, cb(128, EV_CKV))),
                  pl.BlockSpec((QB, 128), lambda i: (i, cb(128, EV_MISC))),
                  pl.BlockSpec((1, A_HEAD_DIM), lambda i: (0, 0)),
                  pl.BlockSpec((1, A_KV_RANK), lambda i: (0, 0)),
                  pl.BlockSpec((A_KV_RANK, 2 * A_HEAD_DIM), lambda i: (0, 0)),
                  pl.BlockSpec((1, A_HEAD_DIM), lambda i: (0, 0))],
        out_specs=(pl.BlockSpec((None, A_HEADS * QB, 128), lambda i: (i, 0, 0)),
                   pl.BlockSpec((None, A_IDX_HEADS * QB, 128), lambda i: (i, 0, 0)),
                   pl.BlockSpec((None, A_IDX_HEADS, QB), lambda i: (i, 0, 0)),
                   pl.BlockSpec((None, QB, 128), lambda i: (i, 0, 0)),
                   pl.BlockSpec((None, VT_ROWS, QB), lambda i: (i, 0, 0))),
        compiler_params=_cparams(("parallel",)),
        name="dsa_prep",
    )(h, h, h, h, a_q_norm.reshape(1, -1), a_kv_norm.reshape(1, -1), w_kv_up.astype(BF16),
      a_k_norm.reshape(1, -1))


def _dsa_kernel(topk, qih_ref, wt_ref, qh_ref, kcat_ref, vt_ref, bias_ref, az_ref, o_ref,
                strip_ref, x_ref, m_ref, acc_ref):
    qt = pl.program_id(1)
    nkb = qt + 1
    t0 = qt * QB
    row = lax.broadcasted_iota(I32, (QB, QB), 0)
    lane = lax.broadcasted_iota(I32, (QB, QB), 1)
    tpos = t0 + lane

    def rows(kb):
        return pl.ds(pl.multiple_of(kb * QB, QB), QB)

    qi = qih_ref[...]
    w = wt_ref[...]

    def score_body(kb, carry):
        s = _dot_nt(kcat_ref[kb], qi)
        tot = jnp.maximum(s[:, 0:QB], 0.0) * w[0:1, :]
        for h in range(1, A_IDX_HEADS):
            tot = tot + jnp.maximum(s[:, h * QB:(h + 1) * QB], 0.0) * w[h:h + 1, :]
        tot = jnp.where(kb * QB + row <= tpos, tot, NEG_INF)
        bits = pltpu.bitcast(tot, I32)
        strip_ref[rows(kb), :] = bits ^ ((bits >> 31) & 0x7FFFFFFF)
        return carry

    lax.fori_loop(0, nkb, score_body, 0)

    def count(pred):
        def body(kb, acc):
            m = jnp.where(pred(strip_ref[rows(kb), :], kb), 1, 0)
            return acc + m.reshape(QB // 8, 8, QB).sum(axis=0)
        acc = lax.fori_loop(0, nkb, body, jnp.zeros((8, QB), I32))
        return acc.sum(axis=0, keepdims=True)

    c0 = count(lambda key, kb: key >= 0)
    nonneg = c0 >= topk
    thr0 = jnp.where(nonneg, 0, INT_MIN)
    cnt0 = jnp.where(nonneg, c0, nkb * QB)

    def bit_body(i, carry):
        thr, cnt = carry
        cand = thr | lax.shift_left(jnp.int32(1), 30 - i)
        c = count(lambda key, kb: key >= cand)
        ok = c >= topk
        return jnp.where(ok, cand, thr), jnp.where(ok, c, cnt)

    thr, cnt = lax.fori_loop(0, 31, bit_body, (thr0, cnt0))

    x_ref[...] = jnp.full((1, QB), 2 ** 30, I32)

    @pl.when(jnp.max(cnt) > topk)
    def _():
        above = count(lambda key, kb: key > thr)
        need = topk - above

        def tie_body(i, xb):
            cand = xb | lax.shift_left(jnp.int32(1), 14 - i)
            c = count(lambda key, kb: (key == thr) & (kb * QB + row < cand))
            return jnp.where(c <= need, cand, xb)

        x_ref[...] = lax.fori_loop(0, 15, tie_body, jnp.zeros((1, QB), I32))

    xb = x_ref[...]

    m_ref[...] = jnp.full(m_ref.shape, -1e30, F32)
    acc_ref[...] = jnp.zeros(acc_ref.shape, F32)
    lt_w = 2 * QB

    def att_block(kb, bias_idx, diag):
        key = strip_ref[rows(kb), :]
        idx = kb * QB + row
        sel = (key > thr) | ((key == thr) & (idx < xb))
        if diag:
            sel = sel & (idx <= tpos)
        pen = jnp.where(sel, 0.0, NEG_INF)
        pen2 = jnp.concatenate([pen, pen], axis=1)
        kblk = kcat_ref[kb]
        vt = vt_ref[kb]
        for lt in range(A_HEADS * QB // lt_w):
            cols = slice(lt * lt_w, (lt + 1) * lt_w)
            lg = _dot_nt(kblk, qh_ref[cols, :]) + pen2
            if bias_idx is not None:
                lg = lg + bias_ref[bias_idx, :, cols]
            m_old = m_ref[:, cols]
            m_new = jnp.maximum(m_old, jnp.max(lg, axis=0, keepdims=True))
            alpha = jnp.exp2(m_old - m_new)
            p = jnp.exp2(lg - m_new).astype(BF16)
            acc_ref[:, cols] = alpha * acc_ref[:, cols] + _dot(vt, p)
            m_ref[:, cols] = m_new

    def far_body(kb, carry):
        att_block(kb, None, False)
        return carry

    lax.fori_loop(0, jnp.maximum(qt - 1, 0), far_body, 0)

    @pl.when(qt >= 1)
    def _():
        att_block(qt - 1, 1, False)

    att_block(qt, 0, True)

    acc = acc_ref[...]
    o_t = acc[0:A_HEAD_DIM, :] / acc[A_HEAD_DIM:A_HEAD_DIM + 1, :]
    pad = jnp.zeros((QB - A_HEAD_DIM, QB), F32)
    pieces = []
    for h in range(A_HEADS):
        blk = jnp.concatenate([o_t[:, h * QB:(h + 1) * QB], pad], axis=0)
        pieces.append(blk.T[:, 0:A_HEAD_DIM])
    att = jnp.concatenate(pieces, axis=1)
    o_ref[...] = (att * _silu(az_ref[...])).astype(BF16)


def _dsa_attend(h, bsz, seq, qh, qih, wt, kcat, vt, bias):
    nq = seq // QB
    topk = min(A_TOPK_MAX, seq // 4)
    blk = lambda b, q: (b * nq + q, 0, 0)
    return pl.pallas_call(
        functools.partial(_dsa_kernel, topk),
        out_shape=jax.ShapeDtypeStruct((bsz * seq, A_HEADS * A_HEAD_DIM), BF16),
        grid=(bsz, nq),
        in_specs=[pl.BlockSpec((None, A_IDX_HEADS * QB, 128), blk),
                  pl.BlockSpec((None, A_IDX_HEADS, QB), blk),
                  pl.BlockSpec((None, A_HEADS * QB, 128), blk),
                  pl.BlockSpec((nq, QB, 128), lambda b, q: (b, 0, 0)),
                  pl.BlockSpec((nq, VT_ROWS, QB), lambda b, q: (b, 0, 0)),
                  pl.BlockSpec((2, QB, A_HEADS * QB), lambda b, q: (0, 0, 0)),
                  pl.BlockSpec((QB, 1024), lambda b, q: (b * nq + q, EV_AZ // 1024))],
        out_specs=pl.BlockSpec((QB, 1024), lambda b, q: (b * nq + q, 0)),
        scratch_shapes=[pltpu.VMEM((seq, QB), I32),
                        pltpu.VMEM((1, QB), I32),
                        pltpu.VMEM((1, A_HEADS * QB), F32),
                        pltpu.VMEM((VT_ROWS, A_HEADS * QB), F32)],
        compiler_params=_cparams(("arbitrary", "arbitrary")),
        name="dsa_attend",
    )(qih, wt, qh, kcat, vt, bias, h)


GDN_TM = 512


def _gdn_prep_kernel(alog_ref, dtb_ref, q_ref, k_ref, v_ref, qh_ref, kh_ref, vh_ref, misc_ref,
                     cq_ref, ck_ref, cv_ref,
                     u_ref, w_ref, qd_ref, kd_ref, attn_ref, egl_ref):
    i = pl.program_id(1)
    hd = pl.program_id(2)
    tm = q_ref.shape[0]
    nchunk = tm // B_CHUNK

    def conv_silu(x_ref, halo_ref, c_ref):
        halo = jnp.where(i > 0, halo_ref[...], 0.0)
        ext = jnp.concatenate([halo, x_ref[...]], axis=0)
        cw = c_ref[...]
        y = ext[8:, :] * cw[CONV_K - 1:CONV_K, :]
        for d in range(1, CONV_K):
            y = y + pltpu.roll(ext, d, 0)[8:, :] * cw[CONV_K - 1 - d:CONV_K - d, :]
        return _silu(y)

    q = conv_silu(q_ref, qh_ref, cq_ref)
    k = conv_silu(k_ref, kh_ref, ck_ref)
    v = conv_silu(v_ref, vh_ref, cv_ref)
    qn = q * lax.rsqrt(jnp.sum(q * q, axis=-1, keepdims=True) + EPS) * (B_HEAD_DIM ** -0.5)
    kn = k * lax.rsqrt(jnp.sum(k * k, axis=-1, keepdims=True) + EPS)

    misc = misc_ref[...]
    sel_r = lax.broadcasted_iota(I32, (128, 128), 0)
    ba = _dot_hi(misc, jnp.where(sel_r == MISC_BA + hd, 1.0, 0.0))
    bb = _dot_hi(misc, jnp.where(sel_r == MISC_BB + hd, 1.0, 0.0))
    beta = _sigmoid(bb)
    xg = ba + dtb_ref[hd]
    softplus = jnp.maximum(xg, 0.0) + jnp.log(1.0 + jnp.exp(-jnp.abs(xg)))
    g = -jnp.exp(jnp.full((1, 128), alog_ref[hd], F32)) * softplus

    sl = 256
    r2 = lax.broadcasted_iota(I32, (sl, sl), 0)
    c2 = lax.broadcasted_iota(I32, (sl, sl), 1)
    same = (r2 // B_CHUNK) == (c2 // B_CHUNK)
    tri = jnp.where(same & (c2 <= r2), 1.0, 0.0)
    blk = jnp.where(same, 1.0, 0.0)
    gc = jnp.concatenate([_dot_hi(tri, g[s * sl:(s + 1) * sl, :]) for s in range(tm // sl)], axis=0)
    gl = jnp.concatenate([_dot_hi(blk, g[s * sl:(s + 1) * sl, :]) for s in range(tm // sl)], axis=0)

    kb = kn * beta
    vb = v * beta
    egc = jnp.exp(gc)
    qd_ref[...] = qn * egc
    kd_ref[...] = kn * jnp.exp(gl - gc)
    kbg = kb * egc

    ri = lax.broadcasted_iota(I32, (B_CHUNK, B_CHUNK), 0)
    ci = lax.broadcasted_iota(I32, (B_CHUNK, B_CHUNK), 1)
    eye = jnp.where(ri == ci, 1.0, 0.0)
    for m in range(tm // 128):
        gct = gc[m * 128:(m + 1) * 128, :].T
        for e in range(2):
            c = 2 * m + e
            rs = slice(c * B_CHUNK, (c + 1) * B_CHUNK)
            diff = gc[rs, 0:B_CHUNK] - gct[0:B_CHUNK, e * B_CHUNK:(e + 1) * B_CHUNK]
            decay = jnp.exp(jnp.where(ri >= ci, diff, NEG_INF))
            kc = kn[rs, :]
            a_mat = jnp.where(ri > ci, _dot_nt(kb[rs, :], kc) * decay, 0.0)
            inv = eye - a_mat
            pw = a_mat
            for _ in range(5):
                pw = _dot_hi(pw, pw)
                inv = inv + _dot_hi(inv, pw)
            u_ref[rs, :] = _dot_hi(inv, vb[rs, :])
            w_ref[rs, :] = _dot_hi(inv, kbg[rs, :])
            attn_ref[rs, :] = jnp.where(ri >= ci, _dot_nt(qn[rs, :], kc) * decay, 0.0)
            egl_ref[c:c + 1, :] = jnp.exp(gl[c * B_CHUNK:c * B_CHUNK + 1, :])
    if nchunk < 8:
        egl_ref[nchunk:, :] = jnp.zeros((8 - nchunk, 128), F32)


def _gdn_prep(h, bsz, seq, b_conv, b_a_log, b_dt_bias):
    tm = min(GDN_TM, seq)
    nt = seq // tm
    nrow = max(tm // B_CHUNK, 8)
    hs = (bsz, B_HEADS, seq, B_HEAD_DIM)
    cur = lambda off: (lambda b, i, hd: (b * nt + i, off // 128 + hd))
    halo = lambda off: (lambda b, i, hd: (jnp.maximum((b * nt + i) * (tm // 8) - 1, 0), off // 128 + hd))
    cw = lambda off: (lambda b, i, hd: (0, off // 128 + hd))
    out = lambda b, i, hd: (b, hd, i, 0)
    smem = pl.BlockSpec(memory_space=pltpu.SMEM)
    return pl.pallas_call(
        _gdn_prep_kernel,
        out_shape=(jax.ShapeDtypeStruct(hs, F32), jax.ShapeDtypeStruct(hs, F32),
                   jax.ShapeDtypeStruct(hs, F32), jax.ShapeDtypeStruct(hs, F32),
                   jax.ShapeDtypeStruct((bsz, B_HEADS, seq, B_CHUNK), F32),
                   jax.ShapeDtypeStruct((bsz, B_HEADS, nt * nrow, 128), F32)),
        grid=(bsz, nt, B_HEADS),
        in_specs=[smem, smem,
                  pl.BlockSpec((tm, 128), cur(EV_BQ)), pl.BlockSpec((tm, 128), cur(EV_BK)),
                  pl.BlockSpec((tm, 128), cur(EV_BV)),
                  pl.BlockSpec((8, 128), halo(EV_BQ)), pl.BlockSpec((8, 128), halo(EV_BK)),
                  pl.BlockSpec((8, 128), halo(EV_BV)),
                  pl.BlockSpec((tm, 128), lambda b, i, hd: (b * nt + i, EV_MISC // 128)),
                  pl.BlockSpec((CONV_K, 128), cw(0)), pl.BlockSpec((CONV_K, 128), cw(1024)),
                  pl.BlockSpec((CONV_K, 128), cw(2048))],
        out_specs=(pl.BlockSpec((None, None, tm, 128), out), pl.BlockSpec((None, None, tm, 128), out),
                   pl.BlockSpec((None, None, tm, 128), out), pl.BlockSpec((None, None, tm, 128), out),
                   pl.BlockSpec((None, None, tm, B_CHUNK), out),
                   pl.BlockSpec((None, None, nrow, 128), out)),
        compiler_params=_cparams(("parallel", "parallel", "parallel")),
        name="gdn_prep",
    )(b_a_log, b_dt_bias, h, h, h, h, h, h, h, b_conv, b_conv, b_conv)


def _gdn_scan_kernel(u_ref, w_ref, qd_ref, kd_ref, attn_ref, egl_ref, z_ref, gn_ref, o_ref, s_ref):
    @pl.when(pl.program_id(2) == 0)
    def _():
        s_ref[...] = jnp.zeros(s_ref.shape, F32)

    tm = u_ref.shape[0]
    state = s_ref[...]
    gain = gn_ref[...]
    for c in range(tm // B_CHUNK):
        rs = slice(c * B_CHUNK, (c + 1) * B_CHUNK)
        sb = state.astype(BF16)
        v_new = u_ref[rs, :] - _dot(w_ref[rs, :].astype(BF16), sb)
        vb = v_new.astype(BF16)
        o = _dot(qd_ref[rs, :].astype(BF16), sb) + _dot(attn_ref[rs, :].astype(BF16), vb)
        state = state * egl_ref[c:c + 1, :] + _dot(kd_ref[rs, :].T.astype(BF16), vb)
        on = o * lax.rsqrt(jnp.mean(o * o, axis=-1, keepdims=True) + EPS) * gain
        o_ref[rs, :] = (on * _silu(z_ref[rs, :])).astype(BF16)
    s_ref[...] = state


def _gdn_scan(h, bsz, seq, u, w, qd, kd, attn, egl, b_out_norm):
    tm = min(GDN_TM, seq)
    nt = seq // tm
    nrow = max(tm // B_CHUNK, 8)
    blk = lambda b, hd, i: (b, hd, i, 0)
    return pl.pallas_call(
        _gdn_scan_kernel,
        out_shape=jax.ShapeDtypeStruct((bsz * seq, B_HEADS * B_HEAD_DIM), BF16),
        grid=(bsz, B_HEADS, nt),
        in_specs=[pl.BlockSpec((None, None, tm, 128), blk), pl.BlockSpec((None, None, tm, 128), blk),
                  pl.BlockSpec((None, None, tm, 128), blk), pl.BlockSpec((None, None, tm, 128), blk),
                  pl.BlockSpec((None, None, tm, B_CHUNK), blk),
                  pl.BlockSpec((None, None, nrow, 128), blk),
                  pl.BlockSpec((tm, 128), lambda b, hd, i: (b * nt + i, EV_BZ // 128 + hd)),
                  pl.BlockSpec((1, 128), lambda b, hd, i: (0, 0))],
        out_specs=pl.BlockSpec((tm, 128), lambda b, hd, i: (b * nt + i, hd)),
        scratch_shapes=[pltpu.VMEM((B_HEAD_DIM, B_HEAD_DIM), F32)],
        compiler_params=_cparams(("parallel", "parallel", "arbitrary")),
        name="gdn_scan",
    )(u, w, qd, kd, attn, egl, h, b_out_norm.reshape(1, -1))


def _rope_kernel(inv_ref, cos_ref, sin_ref):
    tm = cos_ref.shape[0]
    pos = (pl.program_id(0) * tm + lax.broadcasted_iota(I32, (tm, 128), 0)).astype(F32)
    ang = pos * inv_ref[...]
    lane = lax.broadcasted_iota(I32, (tm, 128), 1)
    cos_ref[...] = jnp.cos(ang)
    sin_ref[...] = jnp.where(lane < 64, -jnp.sin(ang), jnp.sin(ang))


def _rope_tables(seq):
    half = C_QK_DIM // 2
    inv = 1.0 / (ROPE_BASE ** jnp.linspace(0.0, 1.0, half, dtype=F32))
    inv2 = jnp.concatenate([inv, inv]).reshape(1, C_QK_DIM)
    tm = min(1024, seq)
    return pl.pallas_call(
        _rope_kernel,
        out_shape=(jax.ShapeDtypeStruct((seq, C_QK_DIM), F32),) * 2,
        grid=(seq // tm,),
        in_specs=[pl.BlockSpec((1, C_QK_DIM), lambda i: (0, 0))],
        out_specs=(pl.BlockSpec((tm, C_QK_DIM), lambda i: (i, 0)),) * 2,
        compiler_params=_cparams(("parallel",)),
        name="rope_tables",
    )(inv2)


RET_TM = 512


def _retention_kernel(lg_ref, q_ref, k_ref, v_ref, z_ref, cos_ref, sin_ref, gn_ref, o_ref,
                      r_ref, dm_ref, zeta_ref, xi_ref):
    hd = pl.program_id(1)
    lg = lg_ref[0, hd]

    @pl.when(pl.program_id(2) == 0)
    def _():
        r_ref[...] = jnp.zeros(r_ref.shape, F32)
        ri = lax.broadcasted_iota(I32, (C_CHUNK, C_CHUNK), 0)
        ci = lax.broadcasted_iota(I32, (C_CHUNK, C_CHUNK), 1)
        diff = (ri - ci).astype(F32)
        dm_ref[...] = jnp.where(diff >= 0, jnp.exp(jnp.maximum(diff, 0.0) * lg), 0.0)
        zeta_ref[...] = jnp.exp((C_CHUNK - 1 - ri).astype(F32) * lg)
        rv = lax.broadcasted_iota(I32, (C_CHUNK, C_V_DIM), 0).astype(F32)
        xi_ref[...] = jnp.exp((rv + 1.0) * lg)

    g_chunk = lg_ref[1, hd]
    tm = q_ref.shape[0]
    state = r_ref[...]
    dmask = dm_ref[...]
    gain = gn_ref[...]
    for c in range(tm // C_CHUNK):
        rs = slice(c * C_CHUNK, (c + 1) * C_CHUNK)
        cos = cos_ref[rs, :]
        sin = sin_ref[rs, :]
        q = q_ref[rs, :]
        k = k_ref[rs, :]
        qr = q * cos + pltpu.roll(q, C_QK_DIM // 2, 1) * sin
        kr = (k * cos + pltpu.roll(k, C_QK_DIM // 2, 1) * sin) * (C_QK_DIM ** -0.5)
        vb = v_ref[rs, :].astype(BF16)
        qb = qr.astype(BF16)
        s = _dot_nt(qb, kr.astype(BF16)) * dmask
        o = _dot(s.astype(BF16), vb) + _dot(qb, state.astype(BF16)) * xi_ref[...]
        state = state * g_chunk + _dot((kr * zeta_ref[...]).T.astype(BF16), vb)
        mu = jnp.mean(o, axis=-1, keepdims=True)
        oc = o - mu
        var = jnp.mean(oc * oc, axis=-1, keepdims=True)
        y = oc * lax.rsqrt(var + EPS) * gain
        o_ref[rs, :] = (y * _silu(z_ref[rs, :])).astype(BF16)
    r_ref[...] = state


def _retention(h, bsz, seq, cos2, sin2, c_out_norm):
    tm = min(RET_TM, seq)
    nt = seq // tm
    lg = np.log1p(-np.exp2(-5.0 - np.arange(C_HEADS, dtype=np.float32))).astype(np.float32)
    lg = np.stack([lg, np.exp(np.float32(C_CHUNK) * lg).astype(np.float32)])
    return pl.pallas_call(
        _retention_kernel,
        out_shape=jax.ShapeDtypeStruct((bsz * seq, C_HEADS * C_V_DIM), BF16),
        grid=(bsz, C_HEADS, nt),
        in_specs=[pl.BlockSpec(memory_space=pltpu.SMEM),
                  pl.BlockSpec((tm, 128), lambda b, hd, i: (b * nt + i, OD_CQ // 128 + hd)),
                  pl.BlockSpec((tm, 128), lambda b, hd, i: (b * nt + i, OD_CK // 128 + hd)),
                  pl.BlockSpec((tm, 256), lambda b, hd, i: (b * nt + i, OD_CV // 256 + hd)),
                  pl.BlockSpec((tm, 256), lambda b, hd, i: (b * nt + i, OD_CZ // 256 + hd)),
                  pl.BlockSpec((tm, 128), lambda b, hd, i: (i, 0)),
                  pl.BlockSpec((tm, 128), lambda b, hd, i: (i, 0)),
                  pl.BlockSpec((1, C_V_DIM), lambda b, hd, i: (0, hd))],
        out_specs=pl.BlockSpec((tm, C_V_DIM), lambda b, hd, i: (b * nt + i, hd)),
        scratch_shapes=[pltpu.VMEM((C_QK_DIM, C_V_DIM), F32),
                        pltpu.VMEM((C_CHUNK, C_CHUNK), F32),
                        pltpu.VMEM((C_CHUNK, C_QK_DIM), F32),
                        pltpu.VMEM((C_CHUNK, C_V_DIM), F32)],
        compiler_params=_cparams(("parallel", "parallel", "arbitrary")),
        name="retention",
    )(jnp.asarray(lg), h, h, h, h, cos2, sin2, c_out_norm.reshape(1, -1))


def _s5_param_kernel(lre_ref, lim_ref, ldt_ref, lrex_ref, limx_ref, ldtx_ref, bre_ref, bim_ref,
                     pre_ref, pim_ref, bbre_ref, bbim_ref):
    def disc(lre, lim, ldt):
        lr = jnp.minimum(lre, -1e-4)
        dt = jnp.exp(ldt)
        mag = jnp.exp(lr * dt)
        return lr, lim, mag * jnp.cos(lim * dt), mag * jnp.sin(lim * dt)

    _, _, a_re, a_im = disc(lre_ref[...], lim_ref[...], ldt_ref[...])
    p_re, p_im = a_re, a_im
    pre_ref[0] = p_re
    pim_ref[0] = p_im
    for r in range(1, 8):
        p_re, p_im = p_re * a_re - p_im * a_im, p_re * a_im + p_im * a_re
        pre_ref[r] = p_re
        pim_ref[r] = p_im
    lr, li, ax_re, ax_im = disc(lrex_ref[...], limx_ref[...], ldtx_ref[...])
    den = lr * lr + li * li
    f_re = ((ax_re - 1.0) * lr + ax_im * li) / den
    f_im = (ax_im * lr - (ax_re - 1.0) * li) / den
    bbre_ref[...] = f_re * bre_ref[...] - f_im * bim_ref[...]
    bbim_ref[...] = f_re * bim_ref[...] + f_im * bre_ref[...]


def _s5_params(lam_re, lam_im, log_dt, b_re, b_im, c_re, c_im):
    g, p, ch = D_GROUPS, D_STATE, D_GROUP
    ldt = jnp.broadcast_to(log_dt[:, None], (g, p))
    rep = lambda a: jnp.repeat(a, ch, axis=1)
    vm = pl.BlockSpec(memory_space=pltpu.VMEM)
    pw_re, pw_im, bb_re, bb_im = pl.pallas_call(
        _s5_param_kernel,
        out_shape=(jax.ShapeDtypeStruct((8, g, p), F32), jax.ShapeDtypeStruct((8, g, p), F32),
                   jax.ShapeDtypeStruct((g, p * ch), F32), jax.ShapeDtypeStruct((g, p * ch), F32)),
        in_specs=[vm] * 8, out_specs=(vm,) * 4,
        name="s5_params",
    )(lam_re, lam_im, ldt, rep(lam_re), rep(lam_im), rep(ldt),
      b_re.reshape(g, p * ch), b_im.reshape(g, p * ch))
    eye = jnp.eye(D_SETS * 4, dtype=F32)

    def pack_b(bb):
        bb = bb.reshape(D_SETS, 16, p, ch)
        return jnp.einsum('sgpi,gh->sgihp', bb, eye).reshape(D_SETS, D_SET_CH, D_SET_ST)

    def pack_c(c):
        c = c.reshape(D_SETS, 16, ch, p)
        return jnp.einsum('sgjp,gh->sgphj', c, eye).reshape(D_SETS, D_SET_ST, D_SET_CH)

    bd = jnp.concatenate([pack_b(bb_re), pack_b(bb_im)], axis=-1).astype(BF16)
    return (pw_re.reshape(8, g * p), pw_im.reshape(8, g * p), bd,
            pack_c(c_re).astype(BF16), pack_c(c_im).astype(BF16))


S5_TM = 256


def _s5_kernel(u_ref, z_ref, pre_ref, pim_ref, bd_ref, cre_ref, cim_ref, dskip_ref, wglu_ref, bglu_ref,
               o_ref, hre_ref, him_ref, xre_ref, xim_ref, y_ref):
    @pl.when(pl.program_id(1) == 0)
    def _():
        hre_ref[...] = jnp.zeros(hre_ref.shape, F32)
        him_ref[...] = jnp.zeros(him_ref.shape, F32)

    tm = u_ref.shape[0]
    u = u_ref[...]
    ub = u.astype(BF16)
    sub = lax.broadcasted_iota(I32, (tm, D_SET_ST), 0) & 7
    for s in range(D_SETS):
        st = slice(s * D_SET_ST, (s + 1) * D_SET_ST)
        bu = _dot(ub[:, s * D_SET_CH:(s + 1) * D_SET_CH], bd_ref[s])
        xr = bu[:, :D_SET_ST]
        xi = bu[:, D_SET_ST:]
        for sh, r in ((1, 0), (2, 1), (4, 3)):
            ar = pre_ref[r:r + 1, st]
            ai = pim_ref[r:r + 1, st]
            keep = sub >= sh
            sr = jnp.where(keep, pltpu.roll(xr, sh, 0), 0.0)
            si = jnp.where(keep, pltpu.roll(xi, sh, 0), 0.0)
            xr, xi = xr + ar * sr - ai * si, xi + ar * si + ai * sr
        xre_ref[...] = xr
        xim_ref[...] = xi
        a8r = pre_ref[:, st]
        a8i = pim_ref[:, st]

        def carry_body(i, carry):
            cr, ci = carry
            rows = pl.ds(pl.multiple_of(i * 8, 8), 8)
            yr = xre_ref[rows, :] + a8r * cr - a8i * ci
            yi = xim_ref[rows, :] + a8r * ci + a8i * cr
            xre_ref[rows, :] = yr
            xim_ref[rows, :] = yi
            return yr[7:8, :], yi[7:8, :]

        cr, ci = lax.fori_loop(0, tm // 8, carry_body, (hre_ref[:, st], him_ref[:, st]))
        hre_ref[:, st] = cr
        him_ref[:, st] = ci
        y_ref[:, s * D_SET_CH:(s + 1) * D_SET_CH] = (
            _dot(xre_ref[...].astype(BF16), cre_ref[s]) - _dot(xim_ref[...].astype(BF16), cim_ref[s]))
    y = y_ref[...] + dskip_ref[...] * u
    y = 0.5 * y * (1.0 + jnp.tanh(math.sqrt(2.0 / math.pi) * (y + 0.044715 * (y * y * y))))
    gate = _sigmoid(_dot(y.astype(BF16), wglu_ref[...]) + bglu_ref[...])
    o_ref[...] = (y * gate * _silu(z_ref[...])).astype(BF16)


def _s5(h, bsz, seq, pw_re, pw_im, bd, cd_re, cd_im, d_skip, w_glu, b_glu):
    tm = min(S5_TM, seq)
    nt = seq // tm
    full = lambda *shape: pl.BlockSpec(shape, lambda b, i: (0,) * len(shape))
    width = D_GROUPS * D_GROUP
    return pl.pallas_call(
        _s5_kernel,
        out_shape=jax.ShapeDtypeStruct((bsz * seq, width), BF16),
        grid=(bsz, nt),
        in_specs=[pl.BlockSpec((tm, width), lambda b, i: (b * nt + i, OD_DU // width)),
                  pl.BlockSpec((tm, width), lambda b, i: (b * nt + i, OD_DZ // width)),
                  full(8, D_GROUPS * D_STATE), full(8, D_GROUPS * D_STATE),
                  full(D_SETS, D_SET_CH, 2 * D_SET_ST),
                  full(D_SETS, D_SET_ST, D_SET_CH), full(D_SETS, D_SET_ST, D_SET_CH),
                  full(1, width), full(width, width), full(1, width)],
        out_specs=pl.BlockSpec((tm, width), lambda b, i: (b * nt + i, 0)),
        scratch_shapes=[pltpu.VMEM((1, D_GROUPS * D_STATE), F32), pltpu.VMEM((1, D_GROUPS * D_STATE), F32),
                        pltpu.VMEM((tm, D_SET_ST), F32), pltpu.VMEM((tm, D_SET_ST), F32),
                        pltpu.VMEM((tm, width), F32)],
        compiler_params=_cparams(("parallel", "arbitrary")),
        name="s5",
    )(h, h, pw_re, pw_im, bd, cd_re, cd_im, d_skip.reshape(1, -1), w_glu.astype(BF16),
      b_glu.reshape(1, -1))


def _pack_even_w(w_in):
    sizes = (1024, A_KV_RANK, A_IDX_HEADS * A_IDX_DIM, A_IDX_DIM, A_IDX_HEADS, 1024,
             3 * 1024, B_HEADS, B_HEADS, 1024)
    parts, start = [], 0
    for s in sizes:
        parts.append(w_in[:, start:start + s])
        start += s
    aq, ckv, qi, ki, wi, az, bqkv, ba, bb, bz = parts
    pad = jnp.zeros((w_in.shape[0], EV_WIDTH - start), w_in.dtype)
    return jnp.concatenate([aq, az, bqkv, bz, qi, ckv, ki, wi, ba, bb, pad], axis=1).astype(BF16)


def _even_layer(x2d, bsz, seq, bias, norm_g, w_in, a_q_norm, a_kv_norm, w_kv_up, a_k_norm,
                b_conv, b_a_log, b_dt_bias, b_out_norm, w_out):
    h = _norm_proj(x2d, norm_g, _pack_even_w(w_in), tn=768)
    qh, qih, wt, kcat, vt = _dsa_prep(h, bsz * seq // QB, a_q_norm, a_kv_norm, w_kv_up, a_k_norm)
    mix_a = _dsa_attend(h, bsz, seq, qh, qih, wt, kcat, vt, bias)
    u, w, qd, kd, attn, egl = _gdn_prep(h, bsz, seq, b_conv, b_a_log, b_dt_bias)
    mix_b = _gdn_scan(h, bsz, seq, u, w, qd, kd, attn, egl, b_out_norm)
    return _out_proj(x2d, mix_a, mix_b, w_out)


def _odd_layer(x2d, bsz, seq, cos2, sin2, norm_g, w_in, c_out_norm, lam_re, lam_im, log_dt,
               b_re, b_im, c_re, c_im, d_skip, w_glu, b_glu, w_out):
    h = _norm_proj(x2d, norm_g, w_in.astype(BF16), tn=1024)
    mix_c = _retention(h, bsz, seq, cos2, sin2, c_out_norm)
    s5p = _s5_params(lam_re, lam_im, log_dt, b_re, b_im, c_re, c_im)
    mix_d = _s5(h, bsz, seq, *s5p, d_skip, w_glu, b_glu)
    return _out_proj(x2d, mix_c, mix_d, w_out)


def kernel(x, rel_bias, ev_norm, ev_w_in, ev_a_q_norm, ev_a_kv_norm, ev_w_kv_up, ev_a_k_norm,
           ev_b_conv, ev_b_a_log, ev_b_dt_bias, ev_b_out_norm, ev_w_out,
           od_norm, od_w_in, od_c_out_norm, od_lam_re, od_lam_im, od_log_dt,
           od_b_re, od_b_im, od_c_re, od_c_im, od_d_skip, od_w_glu, od_b_glu, od_w_out):
    bsz, seq, d = x.shape
    depth = ev_norm.shape[0] + od_norm.shape[0]
    x2d = x.reshape(bsz * seq, d)
    bias = _bias_table(rel_bias)
    cos2, sin2 = _rope_tables(seq)
    for layer in range(depth):
        i = layer // 2
        if layer % 2 == 0:
            x2d = _even_layer(x2d, bsz, seq, bias, ev_norm[i], ev_w_in[i], ev_a_q_norm[i],
                              ev_a_kv_norm[i], ev_w_kv_up[i], ev_a_k_norm[i], ev_b_conv[i],
                              ev_b_a_log[i], ev_b_dt_bias[i], ev_b_out_norm[i], ev_w_out[i])
        else:
            x2d = _odd_layer(x2d, bsz, seq, cos2, sin2, od_norm[i], od_w_in[i], od_c_out_norm[i],
                             od_lam_re[i], od_lam_im[i], od_log_dt[i], od_b_re[i], od_b_im[i],
                             od_c_re[i], od_c_im[i], od_d_skip[i], od_w_glu[i], od_b_glu[i], od_w_out[i])
    return x2d.reshape(bsz, seq, d)
```

```python
import functools
import math

import numpy as np
import jax
import jax.numpy as jnp
from jax import lax
from jax.experimental import pallas as pl
from jax.experimental.pallas import tpu as pltpu

F32 = jnp.float32
BF16 = jnp.bfloat16
I32 = jnp.int32

D_MODEL = 1024
EPS = 1e-6
LOG2E = 1.4426950408889634
NEG_INF = float("-inf")
INT_MIN = -(2 ** 31)

A_HEADS, A_HEAD_DIM, A_KV_RANK = 16, 64, 128
A_IDX_HEADS, A_IDX_DIM, A_TOPK_MAX = 8, 64, 256
QB = 128
REL_BUCKETS, REL_MAX_DIST = 32, 128
VT_ROWS = 80
SB = 2 * QB
SCORE_ROWS = 4 * QB
BIAS_ROWS = 5 * QB
B_HEADS, B_HEAD_DIM, CONV_K, B_CHUNK = 8, 128, 4, 64
C_HEADS, C_QK_DIM, C_V_DIM, C_CHUNK = 4, 128, 256, 128
ROPE_BASE = 10000.0
D_GROUP, D_STATE, D_GROUPS = 16, 64, 64
D_SETS, D_SET_CH, D_SET_ST = 4, 256, 1024

EV_AQ, EV_AZ, EV_BQ, EV_BK, EV_BV, EV_BZ, EV_QI, EV_CKV, EV_MISC = (
    0, 1024, 2048, 3072, 4096, 5120, 6144, 6656, 6784)
EV_WIDTH = 6912
MISC_KI, MISC_WI, MISC_BA, MISC_BB = 0, 64, 72, 80
OD_CQ, OD_CK, OD_CV, OD_CZ, OD_DU, OD_DZ = 0, 512, 1024, 2048, 3072, 4096
OD_WIDTH = 5120

VMEM_LIMIT = 48 * 1024 * 1024


def _cparams(sem):
    return pltpu.CompilerParams(dimension_semantics=sem, vmem_limit_bytes=VMEM_LIMIT)


def _dot(a, b):
    return jnp.dot(a, b, preferred_element_type=F32)


def _dot_nt(a, b):
    return lax.dot_general(a, b, (((1,), (1,)), ((), ())), preferred_element_type=F32)


def _split_bf16(x, n):
    parts = []
    for _ in range(n):
        p = x.astype(BF16)
        parts.append(p)
        x = x - p.astype(F32)
    return parts


def _dot_sel(sel, x):
    sel = sel.astype(BF16)
    hi, mid, lo = _split_bf16(x, 3)
    return _dot(sel, hi) + (_dot(sel, mid) + _dot(sel, lo))


def _dot_xsel(x, sel):
    sel = sel.astype(BF16)
    hi, mid, lo = _split_bf16(x, 3)
    return _dot(hi, sel) + (_dot(mid, sel) + _dot(lo, sel))


def _dot_x3(a, b):
    ah, al = _split_bf16(a, 2)
    bh, bl = _split_bf16(b, 2)
    return _dot(ah, bh) + (_dot(ah, bl) + _dot(al, bh))


def _sigmoid(x):
    return 1.0 / (1.0 + jnp.exp(-x))


def _silu(x):
    return x * _sigmoid(x)


def _norm_proj_kernel(x_ref, g_ref, w_ref, o_ref, xn_ref):
    @pl.when(pl.program_id(1) == 0)
    def _():
        x = x_ref[...]
        ms = jnp.mean(x * x, axis=-1, keepdims=True)
        xn_ref[...] = (x * lax.rsqrt(ms + EPS) * g_ref[...]).astype(BF16)

    o_ref[...] = _dot(xn_ref[...], w_ref[...])


def _norm_proj(x2d, gain, w_bf16, tn, tm=1024):
    t, d = x2d.shape
    n = w_bf16.shape[1]
    tm = min(tm, t)
    return pl.pallas_call(
        _norm_proj_kernel,
        out_shape=jax.ShapeDtypeStruct((t, n), F32),
        grid=(t // tm, n // tn),
        in_specs=[pl.BlockSpec((tm, d), lambda i, j: (i, 0)),
                  pl.BlockSpec((1, d), lambda i, j: (0, 0)),
                  pl.BlockSpec((d, tn), lambda i, j: (0, j))],
        out_specs=pl.BlockSpec((tm, tn), lambda i, j: (i, j)),
        scratch_shapes=[pltpu.VMEM((tm, d), BF16)],
        compiler_params=_cparams(("parallel", "arbitrary")),
        name="norm_proj",
    )(x2d, gain.reshape(1, d), w_bf16)


def _out_proj_kernel(x_ref, a_ref, b_ref, wa_ref, wb_ref, o_ref):
    o_ref[...] = x_ref[...] + _dot(a_ref[...], wa_ref[...]) + _dot(b_ref[...], wb_ref[...])


def _out_proj(x2d, mix_a, mix_b, w_out, tm=512):
    t, d = x2d.shape
    half = mix_a.shape[1]
    tm = min(tm, t)
    wa = w_out[:half].astype(BF16)
    wb = w_out[half:].astype(BF16)
    return pl.pallas_call(
        _out_proj_kernel,
        out_shape=jax.ShapeDtypeStruct((t, d), F32),
        grid=(t // tm,),
        in_specs=[pl.BlockSpec((tm, d), lambda i: (i, 0)),
                  pl.BlockSpec((tm, half), lambda i: (i, 0)),
                  pl.BlockSpec((tm, half), lambda i: (i, 0)),
                  pl.BlockSpec((half, d), lambda i: (0, 0)),
                  pl.BlockSpec((half, d), lambda i: (0, 0))],
        out_specs=pl.BlockSpec((tm, d), lambda i: (i, 0)),
        compiler_params=_cparams(("parallel",)),
        name="out_proj",
    )(x2d, mix_a, mix_b, wa, wb)


def _t5_bucket_starts():
    exact = REL_BUCKETS // 2
    n = np.arange(0, 4 * REL_MAX_DIST, dtype=np.int64)
    ratio = np.maximum(n, 1).astype(np.float32) / np.float32(exact)
    large = exact + (np.log(ratio).astype(np.float32) / np.float32(math.log(REL_MAX_DIST / exact))
                     * np.float32(REL_BUCKETS - exact)).astype(np.int32)
    bucket = np.where(n < exact, n, np.minimum(large, REL_BUCKETS - 1))
    starts = [int(np.argmax(bucket >= b)) for b in range(REL_BUCKETS)]
    assert all(bucket[s] == b for b, s in enumerate(starts)) and np.all(np.diff(bucket) >= 0)
    assert starts[-1] <= QB, "distances beyond one key block must share the last bucket"
    return starts


_BUCKET_STARTS = _t5_bucket_starts()


def _bias_table_kernel(rb_ref, o_ref):
    row = lax.broadcasted_iota(I32, (BIAS_ROWS, QB), 0)
    lane = lax.broadcasted_iota(I32, (BIAS_ROWS, QB), 1)
    dist = lane + 3 * QB - row
    for h in range(A_HEADS):
        val = jnp.full((BIAS_ROWS, QB), rb_ref[0, h], F32)
        for b in range(1, REL_BUCKETS):
            val = jnp.where(dist >= _BUCKET_STARTS[b], rb_ref[b, h], val)
        val = (val - rb_ref[REL_BUCKETS - 1, h]) * LOG2E
        o_ref[:, h * QB:(h + 1) * QB] = jnp.where(dist >= 0, val, 0.0)


def _bias_table(rel_bias):
    return pl.pallas_call(
        _bias_table_kernel,
        out_shape=jax.ShapeDtypeStruct((BIAS_ROWS, A_HEADS * QB), F32),
        in_specs=[pl.BlockSpec(memory_space=pltpu.SMEM)],
        out_specs=pl.BlockSpec(memory_space=pltpu.VMEM),
        name="dsa_bias_table",
    )(rel_bias)


def _dsa_prep_kernel(aq_ref, qi_ref, ckv_ref, misc_ref, gq_ref, gkv_ref, wkv_ref, gk_ref,
                     qh_ref, qih_ref, wt_ref, kcat_ref, vt_ref):
    zeros = jnp.zeros((QB, A_HEAD_DIM), F32)
    aq = aq_ref[...]
    gq = gq_ref[...]
    for h in range(A_HEADS):
        q = aq[:, h * A_HEAD_DIM:(h + 1) * A_HEAD_DIM]
        ms = jnp.mean(q * q, axis=-1, keepdims=True)
        qn = q * lax.rsqrt(ms + EPS) * gq * (A_HEAD_DIM ** -0.5 * LOG2E)
        qh_ref[h * QB:(h + 1) * QB, :] = jnp.concatenate([zeros, qn], axis=1).astype(BF16)
    qi = qi_ref[...]
    for h in range(A_IDX_HEADS):
        qih = qi[:, h * A_IDX_DIM:(h + 1) * A_IDX_DIM]
        qih_ref[h * QB:(h + 1) * QB, :] = jnp.concatenate([qih, zeros], axis=1).astype(BF16)
    misc = misc_ref[...]
    wt_ref[...] = misc.T[MISC_WI:MISC_WI + A_IDX_HEADS, :] * (A_IDX_HEADS ** -0.5 * A_IDX_DIM ** -0.5)
    c = ckv_ref[...]
    cn = c * lax.rsqrt(jnp.mean(c * c, axis=-1, keepdims=True) + EPS) * gkv_ref[...]
    kv = _dot(cn.astype(BF16), wkv_ref[...])
    k = kv[:, :A_HEAD_DIM]
    kn = k * lax.rsqrt(jnp.mean(k * k, axis=-1, keepdims=True) + EPS) * gk_ref[...]
    kcat_ref[...] = jnp.concatenate([misc[:, MISC_KI:MISC_KI + A_IDX_DIM], kn], axis=1).astype(BF16)
    kvt = kv.T
    vt_ref[0:A_HEAD_DIM, :] = kvt[A_HEAD_DIM:, :].astype(BF16)
    ones_row = lax.broadcasted_iota(I32, (VT_ROWS - A_HEAD_DIM, QB), 0) == 0
    vt_ref[A_HEAD_DIM:, :] = jnp.where(ones_row, 1.0, 0.0).astype(BF16)


def _dsa_prep(h, nblk, a_q_norm, a_kv_norm, w_kv_up, a_k_norm):
    cb = lambda width, off: off // width
    return pl.pallas_call(
        _dsa_prep_kernel,
        out_shape=(jax.ShapeDtypeStruct((nblk, A_HEADS * QB, 128), BF16),
                   jax.ShapeDtypeStruct((nblk, A_IDX_HEADS * QB, 128), BF16),
                   jax.ShapeDtypeStruct((nblk, A_IDX_HEADS, QB), F32),
                   jax.ShapeDtypeStruct((nblk, QB, 128), BF16),
                   jax.ShapeDtypeStruct((nblk, VT_ROWS, QB), BF16)),
        grid=(nblk,),
        in_specs=[pl.BlockSpec((QB, 1024), lambda i: (i, cb(1024, EV_AQ))),
                  pl.BlockSpec((QB, 512), lambda i: (i, cb(512, EV_QI))),
                  pl.BlockSpec((QB, 128), lambda i: (i, cb(128, EV_CKV))),
                  pl.BlockSpec((QB, 128), lambda i: (i, cb(128, EV_MISC))),
                  pl.BlockSpec((1, A_HEAD_DIM), lambda i: (0, 0)),
                  pl.BlockSpec((1, A_KV_RANK), lambda i: (0, 0)),
                  pl.BlockSpec((A_KV_RANK, 2 * A_HEAD_DIM), lambda i: (0, 0)),
                  pl.BlockSpec((1, A_HEAD_DIM), lambda i: (0, 0))],
        out_specs=(pl.BlockSpec((None, A_HEADS * QB, 128), lambda i: (i, 0, 0)),
                   pl.BlockSpec((None, A_IDX_HEADS * QB, 128), lambda i: (i, 0, 0)),
                   pl.BlockSpec((None, A_IDX_HEADS, QB), lambda i: (i, 0, 0)),
                   pl.BlockSpec((None, QB, 128), lambda i: (i, 0, 0)),
                   pl.BlockSpec((None, VT_ROWS, QB), lambda i: (i, 0, 0))),
        compiler_params=_cparams(("parallel",)),
        name="dsa_prep",
    )(h, h, h, h, a_q_norm.reshape(1, -1), a_kv_norm.reshape(1, -1), w_kv_up.astype(BF16),
      a_k_norm.reshape(1, -1))


def _dsa_kernel(topk, qih_ref, wt_ref, qh_ref, kcat_ref, vt_ref, bias_ref, az_ref, o_ref,
                strip_ref, x_ref, m_ref, acc_ref, lg_ref, p_ref):
    qt = pl.program_id(1)
    t0 = qt * QB
    n_sc = qt // 4 + 1
    n_sb = qt // 2 + 1
    lane_s = lax.broadcasted_iota(I32, (SCORE_ROWS, QB), 1)
    row_s = lax.broadcasted_iota(I32, (SCORE_ROWS, QB), 0)

    w = wt_ref[...]

    def score_body(j, carry):
        kblk = kcat_ref[pl.ds(j * 4, 4)].reshape(SCORE_ROWS, 128)
        tot = None
        for hp in range(A_IDX_HEADS // 2):
            s = _dot_nt(kblk, qih_ref[hp * SB:(hp + 1) * SB, :])
            for e in range(2):
                h = 2 * hp + e
                term = jnp.maximum(s[:, e * QB:(e + 1) * QB], 0.0) * w[h:h + 1, :]
                tot = term if tot is None else tot + term
        tot = jnp.where(j * SCORE_ROWS + row_s <= t0 + lane_s, tot, NEG_INF)
        bits = pltpu.bitcast(tot, I32)
        strip_ref[pl.ds(pl.multiple_of(j * SCORE_ROWS, SCORE_ROWS), SCORE_ROWS), :] = (
            bits ^ ((bits >> 31) & 0x7FFFFFFF))
        return carry

    lax.fori_loop(0, n_sc, score_body, 0)

    def count(pred):
        def body(j, acc):
            r0 = pl.multiple_of(j * SCORE_ROWS, SCORE_ROWS)
            m = jnp.where(pred(strip_ref[pl.ds(r0, SCORE_ROWS), :], r0), 1, 0)
            return acc + m.reshape(SCORE_ROWS // 32, 32, QB).sum(axis=0)
        acc = lax.fori_loop(0, n_sc, body, jnp.zeros((32, QB), I32))
        return acc.reshape(4, 8, QB).sum(axis=0).sum(axis=0, keepdims=True)

    c0 = count(lambda key, r0: key >= 0)
    nonneg = c0 >= topk
    thr0 = jnp.where(nonneg, 0, INT_MIN)
    cnt0 = jnp.where(nonneg, c0, n_sc * SCORE_ROWS)

    def bit_body(i, carry):
        thr, cnt = carry
        cand = thr | lax.shift_left(jnp.int32(1), 30 - i)
        c = count(lambda key, r0: key >= cand)
        ok = c >= topk
        return jnp.where(ok, cand, thr), jnp.where(ok, c, cnt)

    thr, cnt = lax.fori_loop(0, 31, bit_body, (thr0, cnt0))

    x_ref[...] = jnp.full((1, QB), 2 ** 30, I32)

    @pl.when(jnp.max(cnt) > topk)
    def _():
        above = count(lambda key, r0: key > thr)
        need = topk - above

        def tie_body(i, xb):
            cand = xb | lax.shift_left(jnp.int32(1), 14 - i)
            c = count(lambda key, r0: (key == thr) & (r0 + row_s < cand))
            return jnp.where(c <= need, cand, xb)

        x_ref[...] = lax.fori_loop(0, 15, tie_body, jnp.zeros((1, QB), I32))

    xb = x_ref[...]

    m_ref[...] = jnp.full(m_ref.shape, -1e30, F32)
    acc_ref[...] = jnp.zeros(acc_ref.shape, F32)
    n_lt = A_HEADS * QB // SB
    row_a = lax.broadcasted_iota(I32, (SB, QB), 0)
    lane_a = lax.broadcasted_iota(I32, (SB, QB), 1)

    def att_block(j, near):
        r0 = pl.multiple_of(j * SB, SB)
        key = strip_ref[pl.ds(r0, SB), :]
        idx = r0 + row_a
        sel = (key > thr) | ((key == thr) & (idx < xb))
        if near:
            sel = sel & (idx <= t0 + lane_a)
            b0 = pl.multiple_of((2 * j - qt + 3) * QB, QB)
        pen = jnp.where(sel, 0.0, NEG_INF)
        pen2 = jnp.concatenate([pen, pen], axis=1)
        kblk = kcat_ref[pl.ds(2 * j, 2)].reshape(SB, 128)
        vt = jnp.concatenate([vt_ref[2 * j], vt_ref[2 * j + 1]], axis=1)
        mx = []
        for lt in range(n_lt):
            cols = slice(lt * SB, (lt + 1) * SB)
            lg = _dot_nt(kblk, qh_ref[cols, :]) + pen2
            if near:
                lg = lg + bias_ref[pl.ds(b0, SB), cols]
            lg_ref[:, cols] = lg
            mx.append(jnp.max(lg, axis=0, keepdims=True))
        m_old = m_ref[...]
        m_new = jnp.maximum(m_old, jnp.concatenate(mx, axis=1))
        m_ref[...] = m_new
        for lt in range(n_lt):
            cols = slice(lt * SB, (lt + 1) * SB)
            p_ref[:, cols] = jnp.exp2(lg_ref[:, cols] - m_new[:, cols]).astype(BF16)
        acc_ref[...] = jnp.exp2(m_old - m_new) * acc_ref[...] + _dot(vt, p_ref[...])

    def far_body(j, carry):
        att_block(j, False)
        return carry

    lax.fori_loop(0, jnp.maximum(n_sb - 2, 0), far_body, 0)

    @pl.when(n_sb >= 2)
    def _():
        att_block(n_sb - 2, True)

    att_block(n_sb - 1, True)

    acc = acc_ref[...]
    o_t = acc[0:A_HEAD_DIM, :] / acc[A_HEAD_DIM:A_HEAD_DIM + 1, :]
    pad = jnp.zeros((QB - A_HEAD_DIM, QB), F32)
    pieces = []
    for h in range(A_HEADS):
        blk = jnp.concatenate([o_t[:, h * QB:(h + 1) * QB], pad], axis=0)
        pieces.append(blk.T[:, 0:A_HEAD_DIM])
    att = jnp.concatenate(pieces, axis=1)
    o_ref[...] = (att * _silu(az_ref[...])).astype(BF16)


def _dsa_attend(h, bsz, seq, qh, qih, wt, kcat, vt, bias):
    nq = seq // QB
    topk = min(A_TOPK_MAX, seq // 4)
    blk = lambda b, q: (b * nq + q, 0, 0)
    return pl.pallas_call(
        functools.partial(_dsa_kernel, topk),
        out_shape=jax.ShapeDtypeStruct((bsz * seq, A_HEADS * A_HEAD_DIM), BF16),
        grid=(bsz, nq),
        in_specs=[pl.BlockSpec((None, A_IDX_HEADS * QB, 128), blk),
                  pl.BlockSpec((None, A_IDX_HEADS, QB), blk),
                  pl.BlockSpec((None, A_HEADS * QB, 128), blk),
                  pl.BlockSpec((nq, QB, 128), lambda b, q: (b, 0, 0)),
                  pl.BlockSpec((nq, VT_ROWS, QB), lambda b, q: (b, 0, 0)),
                  pl.BlockSpec((BIAS_ROWS, A_HEADS * QB), lambda b, q: (0, 0)),
                  pl.BlockSpec((QB, 1024), lambda b, q: (b * nq + q, EV_AZ // 1024))],
        out_specs=pl.BlockSpec((QB, 1024), lambda b, q: (b * nq + q, 0)),
        scratch_shapes=[pltpu.VMEM((seq, QB), I32),
                        pltpu.VMEM((1, QB), I32),
                        pltpu.VMEM((1, A_HEADS * QB), F32),
                        pltpu.VMEM((VT_ROWS, A_HEADS * QB), F32),
                        pltpu.VMEM((SB, A_HEADS * QB), F32),
                        pltpu.VMEM((SB, A_HEADS * QB), BF16)],
        compiler_params=_cparams(("arbitrary", "arbitrary")),
        name="dsa_attend",
    )(qih, wt, qh, kcat, vt, bias, h)


GDN_TM = 512


def _gdn_prep_kernel(alog_ref, dtb_ref, q_ref, k_ref, v_ref, qh_ref, kh_ref, vh_ref, misc_ref,
                     cq_ref, ck_ref, cv_ref,
                     u_ref, w_ref, qd_ref, kd_ref, attn_ref, egl_ref):
    i = pl.program_id(1)
    hd = pl.program_id(2)
    tm = q_ref.shape[0]
    nchunk = tm // B_CHUNK

    def conv_silu(x_ref, halo_ref, c_ref):
        halo = jnp.where(i > 0, halo_ref[...], 0.0)
        ext = jnp.concatenate([halo, x_ref[...]], axis=0)
        cw = c_ref[...]
        y = ext[8:, :] * cw[CONV_K - 1:CONV_K, :]
        for d in range(1, CONV_K):
            y = y + pltpu.roll(ext, d, 0)[8:, :] * cw[CONV_K - 1 - d:CONV_K - d, :]
        return _silu(y)

    q = conv_silu(q_ref, qh_ref, cq_ref)
    k = conv_silu(k_ref, kh_ref, ck_ref)
    v = conv_silu(v_ref, vh_ref, cv_ref)
    qn = q * lax.rsqrt(jnp.sum(q * q, axis=-1, keepdims=True) + EPS) * (B_HEAD_DIM ** -0.5)
    kn = k * lax.rsqrt(jnp.sum(k * k, axis=-1, keepdims=True) + EPS)

    misc = misc_ref[...]
    sel_r = lax.broadcasted_iota(I32, (128, 128), 0)
    ba = _dot_xsel(misc, jnp.where(sel_r == MISC_BA + hd, 1.0, 0.0))
    bb = _dot_xsel(misc, jnp.where(sel_r == MISC_BB + hd, 1.0, 0.0))
    beta = _sigmoid(bb)
    xg = ba + dtb_ref[hd]
    softplus = jnp.maximum(xg, 0.0) + jnp.log(1.0 + jnp.exp(-jnp.abs(xg)))
    g = -jnp.exp(jnp.full((1, 128), alog_ref[hd], F32)) * softplus

    sl = 256
    r2 = lax.broadcasted_iota(I32, (sl, sl), 0)
    c2 = lax.broadcasted_iota(I32, (sl, sl), 1)
    same = (r2 // B_CHUNK) == (c2 // B_CHUNK)
    tri = jnp.where(same & (c2 <= r2), 1.0, 0.0)
    blk = jnp.where(same, 1.0, 0.0)
    gc = jnp.concatenate([_dot_sel(tri, g[s * sl:(s + 1) * sl, :]) for s in range(tm // sl)], axis=0)
    gl = jnp.concatenate([_dot_sel(blk, g[s * sl:(s + 1) * sl, :]) for s in range(tm // sl)], axis=0)

    kb = kn * beta
    vb = v * beta
    egc = jnp.exp(gc)
    qd_ref[...] = qn * egc
    kd_ref[...] = kn * jnp.exp(gl - gc)
    kbg = kb * egc

    ri = lax.broadcasted_iota(I32, (B_CHUNK, B_CHUNK), 0)
    ci = lax.broadcasted_iota(I32, (B_CHUNK, B_CHUNK), 1)
    eye = jnp.where(ri == ci, 1.0, 0.0)
    for m in range(tm // 128):
        gct = gc[m * 128:(m + 1) * 128, :].T
        for e in range(2):
            c = 2 * m + e
            rs = slice(c * B_CHUNK, (c + 1) * B_CHUNK)
            diff = gc[rs, 0:B_CHUNK] - gct[0:B_CHUNK, e * B_CHUNK:(e + 1) * B_CHUNK]
            decay = jnp.exp(jnp.where(ri >= ci, diff, NEG_INF))
            kc = kn[rs, :]
            a_mat = jnp.where(ri > ci, _dot_nt(kb[rs, :], kc) * decay, 0.0)
            inv = eye - a_mat
            pw = _dot_x3(a_mat, a_mat)
            for step in range(5):
                if step < 4:
                    both = _dot_x3(jnp.concatenate([inv, pw], axis=0), pw)
                    inv = inv + both[0:B_CHUNK, :]
                    pw = both[B_CHUNK:, :]
                else:
                    inv = inv + _dot_x3(inv, pw)
            sol = _dot_x3(inv, jnp.concatenate([vb[rs, :], kbg[rs, :]], axis=1))
            u_ref[rs, :] = sol[:, 0:B_HEAD_DIM]
            w_ref[rs, :] = sol[:, B_HEAD_DIM:]
            attn_ref[rs, :] = jnp.where(ri >= ci, _dot_nt(qn[rs, :], kc) * decay, 0.0)
            egl_ref[c:c + 1, :] = jnp.exp(gl[c * B_CHUNK:c * B_CHUNK + 1, :])
    if nchunk < 8:
        egl_ref[nchunk:, :] = jnp.zeros((8 - nchunk, 128), F32)


def _gdn_prep(h, bsz, seq, b_conv, b_a_log, b_dt_bias):
    tm = min(GDN_TM, seq)
    nt = seq // tm
    nrow = max(tm // B_CHUNK, 8)
    hs = (bsz, B_HEADS, seq, B_HEAD_DIM)
    cur = lambda off: (lambda b, i, hd: (b * nt + i, off // 128 + hd))
    halo = lambda off: (lambda b, i, hd: (jnp.maximum((b * nt + i) * (tm // 8) - 1, 0), off // 128 + hd))
    cw = lambda off: (lambda b, i, hd: (0, off // 128 + hd))
    out = lambda b, i, hd: (b, hd, i, 0)
    smem = pl.BlockSpec(memory_space=pltpu.SMEM)
    return pl.pallas_call(
        _gdn_prep_kernel,
        out_shape=(jax.ShapeDtypeStruct(hs, F32), jax.ShapeDtypeStruct(hs, F32),
                   jax.ShapeDtypeStruct(hs, F32), jax.ShapeDtypeStruct(hs, F32),
                   jax.ShapeDtypeStruct((bsz, B_HEADS, seq, B_CHUNK), F32),
                   jax.ShapeDtypeStruct((bsz, B_HEADS, nt * nrow, 128), F32)),
        grid=(bsz, nt, B_HEADS),
        in_specs=[smem, smem,
                  pl.BlockSpec((tm, 128), cur(EV_BQ)), pl.BlockSpec((tm, 128), cur(EV_BK)),
                  pl.BlockSpec((tm, 128), cur(EV_BV)),
                  pl.BlockSpec((8, 128), halo(EV_BQ)), pl.BlockSpec((8, 128), halo(EV_BK)),
                  pl.BlockSpec((8, 128), halo(EV_BV)),
                  pl.BlockSpec((tm, 128), lambda b, i, hd: (b * nt + i, EV_MISC // 128)),
                  pl.BlockSpec((CONV_K, 128), cw(0)), pl.BlockSpec((CONV_K, 128), cw(1024)),
                  pl.BlockSpec((CONV_K, 128), cw(2048))],
        out_specs=(pl.BlockSpec((None, None, tm, 128), out), pl.BlockSpec((None, None, tm, 128), out),
                   pl.BlockSpec((None, None, tm, 128), out), pl.BlockSpec((None, None, tm, 128), out),
                   pl.BlockSpec((None, None, tm, B_CHUNK), out),
                   pl.BlockSpec((None, None, nrow, 128), out)),
        compiler_params=_cparams(("parallel", "parallel", "parallel")),
        name="gdn_prep",
    )(b_a_log, b_dt_bias, h, h, h, h, h, h, h, b_conv, b_conv, b_conv)


def _gdn_scan_kernel(u_ref, w_ref, qd_ref, kd_ref, attn_ref, egl_ref, z_ref, gn_ref, o_ref, s_ref):
    @pl.when(pl.program_id(2) == 0)
    def _():
        s_ref[...] = jnp.zeros(s_ref.shape, F32)

    tm = u_ref.shape[0]
    state = s_ref[...]
    gain = gn_ref[...]
    for c in range(tm // B_CHUNK):
        rs = slice(c * B_CHUNK, (c + 1) * B_CHUNK)
        sb = state.astype(BF16)
        v_new = u_ref[rs, :] - _dot(w_ref[rs, :].astype(BF16), sb)
        vb = v_new.astype(BF16)
        o = _dot(qd_ref[rs, :].astype(BF16), sb) + _dot(attn_ref[rs, :].astype(BF16), vb)
        state = state * egl_ref[c:c + 1, :] + _dot(kd_ref[rs, :].T.astype(BF16), vb)
        on = o * lax.rsqrt(jnp.mean(o * o, axis=-1, keepdims=True) + EPS) * gain
        o_ref[rs, :] = (on * _silu(z_ref[rs, :])).astype(BF16)
    s_ref[...] = state


def _gdn_scan(h, bsz, seq, u, w, qd, kd, attn, egl, b_out_norm):
    tm = min(GDN_TM, seq)
    nt = seq // tm
    nrow = max(tm // B_CHUNK, 8)
    blk = lambda b, hd, i: (b, hd, i, 0)
    return pl.pallas_call(
        _gdn_scan_kernel,
        out_shape=jax.ShapeDtypeStruct((bsz * seq, B_HEADS * B_HEAD_DIM), BF16),
        grid=(bsz, B_HEADS, nt),
        in_specs=[pl.BlockSpec((None, None, tm, 128), blk), pl.BlockSpec((None, None, tm, 128), blk),
                  pl.BlockSpec((None, None, tm, 128), blk), pl.BlockSpec((None, None, tm, 128), blk),
                  pl.BlockSpec((None, None, tm, B_CHUNK), blk),
                  pl.BlockSpec((None, None, nrow, 128), blk),
                  pl.BlockSpec((tm, 128), lambda b, hd, i: (b * nt + i, EV_BZ // 128 + hd)),
                  pl.BlockSpec((1, 128), lambda b, hd, i: (0, 0))],
        out_specs=pl.BlockSpec((tm, 128), lambda b, hd, i: (b * nt + i, hd)),
        scratch_shapes=[pltpu.VMEM((B_HEAD_DIM, B_HEAD_DIM), F32)],
        compiler_params=_cparams(("parallel", "parallel", "arbitrary")),
        name="gdn_scan",
    )(u, w, qd, kd, attn, egl, h, b_out_norm.reshape(1, -1))


def _rope_kernel(inv_ref, cos_ref, sin_ref):
    tm = cos_ref.shape[0]
    pos = (pl.program_id(0) * tm + lax.broadcasted_iota(I32, (tm, 128), 0)).astype(F32)
    ang = pos * inv_ref[...]
    lane = lax.broadcasted_iota(I32, (tm, 128), 1)
    cos_ref[...] = jnp.cos(ang)
    sin_ref[...] = jnp.where(lane < 64, -jnp.sin(ang), jnp.sin(ang))


def _rope_tables(seq):
    half = C_QK_DIM // 2
    inv = 1.0 / (ROPE_BASE ** jnp.linspace(0.0, 1.0, half, dtype=F32))
    inv2 = jnp.concatenate([inv, inv]).reshape(1, C_QK_DIM)
    tm = min(1024, seq)
    return pl.pallas_call(
        _rope_kernel,
        out_shape=(jax.ShapeDtypeStruct((seq, C_QK_DIM), F32),) * 2,
        grid=(seq // tm,),
        in_specs=[pl.BlockSpec((1, C_QK_DIM), lambda i: (0, 0))],
        out_specs=(pl.BlockSpec((tm, C_QK_DIM), lambda i: (i, 0)),) * 2,
        compiler_params=_cparams(("parallel",)),
        name="rope_tables",
    )(inv2)


RET_TM = 512


def _retention_kernel(lg_ref, q_ref, k_ref, v_ref, z_ref, cos_ref, sin_ref, gn_ref, o_ref,
                      r_ref, dm_ref, zeta_ref, xi_ref):
    hd = pl.program_id(1)
    lg = lg_ref[0, hd]

    @pl.when(pl.program_id(2) == 0)
    def _():
        r_ref[...] = jnp.zeros(r_ref.shape, F32)
        ri = lax.broadcasted_iota(I32, (C_CHUNK, C_CHUNK), 0)
        ci = lax.broadcasted_iota(I32, (C_CHUNK, C_CHUNK), 1)
        diff = (ri - ci).astype(F32)
        dm_ref[...] = jnp.where(diff >= 0, jnp.exp(jnp.maximum(diff, 0.0) * lg), 0.0)
        zeta_ref[...] = jnp.exp((C_CHUNK - 1 - ri).astype(F32) * lg)
        rv = lax.broadcasted_iota(I32, (C_CHUNK, C_V_DIM), 0).astype(F32)
        xi_ref[...] = jnp.exp((rv + 1.0) * lg)

    g_chunk = lg_ref[1, hd]
    tm = q_ref.shape[0]
    state = r_ref[...]
    dmask = dm_ref[...]
    gain = gn_ref[...]
    for c in range(tm // C_CHUNK):
        rs = slice(c * C_CHUNK, (c + 1) * C_CHUNK)
        cos = cos_ref[rs, :]
        sin = sin_ref[rs, :]
        q = q_ref[rs, :]
        k = k_ref[rs, :]
        qr = q * cos + pltpu.roll(q, C_QK_DIM // 2, 1) * sin
        kr = (k * cos + pltpu.roll(k, C_QK_DIM // 2, 1) * sin) * (C_QK_DIM ** -0.5)
        vb = v_ref[rs, :].astype(BF16)
        qb = qr.astype(BF16)
        s = _dot_nt(qb, kr.astype(BF16)) * dmask
        o = _dot(s.astype(BF16), vb) + _dot(qb, state.astype(BF16)) * xi_ref[...]
        state = state * g_chunk + _dot((kr * zeta_ref[...]).T.astype(BF16), vb)
        mu = jnp.mean(o, axis=-1, keepdims=True)
        oc = o - mu
        var = jnp.mean(oc * oc, axis=-1, keepdims=True)
        y = oc * lax.rsqrt(var + EPS) * gain
        o_ref[rs, :] = (y * _silu(z_ref[rs, :])).astype(BF16)
    r_ref[...] = state


def _retention(h, bsz, seq, cos2, sin2, c_out_norm):
    tm = min(RET_TM, seq)
    nt = seq // tm
    lg = np.log1p(-np.exp2(-5.0 - np.arange(C_HEADS, dtype=np.float32))).astype(np.float32)
    lg = np.stack([lg, np.exp(np.float32(C_CHUNK) * lg).astype(np.float32)])
    return pl.pallas_call(
        _retention_kernel,
        out_shape=jax.ShapeDtypeStruct((bsz * seq, C_HEADS * C_V_DIM), BF16),
        grid=(bsz, C_HEADS, nt),
        in_specs=[pl.BlockSpec(memory_space=pltpu.SMEM),
                  pl.BlockSpec((tm, 128), lambda b, hd, i: (b * nt + i, OD_CQ // 128 + hd)),
                  pl.BlockSpec((tm, 128), lambda b, hd, i: (b * nt + i, OD_CK // 128 + hd)),
                  pl.BlockSpec((tm, 256), lambda b, hd, i: (b * nt + i, OD_CV // 256 + hd)),
                  pl.BlockSpec((tm, 256), lambda b, hd, i: (b * nt + i, OD_CZ // 256 + hd)),
                  pl.BlockSpec((tm, 128), lambda b, hd, i: (i, 0)),
                  pl.BlockSpec((tm, 128), lambda b, hd, i: (i, 0)),
                  pl.BlockSpec((1, C_V_DIM), lambda b, hd, i: (0, hd))],
        out_specs=pl.BlockSpec((tm, C_V_DIM), lambda b, hd, i: (b * nt + i, hd)),
        scratch_shapes=[pltpu.VMEM((C_QK_DIM, C_V_DIM), F32),
                        pltpu.VMEM((C_CHUNK, C_CHUNK), F32),
                        pltpu.VMEM((C_CHUNK, C_QK_DIM), F32),
                        pltpu.VMEM((C_CHUNK, C_V_DIM), F32)],
        compiler_params=_cparams(("parallel", "parallel", "arbitrary")),
        name="retention",
    )(jnp.asarray(lg), h, h, h, h, cos2, sin2, c_out_norm.reshape(1, -1))


def _s5_param_kernel(lre_ref, lim_ref, ldt_ref, lrex_ref, limx_ref, ldtx_ref, bre_ref, bim_ref,
                     pre_ref, pim_ref, bbre_ref, bbim_ref):
    def disc(lre, lim, ldt):
        lr = jnp.minimum(lre, -1e-4)
        dt = jnp.exp(ldt)
        mag = jnp.exp(lr * dt)
        return lr, lim, mag * jnp.cos(lim * dt), mag * jnp.sin(lim * dt)

    _, _, a_re, a_im = disc(lre_ref[...], lim_ref[...], ldt_ref[...])
    p_re, p_im = a_re, a_im
    pre_ref[0] = p_re
    pim_ref[0] = p_im
    for r in range(1, 8):
        p_re, p_im = p_re * a_re - p_im * a_im, p_re * a_im + p_im * a_re
        pre_ref[r] = p_re
        pim_ref[r] = p_im
    lr, li, ax_re, ax_im = disc(lrex_ref[...], limx_ref[...], ldtx_ref[...])
    den = lr * lr + li * li
    f_re = ((ax_re - 1.0) * lr + ax_im * li) / den
    f_im = (ax_im * lr - (ax_re - 1.0) * li) / den
    bbre_ref[...] = f_re * bre_ref[...] - f_im * bim_ref[...]
    bbim_ref[...] = f_re * bim_ref[...] + f_im * bre_ref[...]


def _s5_params(lam_re, lam_im, log_dt, b_re, b_im, c_re, c_im):
    g, p, ch = D_GROUPS, D_STATE, D_GROUP
    ldt = jnp.broadcast_to(log_dt[:, None], (g, p))
    rep = lambda a: jnp.repeat(a, ch, axis=1)
    vm = pl.BlockSpec(memory_space=pltpu.VMEM)
    pw_re, pw_im, bb_re, bb_im = pl.pallas_call(
        _s5_param_kernel,
        out_shape=(jax.ShapeDtypeStruct((8, g, p), F32), jax.ShapeDtypeStruct((8, g, p), F32),
                   jax.ShapeDtypeStruct((g, p * ch), F32), jax.ShapeDtypeStruct((g, p * ch), F32)),
        in_specs=[vm] * 8, out_specs=(vm,) * 4,
        name="s5_params",
    )(lam_re, lam_im, ldt, rep(lam_re), rep(lam_im), rep(ldt),
      b_re.reshape(g, p * ch), b_im.reshape(g, p * ch))
    eye = jnp.eye(D_SETS * 4, dtype=F32)

    def pack_b(bb):
        bb = bb.reshape(D_SETS, 16, p, ch)
        return jnp.einsum('sgpi,gh->sgihp', bb, eye).reshape(D_SETS, D_SET_CH, D_SET_ST)

    def pack_c(c):
        c = c.reshape(D_SETS, 16, ch, p)
        return jnp.einsum('sgjp,gh->sgphj', c, eye).reshape(D_SETS, D_SET_ST, D_SET_CH)

    bd = jnp.concatenate([pack_b(bb_re), pack_b(bb_im)], axis=-1).astype(BF16)
    return (pw_re.reshape(8, g * p), pw_im.reshape(8, g * p), bd,
            pack_c(c_re).astype(BF16), pack_c(c_im).astype(BF16))


S5_TM = 256


def _s5_kernel(u_ref, z_ref, pre_ref, pim_ref, bd_ref, cre_ref, cim_ref, dskip_ref, wglu_ref, bglu_ref,
               o_ref, hre_ref, him_ref, xre_ref, xim_ref, y_ref):
    @pl.when(pl.program_id(1) == 0)
    def _():
        hre_ref[...] = jnp.zeros(hre_ref.shape, F32)
        him_ref[...] = jnp.zeros(him_ref.shape, F32)

    tm = u_ref.shape[0]
    u = u_ref[...]
    ub = u.astype(BF16)
    sub = lax.broadcasted_iota(I32, (tm, D_SET_ST), 0) & 7
    for s in range(D_SETS):
        st = slice(s * D_SET_ST, (s + 1) * D_SET_ST)
        bu = _dot(ub[:, s * D_SET_CH:(s + 1) * D_SET_CH], bd_ref[s])
        xr = bu[:, :D_SET_ST]
        xi = bu[:, D_SET_ST:]
        for sh, r in ((1, 0), (2, 1), (4, 3)):
            ar = pre_ref[r:r + 1, st]
            ai = pim_ref[r:r + 1, st]
            keep = sub >= sh
            sr = jnp.where(keep, pltpu.roll(xr, sh, 0), 0.0)
            si = jnp.where(keep, pltpu.roll(xi, sh, 0), 0.0)
            xr, xi = xr + ar * sr - ai * si, xi + ar * si + ai * sr
        xre_ref[...] = xr
        xim_ref[...] = xi
        a8r = pre_ref[:, st]
        a8i = pim_ref[:, st]

        def carry_body(i, carry):
            cr, ci = carry
            rows = pl.ds(pl.multiple_of(i * 8, 8), 8)
            yr = xre_ref[rows, :] + a8r * cr - a8i * ci
            yi = xim_ref[rows, :] + a8r * ci + a8i * cr
            xre_ref[rows, :] = yr
            xim_ref[rows, :] = yi
            return yr[7:8, :], yi[7:8, :]

        cr, ci = lax.fori_loop(0, tm // 8, carry_body, (hre_ref[:, st], him_ref[:, st]))
        hre_ref[:, st] = cr
        him_ref[:, st] = ci
        y_ref[:, s * D_SET_CH:(s + 1) * D_SET_CH] = (
            _dot(xre_ref[...].astype(BF16), cre_ref[s]) - _dot(xim_ref[...].astype(BF16), cim_ref[s]))
    y = y_ref[...] + dskip_ref[...] * u
    y = 0.5 * y * (1.0 + jnp.tanh(math.sqrt(2.0 / math.pi) * (y + 0.044715 * (y * y * y))))
    gate = _sigmoid(_dot(y.astype(BF16), wglu_ref[...]) + bglu_ref[...])
    o_ref[...] = (y * gate * _silu(z_ref[...])).astype(BF16)


def _s5(h, bsz, seq, pw_re, pw_im, bd, cd_re, cd_im, d_skip, w_glu, b_glu):
    tm = min(S5_TM, seq)
    nt = seq // tm
    full = lambda *shape: pl.BlockSpec(shape, lambda b, i: (0,) * len(shape))
    width = D_GROUPS * D_GROUP
    return pl.pallas_call(
        _s5_kernel,
        out_shape=jax.ShapeDtypeStruct((bsz * seq, width), BF16),
        grid=(bsz, nt),
        in_specs=[pl.BlockSpec((tm, width), lambda b, i: (b * nt + i, OD_DU // width)),
                  pl.BlockSpec((tm, width), lambda b, i: (b * nt + i, OD_DZ // width)),
                  full(8, D_GROUPS * D_STATE), full(8, D_GROUPS * D_STATE),
                  full(D_SETS, D_SET_CH, 2 * D_SET_ST),
                  full(D_SETS, D_SET_ST, D_SET_CH), full(D_SETS, D_SET_ST, D_SET_CH),
                  full(1, width), full(width, width), full(1, width)],
        out_specs=pl.BlockSpec((tm, width), lambda b, i: (b * nt + i, 0)),
        scratch_shapes=[pltpu.VMEM((1, D_GROUPS * D_STATE), F32), pltpu.VMEM((1, D_GROUPS * D_STATE), F32),
                        pltpu.VMEM((tm, D_SET_ST), F32), pltpu.VMEM((tm, D_SET_ST), F32),
                        pltpu.VMEM((tm, width), F32)],
        compiler_params=_cparams(("parallel", "arbitrary")),
        name="s5",
    )(h, h, pw_re, pw_im, bd, cd_re, cd_im, d_skip.reshape(1, -1), w_glu.astype(BF16),
      b_glu.reshape(1, -1))


def _pack_even_w(w_in):
    sizes = (1024, A_KV_RANK, A_IDX_HEADS * A_IDX_DIM, A_IDX_DIM, A_IDX_HEADS, 1024,
             3 * 1024, B_HEADS, B_HEADS, 1024)
    parts, start = [], 0
    for s in sizes:
        parts.append(w_in[:, start:start + s])
        start += s
    aq, ckv, qi, ki, wi, az, bqkv, ba, bb, bz = parts
    pad = jnp.zeros((w_in.shape[0], EV_WIDTH - start), w_in.dtype)
    return jnp.concatenate([aq, az, bqkv, bz, qi, ckv, ki, wi, ba, bb, pad], axis=1).astype(BF16)


def _even_layer(x2d, bsz, seq, bias, norm_g, w_in, a_q_norm, a_kv_norm, w_kv_up, a_k_norm,
                b_conv, b_a_log, b_dt_bias, b_out_norm, w_out):
    h = _norm_proj(x2d, norm_g, _pack_even_w(w_in), tn=768)
    qh, qih, wt, kcat, vt = _dsa_prep(h, bsz * seq // QB, a_q_norm, a_kv_norm, w_kv_up, a_k_norm)
    mix_a = _dsa_attend(h, bsz, seq, qh, qih, wt, kcat, vt, bias)
    u, w, qd, kd, attn, egl = _gdn_prep(h, bsz, seq, b_conv, b_a_log, b_dt_bias)
    mix_b = _gdn_scan(h, bsz, seq, u, w, qd, kd, attn, egl, b_out_norm)
    return _out_proj(x2d, mix_a, mix_b, w_out)


def _odd_layer(x2d, bsz, seq, cos2, sin2, norm_g, w_in, c_out_norm, lam_re, lam_im, log_dt,
               b_re, b_im, c_re, c_im, d_skip, w_glu, b_glu, w_out):
    h = _norm_proj(x2d, norm_g, w_in.astype(BF16), tn=1024)
    mix_c = _retention(h, bsz, seq, cos2, sin2, c_out_norm)
    s5p = _s5_params(lam_re, lam_im, log_dt, b_re, b_im, c_re, c_im)
    mix_d = _s5(h, bsz, seq, *s5p, d_skip, w_glu, b_glu)
    return _out_proj(x2d, mix_c, mix_d, w_out)


def kernel(x, rel_bias, ev_norm, ev_w_in, ev_a_q_norm, ev_a_kv_norm, ev_w_kv_up, ev_a_k_norm,
           ev_b_conv, ev_b_a_log, ev_b_dt_bias, ev_b_out_norm, ev_w_out,
           od_norm, od_w_in, od_c_out_norm, od_lam_re, od_lam_im, od_log_dt,
           od_b_re, od_b_im, od_c_re, od_c_im, od_d_skip, od_w_glu, od_b_glu, od_w_out):
    bsz, seq, d = x.shape
    depth = ev_norm.shape[0] + od_norm.shape[0]
    x2d = x.reshape(bsz * seq, d)
    bias = _bias_table(rel_bias)
    cos2, sin2 = _rope_tables(seq)
    for layer in range(depth):
        i = layer // 2
        if layer % 2 == 0:
            x2d = _even_layer(x2d, bsz, seq, bias, ev_norm[i], ev_w_in[i], ev_a_q_norm[i],
                              ev_a_kv_norm[i], ev_w_kv_up[i], ev_a_k_norm[i], ev_b_conv[i],
                              ev_b_a_log[i], ev_b_dt_bias[i], ev_b_out_norm[i], ev_w_out[i])
        else:
            x2d = _odd_layer(x2d, bsz, seq, cos2, sin2, od_norm[i], od_w_in[i], od_c_out_norm[i],
                             od_lam_re[i], od_lam_im[i], od_log_dt[i], od_b_re[i], od_b_im[i],
                             od_c_re[i], od_c_im[i], od_d_skip[i], od_w_glu[i], od_b_glu[i], od_w_out[i])
    return x2d.reshape(bsz, seq, d)
```

```python
import functools
import math

import numpy as np
import jax
import jax.numpy as jnp
from jax import lax
from jax.experimental import pallas as pl
from jax.experimental.pallas import tpu as pltpu

F32 = jnp.float32
BF16 = jnp.bfloat16
I32 = jnp.int32

D_MODEL = 1024
EPS = 1e-6
LOG2E = 1.4426950408889634
NEG_INF = float("-inf")
INT_MIN = -(2 ** 31)

A_HEADS, A_HEAD_DIM, A_KV_RANK = 16, 64, 128
A_IDX_HEADS, A_IDX_DIM, A_TOPK_MAX = 8, 64, 256
QB = 128
REL_BUCKETS, REL_MAX_DIST = 32, 128
VT_ROWS = 80
SB = 2 * QB
SCORE_ROWS = 4 * QB
BIAS_ROWS = 5 * QB
MASKED = -(2.0 ** 100)
M_INIT = -(2.0 ** 60)
B_HEADS, B_HEAD_DIM, CONV_K, B_CHUNK = 8, 128, 4, 64
C_HEADS, C_QK_DIM, C_V_DIM, C_CHUNK = 4, 128, 256, 128
ROPE_BASE = 10000.0
D_GROUP, D_STATE, D_GROUPS = 16, 64, 64
D_SETS, D_SET_CH, D_SET_ST = 4, 256, 1024

EV_AQ, EV_AZ, EV_BQ, EV_BK, EV_BV, EV_BZ, EV_QI, EV_CKV, EV_MISC = (
    0, 1024, 2048, 3072, 4096, 5120, 6144, 6656, 6784)
EV_WIDTH = 6912
MISC_KI, MISC_WI, MISC_BA, MISC_BB = 0, 64, 72, 80
OD_CQ, OD_CK, OD_CV, OD_CZ, OD_DU, OD_DZ = 0, 512, 1024, 2048, 3072, 4096
OD_WIDTH = 5120

VMEM_LIMIT = 48 * 1024 * 1024


def _cparams(sem):
    return pltpu.CompilerParams(dimension_semantics=sem, vmem_limit_bytes=VMEM_LIMIT)


def _dot(a, b):
    return jnp.dot(a, b, preferred_element_type=F32)


def _dot_nt(a, b):
    return lax.dot_general(a, b, (((1,), (1,)), ((), ())), preferred_element_type=F32)


def _split_bf16(x, n):
    parts = []
    for _ in range(n):
        p = x.astype(BF16)
        parts.append(p)
        x = x - p.astype(F32)
    return parts


def _dot_sel(sel, x):
    sel = sel.astype(BF16)
    hi, mid, lo = _split_bf16(x, 3)
    return _dot(sel, hi) + (_dot(sel, mid) + _dot(sel, lo))


def _dot_xsel(x, sel):
    sel = sel.astype(BF16)
    hi, mid, lo = _split_bf16(x, 3)
    return _dot(hi, sel) + (_dot(mid, sel) + _dot(lo, sel))


def _dot_x3(a, b):
    ah, al = _split_bf16(a, 2)
    bh, bl = _split_bf16(b, 2)
    return _dot(ah, bh) + (_dot(ah, bl) + _dot(al, bh))


def _sigmoid(x):
    return 1.0 / (1.0 + jnp.exp(-x))


def _silu(x):
    return x * _sigmoid(x)


def _norm_proj_kernel(x_ref, g_ref, w_ref, o_ref, xn_ref):
    @pl.when(pl.program_id(1) == 0)
    def _():
        x = x_ref[...]
        ms = jnp.mean(x * x, axis=-1, keepdims=True)
        xn_ref[...] = (x * lax.rsqrt(ms + EPS) * g_ref[...]).astype(BF16)

    o_ref[...] = _dot(xn_ref[...], w_ref[...])


def _norm_proj(x2d, gain, w_bf16, tn, tm=1024):
    t, d = x2d.shape
    n = w_bf16.shape[1]
    tm = min(tm, t)
    return pl.pallas_call(
        _norm_proj_kernel,
        out_shape=jax.ShapeDtypeStruct((t, n), F32),
        grid=(t // tm, n // tn),
        in_specs=[pl.BlockSpec((tm, d), lambda i, j: (i, 0)),
                  pl.BlockSpec((1, d), lambda i, j: (0, 0)),
                  pl.BlockSpec((d, tn), lambda i, j: (0, j))],
        out_specs=pl.BlockSpec((tm, tn), lambda i, j: (i, j)),
        scratch_shapes=[pltpu.VMEM((tm, d), BF16)],
        compiler_params=_cparams(("parallel", "arbitrary")),
        name="norm_proj",
    )(x2d, gain.reshape(1, d), w_bf16)


def _out_proj_kernel(x_ref, a_ref, b_ref, wa_ref, wb_ref, o_ref):
    o_ref[...] = x_ref[...] + _dot(a_ref[...], wa_ref[...]) + _dot(b_ref[...], wb_ref[...])


def _out_proj(x2d, mix_a, mix_b, w_out, tm=512):
    t, d = x2d.shape
    half = mix_a.shape[1]
    tm = min(tm, t)
    wa = w_out[:half].astype(BF16)
    wb = w_out[half:].astype(BF16)
    return pl.pallas_call(
        _out_proj_kernel,
        out_shape=jax.ShapeDtypeStruct((t, d), F32),
        grid=(t // tm,),
        in_specs=[pl.BlockSpec((tm, d), lambda i: (i, 0)),
                  pl.BlockSpec((tm, half), lambda i: (i, 0)),
                  pl.BlockSpec((tm, half), lambda i: (i, 0)),
                  pl.BlockSpec((half, d), lambda i: (0, 0)),
                  pl.BlockSpec((half, d), lambda i: (0, 0))],
        out_specs=pl.BlockSpec((tm, d), lambda i: (i, 0)),
        compiler_params=_cparams(("parallel",)),
        name="out_proj",
    )(x2d, mix_a, mix_b, wa, wb)


def _t5_bucket_starts():
    exact = REL_BUCKETS // 2
    n = np.arange(0, 4 * REL_MAX_DIST, dtype=np.int64)
    ratio = np.maximum(n, 1).astype(np.float32) / np.float32(exact)
    large = exact + (np.log(ratio).astype(np.float32) / np.float32(math.log(REL_MAX_DIST / exact))
                     * np.float32(REL_BUCKETS - exact)).astype(np.int32)
    bucket = np.where(n < exact, n, np.minimum(large, REL_BUCKETS - 1))
    starts = [int(np.argmax(bucket >= b)) for b in range(REL_BUCKETS)]
    assert all(bucket[s] == b for b, s in enumerate(starts)) and np.all(np.diff(bucket) >= 0)
    assert starts[-1] <= QB, "distances beyond one key block must share the last bucket"
    return starts


_BUCKET_STARTS = _t5_bucket_starts()


def _bias_table_kernel(rb_ref, o_ref):
    row = lax.broadcasted_iota(I32, (BIAS_ROWS, QB), 0)
    lane = lax.broadcasted_iota(I32, (BIAS_ROWS, QB), 1)
    dist = lane + 3 * QB - row
    for h in range(A_HEADS):
        val = jnp.full((BIAS_ROWS, QB), rb_ref[0, h], F32)
        for b in range(1, REL_BUCKETS):
            val = jnp.where(dist >= _BUCKET_STARTS[b], rb_ref[b, h], val)
        val = (val - rb_ref[REL_BUCKETS - 1, h]) * LOG2E
        o_ref[:, h * QB:(h + 1) * QB] = jnp.where(dist >= 0, val, 0.0)


def _bias_table(rel_bias):
    return pl.pallas_call(
        _bias_table_kernel,
        out_shape=jax.ShapeDtypeStruct((BIAS_ROWS, A_HEADS * QB), F32),
        in_specs=[pl.BlockSpec(memory_space=pltpu.SMEM)],
        out_specs=pl.BlockSpec(memory_space=pltpu.VMEM),
        name="dsa_bias_table",
    )(rel_bias)


def _dsa_prep_kernel(aq_ref, qi_ref, ckv_ref, misc_ref, gq_ref, gkv_ref, wkv_ref, gk_ref,
                     qta_ref, qit_ref, wt_ref, kidx_ref, katt_ref, vt_ref):
    zeros = jnp.zeros((QB, A_HEAD_DIM), F32)
    aq = aq_ref[...]
    gq = gq_ref[...]
    ident = jnp.where(lax.broadcasted_iota(I32, (QB, QB), 0) == lax.broadcasted_iota(I32, (QB, QB), 1),
                      1.0, 0.0).astype(BF16)
    for h in range(A_HEADS):
        q = aq[:, h * A_HEAD_DIM:(h + 1) * A_HEAD_DIM]
        ms = jnp.mean(q * q, axis=-1, keepdims=True)
        qn = q * lax.rsqrt(ms + EPS) * gq * (A_HEAD_DIM ** -0.5 * LOG2E)
        qta_ref[0:QB, h * QB:(h + 1) * QB] = jnp.concatenate([qn, zeros], axis=1).T.astype(BF16)
        qta_ref[QB:2 * QB, h * QB:(h + 1) * QB] = ident
    qi = qi_ref[...]
    for h in range(A_IDX_HEADS):
        qih = qi[:, h * A_IDX_DIM:(h + 1) * A_IDX_DIM]
        qit_ref[:, h * QB:(h + 1) * QB] = jnp.concatenate([qih, zeros], axis=1).T.astype(BF16)
    misc = misc_ref[...]
    wt_ref[...] = misc.T[MISC_WI:MISC_WI + A_IDX_HEADS, :] * (A_IDX_HEADS ** -0.5 * A_IDX_DIM ** -0.5)
    c = ckv_ref[...]
    cn = c * lax.rsqrt(jnp.mean(c * c, axis=-1, keepdims=True) + EPS) * gkv_ref[...]
    kv = _dot(cn.astype(BF16), wkv_ref[...])
    k = kv[:, :A_HEAD_DIM]
    kn = k * lax.rsqrt(jnp.mean(k * k, axis=-1, keepdims=True) + EPS) * gk_ref[...]
    kidx_ref[...] = jnp.concatenate([misc[:, MISC_KI:MISC_KI + A_IDX_DIM], zeros], axis=1).astype(BF16)
    katt_ref[...] = jnp.concatenate([kn, zeros], axis=1).astype(BF16)
    kvt = kv.T
    vt_ref[0:A_HEAD_DIM, :] = kvt[A_HEAD_DIM:, :].astype(BF16)
    ones_row = lax.broadcasted_iota(I32, (VT_ROWS - A_HEAD_DIM, QB), 0) == 0
    vt_ref[A_HEAD_DIM:, :] = jnp.where(ones_row, 1.0, 0.0).astype(BF16)


def _dsa_prep(h, nblk, a_q_norm, a_kv_norm, w_kv_up, a_k_norm):
    cb = lambda width, off: off // width
    return pl.pallas_call(
        _dsa_prep_kernel,
        out_shape=(jax.ShapeDtypeStruct((nblk, 2 * QB, A_HEADS * QB), BF16),
                   jax.ShapeDtypeStruct((nblk, QB, A_IDX_HEADS * QB), BF16),
                   jax.ShapeDtypeStruct((nblk, A_IDX_HEADS, QB), F32),
                   jax.ShapeDtypeStruct((nblk, QB, 128), BF16),
                   jax.ShapeDtypeStruct((nblk, QB, 128), BF16),
                   jax.ShapeDtypeStruct((nblk, VT_ROWS, QB), BF16)),
        grid=(nblk,),
        in_specs=[pl.BlockSpec((QB, 1024), lambda i: (i, cb(1024, EV_AQ))),
                  pl.BlockSpec((QB, 512), lambda i: (i, cb(512, EV_QI))),
                  pl.BlockSpec((QB, 128), lambda i: (i, cb(128, EV_CKV))),
                  pl.BlockSpec((QB, 128), lambda i: (i, cb(128, EV_MISC))),
                  pl.BlockSpec((1, A_HEAD_DIM), lambda i: (0, 0)),
                  pl.BlockSpec((1, A_KV_RANK), lambda i: (0, 0)),
                  pl.BlockSpec((A_KV_RANK, 2 * A_HEAD_DIM), lambda i: (0, 0)),
                  pl.BlockSpec((1, A_HEAD_DIM), lambda i: (0, 0))],
        out_specs=(pl.BlockSpec((None, 2 * QB, A_HEADS * QB), lambda i: (i, 0, 0)),
                   pl.BlockSpec((None, QB, A_IDX_HEADS * QB), lambda i: (i, 0, 0)),
                   pl.BlockSpec((None, A_IDX_HEADS, QB), lambda i: (i, 0, 0)),
                   pl.BlockSpec((None, QB, 128), lambda i: (i, 0, 0)),
                   pl.BlockSpec((None, QB, 128), lambda i: (i, 0, 0)),
                   pl.BlockSpec((None, VT_ROWS, QB), lambda i: (i, 0, 0))),
        compiler_params=_cparams(("parallel",)),
        name="dsa_prep",
    )(h, h, h, h, a_q_norm.reshape(1, -1), a_kv_norm.reshape(1, -1), w_kv_up.astype(BF16),
      a_k_norm.reshape(1, -1))


def _dsa_kernel(topk, qit_ref, wt_ref, qta_ref, kidx_ref, katt_ref, vt_ref, bias_ref, az_ref, o_ref,
                strip_ref, x_ref, m_ref, acc_ref, lg_ref, mx_ref, p_ref):
    qt = pl.program_id(1)
    t0 = qt * QB
    n_sc = qt // 4 + 1
    n_sb = qt // 2 + 1
    lane_s = lax.broadcasted_iota(I32, (SCORE_ROWS, QB), 1)
    row_s = lax.broadcasted_iota(I32, (SCORE_ROWS, QB), 0)

    w = wt_ref[...]

    def score_body(j, carry):
        kblk = kidx_ref[pl.ds(j * 4, 4)].reshape(SCORE_ROWS, 128)
        tot = None
        for hp in range(A_IDX_HEADS // 2):
            s = _dot(kblk, qit_ref[:, hp * SB:(hp + 1) * SB])
            for e in range(2):
                h = 2 * hp + e
                term = jnp.maximum(s[:, e * QB:(e + 1) * QB], 0.0) * w[h:h + 1, :]
                tot = term if tot is None else tot + term
        tot = jnp.where(j * SCORE_ROWS + row_s <= t0 + lane_s, tot, NEG_INF)
        bits = pltpu.bitcast(tot, I32)
        strip_ref[pl.ds(pl.multiple_of(j * SCORE_ROWS, SCORE_ROWS), SCORE_ROWS), :] = (
            bits ^ ((bits >> 31) & 0x7FFFFFFF))
        return carry

    lax.fori_loop(0, n_sc, score_body, 0)

    def count(pred):
        def body(j, acc):
            r0 = pl.multiple_of(j * SCORE_ROWS, SCORE_ROWS)
            m = jnp.where(pred(strip_ref[pl.ds(r0, SCORE_ROWS), :], r0), 1, 0)
            return acc + m.reshape(SCORE_ROWS // 32, 32, QB).sum(axis=0)
        acc = lax.fori_loop(0, n_sc, body, jnp.zeros((32, QB), I32))
        return acc.reshape(4, 8, QB).sum(axis=0).sum(axis=0, keepdims=True)

    c0 = count(lambda key, r0: key >= 0)
    nonneg = c0 >= topk
    thr0 = jnp.where(nonneg, 0, INT_MIN)
    cnt0 = jnp.where(nonneg, c0, n_sc * SCORE_ROWS)

    def bit_body(carry):
        i, thr, cnt = carry
        cand = thr | lax.shift_left(jnp.int32(1), 30 - i)
        c = count(lambda key, r0: key >= cand)
        ok = c >= topk
        return i + 1, jnp.where(ok, cand, thr), jnp.where(ok, c, cnt)

    _, thr, cnt = lax.while_loop(lambda c: (c[0] < 31) & (jnp.max(c[2]) > topk), bit_body,
                                 (jnp.int32(0), thr0, cnt0))

    x_ref[...] = jnp.full((1, QB), 2 ** 30, I32)

    @pl.when(jnp.max(cnt) > topk)
    def _():
        above = count(lambda key, r0: key > thr)
        need = topk - above

        def tie_body(i, xb):
            cand = xb | lax.shift_left(jnp.int32(1), 14 - i)
            c = count(lambda key, r0: (key == thr) & (r0 + row_s < cand))
            return jnp.where(c <= need, cand, xb)

        x_ref[...] = lax.fori_loop(0, 15, tie_body, jnp.zeros((1, QB), I32))

    xb = x_ref[...]

    m_ref[...] = jnp.full(m_ref.shape, M_INIT, F32)
    acc_ref[...] = jnp.zeros(acc_ref.shape, F32)
    n_lt = A_HEADS * QB // SB
    row_a = lax.broadcasted_iota(I32, (SB, QB), 0)
    lane_a = lax.broadcasted_iota(I32, (SB, QB), 1)

    def logits(j, slot, near):
        r0 = pl.multiple_of(j * SB, SB)
        key = strip_ref[pl.ds(r0, SB), :]
        idx = r0 + row_a
        sel = (key > thr) | ((key == thr) & (idx < xb))
        if near:
            sel = sel & (idx <= t0 + lane_a)
            b0 = pl.multiple_of((2 * j - qt + 3) * QB, QB)
        pen = jnp.where(sel, 0.0, MASKED).astype(BF16)
        kaug = jnp.concatenate([katt_ref[pl.ds(2 * j, 2)].reshape(SB, 128), pen], axis=1)
        for lt in range(n_lt):
            cols = slice(lt * SB, (lt + 1) * SB)
            lg = _dot(kaug, qta_ref[:, cols])
            if near:
                lg = lg + bias_ref[pl.ds(b0, SB), cols]
            lg = lg.astype(BF16)
            lg_ref[slot, :, cols] = lg
            mx_ref[slot, :, cols] = jnp.max(lg, axis=0, keepdims=True).astype(F32)

    def accumulate(j, slot):
        vt = jnp.concatenate([vt_ref[2 * j], vt_ref[2 * j + 1]], axis=1)
        m_old = m_ref[...]
        m_new = jnp.maximum(m_old, mx_ref[slot])
        m_ref[...] = m_new
        m_b = m_new.astype(BF16)
        for lt in range(n_lt):
            cols = slice(lt * SB, (lt + 1) * SB)
            p_ref[:, cols] = jnp.exp2(lg_ref[slot, :, cols] - m_b[:, cols])
        acc_ref[...] = jnp.exp2(m_old - m_new) * acc_ref[...] + _dot(vt, p_ref[...])

    n_far = jnp.maximum(n_sb - 2, 0)
    n_pair = jnp.maximum(n_far - 1, 0) // 2
    n_rem = n_far - 2 * n_pair

    @pl.when(n_far > 0)
    def _():
        logits(0, 0, False)

    def pair_body(jj, carry):
        j = 2 * jj
        logits(j + 1, 1, False)
        accumulate(j, 0)
        logits(j + 2, 0, False)
        accumulate(j + 1, 1)
        return carry

    lax.fori_loop(0, n_pair, pair_body, 0)

    @pl.when(n_rem == 1)
    def _():
        accumulate(n_far - 1, 0)

    @pl.when(n_rem == 2)
    def _():
        logits(n_far - 1, 1, False)
        accumulate(n_far - 2, 0)
        accumulate(n_far - 1, 1)

    @pl.when(n_sb >= 2)
    def _():
        logits(n_sb - 2, 0, True)
        accumulate(n_sb - 2, 0)

    logits(n_sb - 1, 0, True)
    accumulate(n_sb - 1, 0)

    acc = acc_ref[...]
    o_t = acc[0:A_HEAD_DIM, :] / acc[A_HEAD_DIM:A_HEAD_DIM + 1, :]
    pad = jnp.zeros((QB - A_HEAD_DIM, QB), F32)
    pieces = []
    for h in range(A_HEADS):
        blk = jnp.concatenate([o_t[:, h * QB:(h + 1) * QB], pad], axis=0)
        pieces.append(blk.T[:, 0:A_HEAD_DIM])
    att = jnp.concatenate(pieces, axis=1)
    o_ref[...] = (att * _silu(az_ref[...])).astype(BF16)


def _dsa_attend(h, bsz, seq, qta, qit, wt, kidx, katt, vt, bias):
    nq = seq // QB
    topk = min(A_TOPK_MAX, seq // 4)
    blk = lambda b, q: (b * nq + q, 0, 0)
    per_batch = lambda b, q: (b, 0, 0)
    once = pl.Buffered(1)
    return pl.pallas_call(
        functools.partial(_dsa_kernel, topk),
        out_shape=jax.ShapeDtypeStruct((bsz * seq, A_HEADS * A_HEAD_DIM), BF16),
        grid=(bsz, nq),
        in_specs=[pl.BlockSpec((None, QB, A_IDX_HEADS * QB), blk),
                  pl.BlockSpec((None, A_IDX_HEADS, QB), blk),
                  pl.BlockSpec((None, 2 * QB, A_HEADS * QB), blk),
                  pl.BlockSpec((nq, QB, 128), per_batch, pipeline_mode=once),
                  pl.BlockSpec((nq, QB, 128), per_batch, pipeline_mode=once),
                  pl.BlockSpec((nq, VT_ROWS, QB), per_batch, pipeline_mode=once),
                  pl.BlockSpec((BIAS_ROWS, A_HEADS * QB), lambda b, q: (0, 0), pipeline_mode=once),
                  pl.BlockSpec((QB, 1024), lambda b, q: (b * nq + q, EV_AZ // 1024))],
        out_specs=pl.BlockSpec((QB, 1024), lambda b, q: (b * nq + q, 0)),
        scratch_shapes=[pltpu.VMEM((seq, QB), I32),
                        pltpu.VMEM((1, QB), I32),
                        pltpu.VMEM((1, A_HEADS * QB), F32),
                        pltpu.VMEM((VT_ROWS, A_HEADS * QB), F32),
                        pltpu.VMEM((2, SB, A_HEADS * QB), BF16),
                        pltpu.VMEM((2, 1, A_HEADS * QB), F32),
                        pltpu.VMEM((SB, A_HEADS * QB), BF16)],
        compiler_params=_cparams(("arbitrary", "arbitrary")),
        name="dsa_attend",
    )(qit, wt, qta, kidx, katt, vt, bias, h)


GDN_TM = 512
GDN_GROUP = 4


def _gdn_prep_kernel(alog_ref, dtb_ref, q_ref, k_ref, v_ref, qh_ref, kh_ref, vh_ref, misc_ref,
                     cq_ref, ck_ref, cv_ref,
                     u_ref, w_ref, qd_ref, kdt_ref, attn_ref, egl_ref):
    i = pl.program_id(1)
    hd = pl.program_id(2)
    tm = q_ref.shape[0]
    nchunk = tm // B_CHUNK

    def conv_silu(x_ref, halo_ref, c_ref):
        halo = jnp.where(i > 0, halo_ref[...], 0.0)
        ext = jnp.concatenate([halo, x_ref[...]], axis=0)
        cw = c_ref[...]
        y = ext[8:, :] * cw[CONV_K - 1:CONV_K, :]
        for d in range(1, CONV_K):
            y = y + pltpu.roll(ext, d, 0)[8:, :] * cw[CONV_K - 1 - d:CONV_K - d, :]
        return _silu(y)

    q = conv_silu(q_ref, qh_ref, cq_ref)
    k = conv_silu(k_ref, kh_ref, ck_ref)
    v = conv_silu(v_ref, vh_ref, cv_ref)
    qn = q * lax.rsqrt(jnp.sum(q * q, axis=-1, keepdims=True) + EPS) * (B_HEAD_DIM ** -0.5)
    kn = k * lax.rsqrt(jnp.sum(k * k, axis=-1, keepdims=True) + EPS)

    misc = misc_ref[...]
    sel_r = lax.broadcasted_iota(I32, (128, 256), 0)
    sel_c = lax.broadcasted_iota(I32, (128, 256), 1)
    pick = jnp.where(sel_r == jnp.where(sel_c < 128, MISC_BA + hd, MISC_BB + hd), 1.0, 0.0)
    bab = _dot_xsel(misc, pick)
    ba = bab[:, 0:128]
    bb = bab[:, 128:256]
    beta = _sigmoid(bb)
    xg = ba + dtb_ref[hd]
    softplus = jnp.maximum(xg, 0.0) + jnp.log(1.0 + jnp.exp(-jnp.abs(xg)))
    g = -jnp.exp(jnp.full((1, 128), alog_ref[hd], F32)) * softplus

    sl = 256
    r2 = lax.broadcasted_iota(I32, (sl, sl), 0)
    c2 = lax.broadcasted_iota(I32, (sl, sl), 1)
    same = (r2 // B_CHUNK) == (c2 // B_CHUNK)
    tri = jnp.where(same & (c2 <= r2), 1.0, 0.0)
    blk = jnp.where(same, 1.0, 0.0)
    nsl = tm // sl
    both = _dot_sel(jnp.concatenate([tri, blk], axis=0),
                    jnp.concatenate([g[s * sl:(s + 1) * sl, :] for s in range(nsl)], axis=1))
    gc = jnp.concatenate([both[0:sl, s * 128:(s + 1) * 128] for s in range(nsl)], axis=0)
    gl = jnp.concatenate([both[sl:2 * sl, s * 128:(s + 1) * 128] for s in range(nsl)], axis=0)

    kb = kn * beta
    vb = v * beta
    egc = jnp.exp(gc)
    qd_ref[...] = (qn * egc).astype(BF16)
    kd = kn * jnp.exp(gl - gc)
    for m in range(tm // 128):
        kdt_ref[:, m * 128:(m + 1) * 128] = kd[m * 128:(m + 1) * 128, :].T.astype(BF16)
    kbg = kb * egc

    gw = GDN_GROUP * B_CHUNK
    ri = lax.broadcasted_iota(I32, (B_CHUNK, gw), 0)
    lj = lax.broadcasted_iota(I32, (B_CHUNK, gw), 1)
    lb = lj // B_CHUNK
    lj = lj % B_CHUNK
    bmask = (lax.broadcasted_iota(I32, (gw, gw), 0) // B_CHUNK
             == lax.broadcasted_iota(I32, (gw, gw), 1) // B_CHUNK)

    def fold(x):
        xm = jnp.where(bmask, x, 0.0)
        out = xm[0:B_CHUNK, :]
        for c in range(1, GDN_GROUP):
            out = out + xm[c * B_CHUNK:(c + 1) * B_CHUNK, :]
        return out

    def bdiag(cat):
        return jnp.where(bmask, jnp.concatenate([cat] * GDN_GROUP, axis=0), 0.0)

    for grp in range(tm // gw):
        rs = slice(grp * gw, (grp + 1) * gw)
        gcs = gc[rs, :]
        col = jnp.concatenate([gcs[0:B_CHUNK, :]] * 2, axis=1)
        for c in range(1, GDN_GROUP):
            col = jnp.where(lb == c, jnp.concatenate([gcs[c * B_CHUNK:(c + 1) * B_CHUNK, :]] * 2, axis=1), col)
        rowf = jnp.concatenate([gcs[m * 128:(m + 1) * 128, :].T[0:B_CHUNK, :] for m in range(gw // 128)],
                               axis=1)
        decay = jnp.exp(jnp.where(ri >= lj, col - rowf, NEG_INF))
        kg = kn[rs, :]
        a_cat = jnp.where(ri > lj, fold(_dot_nt(kb[rs, :], kg)) * decay, 0.0)
        attn_ref[grp] = jnp.where(ri >= lj, fold(_dot_nt(qn[rs, :], kg)) * decay, 0.0).astype(BF16)
        inv = jnp.where(ri == lj, 1.0, 0.0) - a_cat
        pw = _dot_x3(a_cat, bdiag(a_cat))
        for step in range(5):
            if step < 4:
                both = _dot_x3(jnp.concatenate([inv, pw], axis=0), bdiag(pw))
                inv = inv + both[0:B_CHUNK, :]
                pw = both[B_CHUNK:, :]
            else:
                inv = inv + _dot_x3(inv, bdiag(pw))
        sol = _dot_x3(bdiag(inv), jnp.concatenate([vb[rs, :], kbg[rs, :]], axis=1))
        u_ref[rs, :] = sol[:, 0:B_HEAD_DIM]
        w_ref[rs, :] = sol[:, B_HEAD_DIM:].astype(BF16)
    for c in range(nchunk):
        egl_ref[c:c + 1, :] = jnp.exp(gl[c * B_CHUNK:c * B_CHUNK + 1, :])
    if nchunk < 8:
        egl_ref[nchunk:, :] = jnp.zeros((8 - nchunk, 128), F32)


def _gdn_prep(h, bsz, seq, b_conv, b_a_log, b_dt_bias):
    tm = min(GDN_TM, seq)
    nt = seq // tm
    nrow = max(tm // B_CHUNK, 8)
    gw = GDN_GROUP * B_CHUNK
    hs = (bsz, B_HEADS, seq, B_HEAD_DIM)
    cur = lambda off: (lambda b, i, hd: (b * nt + i, off // 128 + hd))
    halo = lambda off: (lambda b, i, hd: (jnp.maximum((b * nt + i) * (tm // 8) - 1, 0), off // 128 + hd))
    cw = lambda off: (lambda b, i, hd: (0, off // 128 + hd))
    out = lambda b, i, hd: (b, hd, i, 0)
    smem = pl.BlockSpec(memory_space=pltpu.SMEM)
    return pl.pallas_call(
        _gdn_prep_kernel,
        out_shape=(jax.ShapeDtypeStruct(hs, F32), jax.ShapeDtypeStruct(hs, BF16),
                   jax.ShapeDtypeStruct(hs, BF16),
                   jax.ShapeDtypeStruct((bsz, B_HEADS, B_HEAD_DIM, seq), BF16),
                   jax.ShapeDtypeStruct((bsz, B_HEADS, seq // gw, B_CHUNK, gw), BF16),
                   jax.ShapeDtypeStruct((bsz, B_HEADS, nt * nrow, 128), F32)),
        grid=(bsz, nt, B_HEADS),
        in_specs=[smem, smem,
                  pl.BlockSpec((tm, 128), cur(EV_BQ)), pl.BlockSpec((tm, 128), cur(EV_BK)),
                  pl.BlockSpec((tm, 128), cur(EV_BV)),
                  pl.BlockSpec((8, 128), halo(EV_BQ)), pl.BlockSpec((8, 128), halo(EV_BK)),
                  pl.BlockSpec((8, 128), halo(EV_BV)),
                  pl.BlockSpec((tm, 128), lambda b, i, hd: (b * nt + i, EV_MISC // 128)),
                  pl.BlockSpec((CONV_K, 128), cw(0)), pl.BlockSpec((CONV_K, 128), cw(1024)),
                  pl.BlockSpec((CONV_K, 128), cw(2048))],
        out_specs=(pl.BlockSpec((None, None, tm, 128), out), pl.BlockSpec((None, None, tm, 128), out),
                   pl.BlockSpec((None, None, tm, 128), out),
                   pl.BlockSpec((None, None, B_HEAD_DIM, tm), lambda b, i, hd: (b, hd, 0, i)),
                   pl.BlockSpec((None, None, tm // gw, B_CHUNK, gw), lambda b, i, hd: (b, hd, i, 0, 0)),
                   pl.BlockSpec((None, None, nrow, 128), out)),
        compiler_params=_cparams(("parallel", "parallel", "parallel")),
        name="gdn_prep",
    )(b_a_log, b_dt_bias, h, h, h, h, h, h, h, b_conv, b_conv, b_conv)


def _gdn_scan_kernel(u_ref, w_ref, qd_ref, kdt_ref, attn_ref, egl_ref, z_ref, gn_ref, o_ref, s_ref):
    @pl.when(pl.program_id(2) == 0)
    def _():
        s_ref[...] = jnp.zeros(s_ref.shape, F32)

    tm = u_ref.shape[0]
    gw = GDN_GROUP * B_CHUNK
    state = s_ref[...]
    gain = gn_ref[...]
    zero = jnp.zeros((B_CHUNK, B_HEAD_DIM), BF16)
    for c in range(tm // B_CHUNK):
        grp, ci = divmod(c, GDN_GROUP)
        rs = slice(c * B_CHUNK, (c + 1) * B_CHUNK)
        sb = state.astype(BF16)
        v_new = u_ref[rs, :] - _dot(w_ref[rs, :], sb)
        vpad = jnp.concatenate([zero] * ci + [v_new.astype(BF16)] + [zero] * (GDN_GROUP - 1 - ci), axis=0)
        o = _dot(qd_ref[rs, :], sb) + _dot(attn_ref[grp], vpad)
        state = state * egl_ref[c:c + 1, :] + _dot(kdt_ref[:, grp * gw:(grp + 1) * gw], vpad)
        on = o * lax.rsqrt(jnp.mean(o * o, axis=-1, keepdims=True) + EPS) * gain
        o_ref[rs, :] = (on * _silu(z_ref[rs, :])).astype(BF16)
    s_ref[...] = state


def _gdn_scan(h, bsz, seq, u, w, qd, kd, attn, egl, b_out_norm):
    tm = min(GDN_TM, seq)
    nt = seq // tm
    nrow = max(tm // B_CHUNK, 8)
    gw = GDN_GROUP * B_CHUNK
    blk = lambda b, hd, i: (b, hd, i, 0)
    return pl.pallas_call(
        _gdn_scan_kernel,
        out_shape=jax.ShapeDtypeStruct((bsz * seq, B_HEADS * B_HEAD_DIM), BF16),
        grid=(bsz, B_HEADS, nt),
        in_specs=[pl.BlockSpec((None, None, tm, 128), blk), pl.BlockSpec((None, None, tm, 128), blk),
                  pl.BlockSpec((None, None, tm, 128), blk),
                  pl.BlockSpec((None, None, B_HEAD_DIM, tm), lambda b, hd, i: (b, hd, 0, i)),
                  pl.BlockSpec((None, None, tm // gw, B_CHUNK, gw), lambda b, hd, i: (b, hd, i, 0, 0)),
                  pl.BlockSpec((None, None, nrow, 128), blk),
                  pl.BlockSpec((tm, 128), lambda b, hd, i: (b * nt + i, EV_BZ // 128 + hd)),
                  pl.BlockSpec((1, 128), lambda b, hd, i: (0, 0))],
        out_specs=pl.BlockSpec((tm, 128), lambda b, hd, i: (b * nt + i, hd)),
        scratch_shapes=[pltpu.VMEM((B_HEAD_DIM, B_HEAD_DIM), F32)],
        compiler_params=_cparams(("parallel", "parallel", "arbitrary")),
        name="gdn_scan",
    )(u, w, qd, kd, attn, egl, h, b_out_norm.reshape(1, -1))


def _rope_kernel(inv_ref, cos_ref, sin_ref):
    tm = cos_ref.shape[0]
    pos = (pl.program_id(0) * tm + lax.broadcasted_iota(I32, (tm, 128), 0)).astype(F32)
    ang = pos * inv_ref[...]
    lane = lax.broadcasted_iota(I32, (tm, 128), 1)
    cos_ref[...] = jnp.cos(ang)
    sin_ref[...] = jnp.where(lane < 64, -jnp.sin(ang), jnp.sin(ang))


def _rope_tables(seq):
    half = C_QK_DIM // 2
    inv = 1.0 / (ROPE_BASE ** jnp.linspace(0.0, 1.0, half, dtype=F32))
    inv2 = jnp.concatenate([inv, inv]).reshape(1, C_QK_DIM)
    tm = min(1024, seq)
    return pl.pallas_call(
        _rope_kernel,
        out_shape=(jax.ShapeDtypeStruct((seq, C_QK_DIM), F32),) * 2,
        grid=(seq // tm,),
        in_specs=[pl.BlockSpec((1, C_QK_DIM), lambda i: (0, 0))],
        out_specs=(pl.BlockSpec((tm, C_QK_DIM), lambda i: (i, 0)),) * 2,
        compiler_params=_cparams(("parallel",)),
        name="rope_tables",
    )(inv2)


RET_TM = 512


def _retention_kernel(lg_ref, q_ref, k_ref, v_ref, z_ref, cos_ref, sin_ref, gn_ref, o_ref,
                      r_ref, dm_ref, zeta_ref, xi_ref):
    hd = pl.program_id(1)
    lg = lg_ref[0, hd]

    @pl.when(pl.program_id(2) == 0)
    def _():
        r_ref[...] = jnp.zeros(r_ref.shape, F32)
        ri = lax.broadcasted_iota(I32, (C_CHUNK, C_CHUNK), 0)
        ci = lax.broadcasted_iota(I32, (C_CHUNK, C_CHUNK), 1)
        diff = (ri - ci).astype(F32)
        dm_ref[...] = jnp.where(diff >= 0, jnp.exp(jnp.maximum(diff, 0.0) * lg), 0.0)
        zeta_ref[...] = jnp.exp((C_CHUNK - 1 - ri).astype(F32) * lg)
        rv = lax.broadcasted_iota(I32, (C_CHUNK, C_V_DIM), 0).astype(F32)
        xi_ref[...] = jnp.exp((rv + 1.0) * lg)

    g_chunk = lg_ref[1, hd]
    tm = q_ref.shape[0]
    state = r_ref[...]
    dmask = dm_ref[...]
    gain = gn_ref[...]
    for c in range(tm // C_CHUNK):
        rs = slice(c * C_CHUNK, (c + 1) * C_CHUNK)
        cos = cos_ref[rs, :]
        sin = sin_ref[rs, :]
        q = q_ref[rs, :]
        k = k_ref[rs, :]
        qr = q * cos + pltpu.roll(q, C_QK_DIM // 2, 1) * sin
        kr = (k * cos + pltpu.roll(k, C_QK_DIM // 2, 1) * sin) * (C_QK_DIM ** -0.5)
        vb = v_ref[rs, :].astype(BF16)
        qb = qr.astype(BF16)
        s = _dot_nt(qb, kr.astype(BF16)) * dmask
        o = _dot(s.astype(BF16), vb) + _dot(qb, state.astype(BF16)) * xi_ref[...]
        state = state * g_chunk + _dot((kr * zeta_ref[...]).T.astype(BF16), vb)
        mu = jnp.mean(o, axis=-1, keepdims=True)
        oc = o - mu
        var = jnp.mean(oc * oc, axis=-1, keepdims=True)
        y = oc * lax.rsqrt(var + EPS) * gain
        o_ref[rs, :] = (y * _silu(z_ref[rs, :])).astype(BF16)
    r_ref[...] = state


def _retention(h, bsz, seq, cos2, sin2, c_out_norm):
    tm = min(RET_TM, seq)
    nt = seq // tm
    lg = np.log1p(-np.exp2(-5.0 - np.arange(C_HEADS, dtype=np.float32))).astype(np.float32)
    lg = np.stack([lg, np.exp(np.float32(C_CHUNK) * lg).astype(np.float32)])
    return pl.pallas_call(
        _retention_kernel,
        out_shape=jax.ShapeDtypeStruct((bsz * seq, C_HEADS * C_V_DIM), BF16),
        grid=(bsz, C_HEADS, nt),
        in_specs=[pl.BlockSpec(memory_space=pltpu.SMEM),
                  pl.BlockSpec((tm, 128), lambda b, hd, i: (b * nt + i, OD_CQ // 128 + hd)),
                  pl.BlockSpec((tm, 128), lambda b, hd, i: (b * nt + i, OD_CK // 128 + hd)),
                  pl.BlockSpec((tm, 256), lambda b, hd, i: (b * nt + i, OD_CV // 256 + hd)),
                  pl.BlockSpec((tm, 256), lambda b, hd, i: (b * nt + i, OD_CZ // 256 + hd)),
                  pl.BlockSpec((tm, 128), lambda b, hd, i: (i, 0)),
                  pl.BlockSpec((tm, 128), lambda b, hd, i: (i, 0)),
                  pl.BlockSpec((1, C_V_DIM), lambda b, hd, i: (0, hd))],
        out_specs=pl.BlockSpec((tm, C_V_DIM), lambda b, hd, i: (b * nt + i, hd)),
        scratch_shapes=[pltpu.VMEM((C_QK_DIM, C_V_DIM), F32),
                        pltpu.VMEM((C_CHUNK, C_CHUNK), F32),
                        pltpu.VMEM((C_CHUNK, C_QK_DIM), F32),
                        pltpu.VMEM((C_CHUNK, C_V_DIM), F32)],
        compiler_params=_cparams(("parallel", "parallel", "arbitrary")),
        name="retention",
    )(jnp.asarray(lg), h, h, h, h, cos2, sin2, c_out_norm.reshape(1, -1))


def _s5_param_kernel(lre_ref, lim_ref, ldt_ref, lrex_ref, limx_ref, ldtx_ref, bre_ref, bim_ref,
                     pre_ref, pim_ref, bbre_ref, bbim_ref):
    def disc(lre, lim, ldt):
        lr = jnp.minimum(lre, -1e-4)
        dt = jnp.exp(ldt)
        mag = jnp.exp(lr * dt)
        return lr, lim, mag * jnp.cos(lim * dt), mag * jnp.sin(lim * dt)

    _, _, a_re, a_im = disc(lre_ref[...], lim_ref[...], ldt_ref[...])
    p_re, p_im = a_re, a_im
    pre_ref[0] = p_re
    pim_ref[0] = p_im
    for r in range(1, 8):
        p_re, p_im = p_re * a_re - p_im * a_im, p_re * a_im + p_im * a_re
        pre_ref[r] = p_re
        pim_ref[r] = p_im
    lr, li, ax_re, ax_im = disc(lrex_ref[...], limx_ref[...], ldtx_ref[...])
    den = lr * lr + li * li
    f_re = ((ax_re - 1.0) * lr + ax_im * li) / den
    f_im = (ax_im * lr - (ax_re - 1.0) * li) / den
    bbre_ref[...] = f_re * bre_ref[...] - f_im * bim_ref[...]
    bbim_ref[...] = f_re * bim_ref[...] + f_im * bre_ref[...]


def _s5_params(lam_re, lam_im, log_dt, b_re, b_im, c_re, c_im):
    g, p, ch = D_GROUPS, D_STATE, D_GROUP
    ldt = jnp.broadcast_to(log_dt[:, None], (g, p))
    rep = lambda a: jnp.repeat(a, ch, axis=1)
    vm = pl.BlockSpec(memory_space=pltpu.VMEM)
    pw_re, pw_im, bb_re, bb_im = pl.pallas_call(
        _s5_param_kernel,
        out_shape=(jax.ShapeDtypeStruct((8, g, p), F32), jax.ShapeDtypeStruct((8, g, p), F32),
                   jax.ShapeDtypeStruct((g, p * ch), F32), jax.ShapeDtypeStruct((g, p * ch), F32)),
        in_specs=[vm] * 8, out_specs=(vm,) * 4,
        name="s5_params",
    )(lam_re, lam_im, ldt, rep(lam_re), rep(lam_im), rep(ldt),
      b_re.reshape(g, p * ch), b_im.reshape(g, p * ch))
    eye = jnp.eye(D_SETS * 4, dtype=F32)

    def pack_b(bb):
        bb = bb.reshape(D_SETS, 16, p, ch)
        return jnp.einsum('sgpi,gh->sgihp', bb, eye).reshape(D_SETS, D_SET_CH, D_SET_ST)

    def pack_c(c):
        c = c.reshape(D_SETS, 16, ch, p)
        return jnp.einsum('sgjp,gh->sgphj', c, eye).reshape(D_SETS, D_SET_ST, D_SET_CH)

    bd = jnp.concatenate([pack_b(bb_re), pack_b(bb_im)], axis=-1).astype(BF16)
    return (pw_re.reshape(8, g * p), pw_im.reshape(8, g * p), bd,
            pack_c(c_re).astype(BF16), pack_c(c_im).astype(BF16))


S5_TM = 256


def _s5_kernel(u_ref, z_ref, pre_ref, pim_ref, bd_ref, cre_ref, cim_ref, dskip_ref, wglu_ref, bglu_ref,
               o_ref, hre_ref, him_ref, xre_ref, xim_ref, y_ref):
    @pl.when(pl.program_id(1) == 0)
    def _():
        hre_ref[...] = jnp.zeros(hre_ref.shape, F32)
        him_ref[...] = jnp.zeros(him_ref.shape, F32)

    tm = u_ref.shape[0]
    u = u_ref[...]
    ub = u.astype(BF16)
    sub = lax.broadcasted_iota(I32, (tm, D_SET_ST), 0) & 7
    for s in range(D_SETS):
        st = slice(s * D_SET_ST, (s + 1) * D_SET_ST)
        bu = _dot(ub[:, s * D_SET_CH:(s + 1) * D_SET_CH], bd_ref[s])
        xr = bu[:, :D_SET_ST]
        xi = bu[:, D_SET_ST:]
        for sh, r in ((1, 0), (2, 1), (4, 3)):
            ar = pre_ref[r:r + 1, st]
            ai = pim_ref[r:r + 1, st]
            keep = sub >= sh
            sr = jnp.where(keep, pltpu.roll(xr, sh, 0), 0.0)
            si = jnp.where(keep, pltpu.roll(xi, sh, 0), 0.0)
            xr, xi = xr + ar * sr - ai * si, xi + ar * si + ai * sr
        xre_ref[...] = xr
        xim_ref[...] = xi
        a8r = pre_ref[:, st]
        a8i = pim_ref[:, st]

        def carry_body(i, carry):
            cr, ci = carry
            rows = pl.ds(pl.multiple_of(i * 8, 8), 8)
            yr = xre_ref[rows, :] + a8r * cr - a8i * ci
            yi = xim_ref[rows, :] + a8r * ci + a8i * cr
            xre_ref[rows, :] = yr
            xim_ref[rows, :] = yi
            return yr[7:8, :], yi[7:8, :]

        cr, ci = lax.fori_loop(0, tm // 8, carry_body, (hre_ref[:, st], him_ref[:, st]))
        hre_ref[:, st] = cr
        him_ref[:, st] = ci
        y_ref[:, s * D_SET_CH:(s + 1) * D_SET_CH] = (
            _dot(xre_ref[...].astype(BF16), cre_ref[s]) - _dot(xim_ref[...].astype(BF16), cim_ref[s]))
    y = y_ref[...] + dskip_ref[...] * u
    y = 0.5 * y * (1.0 + jnp.tanh(math.sqrt(2.0 / math.pi) * (y + 0.044715 * (y * y * y))))
    gate = _sigmoid(_dot(y.astype(BF16), wglu_ref[...]) + bglu_ref[...])
    o_ref[...] = (y * gate * _silu(z_ref[...])).astype(BF16)


def _s5(h, bsz, seq, pw_re, pw_im, bd, cd_re, cd_im, d_skip, w_glu, b_glu):
    tm = min(S5_TM, seq)
    nt = seq // tm
    full = lambda *shape: pl.BlockSpec(shape, lambda b, i: (0,) * len(shape))
    width = D_GROUPS * D_GROUP
    return pl.pallas_call(
        _s5_kernel,
        out_shape=jax.ShapeDtypeStruct((bsz * seq, width), BF16),
        grid=(bsz, nt),
        in_specs=[pl.BlockSpec((tm, width), lambda b, i: (b * nt + i, OD_DU // width)),
                  pl.BlockSpec((tm, width), lambda b, i: (b * nt + i, OD_DZ // width)),
                  full(8, D_GROUPS * D_STATE), full(8, D_GROUPS * D_STATE),
                  full(D_SETS, D_SET_CH, 2 * D_SET_ST),
                  full(D_SETS, D_SET_ST, D_SET_CH), full(D_SETS, D_SET_ST, D_SET_CH),
                  full(1, width), full(width, width), full(1, width)],
        out_specs=pl.BlockSpec((tm, width), lambda b, i: (b * nt + i, 0)),
        scratch_shapes=[pltpu.VMEM((1, D_GROUPS * D_STATE), F32), pltpu.VMEM((1, D_GROUPS * D_STATE), F32),
                        pltpu.VMEM((tm, D_SET_ST), F32), pltpu.VMEM((tm, D_SET_ST), F32),
                        pltpu.VMEM((tm, width), F32)],
        compiler_params=_cparams(("parallel", "arbitrary")),
        name="s5",
    )(h, h, pw_re, pw_im, bd, cd_re, cd_im, d_skip.reshape(1, -1), w_glu.astype(BF16),
      b_glu.reshape(1, -1))


def _pack_even_w(w_in):
    sizes = (1024, A_KV_RANK, A_IDX_HEADS * A_IDX_DIM, A_IDX_DIM, A_IDX_HEADS, 1024,
             3 * 1024, B_HEADS, B_HEADS, 1024)
    parts, start = [], 0
    for s in sizes:
        parts.append(w_in[:, start:start + s])
        start += s
    aq, ckv, qi, ki, wi, az, bqkv, ba, bb, bz = parts
    pad = jnp.zeros((w_in.shape[0], EV_WIDTH - start), w_in.dtype)
    return jnp.concatenate([aq, az, bqkv, bz, qi, ckv, ki, wi, ba, bb, pad], axis=1).astype(BF16)


def _even_layer(x2d, bsz, seq, bias, norm_g, w_in, a_q_norm, a_kv_norm, w_kv_up, a_k_norm,
                b_conv, b_a_log, b_dt_bias, b_out_norm, w_out):
    h = _norm_proj(x2d, norm_g, _pack_even_w(w_in), tn=768)
    qta, qit, wt, kidx, katt, vt = _dsa_prep(h, bsz * seq // QB, a_q_norm, a_kv_norm, w_kv_up, a_k_norm)
    mix_a = _dsa_attend(h, bsz, seq, qta, qit, wt, kidx, katt, vt, bias)
    u, w, qd, kd, attn, egl = _gdn_prep(h, bsz, seq, b_conv, b_a_log, b_dt_bias)
    mix_b = _gdn_scan(h, bsz, seq, u, w, qd, kd, attn, egl, b_out_norm)
    return _out_proj(x2d, mix_a, mix_b, w_out)


def _odd_layer(x2d, bsz, seq, cos2, sin2, norm_g, w_in, c_out_norm, lam_re, lam_im, log_dt,
               b_re, b_im, c_re, c_im, d_skip, w_glu, b_glu, w_out):
    h = _norm_proj(x2d, norm_g, w_in.astype(BF16), tn=1024)
    mix_c = _retention(h, bsz, seq, cos2, sin2, c_out_norm)
    s5p = _s5_params(lam_re, lam_im, log_dt, b_re, b_im, c_re, c_im)
    mix_d = _s5(h, bsz, seq, *s5p, d_skip, w_glu, b_glu)
    return _out_proj(x2d, mix_c, mix_d, w_out)


def kernel(x, rel_bias, ev_norm, ev_w_in, ev_a_q_norm, ev_a_kv_norm, ev_w_kv_up, ev_a_k_norm,
           ev_b_conv, ev_b_a_log, ev_b_dt_bias, ev_b_out_norm, ev_w_out,
           od_norm, od_w_in, od_c_out_norm, od_lam_re, od_lam_im, od_log_dt,
           od_b_re, od_b_im, od_c_re, od_c_im, od_d_skip, od_w_glu, od_b_glu, od_w_out):
    bsz, seq, d = x.shape
    depth = ev_norm.shape[0] + od_norm.shape[0]
    x2d = x.reshape(bsz * seq, d)
    bias = _bias_table(rel_bias)
    cos2, sin2 = _rope_tables(seq)
    for layer in range(depth):
        i = layer // 2
        if layer % 2 == 0:
            x2d = _even_layer(x2d, bsz, seq, bias, ev_norm[i], ev_w_in[i], ev_a_q_norm[i],
                              ev_a_kv_norm[i], ev_w_kv_up[i], ev_a_k_norm[i], ev_b_conv[i],
                              ev_b_a_log[i], ev_b_dt_bias[i], ev_b_out_norm[i], ev_w_out[i])
        else:
            x2d = _odd_layer(x2d, bsz, seq, cos2, sin2, od_norm[i], od_w_in[i], od_c_out_norm[i],
                             od_lam_re[i], od_lam_im[i], od_log_dt[i], od_b_re[i], od_b_im[i],
                             od_c_re[i], od_c_im[i], od_d_skip[i], od_w_glu[i], od_b_glu[i], od_w_out[i])
    return x2d.reshape(bsz, seq, d)
```

```python
import functools
import math

import numpy as np
import jax
import jax.numpy as jnp
from jax import lax
from jax.experimental import pallas as pl
from jax.experimental.pallas import tpu as pltpu

F32 = jnp.float32
BF16 = jnp.bfloat16
I32 = jnp.int32

D_MODEL = 1024
EPS = 1e-6
LOG2E = 1.4426950408889634
NEG_INF = float("-inf")
INT_MIN = -(2 ** 31)
HI16 = -(2 ** 16)

A_HEADS, A_HEAD_DIM, A_KV_RANK = 16, 64, 128
A_IDX_HEADS, A_IDX_DIM, A_TOPK_MAX = 8, 64, 256
QB = 128
REL_BUCKETS, REL_MAX_DIST = 32, 128
VT_ROWS = 80
SB = 2 * QB
SCORE_ROWS = 4 * QB
BIAS_ROWS = 5 * QB
MASKED = -(2.0 ** 100)
M_INIT = -(2.0 ** 60)
B_HEADS, B_HEAD_DIM, CONV_K, B_CHUNK = 8, 128, 4, 64
C_HEADS, C_QK_DIM, C_V_DIM, C_CHUNK = 4, 128, 256, 128
ROPE_BASE = 10000.0
D_GROUP, D_STATE, D_GROUPS = 16, 64, 64
D_SETS, D_SET_CH, D_SET_ST = 4, 256, 1024

EV_AQ, EV_AZ, EV_BQ, EV_BK, EV_BV, EV_BZ, EV_QI, EV_CKV, EV_MISC = (
    0, 1024, 2048, 3072, 4096, 5120, 6144, 6656, 6784)
EV_WIDTH = 6912
MISC_KI, MISC_WI, MISC_BA, MISC_BB = 0, 64, 72, 80
OD_CQ, OD_CK, OD_CV, OD_CZ, OD_DU, OD_DZ = 0, 512, 1024, 2048, 3072, 4096
OD_WIDTH = 5120

VMEM_LIMIT = 48 * 1024 * 1024


def _cparams(sem):
    return pltpu.CompilerParams(dimension_semantics=sem, vmem_limit_bytes=VMEM_LIMIT)


def _dot(a, b):
    return jnp.dot(a, b, preferred_element_type=F32)


def _dot_nt(a, b):
    return lax.dot_general(a, b, (((1,), (1,)), ((), ())), preferred_element_type=F32)


def _split_bf16(x, n):
    parts = []
    for _ in range(n):
        p = x.astype(BF16)
        parts.append(p)
        x = x - p.astype(F32)
    return parts


def _dot_sel(sel, x):
    sel = sel.astype(BF16)
    hi, mid, lo = _split_bf16(x, 3)
    return _dot(sel, hi) + (_dot(sel, mid) + _dot(sel, lo))


def _dot_xsel(x, sel):
    sel = sel.astype(BF16)
    hi, mid, lo = _split_bf16(x, 3)
    return _dot(hi, sel) + (_dot(mid, sel) + _dot(lo, sel))


def _dot_x3(a, b):
    ah, al = _split_bf16(a, 2)
    bh, bl = _split_bf16(b, 2)
    return _dot(ah, bh) + (_dot(ah, bl) + _dot(al, bh))


def _sigmoid(x):
    return 1.0 / (1.0 + jnp.exp(-x))


def _silu(x):
    return x * _sigmoid(x)


def _norm_proj_kernel(x_ref, g_ref, w_ref, o_ref, xn_ref):
    @pl.when(pl.program_id(1) == 0)
    def _():
        x = x_ref[...]
        ms = jnp.mean(x * x, axis=-1, keepdims=True)
        xn_ref[...] = (x * lax.rsqrt(ms + EPS) * g_ref[...]).astype(BF16)

    o_ref[...] = _dot(xn_ref[...], w_ref[...])


def _norm_proj(x2d, gain, w_bf16, tn, tm=1024):
    t, d = x2d.shape
    n = w_bf16.shape[1]
    tm = min(tm, t)
    return pl.pallas_call(
        _norm_proj_kernel,
        out_shape=jax.ShapeDtypeStruct((t, n), F32),
        grid=(t // tm, n // tn),
        in_specs=[pl.BlockSpec((tm, d), lambda i, j: (i, 0)),
                  pl.BlockSpec((1, d), lambda i, j: (0, 0)),
                  pl.BlockSpec((d, tn), lambda i, j: (0, j))],
        out_specs=pl.BlockSpec((tm, tn), lambda i, j: (i, j)),
        scratch_shapes=[pltpu.VMEM((tm, d), BF16)],
        compiler_params=_cparams(("parallel", "arbitrary")),
        name="norm_proj",
    )(x2d, gain.reshape(1, d), w_bf16)


def _out_proj_kernel(x_ref, a_ref, b_ref, wa_ref, wb_ref, o_ref):
    o_ref[...] = x_ref[...] + _dot(a_ref[...], wa_ref[...]) + _dot(b_ref[...], wb_ref[...])


def _out_proj(x2d, mix_a, mix_b, w_out, tm=512):
    t, d = x2d.shape
    half = mix_a.shape[1]
    tm = min(tm, t)
    wa = w_out[:half].astype(BF16)
    wb = w_out[half:].astype(BF16)
    return pl.pallas_call(
        _out_proj_kernel,
        out_shape=jax.ShapeDtypeStruct((t, d), F32),
        grid=(t // tm,),
        in_specs=[pl.BlockSpec((tm, d), lambda i: (i, 0)),
                  pl.BlockSpec((tm, half), lambda i: (i, 0)),
                  pl.BlockSpec((tm, half), lambda i: (i, 0)),
                  pl.BlockSpec((half, d), lambda i: (0, 0)),
                  pl.BlockSpec((half, d), lambda i: (0, 0))],
        out_specs=pl.BlockSpec((tm, d), lambda i: (i, 0)),
        compiler_params=_cparams(("parallel",)),
        name="out_proj",
    )(x2d, mix_a, mix_b, wa, wb)


def _t5_bucket_starts():
    exact = REL_BUCKETS // 2
    n = np.arange(0, 4 * REL_MAX_DIST, dtype=np.int64)
    ratio = np.maximum(n, 1).astype(np.float32) / np.float32(exact)
    large = exact + (np.log(ratio).astype(np.float32) / np.float32(math.log(REL_MAX_DIST / exact))
                     * np.float32(REL_BUCKETS - exact)).astype(np.int32)
    bucket = np.where(n < exact, n, np.minimum(large, REL_BUCKETS - 1))
    starts = [int(np.argmax(bucket >= b)) for b in range(REL_BUCKETS)]
    assert all(bucket[s] == b for b, s in enumerate(starts)) and np.all(np.diff(bucket) >= 0)
    assert starts[-1] <= QB, "distances beyond one key block must share the last bucket"
    return starts


_BUCKET_STARTS = _t5_bucket_starts()


def _bias_table_kernel(rb_ref, o_ref):
    row = lax.broadcasted_iota(I32, (BIAS_ROWS, QB), 0)
    lane = lax.broadcasted_iota(I32, (BIAS_ROWS, QB), 1)
    dist = lane + 3 * QB - row
    for h in range(A_HEADS):
        val = jnp.full((BIAS_ROWS, QB), rb_ref[0, h], F32)
        for b in range(1, REL_BUCKETS):
            val = jnp.where(dist >= _BUCKET_STARTS[b], rb_ref[b, h], val)
        val = (val - rb_ref[REL_BUCKETS - 1, h]) * LOG2E
        o_ref[:, h * QB:(h + 1) * QB] = jnp.where(dist >= 0, val, 0.0)


def _bias_table(rel_bias):
    return pl.pallas_call(
        _bias_table_kernel,
        out_shape=jax.ShapeDtypeStruct((BIAS_ROWS, A_HEADS * QB), F32),
        in_specs=[pl.BlockSpec(memory_space=pltpu.SMEM)],
        out_specs=pl.BlockSpec(memory_space=pltpu.VMEM),
        name="dsa_bias_table",
    )(rel_bias)


def _dsa_prep_kernel(aq_ref, qi_ref, ckv_ref, misc_ref, gq_ref, gkv_ref, wkv_ref, gk_ref,
                     qta_ref, qit_ref, wt_ref, kidx_ref, katt_ref, vt_ref):
    zeros = jnp.zeros((QB, A_HEAD_DIM), F32)
    aq = aq_ref[...]
    gq = gq_ref[...]
    ident = jnp.where(lax.broadcasted_iota(I32, (QB, QB), 0) == lax.broadcasted_iota(I32, (QB, QB), 1),
                      1.0, 0.0).astype(BF16)
    for h in range(A_HEADS):
        q = aq[:, h * A_HEAD_DIM:(h + 1) * A_HEAD_DIM]
        ms = jnp.mean(q * q, axis=-1, keepdims=True)
        qn = q * lax.rsqrt(ms + EPS) * gq * (A_HEAD_DIM ** -0.5 * LOG2E)
        qta_ref[0:QB, h * QB:(h + 1) * QB] = jnp.concatenate([qn, zeros], axis=1).T.astype(BF16)
        qta_ref[QB:2 * QB, h * QB:(h + 1) * QB] = ident
    qi = qi_ref[...]
    for h in range(A_IDX_HEADS):
        qih = qi[:, h * A_IDX_DIM:(h + 1) * A_IDX_DIM]
        qit_ref[:, h * QB:(h + 1) * QB] = jnp.concatenate([qih, zeros], axis=1).T.astype(BF16)
    misc = misc_ref[...]
    wt_ref[...] = misc.T[MISC_WI:MISC_WI + A_IDX_HEADS, :] * (A_IDX_HEADS ** -0.5 * A_IDX_DIM ** -0.5)
    c = ckv_ref[...]
    cn = c * lax.rsqrt(jnp.mean(c * c, axis=-1, keepdims=True) + EPS) * gkv_ref[...]
    kv = _dot(cn.astype(BF16), wkv_ref[...])
    k = kv[:, :A_HEAD_DIM]
    kn = k * lax.rsqrt(jnp.mean(k * k, axis=-1, keepdims=True) + EPS) * gk_ref[...]
    kidx_ref[...] = jnp.concatenate([misc[:, MISC_KI:MISC_KI + A_IDX_DIM], zeros], axis=1).astype(BF16)
    katt_ref[...] = jnp.concatenate([kn, zeros], axis=1).astype(BF16)
    kvt = kv.T
    vt_ref[0:A_HEAD_DIM, :] = kvt[A_HEAD_DIM:, :].astype(BF16)
    ones_row = lax.broadcasted_iota(I32, (VT_ROWS - A_HEAD_DIM, QB), 0) == 0
    vt_ref[A_HEAD_DIM:, :] = jnp.where(ones_row, 1.0, 0.0).astype(BF16)


def _dsa_prep(h, nblk, a_q_norm, a_kv_norm, w_kv_up, a_k_norm):
    cb = lambda width, off: off // width
    return pl.pallas_call(
        _dsa_prep_kernel,
        out_shape=(jax.ShapeDtypeStruct((nblk, 2 * QB, A_HEADS * QB), BF16),
                   jax.ShapeDtypeStruct((nblk, QB, A_IDX_HEADS * QB), BF16),
                   jax.ShapeDtypeStruct((nblk, A_IDX_HEADS, QB), F32),
                   jax.ShapeDtypeStruct((nblk, QB, 128), BF16),
                   jax.ShapeDtypeStruct((nblk, QB, 128), BF16),
                   jax.ShapeDtypeStruct((nblk, VT_ROWS, QB), BF16)),
        grid=(nblk,),
        in_specs=[pl.BlockSpec((QB, 1024), lambda i: (i, cb(1024, EV_AQ))),
                  pl.BlockSpec((QB, 512), lambda i: (i, cb(512, EV_QI))),
                  pl.BlockSpec((QB, 128), lambda i: (i, cb(128, EV_CKV))),
                  pl.BlockSpec((QB, 128), lambda i: (i, cb(128, EV_MISC))),
                  pl.BlockSpec((1, A_HEAD_DIM), lambda i: (0, 0)),
                  pl.BlockSpec((1, A_KV_RANK), lambda i: (0, 0)),
                  pl.BlockSpec((A_KV_RANK, 2 * A_HEAD_DIM), lambda i: (0, 0)),
                  pl.BlockSpec((1, A_HEAD_DIM), lambda i: (0, 0))],
        out_specs=(pl.BlockSpec((None, 2 * QB, A_HEADS * QB), lambda i: (i, 0, 0)),
                   pl.BlockSpec((None, QB, A_IDX_HEADS * QB), lambda i: (i, 0, 0)),
                   pl.BlockSpec((None, A_IDX_HEADS, QB), lambda i: (i, 0, 0)),
                   pl.BlockSpec((None, QB, 128), lambda i: (i, 0, 0)),
                   pl.BlockSpec((None, QB, 128), lambda i: (i, 0, 0)),
                   pl.BlockSpec((None, VT_ROWS, QB), lambda i: (i, 0, 0))),
        compiler_params=_cparams(("parallel",)),
        name="dsa_prep",
    )(h, h, h, h, a_q_norm.reshape(1, -1), a_kv_norm.reshape(1, -1), w_kv_up.astype(BF16),
      a_k_norm.reshape(1, -1))


def _dsa_kernel(topk, qit_ref, wt_ref, qta_ref, kidx_ref, katt_ref, vt_ref, bias_ref, az_ref, o_ref,
                strip_ref, hi_ref, x_ref, m_ref, acc_ref, lg_ref, mx_ref, p_ref):
    qt = pl.program_id(1)
    t0 = qt * QB
    n_sc = qt // 4 + 1
    n_sb = qt // 2 + 1
    lane_s = lax.broadcasted_iota(I32, (SCORE_ROWS, QB), 1)
    row_s = lax.broadcasted_iota(I32, (SCORE_ROWS, QB), 0)

    w = wt_ref[...]

    def score_body(j, carry):
        kblk = kidx_ref[pl.ds(j * 4, 4)].reshape(SCORE_ROWS, 128)
        tot = None
        for hp in range(A_IDX_HEADS // 2):
            s = _dot(kblk, qit_ref[:, hp * SB:(hp + 1) * SB])
            for e in range(2):
                h = 2 * hp + e
                term = jnp.maximum(s[:, e * QB:(e + 1) * QB], 0.0) * w[h:h + 1, :]
                tot = term if tot is None else tot + term
        tot = jnp.where(j * SCORE_ROWS + row_s <= t0 + lane_s, tot, NEG_INF)
        bits = pltpu.bitcast(tot, I32)
        key = bits ^ ((bits >> 31) & 0x7FFFFFFF)
        rows = pl.ds(pl.multiple_of(j * SCORE_ROWS, SCORE_ROWS), SCORE_ROWS)
        strip_ref[rows, :] = key
        hi_ref[rows, :] = (key >> 16).astype(jnp.int16)
        return carry

    lax.fori_loop(0, n_sc, score_body, 0)

    def count(pred):
        def body(j, acc):
            r0 = pl.multiple_of(j * SCORE_ROWS, SCORE_ROWS)
            m = jnp.where(pred(strip_ref[pl.ds(r0, SCORE_ROWS), :], r0), 1, 0)
            return acc + m.reshape(SCORE_ROWS // 32, 32, QB).sum(axis=0)
        acc = lax.fori_loop(0, n_sc, body, jnp.zeros((32, QB), I32))
        return acc.reshape(4, 8, QB).sum(axis=0).sum(axis=0, keepdims=True)

    one_h = jnp.ones((), jnp.int16)
    zero_h = jnp.zeros((), jnp.int16)

    def count_hi(cand):
        cand_h = (cand >> 16).astype(jnp.int16)

        def body(j, acc):
            r0 = pl.multiple_of(j * SCORE_ROWS, SCORE_ROWS)
            m = jnp.where(hi_ref[pl.ds(r0, SCORE_ROWS), :] >= cand_h, one_h, zero_h)
            parts = [m[r:r + 32, :] for r in range(0, SCORE_ROWS, 32)]
            while len(parts) > 1:
                parts = [parts[i] + parts[i + 1] for i in range(0, len(parts), 2)]
            return acc + parts[0]
        acc = lax.fori_loop(0, n_sc, body, jnp.zeros((32, QB), jnp.int16))
        return acc.astype(I32).reshape(4, 8, QB).sum(axis=0).sum(axis=0, keepdims=True)

    c0 = count_hi(jnp.zeros((1, QB), I32))
    nonneg = c0 >= topk
    thr0 = jnp.where(nonneg, 0, INT_MIN)
    cnt0 = jnp.where(nonneg, c0, n_sc * SCORE_ROWS)

    def bit_steps(counter, n):
        def one(_, carry):
            i, thr, cnt = carry
            cand = thr | lax.shift_left(jnp.int32(1), 30 - i)
            c = counter(cand)
            ok = c >= topk
            return i + 1, jnp.where(ok, cand, thr), jnp.where(ok, c, cnt)
        return lambda carry: lax.fori_loop(0, n, one, carry)

    state = bit_steps(count_hi, 15)((jnp.int32(0), thr0, cnt0))
    _, thr, cnt = lax.while_loop(lambda c: (c[0] < 31) & (jnp.max(c[2]) > topk),
                                 bit_steps(lambda cand: count(lambda key, r0: key >= cand), 4), state)

    x_ref[...] = jnp.full((1, QB), 2 ** 30, I32)

    @pl.when(jnp.max(cnt) > topk)
    def _():
        above = count(lambda key, r0: key > thr)
        need = topk - above

        def tie_body(i, xb):
            cand = xb | lax.shift_left(jnp.int32(1), 14 - i)
            c = count(lambda key, r0: (key == thr) & (r0 + row_s < cand))
            return jnp.where(c <= need, cand, xb)

        x_ref[...] = lax.fori_loop(0, 15, tie_body, jnp.zeros((1, QB), I32))

    xb = x_ref[...]

    m_ref[...] = jnp.full(m_ref.shape, M_INIT, F32)
    acc_ref[...] = jnp.zeros(acc_ref.shape, F32)
    n_lt = A_HEADS * QB // SB
    row_a = lax.broadcasted_iota(I32, (SB, QB), 0)
    lane_a = lax.broadcasted_iota(I32, (SB, QB), 1)

    def logits(j, slot, near):
        r0 = pl.multiple_of(j * SB, SB)
        key = strip_ref[pl.ds(r0, SB), :]
        idx = r0 + row_a
        sel = (key > thr) | ((key == thr) & (idx < xb))
        if near:
            sel = sel & (idx <= t0 + lane_a)
            b0 = pl.multiple_of((2 * j - qt + 3) * QB, QB)
        pen = jnp.where(sel, 0.0, MASKED).astype(BF16)
        kaug = jnp.concatenate([katt_ref[pl.ds(2 * j, 2)].reshape(SB, 128), pen], axis=1)
        for lt in range(n_lt):
            cols = slice(lt * SB, (lt + 1) * SB)
            lg = _dot(kaug, qta_ref[:, cols])
            if near:
                lg = lg + bias_ref[pl.ds(b0, SB), cols]
            lg = lg.astype(BF16)
            lg_ref[slot, :, cols] = lg
            mx_ref[slot, :, cols] = jnp.max(lg, axis=0, keepdims=True).astype(F32)

    def accumulate(j, slot):
        vt = jnp.concatenate([vt_ref[2 * j], vt_ref[2 * j + 1]], axis=1)
        m_old = m_ref[...]
        m_new = jnp.maximum(m_old, mx_ref[slot])
        m_ref[...] = m_new
        m_b = m_new.astype(BF16)
        for lt in range(n_lt):
            cols = slice(lt * SB, (lt + 1) * SB)
            p_ref[:, cols] = jnp.exp2(lg_ref[slot, :, cols] - m_b[:, cols])
        acc_ref[...] = jnp.exp2(m_old - m_new) * acc_ref[...] + _dot(vt, p_ref[...])

    n_far = jnp.maximum(n_sb - 2, 0)
    n_pair = jnp.maximum(n_far - 1, 0) // 2
    n_rem = n_far - 2 * n_pair

    @pl.when(n_far > 0)
    def _():
        logits(0, 0, False)

    def pair_body(jj, carry):
        j = 2 * jj
        logits(j + 1, 1, False)
        accumulate(j, 0)
        logits(j + 2, 0, False)
        accumulate(j + 1, 1)
        return carry

    lax.fori_loop(0, n_pair, pair_body, 0)

    @pl.when(n_rem == 1)
    def _():
        accumulate(n_far - 1, 0)

    @pl.when(n_rem == 2)
    def _():
        logits(n_far - 1, 1, False)
        accumulate(n_far - 2, 0)
        accumulate(n_far - 1, 1)

    @pl.when(n_sb >= 2)
    def _():
        logits(n_sb - 2, 0, True)
        accumulate(n_sb - 2, 0)

    logits(n_sb - 1, 0, True)
    accumulate(n_sb - 1, 0)

    acc = acc_ref[...]
    o_t = acc[0:A_HEAD_DIM, :] / acc[A_HEAD_DIM:A_HEAD_DIM + 1, :]
    pad = jnp.zeros((QB - A_HEAD_DIM, QB), F32)
    pieces = []
    for h in range(A_HEADS):
        blk = jnp.concatenate([o_t[:, h * QB:(h + 1) * QB], pad], axis=0)
        pieces.append(blk.T[:, 0:A_HEAD_DIM])
    att = jnp.concatenate(pieces, axis=1)
    o_ref[...] = (att * _silu(az_ref[...])).astype(BF16)


def _dsa_attend(h, bsz, seq, qta, qit, wt, kidx, katt, vt, bias):
    nq = seq // QB
    topk = min(A_TOPK_MAX, seq // 4)
    blk = lambda b, q: (b * nq + q, 0, 0)
    per_batch = lambda b, q: (b, 0, 0)
    once = pl.Buffered(1)
    return pl.pallas_call(
        functools.partial(_dsa_kernel, topk),
        out_shape=jax.ShapeDtypeStruct((bsz * seq, A_HEADS * A_HEAD_DIM), BF16),
        grid=(bsz, nq),
        in_specs=[pl.BlockSpec((None, QB, A_IDX_HEADS * QB), blk),
                  pl.BlockSpec((None, A_IDX_HEADS, QB), blk),
                  pl.BlockSpec((None, 2 * QB, A_HEADS * QB), blk),
                  pl.BlockSpec((nq, QB, 128), per_batch, pipeline_mode=once),
                  pl.BlockSpec((nq, QB, 128), per_batch, pipeline_mode=once),
                  pl.BlockSpec((nq, VT_ROWS, QB), per_batch, pipeline_mode=once),
                  pl.BlockSpec((BIAS_ROWS, A_HEADS * QB), lambda b, q: (0, 0), pipeline_mode=once),
                  pl.BlockSpec((QB, 1024), lambda b, q: (b * nq + q, EV_AZ // 1024))],
        out_specs=pl.BlockSpec((QB, 1024), lambda b, q: (b * nq + q, 0)),
        scratch_shapes=[pltpu.VMEM((seq, QB), I32),
                        pltpu.VMEM((seq, QB), jnp.int16),
                        pltpu.VMEM((1, QB), I32),
                        pltpu.VMEM((1, A_HEADS * QB), F32),
                        pltpu.VMEM((VT_ROWS, A_HEADS * QB), F32),
                        pltpu.VMEM((2, SB, A_HEADS * QB), BF16),
                        pltpu.VMEM((2, 1, A_HEADS * QB), F32),
                        pltpu.VMEM((SB, A_HEADS * QB), BF16)],
        compiler_params=_cparams(("arbitrary", "arbitrary")),
        name="dsa_attend",
    )(qit, wt, qta, kidx, katt, vt, bias, h)


GDN_TM = 512
GDN_GROUP = 4
GDN_SCAN_HEADS = 4


def _gdn_prep_kernel(alog_ref, dtb_ref, q_ref, k_ref, v_ref, qh_ref, kh_ref, vh_ref, misc_ref,
                     cq_ref, ck_ref, cv_ref,
                     u_ref, w_ref, qd_ref, kdt_ref, attn_ref, egl_ref):
    i = pl.program_id(1)
    hd = pl.program_id(2)
    tm = q_ref.shape[0]
    nchunk = tm // B_CHUNK

    def conv_silu(x_ref, halo_ref, c_ref):
        halo = jnp.where(i > 0, halo_ref[...], 0.0)
        ext = jnp.concatenate([halo, x_ref[...]], axis=0)
        cw = c_ref[...]
        y = ext[8:, :] * cw[CONV_K - 1:CONV_K, :]
        for d in range(1, CONV_K):
            y = y + pltpu.roll(ext, d, 0)[8:, :] * cw[CONV_K - 1 - d:CONV_K - d, :]
        return _silu(y)

    q = conv_silu(q_ref, qh_ref, cq_ref)
    k = conv_silu(k_ref, kh_ref, ck_ref)
    v = conv_silu(v_ref, vh_ref, cv_ref)
    qn = q * lax.rsqrt(jnp.sum(q * q, axis=-1, keepdims=True) + EPS) * (B_HEAD_DIM ** -0.5)
    kn = k * lax.rsqrt(jnp.sum(k * k, axis=-1, keepdims=True) + EPS)

    misc = misc_ref[...]
    sel_r = lax.broadcasted_iota(I32, (128, 256), 0)
    sel_c = lax.broadcasted_iota(I32, (128, 256), 1)
    pick = jnp.where(sel_r == jnp.where(sel_c < 128, MISC_BA + hd, MISC_BB + hd), 1.0, 0.0)
    bab = _dot_xsel(misc, pick)
    ba = bab[:, 0:128]
    bb = bab[:, 128:256]
    beta = _sigmoid(bb)
    xg = ba + dtb_ref[hd]
    softplus = jnp.maximum(xg, 0.0) + jnp.log(1.0 + jnp.exp(-jnp.abs(xg)))
    g = -jnp.exp(jnp.full((1, 128), alog_ref[hd], F32)) * softplus

    sl = 256
    r2 = lax.broadcasted_iota(I32, (sl, sl), 0)
    c2 = lax.broadcasted_iota(I32, (sl, sl), 1)
    same = (r2 // B_CHUNK) == (c2 // B_CHUNK)
    tri = jnp.where(same & (c2 <= r2), 1.0, 0.0)
    blk = jnp.where(same, 1.0, 0.0)
    nsl = tm // sl
    both = _dot_sel(jnp.concatenate([tri, blk], axis=0),
                    jnp.concatenate([g[s * sl:(s + 1) * sl, :] for s in range(nsl)], axis=1))
    gc = jnp.concatenate([both[0:sl, s * 128:(s + 1) * 128] for s in range(nsl)], axis=0)
    gl = jnp.concatenate([both[sl:2 * sl, s * 128:(s + 1) * 128] for s in range(nsl)], axis=0)

    kb = kn * beta
    vb = v * beta
    egc = jnp.exp(gc)
    qd_ref[...] = (qn * egc).astype(BF16)
    kd = kn * jnp.exp(gl - gc)
    for m in range(tm // 128):
        kdt_ref[:, m * 128:(m + 1) * 128] = kd[m * 128:(m + 1) * 128, :].T.astype(BF16)
    kbg = kb * egc

    gw = GDN_GROUP * B_CHUNK
    ri = lax.broadcasted_iota(I32, (B_CHUNK, gw), 0)
    lj = lax.broadcasted_iota(I32, (B_CHUNK, gw), 1)
    lb = lj // B_CHUNK
    lj = lj % B_CHUNK
    bmask = (lax.broadcasted_iota(I32, (gw, gw), 0) // B_CHUNK
             == lax.broadcasted_iota(I32, (gw, gw), 1) // B_CHUNK)

    def fold(x):
        xm = jnp.where(bmask, x, 0.0)
        out = xm[0:B_CHUNK, :]
        for c in range(1, GDN_GROUP):
            out = out + xm[c * B_CHUNK:(c + 1) * B_CHUNK, :]
        return out

    def bdiag(cat):
        return jnp.where(bmask, jnp.concatenate([cat] * GDN_GROUP, axis=0), 0.0)

    for grp in range(tm // gw):
        rs = slice(grp * gw, (grp + 1) * gw)
        gcs = gc[rs, :]
        col = jnp.concatenate([gcs[0:B_CHUNK, :]] * 2, axis=1)
        for c in range(1, GDN_GROUP):
            col = jnp.where(lb == c, jnp.concatenate([gcs[c * B_CHUNK:(c + 1) * B_CHUNK, :]] * 2, axis=1), col)
        rowf = jnp.concatenate([gcs[m * 128:(m + 1) * 128, :].T[0:B_CHUNK, :] for m in range(gw // 128)],
                               axis=1)
        decay = jnp.exp(jnp.where(ri >= lj, col - rowf, NEG_INF))
        kg = kn[rs, :]
        a_cat = jnp.where(ri > lj, fold(_dot_nt(kb[rs, :], kg)) * decay, 0.0)
        attn_ref[grp] = jnp.where(ri >= lj, fold(_dot_nt(qn[rs, :], kg)) * decay, 0.0).astype(BF16)
        inv = jnp.where(ri == lj, 1.0, 0.0) - a_cat
        pw = _dot_x3(a_cat, bdiag(a_cat))
        for step in range(5):
            if step < 4:
                both = _dot_x3(jnp.concatenate([inv, pw], axis=0), bdiag(pw))
                inv = inv + both[0:B_CHUNK, :]
                pw = both[B_CHUNK:, :]
            else:
                inv = inv + _dot_x3(inv, bdiag(pw))
        sol = _dot_x3(bdiag(inv), jnp.concatenate([vb[rs, :], kbg[rs, :]], axis=1))
        u_ref[rs, :] = sol[:, 0:B_HEAD_DIM]
        w_ref[rs, :] = sol[:, B_HEAD_DIM:].astype(BF16)
    for c in range(nchunk):
        egl_ref[c:c + 1, :] = jnp.exp(gl[c * B_CHUNK:c * B_CHUNK + 1, :])
    if nchunk < 8:
        egl_ref[nchunk:, :] = jnp.zeros((8 - nchunk, 128), F32)


def _gdn_prep(h, bsz, seq, b_conv, b_a_log, b_dt_bias):
    tm = min(GDN_TM, seq)
    nt = seq // tm
    nrow = max(tm // B_CHUNK, 8)
    gw = GDN_GROUP * B_CHUNK
    hs = (bsz, B_HEADS, seq, B_HEAD_DIM)
    cur = lambda off: (lambda b, i, hd: (b * nt + i, off // 128 + hd))
    halo = lambda off: (lambda b, i, hd: (jnp.maximum((b * nt + i) * (tm // 8) - 1, 0), off // 128 + hd))
    cw = lambda off: (lambda b, i, hd: (0, off // 128 + hd))
    out = lambda b, i, hd: (b, hd, i, 0)
    smem = pl.BlockSpec(memory_space=pltpu.SMEM)
    return pl.pallas_call(
        _gdn_prep_kernel,
        out_shape=(jax.ShapeDtypeStruct(hs, F32), jax.ShapeDtypeStruct(hs, BF16),
                   jax.ShapeDtypeStruct(hs, BF16),
                   jax.ShapeDtypeStruct((bsz, B_HEADS, B_HEAD_DIM, seq), BF16),
                   jax.ShapeDtypeStruct((bsz, B_HEADS, seq // gw, B_CHUNK, gw), BF16),
                   jax.ShapeDtypeStruct((bsz, B_HEADS, nt * nrow, 128), F32)),
        grid=(bsz, nt, B_HEADS),
        in_specs=[smem, smem,
                  pl.BlockSpec((tm, 128), cur(EV_BQ)), pl.BlockSpec((tm, 128), cur(EV_BK)),
                  pl.BlockSpec((tm, 128), cur(EV_BV)),
                  pl.BlockSpec((8, 128), halo(EV_BQ)), pl.BlockSpec((8, 128), halo(EV_BK)),
                  pl.BlockSpec((8, 128), halo(EV_BV)),
                  pl.BlockSpec((tm, 128), lambda b, i, hd: (b * nt + i, EV_MISC // 128)),
                  pl.BlockSpec((CONV_K, 128), cw(0)), pl.BlockSpec((CONV_K, 128), cw(1024)),
                  pl.BlockSpec((CONV_K, 128), cw(2048))],
        out_specs=(pl.BlockSpec((None, None, tm, 128), out), pl.BlockSpec((None, None, tm, 128), out),
                   pl.BlockSpec((None, None, tm, 128), out),
                   pl.BlockSpec((None, None, B_HEAD_DIM, tm), lambda b, i, hd: (b, hd, 0, i)),
                   pl.BlockSpec((None, None, tm // gw, B_CHUNK, gw), lambda b, i, hd: (b, hd, i, 0, 0)),
                   pl.BlockSpec((None, None, nrow, 128), out)),
        compiler_params=_cparams(("parallel", "parallel", "parallel")),
        name="gdn_prep",
    )(b_a_log, b_dt_bias, h, h, h, h, h, h, h, b_conv, b_conv, b_conv)


def _gdn_scan_kernel(u_ref, w_ref, qd_ref, kdt_ref, attn_ref, egl_ref, z_ref, gn_ref, o_ref, s_ref):
    @pl.when(pl.program_id(2) == 0)
    def _():
        s_ref[...] = jnp.zeros(s_ref.shape, F32)

    tm = u_ref.shape[1]
    gw = GDN_GROUP * B_CHUNK
    gain = gn_ref[...]
    zero = jnp.zeros((B_CHUNK, B_HEAD_DIM), BF16)
    states = [s_ref[hh] for hh in range(GDN_SCAN_HEADS)]
    for c in range(tm // B_CHUNK):
        grp, ci = divmod(c, GDN_GROUP)
        rs = slice(c * B_CHUNK, (c + 1) * B_CHUNK)
        for hh in range(GDN_SCAN_HEADS):
            sb = states[hh].astype(BF16)
            v_new = u_ref[hh, rs, :] - _dot(w_ref[hh, rs, :], sb)
            vpad = jnp.concatenate([zero] * ci + [v_new.astype(BF16)] + [zero] * (GDN_GROUP - 1 - ci), axis=0)
            o = _dot(qd_ref[hh, rs, :], sb) + _dot(attn_ref[hh, grp], vpad)
            states[hh] = (states[hh] * egl_ref[hh, c:c + 1, :]
                          + _dot(kdt_ref[hh, :, grp * gw:(grp + 1) * gw], vpad))
            on = o * lax.rsqrt(jnp.mean(o * o, axis=-1, keepdims=True) + EPS) * gain
            lanes = slice(hh * B_HEAD_DIM, (hh + 1) * B_HEAD_DIM)
            o_ref[rs, lanes] = (on * _silu(z_ref[rs, lanes])).astype(BF16)
    for hh in range(GDN_SCAN_HEADS):
        s_ref[hh] = states[hh]


def _gdn_scan(h, bsz, seq, u, w, qd, kd, attn, egl, b_out_norm):
    tm = min(GDN_TM, seq)
    nt = seq // tm
    nrow = max(tm // B_CHUNK, 8)
    gw = GDN_GROUP * B_CHUNK
    hg = GDN_SCAN_HEADS
    blk = lambda b, hp, i: (b, hp, i, 0)
    return pl.pallas_call(
        _gdn_scan_kernel,
        out_shape=jax.ShapeDtypeStruct((bsz * seq, B_HEADS * B_HEAD_DIM), BF16),
        grid=(bsz, B_HEADS // hg, nt),
        in_specs=[pl.BlockSpec((None, hg, tm, 128), blk), pl.BlockSpec((None, hg, tm, 128), blk),
                  pl.BlockSpec((None, hg, tm, 128), blk),
                  pl.BlockSpec((None, hg, B_HEAD_DIM, tm), lambda b, hp, i: (b, hp, 0, i)),
                  pl.BlockSpec((None, hg, tm // gw, B_CHUNK, gw), lambda b, hp, i: (b, hp, i, 0, 0)),
                  pl.BlockSpec((None, hg, nrow, 128), blk),
                  pl.BlockSpec((tm, hg * 128), lambda b, hp, i: (b * nt + i, EV_BZ // (hg * 128) + hp)),
                  pl.BlockSpec((1, 128), lambda b, hp, i: (0, 0))],
        out_specs=pl.BlockSpec((tm, hg * 128), lambda b, hp, i: (b * nt + i, hp)),
        scratch_shapes=[pltpu.VMEM((hg, B_HEAD_DIM, B_HEAD_DIM), F32)],
        compiler_params=_cparams(("parallel", "parallel", "arbitrary")),
        name="gdn_scan",
    )(u, w, qd, kd, attn, egl, h, b_out_norm.reshape(1, -1))


def _rope_kernel(inv_ref, cos_ref, sin_ref):
    tm = cos_ref.shape[0]
    pos = (pl.program_id(0) * tm + lax.broadcasted_iota(I32, (tm, 128), 0)).astype(F32)
    ang = pos * inv_ref[...]
    lane = lax.broadcasted_iota(I32, (tm, 128), 1)
    cos_ref[...] = jnp.cos(ang)
    sin_ref[...] = jnp.where(lane < 64, -jnp.sin(ang), jnp.sin(ang))


def _rope_tables(seq):
    half = C_QK_DIM // 2
    inv = 1.0 / (ROPE_BASE ** jnp.linspace(0.0, 1.0, half, dtype=F32))
    inv2 = jnp.concatenate([inv, inv]).reshape(1, C_QK_DIM)
    tm = min(1024, seq)
    return pl.pallas_call(
        _rope_kernel,
        out_shape=(jax.ShapeDtypeStruct((seq, C_QK_DIM), F32),) * 2,
        grid=(seq // tm,),
        in_specs=[pl.BlockSpec((1, C_QK_DIM), lambda i: (0, 0))],
        out_specs=(pl.BlockSpec((tm, C_QK_DIM), lambda i: (i, 0)),) * 2,
        compiler_params=_cparams(("parallel",)),
        name="rope_tables",
    )(inv2)


RET_TM = 512


def _retention_kernel(lg_ref, q_ref, k_ref, v_ref, z_ref, cos_ref, sin_ref, gn_ref, o_ref,
                      r_ref, dm_ref, zeta_ref, xi_ref):
    hd = pl.program_id(1)
    lg = lg_ref[0, hd]

    @pl.when(pl.program_id(2) == 0)
    def _():
        r_ref[...] = jnp.zeros(r_ref.shape, F32)
        ri = lax.broadcasted_iota(I32, (C_CHUNK, C_CHUNK), 0)
        ci = lax.broadcasted_iota(I32, (C_CHUNK, C_CHUNK), 1)
        diff = (ri - ci).astype(F32)
        dm_ref[...] = jnp.where(diff >= 0, jnp.exp(jnp.maximum(diff, 0.0) * lg), 0.0)
        zeta_ref[...] = jnp.exp((C_CHUNK - 1 - ri).astype(F32) * lg)
        rv = lax.broadcasted_iota(I32, (C_CHUNK, C_V_DIM), 0).astype(F32)
        xi_ref[...] = jnp.exp((rv + 1.0) * lg)

    g_chunk = lg_ref[1, hd]
    tm = q_ref.shape[0]
    state = r_ref[...]
    dmask = dm_ref[...]
    gain = gn_ref[...]
    for c in range(tm // C_CHUNK):
        rs = slice(c * C_CHUNK, (c + 1) * C_CHUNK)
        cos = cos_ref[rs, :]
        sin = sin_ref[rs, :]
        q = q_ref[rs, :]
        k = k_ref[rs, :]
        qr = q * cos + pltpu.roll(q, C_QK_DIM // 2, 1) * sin
        kr = (k * cos + pltpu.roll(k, C_QK_DIM // 2, 1) * sin) * (C_QK_DIM ** -0.5)
        vb = v_ref[rs, :].astype(BF16)
        qb = qr.astype(BF16)
        s = _dot_nt(qb, kr.astype(BF16)) * dmask
        o = _dot(s.astype(BF16), vb) + _dot(qb, state.astype(BF16)) * xi_ref[...]
        state = state * g_chunk + _dot((kr * zeta_ref[...]).T.astype(BF16), vb)
        mu = jnp.mean(o, axis=-1, keepdims=True)
        oc = o - mu
        var = jnp.mean(oc * oc, axis=-1, keepdims=True)
        y = oc * lax.rsqrt(var + EPS) * gain
        o_ref[rs, :] = (y * _silu(z_ref[rs, :])).astype(BF16)
    r_ref[...] = state


def _retention(h, bsz, seq, cos2, sin2, c_out_norm):
    tm = min(RET_TM, seq)
    nt = seq // tm
    lg = np.log1p(-np.exp2(-5.0 - np.arange(C_HEADS, dtype=np.float32))).astype(np.float32)
    lg = np.stack([lg, np.exp(np.float32(C_CHUNK) * lg).astype(np.float32)])
    return pl.pallas_call(
        _retention_kernel,
        out_shape=jax.ShapeDtypeStruct((bsz * seq, C_HEADS * C_V_DIM), BF16),
        grid=(bsz, C_HEADS, nt),
        in_specs=[pl.BlockSpec(memory_space=pltpu.SMEM),
                  pl.BlockSpec((tm, 128), lambda b, hd, i: (b * nt + i, OD_CQ // 128 + hd)),
                  pl.BlockSpec((tm, 128), lambda b, hd, i: (b * nt + i, OD_CK // 128 + hd)),
                  pl.BlockSpec((tm, 256), lambda b, hd, i: (b * nt + i, OD_CV // 256 + hd)),
                  pl.BlockSpec((tm, 256), lambda b, hd, i: (b * nt + i, OD_CZ // 256 + hd)),
                  pl.BlockSpec((tm, 128), lambda b, hd, i: (i, 0)),
                  pl.BlockSpec((tm, 128), lambda b, hd, i: (i, 0)),
                  pl.BlockSpec((1, C_V_DIM), lambda b, hd, i: (0, hd))],
        out_specs=pl.BlockSpec((tm, C_V_DIM), lambda b, hd, i: (b * nt + i, hd)),
        scratch_shapes=[pltpu.VMEM((C_QK_DIM, C_V_DIM), F32),
                        pltpu.VMEM((C_CHUNK, C_CHUNK), F32),
                        pltpu.VMEM((C_CHUNK, C_QK_DIM), F32),
                        pltpu.VMEM((C_CHUNK, C_V_DIM), F32)],
        compiler_params=_cparams(("parallel", "parallel", "arbitrary")),
        name="retention",
    )(jnp.asarray(lg), h, h, h, h, cos2, sin2, c_out_norm.reshape(1, -1))


def _s5_param_kernel(lre_ref, lim_ref, ldt_ref, lrex_ref, limx_ref, ldtx_ref, bre_ref, bim_ref,
                     are_ref, aim_ref, bbre_ref, bbim_ref):
    def disc(lre, lim, ldt):
        lr = jnp.minimum(lre, -1e-4)
        dt = jnp.exp(ldt)
        mag = jnp.exp(lr * dt)
        return lr, lim, mag * jnp.cos(lim * dt), mag * jnp.sin(lim * dt)

    _, _, a_re, a_im = disc(lre_ref[...], lim_ref[...], ldt_ref[...])
    are_ref[...] = a_re
    aim_ref[...] = a_im
    lr, li, ax_re, ax_im = disc(lrex_ref[...], limx_ref[...], ldtx_ref[...])
    den = lr * lr + li * li
    f_re = ((ax_re - 1.0) * lr + ax_im * li) / den
    f_im = (ax_im * lr - (ax_re - 1.0) * li) / den
    bbre_ref[...] = f_re * bre_ref[...] - f_im * bim_ref[...]
    bbim_ref[...] = f_re * bim_ref[...] + f_im * bre_ref[...]


def _s5_params(lam_re, lam_im, log_dt, b_re, b_im, c_re, c_im):
    g, p, ch = D_GROUPS, D_STATE, D_GROUP
    ldt = jnp.broadcast_to(log_dt[:, None], (g, p))
    rep = lambda a: jnp.repeat(a, ch, axis=1)
    vm = pl.BlockSpec(memory_space=pltpu.VMEM)
    a_re, a_im, bb_re, bb_im = pl.pallas_call(
        _s5_param_kernel,
        out_shape=(jax.ShapeDtypeStruct((g, p), F32), jax.ShapeDtypeStruct((g, p), F32),
                   jax.ShapeDtypeStruct((g, p * ch), F32), jax.ShapeDtypeStruct((g, p * ch), F32)),
        in_specs=[vm] * 8, out_specs=(vm,) * 4,
        name="s5_params",
    )(lam_re, lam_im, ldt, rep(lam_re), rep(lam_im), rep(ldt),
      b_re.reshape(g, p * ch), b_im.reshape(g, p * ch))
    eye = jnp.eye(D_SETS * 4, dtype=F32)

    def pack_b(bb):
        bb = bb.reshape(D_SETS, 16, p, ch)
        return jnp.einsum('sgpi,gh->sgihp', bb, eye).reshape(D_SETS, D_SET_CH, D_SET_ST)

    def pack_c(c):
        c = c.reshape(D_SETS, 16, ch, p)
        return jnp.einsum('sgjp,gh->sgphj', c, eye).reshape(D_SETS, D_SET_ST, D_SET_CH)

    bd = jnp.concatenate([pack_b(bb_re), pack_b(bb_im)], axis=-1).astype(BF16)
    slab = (D_SETS, S5_SLAB, 128)
    return (a_re.reshape(slab), a_im.reshape(slab), bd,
            pack_c(c_re).astype(BF16), pack_c(c_im).astype(BF16))


S5_TM = 256
S5_SLAB = D_SET_ST // 128


def _s5_kernel(u_ref, z_ref, are_ref, aim_ref, bd_ref, cre_ref, cim_ref, dskip_ref, wglu_ref, bglu_ref,
               o_ref, hre_ref, him_ref, y_ref, *x_refs):
    @pl.when(pl.program_id(1) == 0)
    def _():
        hre_ref[...] = jnp.zeros(hre_ref.shape, F32)
        him_ref[...] = jnp.zeros(him_ref.shape, F32)

    xre, xim = x_refs[:D_SETS], x_refs[D_SETS:]
    tm = u_ref.shape[0]
    u = u_ref[...]
    ub = u.astype(BF16)
    for s in range(D_SETS):
        bu = _dot(ub[:, s * D_SET_CH:(s + 1) * D_SET_CH], bd_ref[s])
        for k in range(S5_SLAB):
            xre[s][pl.ds(k, tm, stride=S5_SLAB), :] = bu[:, k * 128:(k + 1) * 128]
            xim[s][pl.ds(k, tm, stride=S5_SLAB), :] = bu[:, D_SET_ST + k * 128:D_SET_ST + (k + 1) * 128]
    a_re = [are_ref[s] for s in range(D_SETS)]
    a_im = [aim_ref[s] for s in range(D_SETS)]

    def step(t, carry):
        rows = pl.ds(pl.multiple_of(t * S5_SLAB, S5_SLAB), S5_SLAB)
        out = []
        for s in range(D_SETS):
            hr, hi = carry[2 * s], carry[2 * s + 1]
            xr = xre[s][rows, :] + a_re[s] * hr - a_im[s] * hi
            xi = xim[s][rows, :] + a_re[s] * hi + a_im[s] * hr
            xre[s][rows, :] = xr
            xim[s][rows, :] = xi
            out += [xr, xi]
        return tuple(out)

    carry = []
    for s in range(D_SETS):
        carry += [hre_ref[s], him_ref[s]]
    carry = lax.fori_loop(0, tm, step, tuple(carry), unroll=8)
    for s in range(D_SETS):
        hre_ref[s] = carry[2 * s]
        him_ref[s] = carry[2 * s + 1]
        x_r = jnp.concatenate([xre[s][pl.ds(k, tm, stride=S5_SLAB), :] for k in range(S5_SLAB)], axis=1)
        x_i = jnp.concatenate([xim[s][pl.ds(k, tm, stride=S5_SLAB), :] for k in range(S5_SLAB)], axis=1)
        y_ref[:, s * D_SET_CH:(s + 1) * D_SET_CH] = (
            _dot(x_r.astype(BF16), cre_ref[s]) - _dot(x_i.astype(BF16), cim_ref[s]))
    y = y_ref[...] + dskip_ref[...] * u
    y = 0.5 * y * (1.0 + jnp.tanh(math.sqrt(2.0 / math.pi) * (y + 0.044715 * (y * y * y))))
    gate = _sigmoid(_dot(y.astype(BF16), wglu_ref[...]) + bglu_ref[...])
    o_ref[...] = (y * gate * _silu(z_ref[...])).astype(BF16)


def _s5(h, bsz, seq, a_re, a_im, bd, cd_re, cd_im, d_skip, w_glu, b_glu):
    tm = min(S5_TM, seq)
    nt = seq // tm
    full = lambda *shape: pl.BlockSpec(shape, lambda b, i: (0,) * len(shape))
    width = D_GROUPS * D_GROUP
    return pl.pallas_call(
        _s5_kernel,
        out_shape=jax.ShapeDtypeStruct((bsz * seq, width), BF16),
        grid=(bsz, nt),
        in_specs=[pl.BlockSpec((tm, width), lambda b, i: (b * nt + i, OD_DU // width)),
                  pl.BlockSpec((tm, width), lambda b, i: (b * nt + i, OD_DZ // width)),
                  full(D_SETS, S5_SLAB, 128), full(D_SETS, S5_SLAB, 128),
                  full(D_SETS, D_SET_CH, 2 * D_SET_ST),
                  full(D_SETS, D_SET_ST, D_SET_CH), full(D_SETS, D_SET_ST, D_SET_CH),
                  full(1, width), full(width, width), full(1, width)],
        out_specs=pl.BlockSpec((tm, width), lambda b, i: (b * nt + i, 0)),
        scratch_shapes=([pltpu.VMEM((D_SETS, S5_SLAB, 128), F32), pltpu.VMEM((D_SETS, S5_SLAB, 128), F32),
                         pltpu.VMEM((tm, width), F32)]
                        + [pltpu.VMEM((tm * S5_SLAB, 128), F32)] * (2 * D_SETS)),
        compiler_params=_cparams(("parallel", "arbitrary")),
        name="s5",
    )(h, h, a_re, a_im, bd, cd_re, cd_im, d_skip.reshape(1, -1), w_glu.astype(BF16),
      b_glu.reshape(1, -1))


def _pack_even_w(w_in):
    sizes = (1024, A_KV_RANK, A_IDX_HEADS * A_IDX_DIM, A_IDX_DIM, A_IDX_HEADS, 1024,
             3 * 1024, B_HEADS, B_HEADS, 1024)
    parts, start = [], 0
    for s in sizes:
        parts.append(w_in[:, start:start + s])
        start += s
    aq, ckv, qi, ki, wi, az, bqkv, ba, bb, bz = parts
    pad = jnp.zeros((w_in.shape[0], EV_WIDTH - start), w_in.dtype)
    return jnp.concatenate([aq, az, bqkv, bz, qi, ckv, ki, wi, ba, bb, pad], axis=1).astype(BF16)


def _even_layer(x2d, bsz, seq, bias, norm_g, w_in, a_q_norm, a_kv_norm, w_kv_up, a_k_norm,
                b_conv, b_a_log, b_dt_bias, b_out_norm, w_out):
    h = _norm_proj(x2d, norm_g, _pack_even_w(w_in), tn=768)
    qta, qit, wt, kidx, katt, vt = _dsa_prep(h, bsz * seq // QB, a_q_norm, a_kv_norm, w_kv_up, a_k_norm)
    mix_a = _dsa_attend(h, bsz, seq, qta, qit, wt, kidx, katt, vt, bias)
    u, w, qd, kd, attn, egl = _gdn_prep(h, bsz, seq, b_conv, b_a_log, b_dt_bias)
    mix_b = _gdn_scan(h, bsz, seq, u, w, qd, kd, attn, egl, b_out_norm)
    return _out_proj(x2d, mix_a, mix_b, w_out)


def _odd_layer(x2d, bsz, seq, cos2, sin2, norm_g, w_in, c_out_norm, lam_re, lam_im, log_dt,
               b_re, b_im, c_re, c_im, d_skip, w_glu, b_glu, w_out):
    h = _norm_proj(x2d, norm_g, w_in.astype(BF16), tn=1024)
    mix_c = _retention(h, bsz, seq, cos2, sin2, c_out_norm)
    s5p = _s5_params(lam_re, lam_im, log_dt, b_re, b_im, c_re, c_im)
    mix_d = _s5(h, bsz, seq, *s5p, d_skip, w_glu, b_glu)
    return _out_proj(x2d, mix_c, mix_d, w_out)


def kernel(x, rel_bias, ev_norm, ev_w_in, ev_a_q_norm, ev_a_kv_norm, ev_w_kv_up, ev_a_k_norm,
           ev_b_conv, ev_b_a_log, ev_b_dt_bias, ev_b_out_norm, ev_w_out,
           od_norm, od_w_in, od_c_out_norm, od_lam_re, od_lam_im, od_log_dt,
           od_b_re, od_b_im, od_c_re, od_c_im, od_d_skip, od_w_glu, od_b_glu, od_w_out):
    bsz, seq, d = x.shape
    depth = ev_norm.shape[0] + od_norm.shape[0]
    x2d = x.reshape(bsz * seq, d)
    bias = _bias_table(rel_bias)
    cos2, sin2 = _rope_tables(seq)
    for layer in range(depth):
        i = layer // 2
        if layer % 2 == 0:
            x2d = _even_layer(x2d, bsz, seq, bias, ev_norm[i], ev_w_in[i], ev_a_q_norm[i],
                              ev_a_kv_norm[i], ev_w_kv_up[i], ev_a_k_norm[i], ev_b_conv[i],
                              ev_b_a_log[i], ev_b_dt_bias[i], ev_b_out_norm[i], ev_w_out[i])
        else:
            x2d = _odd_layer(x2d, bsz, seq, cos2, sin2, od_norm[i], od_w_in[i], od_c_out_norm[i],
                             od_lam_re[i], od_lam_im[i], od_log_dt[i], od_b_re[i], od_b_im[i],
                             od_c_re[i], od_c_im[i], od_d_skip[i], od_w_glu[i], od_b_glu[i], od_w_out[i])
    return x2d.reshape(bsz, seq, d)
```

```python
import functools
import math

import numpy as np
import jax
import jax.numpy as jnp
from jax import lax
from jax.experimental import pallas as pl
from jax.experimental.pallas import tpu as pltpu

F32 = jnp.float32
BF16 = jnp.bfloat16
I32 = jnp.int32

D_MODEL = 1024
EPS = 1e-6
LOG2E = 1.4426950408889634
NEG_INF = float("-inf")
INT_MIN = -(2 ** 31)
HI16 = -(2 ** 16)

A_HEADS, A_HEAD_DIM, A_KV_RANK = 16, 64, 128
A_IDX_HEADS, A_IDX_DIM, A_TOPK_MAX = 8, 64, 256
QB = 128
REL_BUCKETS, REL_MAX_DIST = 32, 128
VT_ROWS = 80
SB = 2 * QB
SCORE_ROWS = 4 * QB
BIAS_ROWS = 5 * QB
MASKED = -(2.0 ** 100)
M_INIT = -(2.0 ** 60)
TIE_WALK_MAX = 8
B_HEADS, B_HEAD_DIM, CONV_K, B_CHUNK = 8, 128, 4, 64
C_HEADS, C_QK_DIM, C_V_DIM, C_CHUNK = 4, 128, 256, 128
ROPE_BASE = 10000.0
D_GROUP, D_STATE, D_GROUPS = 16, 64, 64
D_SETS, D_SET_CH, D_SET_ST = 4, 256, 1024

EV_AQ, EV_AZ, EV_BQ, EV_BK, EV_BV, EV_BZ, EV_QI, EV_CKV, EV_MISC = (
    0, 1024, 2048, 3072, 4096, 5120, 6144, 6656, 6784)
EV_WIDTH = 6912
MISC_KI, MISC_WI, MISC_BA, MISC_BB = 0, 64, 72, 80
OD_CQ, OD_CK, OD_CV, OD_CZ, OD_DU, OD_DZ = 0, 512, 1024, 2048, 3072, 4096
OD_WIDTH = 5120

VMEM_LIMIT = 48 * 1024 * 1024


def _cparams(sem):
    return pltpu.CompilerParams(dimension_semantics=sem, vmem_limit_bytes=VMEM_LIMIT)


def _dot(a, b):
    return jnp.dot(a, b, preferred_element_type=F32)


def _dot_nt(a, b):
    return lax.dot_general(a, b, (((1,), (1,)), ((), ())), preferred_element_type=F32)


def _split_bf16(x, n):
    parts = []
    for _ in range(n):
        p = x.astype(BF16)
        parts.append(p)
        x = x - p.astype(F32)
    return parts


def _dot_sel(sel, x):
    sel = sel.astype(BF16)
    hi, mid, lo = _split_bf16(x, 3)
    return _dot(sel, hi) + (_dot(sel, mid) + _dot(sel, lo))


def _dot_xsel(x, sel):
    sel = sel.astype(BF16)
    hi, mid, lo = _split_bf16(x, 3)
    return _dot(hi, sel) + (_dot(mid, sel) + _dot(lo, sel))


def _dot_x3(a, b):
    ah, al = _split_bf16(a, 2)
    bh, bl = _split_bf16(b, 2)
    return _dot(ah, bh) + (_dot(ah, bl) + _dot(al, bh))


def _sigmoid(x):
    return 1.0 / (1.0 + jnp.exp(-x))


def _silu(x):
    return x * _sigmoid(x)


def _norm_proj_kernel(x_ref, g_ref, w_ref, o_ref, xn_ref):
    @pl.when(pl.program_id(1) == 0)
    def _():
        x = x_ref[...]
        ms = jnp.mean(x * x, axis=-1, keepdims=True)
        xn_ref[...] = (x * lax.rsqrt(ms + EPS) * g_ref[...]).astype(BF16)

    o_ref[...] = _dot(xn_ref[...], w_ref[...])


def _norm_proj(x2d, gain, w_bf16, tn, tm=1024):
    t, d = x2d.shape
    n = w_bf16.shape[1]
    tm = min(tm, t)
    return pl.pallas_call(
        _norm_proj_kernel,
        out_shape=jax.ShapeDtypeStruct((t, n), F32),
        grid=(t // tm, n // tn),
        in_specs=[pl.BlockSpec((tm, d), lambda i, j: (i, 0)),
                  pl.BlockSpec((1, d), lambda i, j: (0, 0)),
                  pl.BlockSpec((d, tn), lambda i, j: (0, j))],
        out_specs=pl.BlockSpec((tm, tn), lambda i, j: (i, j)),
        scratch_shapes=[pltpu.VMEM((tm, d), BF16)],
        compiler_params=_cparams(("parallel", "arbitrary")),
        name="norm_proj",
    )(x2d, gain.reshape(1, d), w_bf16)


def _out_proj_kernel(x_ref, a_ref, b_ref, wa_ref, wb_ref, o_ref):
    o_ref[...] = x_ref[...] + _dot(a_ref[...], wa_ref[...]) + _dot(b_ref[...], wb_ref[...])


def _out_proj(x2d, mix_a, mix_b, w_out, tm=512):
    t, d = x2d.shape
    half = mix_a.shape[1]
    tm = min(tm, t)
    wa = w_out[:half].astype(BF16)
    wb = w_out[half:].astype(BF16)
    return pl.pallas_call(
        _out_proj_kernel,
        out_shape=jax.ShapeDtypeStruct((t, d), F32),
        grid=(t // tm,),
        in_specs=[pl.BlockSpec((tm, d), lambda i: (i, 0)),
                  pl.BlockSpec((tm, half), lambda i: (i, 0)),
                  pl.BlockSpec((tm, half), lambda i: (i, 0)),
                  pl.BlockSpec((half, d), lambda i: (0, 0)),
                  pl.BlockSpec((half, d), lambda i: (0, 0))],
        out_specs=pl.BlockSpec((tm, d), lambda i: (i, 0)),
        compiler_params=_cparams(("parallel",)),
        name="out_proj",
    )(x2d, mix_a, mix_b, wa, wb)


def _t5_bucket_starts():
    exact = REL_BUCKETS // 2
    n = np.arange(0, 4 * REL_MAX_DIST, dtype=np.int64)
    ratio = np.maximum(n, 1).astype(np.float32) / np.float32(exact)
    large = exact + (np.log(ratio).astype(np.float32) / np.float32(math.log(REL_MAX_DIST / exact))
                     * np.float32(REL_BUCKETS - exact)).astype(np.int32)
    bucket = np.where(n < exact, n, np.minimum(large, REL_BUCKETS - 1))
    starts = [int(np.argmax(bucket >= b)) for b in range(REL_BUCKETS)]
    assert all(bucket[s] == b for b, s in enumerate(starts)) and np.all(np.diff(bucket) >= 0)
    assert starts[-1] <= QB, "distances beyond one key block must share the last bucket"
    return starts


_BUCKET_STARTS = _t5_bucket_starts()


def _bias_table_kernel(rb_ref, o_ref):
    row = lax.broadcasted_iota(I32, (BIAS_ROWS, QB), 0)
    lane = lax.broadcasted_iota(I32, (BIAS_ROWS, QB), 1)
    dist = lane + 3 * QB - row
    for h in range(A_HEADS):
        val = jnp.full((BIAS_ROWS, QB), rb_ref[0, h], F32)
        for b in range(1, REL_BUCKETS):
            val = jnp.where(dist >= _BUCKET_STARTS[b], rb_ref[b, h], val)
        val = (val - rb_ref[REL_BUCKETS - 1, h]) * LOG2E
        o_ref[:, h * QB:(h + 1) * QB] = jnp.where(dist >= 0, val, 0.0)


def _bias_table(rel_bias):
    return pl.pallas_call(
        _bias_table_kernel,
        out_shape=jax.ShapeDtypeStruct((BIAS_ROWS, A_HEADS * QB), F32),
        in_specs=[pl.BlockSpec(memory_space=pltpu.SMEM)],
        out_specs=pl.BlockSpec(memory_space=pltpu.VMEM),
        name="dsa_bias_table",
    )(rel_bias)


def _dsa_prep_kernel(aq_ref, qi_ref, ckv_ref, misc_ref, gq_ref, gkv_ref, wkv_ref, gk_ref,
                     qta_ref, qit_ref, wt_ref, kidx_ref, katt_ref, vt_ref):
    zeros = jnp.zeros((QB, A_HEAD_DIM), F32)
    aq = aq_ref[...]
    gq = gq_ref[...]
    ident = jnp.where(lax.broadcasted_iota(I32, (QB, QB), 0) == lax.broadcasted_iota(I32, (QB, QB), 1),
                      1.0, 0.0).astype(BF16)
    for h in range(A_HEADS):
        q = aq[:, h * A_HEAD_DIM:(h + 1) * A_HEAD_DIM]
        ms = jnp.mean(q * q, axis=-1, keepdims=True)
        qn = q * lax.rsqrt(ms + EPS) * gq * (A_HEAD_DIM ** -0.5 * LOG2E)
        qta_ref[0:QB, h * QB:(h + 1) * QB] = jnp.concatenate([qn, zeros], axis=1).T.astype(BF16)
        qta_ref[QB:2 * QB, h * QB:(h + 1) * QB] = ident
    qi = qi_ref[...]
    for h in range(A_IDX_HEADS):
        qih = qi[:, h * A_IDX_DIM:(h + 1) * A_IDX_DIM]
        qit_ref[:, h * QB:(h + 1) * QB] = jnp.concatenate([qih, zeros], axis=1).T.astype(BF16)
    misc = misc_ref[...]
    wt_ref[...] = misc.T[MISC_WI:MISC_WI + A_IDX_HEADS, :] * (A_IDX_HEADS ** -0.5 * A_IDX_DIM ** -0.5)
    c = ckv_ref[...]
    cn = c * lax.rsqrt(jnp.mean(c * c, axis=-1, keepdims=True) + EPS) * gkv_ref[...]
    kv = _dot(cn.astype(BF16), wkv_ref[...])
    k = kv[:, :A_HEAD_DIM]
    kn = k * lax.rsqrt(jnp.mean(k * k, axis=-1, keepdims=True) + EPS) * gk_ref[...]
    kidx_ref[...] = jnp.concatenate([misc[:, MISC_KI:MISC_KI + A_IDX_DIM], zeros], axis=1).astype(BF16)
    katt_ref[...] = jnp.concatenate([kn, zeros], axis=1).astype(BF16)
    kvt = kv.T
    vt_ref[0:A_HEAD_DIM, :] = kvt[A_HEAD_DIM:, :].astype(BF16)
    ones_row = lax.broadcasted_iota(I32, (VT_ROWS - A_HEAD_DIM, QB), 0) == 0
    vt_ref[A_HEAD_DIM:, :] = jnp.where(ones_row, 1.0, 0.0).astype(BF16)


def _dsa_prep(h, nblk, a_q_norm, a_kv_norm, w_kv_up, a_k_norm):
    cb = lambda width, off: off // width
    return pl.pallas_call(
        _dsa_prep_kernel,
        out_shape=(jax.ShapeDtypeStruct((nblk, 2 * QB, A_HEADS * QB), BF16),
                   jax.ShapeDtypeStruct((nblk, QB, A_IDX_HEADS * QB), BF16),
                   jax.ShapeDtypeStruct((nblk, A_IDX_HEADS, QB), F32),
                   jax.ShapeDtypeStruct((nblk, QB, 128), BF16),
                   jax.ShapeDtypeStruct((nblk, QB, 128), BF16),
                   jax.ShapeDtypeStruct((nblk, VT_ROWS, QB), BF16)),
        grid=(nblk,),
        in_specs=[pl.BlockSpec((QB, 1024), lambda i: (i, cb(1024, EV_AQ))),
                  pl.BlockSpec((QB, 512), lambda i: (i, cb(512, EV_QI))),
                  pl.BlockSpec((QB, 128), lambda i: (i, cb(128, EV_CKV))),
                  pl.BlockSpec((QB, 128), lambda i: (i, cb(128, EV_MISC))),
                  pl.BlockSpec((1, A_HEAD_DIM), lambda i: (0, 0)),
                  pl.BlockSpec((1, A_KV_RANK), lambda i: (0, 0)),
                  pl.BlockSpec((A_KV_RANK, 2 * A_HEAD_DIM), lambda i: (0, 0)),
                  pl.BlockSpec((1, A_HEAD_DIM), lambda i: (0, 0))],
        out_specs=(pl.BlockSpec((None, 2 * QB, A_HEADS * QB), lambda i: (i, 0, 0)),
                   pl.BlockSpec((None, QB, A_IDX_HEADS * QB), lambda i: (i, 0, 0)),
                   pl.BlockSpec((None, A_IDX_HEADS, QB), lambda i: (i, 0, 0)),
                   pl.BlockSpec((None, QB, 128), lambda i: (i, 0, 0)),
                   pl.BlockSpec((None, QB, 128), lambda i: (i, 0, 0)),
                   pl.BlockSpec((None, VT_ROWS, QB), lambda i: (i, 0, 0))),
        compiler_params=_cparams(("parallel",)),
        name="dsa_prep",
    )(h, h, h, h, a_q_norm.reshape(1, -1), a_kv_norm.reshape(1, -1), w_kv_up.astype(BF16),
      a_k_norm.reshape(1, -1))


def _dsa_kernel(topk, qit_ref, wt_ref, qta_ref, kidx_ref, katt_ref, vt_ref, bias_ref, az_ref, o_ref,
                strip_ref, hi_ref, x_ref, m_ref, acc_ref, lg_ref, mx_ref, p_ref):
    qt = pl.program_id(1)
    t0 = qt * QB
    n_sc = qt // 4 + 1
    n_sb = qt // 2 + 1
    lane_s = lax.broadcasted_iota(I32, (SCORE_ROWS, QB), 1)
    row_s = lax.broadcasted_iota(I32, (SCORE_ROWS, QB), 0)

    w = wt_ref[...]

    def score_body(j, carry):
        kblk = kidx_ref[pl.ds(j * 4, 4)].reshape(SCORE_ROWS, 128)
        tot = None
        for hp in range(A_IDX_HEADS // 2):
            s = _dot(kblk, qit_ref[:, hp * SB:(hp + 1) * SB])
            for e in range(2):
                h = 2 * hp + e
                term = jnp.maximum(s[:, e * QB:(e + 1) * QB], 0.0) * w[h:h + 1, :]
                tot = term if tot is None else tot + term
        tot = jnp.where(j * SCORE_ROWS + row_s <= t0 + lane_s, tot, NEG_INF)
        bits = pltpu.bitcast(tot, I32)
        key = bits ^ ((bits >> 31) & 0x7FFFFFFF)
        rows = pl.ds(pl.multiple_of(j * SCORE_ROWS, SCORE_ROWS), SCORE_ROWS)
        strip_ref[rows, :] = key
        hi_ref[rows, :] = (key >> 16).astype(jnp.int16)
        return carry

    lax.fori_loop(0, n_sc, score_body, 0)

    def count(pred):
        def body(j, acc):
            r0 = pl.multiple_of(j * SCORE_ROWS, SCORE_ROWS)
            m = jnp.where(pred(strip_ref[pl.ds(r0, SCORE_ROWS), :], r0), 1, 0)
            return acc + m.reshape(SCORE_ROWS // 32, 32, QB).sum(axis=0)
        acc = lax.fori_loop(0, n_sc, body, jnp.zeros((32, QB), I32))
        return acc.reshape(4, 8, QB).sum(axis=0).sum(axis=0, keepdims=True)

    one_h = jnp.ones((), jnp.int16)
    zero_h = jnp.zeros((), jnp.int16)

    def count_hi(cand):
        cand_h = (cand >> 16).astype(jnp.int16)

        def body(j, acc):
            r0 = pl.multiple_of(j * SCORE_ROWS, SCORE_ROWS)
            m = jnp.where(hi_ref[pl.ds(r0, SCORE_ROWS), :] >= cand_h, one_h, zero_h)
            parts = [m[r:r + 32, :] for r in range(0, SCORE_ROWS, 32)]
            while len(parts) > 1:
                parts = [parts[i] + parts[i + 1] for i in range(0, len(parts), 2)]
            return acc + parts[0]
        acc = lax.fori_loop(0, n_sc, body, jnp.zeros((32, QB), jnp.int16))
        return acc.astype(I32).reshape(4, 8, QB).sum(axis=0).sum(axis=0, keepdims=True)

    c0 = count_hi(jnp.zeros((1, QB), I32))
    nonneg = c0 >= topk
    thr0 = jnp.where(nonneg, 0, INT_MIN)
    cnt0 = jnp.where(nonneg, c0, n_sc * SCORE_ROWS)

    def bit_steps(counter, n):
        def one(_, carry):
            i, thr, cnt = carry
            cand = thr | lax.shift_left(jnp.int32(1), 30 - i)
            c = counter(cand)
            ok = c >= topk
            return i + 1, jnp.where(ok, cand, thr), jnp.where(ok, c, cnt)
        return lambda carry: lax.fori_loop(0, n, one, carry)

    state = bit_steps(count_hi, 15)((jnp.int32(0), thr0, cnt0))
    _, thr, cnt = lax.while_loop(lambda c: (c[0] < 31) & (jnp.max(c[2]) > topk),
                                 bit_steps(lambda cand: count(lambda key, r0: key >= cand), 4), state)

    x_ref[...] = jnp.full((1, QB), 2 ** 30, I32)
    tied = cnt > topk

    @pl.when(jnp.max(cnt) > topk)
    def _():
        above = count(lambda key, r0: key > thr)
        need = jnp.where(tied, topk - above, 0)
        max_need = jnp.max(need)

        @pl.when(max_need <= TIE_WALK_MAX)
        def _():
            def next_tie(prev):
                def body(j, acc):
                    r0 = pl.multiple_of(j * SCORE_ROWS, SCORE_ROWS)
                    idx = r0 + row_s
                    hit = (strip_ref[pl.ds(r0, SCORE_ROWS), :] == thr) & (idx > prev)
                    return jnp.minimum(acc, jnp.where(hit, idx, 2 ** 30)
                                       .reshape(SCORE_ROWS // 32, 32, QB).min(axis=0))
                acc = lax.fori_loop(0, n_sc, body, jnp.full((32, QB), 2 ** 30, I32))
                return acc.reshape(4, 8, QB).min(axis=0).min(axis=0, keepdims=True)

            last = lax.fori_loop(0, max_need, lambda r, prev: jnp.where(r < need, next_tie(prev), prev),
                                 jnp.full((1, QB), -1, I32))
            x_ref[...] = jnp.where(tied, last + 1, 2 ** 30)

        @pl.when(max_need > TIE_WALK_MAX)
        def _():
            def tie_body(i, xb):
                cand = xb | lax.shift_left(jnp.int32(1), 14 - i)
                c = count(lambda key, r0: (key == thr) & (r0 + row_s < cand))
                return jnp.where(c <= need, cand, xb)

            xb = lax.fori_loop(0, 15, tie_body, jnp.zeros((1, QB), I32))
            x_ref[...] = jnp.where(tied, xb, 2 ** 30)

    xb = x_ref[...]

    m_ref[...] = jnp.full(m_ref.shape, M_INIT, F32)
    acc_ref[...] = jnp.zeros(acc_ref.shape, F32)
    n_lt = A_HEADS * QB // SB
    row_a = lax.broadcasted_iota(I32, (SB, QB), 0)
    lane_a = lax.broadcasted_iota(I32, (SB, QB), 1)

    def logits(j, slot, near):
        r0 = pl.multiple_of(j * SB, SB)
        key = strip_ref[pl.ds(r0, SB), :]
        idx = r0 + row_a
        sel = (key > thr) | ((key == thr) & (idx < xb))
        if near:
            sel = sel & (idx <= t0 + lane_a)
            b0 = pl.multiple_of((2 * j - qt + 3) * QB, QB)
        pen = jnp.where(sel, 0.0, MASKED).astype(BF16)
        kaug = jnp.concatenate([katt_ref[pl.ds(2 * j, 2)].reshape(SB, 128), pen], axis=1)
        for lt in range(n_lt):
            cols = slice(lt * SB, (lt + 1) * SB)
            lg = _dot(kaug, qta_ref[:, cols])
            if near:
                lg = lg + bias_ref[pl.ds(b0, SB), cols]
            lg = lg.astype(BF16)
            lg_ref[slot, :, cols] = lg
            mx_ref[slot, :, cols] = jnp.max(lg, axis=0, keepdims=True).astype(F32)

    def accumulate(j, slot):
        vt = jnp.concatenate([vt_ref[2 * j], vt_ref[2 * j + 1]], axis=1)
        m_old = m_ref[...]
        m_new = jnp.maximum(m_old, mx_ref[slot])
        m_ref[...] = m_new
        m_b = m_new.astype(BF16)
        for lt in range(n_lt):
            cols = slice(lt * SB, (lt + 1) * SB)
            p_ref[:, cols] = jnp.exp2(lg_ref[slot, :, cols] - m_b[:, cols])
        acc_ref[...] = jnp.exp2(m_old - m_new) * acc_ref[...] + _dot(vt, p_ref[...])

    n_far = jnp.maximum(n_sb - 2, 0)
    n_pair = jnp.maximum(n_far - 1, 0) // 2
    n_rem = n_far - 2 * n_pair

    @pl.when(n_far > 0)
    def _():
        logits(0, 0, False)

    def pair_body(jj, carry):
        j = 2 * jj
        logits(j + 1, 1, False)
        accumulate(j, 0)
        logits(j + 2, 0, False)
        accumulate(j + 1, 1)
        return carry

    lax.fori_loop(0, n_pair, pair_body, 0)

    @pl.when(n_rem == 1)
    def _():
        accumulate(n_far - 1, 0)

    @pl.when(n_rem == 2)
    def _():
        logits(n_far - 1, 1, False)
        accumulate(n_far - 2, 0)
        accumulate(n_far - 1, 1)

    @pl.when(n_sb >= 2)
    def _():
        logits(n_sb - 2, 0, True)
        accumulate(n_sb - 2, 0)

    logits(n_sb - 1, 0, True)
    accumulate(n_sb - 1, 0)

    acc = acc_ref[...]
    o_t = acc[0:A_HEAD_DIM, :] / acc[A_HEAD_DIM:A_HEAD_DIM + 1, :]
    pad = jnp.zeros((QB - A_HEAD_DIM, QB), F32)
    pieces = []
    for h in range(A_HEADS):
        blk = jnp.concatenate([o_t[:, h * QB:(h + 1) * QB], pad], axis=0)
        pieces.append(blk.T[:, 0:A_HEAD_DIM])
    att = jnp.concatenate(pieces, axis=1)
    o_ref[...] = (att * _silu(az_ref[...])).astype(BF16)


def _dsa_attend(h, bsz, seq, qta, qit, wt, kidx, katt, vt, bias):
    nq = seq // QB
    topk = min(A_TOPK_MAX, seq // 4)
    blk = lambda b, q: (b * nq + q, 0, 0)
    per_batch = lambda b, q: (b, 0, 0)
    once = pl.Buffered(1)
    return pl.pallas_call(
        functools.partial(_dsa_kernel, topk),
        out_shape=jax.ShapeDtypeStruct((bsz * seq, A_HEADS * A_HEAD_DIM), BF16),
        grid=(bsz, nq),
        in_specs=[pl.BlockSpec((None, QB, A_IDX_HEADS * QB), blk),
                  pl.BlockSpec((None, A_IDX_HEADS, QB), blk),
                  pl.BlockSpec((None, 2 * QB, A_HEADS * QB), blk),
                  pl.BlockSpec((nq, QB, 128), per_batch, pipeline_mode=once),
                  pl.BlockSpec((nq, QB, 128), per_batch, pipeline_mode=once),
                  pl.BlockSpec((nq, VT_ROWS, QB), per_batch, pipeline_mode=once),
                  pl.BlockSpec((BIAS_ROWS, A_HEADS * QB), lambda b, q: (0, 0), pipeline_mode=once),
                  pl.BlockSpec((QB, 1024), lambda b, q: (b * nq + q, EV_AZ // 1024))],
        out_specs=pl.BlockSpec((QB, 1024), lambda b, q: (b * nq + q, 0)),
        scratch_shapes=[pltpu.VMEM((seq, QB), I32),
                        pltpu.VMEM((seq, QB), jnp.int16),
                        pltpu.VMEM((1, QB), I32),
                        pltpu.VMEM((1, A_HEADS * QB), F32),
                        pltpu.VMEM((VT_ROWS, A_HEADS * QB), F32),
                        pltpu.VMEM((2, SB, A_HEADS * QB), BF16),
                        pltpu.VMEM((2, 1, A_HEADS * QB), F32),
                        pltpu.VMEM((SB, A_HEADS * QB), BF16)],
        compiler_params=_cparams(("arbitrary", "arbitrary")),
        name="dsa_attend",
    )(qit, wt, qta, kidx, katt, vt, bias, h)


GDN_TM = 512
GDN_GROUP = 4
GDN_SCAN_HEADS = 4


def _gdn_prep_kernel(alog_ref, dtb_ref, q_ref, k_ref, v_ref, qh_ref, kh_ref, vh_ref, misc_ref,
                     cq_ref, ck_ref, cv_ref,
                     u_ref, w_ref, qd_ref, kdt_ref, attn_ref, egl_ref):
    i = pl.program_id(1)
    hd = pl.program_id(2)
    tm = q_ref.shape[0]
    nchunk = tm // B_CHUNK

    def conv_silu(x_ref, halo_ref, c_ref):
        halo = jnp.where(i > 0, halo_ref[...], 0.0)
        ext = jnp.concatenate([halo, x_ref[...]], axis=0)
        cw = c_ref[...]
        y = ext[8:, :] * cw[CONV_K - 1:CONV_K, :]
        for d in range(1, CONV_K):
            y = y + pltpu.roll(ext, d, 0)[8:, :] * cw[CONV_K - 1 - d:CONV_K - d, :]
        return _silu(y)

    q = conv_silu(q_ref, qh_ref, cq_ref)
    k = conv_silu(k_ref, kh_ref, ck_ref)
    v = conv_silu(v_ref, vh_ref, cv_ref)
    qn = q * lax.rsqrt(jnp.sum(q * q, axis=-1, keepdims=True) + EPS) * (B_HEAD_DIM ** -0.5)
    kn = k * lax.rsqrt(jnp.sum(k * k, axis=-1, keepdims=True) + EPS)

    misc = misc_ref[...]
    sel_r = lax.broadcasted_iota(I32, (128, 256), 0)
    sel_c = lax.broadcasted_iota(I32, (128, 256), 1)
    pick = jnp.where(sel_r == jnp.where(sel_c < 128, MISC_BA + hd, MISC_BB + hd), 1.0, 0.0)
    bab = _dot_xsel(misc, pick)
    ba = bab[:, 0:128]
    bb = bab[:, 128:256]
    beta = _sigmoid(bb)
    xg = ba + dtb_ref[hd]
    softplus = jnp.maximum(xg, 0.0) + jnp.log(1.0 + jnp.exp(-jnp.abs(xg)))
    g = -jnp.exp(jnp.full((1, 128), alog_ref[hd], F32)) * softplus

    sl = 256
    r2 = lax.broadcasted_iota(I32, (sl, sl), 0)
    c2 = lax.broadcasted_iota(I32, (sl, sl), 1)
    same = (r2 // B_CHUNK) == (c2 // B_CHUNK)
    tri = jnp.where(same & (c2 <= r2), 1.0, 0.0)
    blk = jnp.where(same, 1.0, 0.0)
    nsl = tm // sl
    both = _dot_sel(jnp.concatenate([tri, blk], axis=0),
                    jnp.concatenate([g[s * sl:(s + 1) * sl, :] for s in range(nsl)], axis=1))
    gc = jnp.concatenate([both[0:sl, s * 128:(s + 1) * 128] for s in range(nsl)], axis=0)
    gl = jnp.concatenate([both[sl:2 * sl, s * 128:(s + 1) * 128] for s in range(nsl)], axis=0)

    kb = kn * beta
    vb = v * beta
    egc = jnp.exp(gc)
    qd_ref[...] = (qn * egc).astype(BF16)
    kd = kn * jnp.exp(gl - gc)
    for m in range(tm // 128):
        kdt_ref[:, m * 128:(m + 1) * 128] = kd[m * 128:(m + 1) * 128, :].T.astype(BF16)
    kbg = kb * egc

    gw = GDN_GROUP * B_CHUNK
    ri = lax.broadcasted_iota(I32, (B_CHUNK, gw), 0)
    lj = lax.broadcasted_iota(I32, (B_CHUNK, gw), 1)
    lb = lj // B_CHUNK
    lj = lj % B_CHUNK
    bmask = (lax.broadcasted_iota(I32, (gw, gw), 0) // B_CHUNK
             == lax.broadcasted_iota(I32, (gw, gw), 1) // B_CHUNK)

    def fold(x):
        xm = jnp.where(bmask, x, 0.0)
        out = xm[0:B_CHUNK, :]
        for c in range(1, GDN_GROUP):
            out = out + xm[c * B_CHUNK:(c + 1) * B_CHUNK, :]
        return out

    def bdiag(cat):
        return jnp.where(bmask, jnp.concatenate([cat] * GDN_GROUP, axis=0), 0.0)

    for grp in range(tm // gw):
        rs = slice(grp * gw, (grp + 1) * gw)
        gcs = gc[rs, :]
        col = jnp.concatenate([gcs[0:B_CHUNK, :]] * 2, axis=1)
        for c in range(1, GDN_GROUP):
            col = jnp.where(lb == c, jnp.concatenate([gcs[c * B_CHUNK:(c + 1) * B_CHUNK, :]] * 2, axis=1), col)
        rowf = jnp.concatenate([gcs[m * 128:(m + 1) * 128, :].T[0:B_CHUNK, :] for m in range(gw // 128)],
                               axis=1)
        decay = jnp.exp(jnp.where(ri >= lj, col - rowf, NEG_INF))
        kg = kn[rs, :]
        a_cat = jnp.where(ri > lj, fold(_dot_nt(kb[rs, :], kg)) * decay, 0.0)
        attn_ref[grp] = jnp.where(ri >= lj, fold(_dot_nt(qn[rs, :], kg)) * decay, 0.0).astype(BF16)
        inv = jnp.where(ri == lj, 1.0, 0.0) - a_cat
        pw = _dot_x3(a_cat, bdiag(a_cat))
        for step in range(5):
            if step < 4:
                both = _dot_x3(jnp.concatenate([inv, pw], axis=0), bdiag(pw))
                inv = inv + both[0:B_CHUNK, :]
                pw = both[B_CHUNK:, :]
            else:
                inv = inv + _dot_x3(inv, bdiag(pw))
        sol = _dot_x3(bdiag(inv), jnp.concatenate([vb[rs, :], kbg[rs, :]], axis=1))
        u_ref[rs, :] = sol[:, 0:B_HEAD_DIM]
        w_ref[rs, :] = sol[:, B_HEAD_DIM:].astype(BF16)
    for c in range(nchunk):
        egl_ref[c:c + 1, :] = jnp.exp(gl[c * B_CHUNK:c * B_CHUNK + 1, :])
    if nchunk < 8:
        egl_ref[nchunk:, :] = jnp.zeros((8 - nchunk, 128), F32)


def _gdn_prep(h, bsz, seq, b_conv, b_a_log, b_dt_bias):
    tm = min(GDN_TM, seq)
    nt = seq // tm
    nrow = max(tm // B_CHUNK, 8)
    gw = GDN_GROUP * B_CHUNK
    hs = (bsz, B_HEADS, seq, B_HEAD_DIM)
    cur = lambda off: (lambda b, i, hd: (b * nt + i, off // 128 + hd))
    halo = lambda off: (lambda b, i, hd: (jnp.maximum((b * nt + i) * (tm // 8) - 1, 0), off // 128 + hd))
    cw = lambda off: (lambda b, i, hd: (0, off // 128 + hd))
    out = lambda b, i, hd: (b, hd, i, 0)
    smem = pl.BlockSpec(memory_space=pltpu.SMEM)
    return pl.pallas_call(
        _gdn_prep_kernel,
        out_shape=(jax.ShapeDtypeStruct(hs, F32), jax.ShapeDtypeStruct(hs, BF16),
                   jax.ShapeDtypeStruct(hs, BF16),
                   jax.ShapeDtypeStruct((bsz, B_HEADS, B_HEAD_DIM, seq), BF16),
                   jax.ShapeDtypeStruct((bsz, B_HEADS, seq // gw, B_CHUNK, gw), BF16),
                   jax.ShapeDtypeStruct((bsz, B_HEADS, nt * nrow, 128), F32)),
        grid=(bsz, nt, B_HEADS),
        in_specs=[smem, smem,
                  pl.BlockSpec((tm, 128), cur(EV_BQ)), pl.BlockSpec((tm, 128), cur(EV_BK)),
                  pl.BlockSpec((tm, 128), cur(EV_BV)),
                  pl.BlockSpec((8, 128), halo(EV_BQ)), pl.BlockSpec((8, 128), halo(EV_BK)),
                  pl.BlockSpec((8, 128), halo(EV_BV)),
                  pl.BlockSpec((tm, 128), lambda b, i, hd: (b * nt + i, EV_MISC // 128)),
                  pl.BlockSpec((CONV_K, 128), cw(0)), pl.BlockSpec((CONV_K, 128), cw(1024)),
                  pl.BlockSpec((CONV_K, 128), cw(2048))],
        out_specs=(pl.BlockSpec((None, None, tm, 128), out), pl.BlockSpec((None, None, tm, 128), out),
                   pl.BlockSpec((None, None, tm, 128), out),
                   pl.BlockSpec((None, None, B_HEAD_DIM, tm), lambda b, i, hd: (b, hd, 0, i)),
                   pl.BlockSpec((None, None, tm // gw, B_CHUNK, gw), lambda b, i, hd: (b, hd, i, 0, 0)),
                   pl.BlockSpec((None, None, nrow, 128), out)),
        compiler_params=_cparams(("parallel", "parallel", "parallel")),
        name="gdn_prep",
    )(b_a_log, b_dt_bias, h, h, h, h, h, h, h, b_conv, b_conv, b_conv)


def _gdn_scan_kernel(u_ref, w_ref, qd_ref, kdt_ref, attn_ref, egl_ref, z_ref, gn_ref, o_ref, s_ref):
    @pl.when(pl.program_id(2) == 0)
    def _():
        s_ref[...] = jnp.zeros(s_ref.shape, F32)

    tm = u_ref.shape[1]
    gw = GDN_GROUP * B_CHUNK
    gain = gn_ref[...]
    zero = jnp.zeros((B_CHUNK, B_HEAD_DIM), BF16)
    states = [s_ref[hh] for hh in range(GDN_SCAN_HEADS)]
    for c in range(tm // B_CHUNK):
        grp, ci = divmod(c, GDN_GROUP)
        rs = slice(c * B_CHUNK, (c + 1) * B_CHUNK)
        for hh in range(GDN_SCAN_HEADS):
            sb = states[hh].astype(BF16)
            v_new = u_ref[hh, rs, :] - _dot(w_ref[hh, rs, :], sb)
            vpad = jnp.concatenate([zero] * ci + [v_new.astype(BF16)] + [zero] * (GDN_GROUP - 1 - ci), axis=0)
            o = _dot(qd_ref[hh, rs, :], sb) + _dot(attn_ref[hh, grp], vpad)
            states[hh] = (states[hh] * egl_ref[hh, c:c + 1, :]
                          + _dot(kdt_ref[hh, :, grp * gw:(grp + 1) * gw], vpad))
            on = o * lax.rsqrt(jnp.mean(o * o, axis=-1, keepdims=True) + EPS) * gain
            lanes = slice(hh * B_HEAD_DIM, (hh + 1) * B_HEAD_DIM)
            o_ref[rs, lanes] = (on * _silu(z_ref[rs, lanes])).astype(BF16)
    for hh in range(GDN_SCAN_HEADS):
        s_ref[hh] = states[hh]


def _gdn_scan(h, bsz, seq, u, w, qd, kd, attn, egl, b_out_norm):
    tm = min(GDN_TM, seq)
    nt = seq // tm
    nrow = max(tm // B_CHUNK, 8)
    gw = GDN_GROUP * B_CHUNK
    hg = GDN_SCAN_HEADS
    blk = lambda b, hp, i: (b, hp, i, 0)
    return pl.pallas_call(
        _gdn_scan_kernel,
        out_shape=jax.ShapeDtypeStruct((bsz * seq, B_HEADS * B_HEAD_DIM), BF16),
        grid=(bsz, B_HEADS // hg, nt),
        in_specs=[pl.BlockSpec((None, hg, tm, 128), blk), pl.BlockSpec((None, hg, tm, 128), blk),
                  pl.BlockSpec((None, hg, tm, 128), blk),
                  pl.BlockSpec((None, hg, B_HEAD_DIM, tm), lambda b, hp, i: (b, hp, 0, i)),
                  pl.BlockSpec((None, hg, tm // gw, B_CHUNK, gw), lambda b, hp, i: (b, hp, i, 0, 0)),
                  pl.BlockSpec((None, hg, nrow, 128), blk),
                  pl.BlockSpec((tm, hg * 128), lambda b, hp, i: (b * nt + i, EV_BZ // (hg * 128) + hp)),
                  pl.BlockSpec((1, 128), lambda b, hp, i: (0, 0))],
        out_specs=pl.BlockSpec((tm, hg * 128), lambda b, hp, i: (b * nt + i, hp)),
        scratch_shapes=[pltpu.VMEM((hg, B_HEAD_DIM, B_HEAD_DIM), F32)],
        compiler_params=_cparams(("parallel", "parallel", "arbitrary")),
        name="gdn_scan",
    )(u, w, qd, kd, attn, egl, h, b_out_norm.reshape(1, -1))


def _rope_kernel(inv_ref, cos_ref, sin_ref):
    tm = cos_ref.shape[0]
    pos = (pl.program_id(0) * tm + lax.broadcasted_iota(I32, (tm, 128), 0)).astype(F32)
    ang = pos * inv_ref[...]
    lane = lax.broadcasted_iota(I32, (tm, 128), 1)
    cos_ref[...] = jnp.cos(ang)
    sin_ref[...] = jnp.where(lane < 64, -jnp.sin(ang), jnp.sin(ang))


def _rope_tables(seq):
    half = C_QK_DIM // 2
    inv = 1.0 / (ROPE_BASE ** jnp.linspace(0.0, 1.0, half, dtype=F32))
    inv2 = jnp.concatenate([inv, inv]).reshape(1, C_QK_DIM)
    tm = min(1024, seq)
    return pl.pallas_call(
        _rope_kernel,
        out_shape=(jax.ShapeDtypeStruct((seq, C_QK_DIM), F32),) * 2,
        grid=(seq // tm,),
        in_specs=[pl.BlockSpec((1, C_QK_DIM), lambda i: (0, 0))],
        out_specs=(pl.BlockSpec((tm, C_QK_DIM), lambda i: (i, 0)),) * 2,
        compiler_params=_cparams(("parallel",)),
        name="rope_tables",
    )(inv2)


RET_TM = 512


def _retention_kernel(lg_ref, q_ref, k_ref, v_ref, z_ref, cos_ref, sin_ref, gn_ref, o_ref,
                      r_ref, dm_ref, zeta_ref, xi_ref):
    hd = pl.program_id(1)
    lg = lg_ref[0, hd]

    @pl.when(pl.program_id(2) == 0)
    def _():
        r_ref[...] = jnp.zeros(r_ref.shape, F32)
        ri = lax.broadcasted_iota(I32, (C_CHUNK, C_CHUNK), 0)
        ci = lax.broadcasted_iota(I32, (C_CHUNK, C_CHUNK), 1)
        diff = (ri - ci).astype(F32)
        dm_ref[...] = jnp.where(diff >= 0, jnp.exp(jnp.maximum(diff, 0.0) * lg), 0.0)
        zeta_ref[...] = jnp.exp((C_CHUNK - 1 - ri).astype(F32) * lg)
        rv = lax.broadcasted_iota(I32, (C_CHUNK, C_V_DIM), 0).astype(F32)
        xi_ref[...] = jnp.exp((rv + 1.0) * lg)

    g_chunk = lg_ref[1, hd]
    tm = q_ref.shape[0]
    state = r_ref[...]
    dmask = dm_ref[...]
    gain = gn_ref[...]
    for c in range(tm // C_CHUNK):
        rs = slice(c * C_CHUNK, (c + 1) * C_CHUNK)
        cos = cos_ref[rs, :]
        sin = sin_ref[rs, :]
        q = q_ref[rs, :]
        k = k_ref[rs, :]
        qr = q * cos + pltpu.roll(q, C_QK_DIM // 2, 1) * sin
        kr = (k * cos + pltpu.roll(k, C_QK_DIM // 2, 1) * sin) * (C_QK_DIM ** -0.5)
        vb = v_ref[rs, :].astype(BF16)
        qb = qr.astype(BF16)
        s = _dot_nt(qb, kr.astype(BF16)) * dmask
        o = _dot(s.astype(BF16), vb) + _dot(qb, state.astype(BF16)) * xi_ref[...]
        state = state * g_chunk + _dot((kr * zeta_ref[...]).T.astype(BF16), vb)
        mu = jnp.mean(o, axis=-1, keepdims=True)
        oc = o - mu
        var = jnp.mean(oc * oc, axis=-1, keepdims=True)
        y = oc * lax.rsqrt(var + EPS) * gain
        o_ref[rs, :] = (y * _silu(z_ref[rs, :])).astype(BF16)
    r_ref[...] = state


def _retention(h, bsz, seq, cos2, sin2, c_out_norm):
    tm = min(RET_TM, seq)
    nt = seq // tm
    lg = np.log1p(-np.exp2(-5.0 - np.arange(C_HEADS, dtype=np.float32))).astype(np.float32)
    lg = np.stack([lg, np.exp(np.float32(C_CHUNK) * lg).astype(np.float32)])
    return pl.pallas_call(
        _retention_kernel,
        out_shape=jax.ShapeDtypeStruct((bsz * seq, C_HEADS * C_V_DIM), BF16),
        grid=(bsz, C_HEADS, nt),
        in_specs=[pl.BlockSpec(memory_space=pltpu.SMEM),
                  pl.BlockSpec((tm, 128), lambda b, hd, i: (b * nt + i, OD_CQ // 128 + hd)),
                  pl.BlockSpec((tm, 128), lambda b, hd, i: (b * nt + i, OD_CK // 128 + hd)),
                  pl.BlockSpec((tm, 256), lambda b, hd, i: (b * nt + i, OD_CV // 256 + hd)),
                  pl.BlockSpec((tm, 256), lambda b, hd, i: (b * nt + i, OD_CZ // 256 + hd)),
                  pl.BlockSpec((tm, 128), lambda b, hd, i: (i, 0)),
                  pl.BlockSpec((tm, 128), lambda b, hd, i: (i, 0)),
                  pl.BlockSpec((1, C_V_DIM), lambda b, hd, i: (0, hd))],
        out_specs=pl.BlockSpec((tm, C_V_DIM), lambda b, hd, i: (b * nt + i, hd)),
        scratch_shapes=[pltpu.VMEM((C_QK_DIM, C_V_DIM), F32),
                        pltpu.VMEM((C_CHUNK, C_CHUNK), F32),
                        pltpu.VMEM((C_CHUNK, C_QK_DIM), F32),
                        pltpu.VMEM((C_CHUNK, C_V_DIM), F32)],
        compiler_params=_cparams(("parallel", "parallel", "arbitrary")),
        name="retention",
    )(jnp.asarray(lg), h, h, h, h, cos2, sin2, c_out_norm.reshape(1, -1))


def _s5_param_kernel(lre_ref, lim_ref, ldt_ref, lrex_ref, limx_ref, ldtx_ref, bre_ref, bim_ref,
                     are_ref, aim_ref, bbre_ref, bbim_ref):
    def disc(lre, lim, ldt):
        lr = jnp.minimum(lre, -1e-4)
        dt = jnp.exp(ldt)
        mag = jnp.exp(lr * dt)
        return lr, lim, mag * jnp.cos(lim * dt), mag * jnp.sin(lim * dt)

    _, _, a_re, a_im = disc(lre_ref[...], lim_ref[...], ldt_ref[...])
    are_ref[...] = a_re
    aim_ref[...] = a_im
    lr, li, ax_re, ax_im = disc(lrex_ref[...], limx_ref[...], ldtx_ref[...])
    den = lr * lr + li * li
    f_re = ((ax_re - 1.0) * lr + ax_im * li) / den
    f_im = (ax_im * lr - (ax_re - 1.0) * li) / den
    bbre_ref[...] = f_re * bre_ref[...] - f_im * bim_ref[...]
    bbim_ref[...] = f_re * bim_ref[...] + f_im * bre_ref[...]


def _s5_params(lam_re, lam_im, log_dt, b_re, b_im, c_re, c_im):
    g, p, ch = D_GROUPS, D_STATE, D_GROUP
    ldt = jnp.broadcast_to(log_dt[:, None], (g, p))
    rep = lambda a: jnp.repeat(a, ch, axis=1)
    vm = pl.BlockSpec(memory_space=pltpu.VMEM)
    a_re, a_im, bb_re, bb_im = pl.pallas_call(
        _s5_param_kernel,
        out_shape=(jax.ShapeDtypeStruct((g, p), F32), jax.ShapeDtypeStruct((g, p), F32),
                   jax.ShapeDtypeStruct((g, p * ch), F32), jax.ShapeDtypeStruct((g, p * ch), F32)),
        in_specs=[vm] * 8, out_specs=(vm,) * 4,
        name="s5_params",
    )(lam_re, lam_im, ldt, rep(lam_re), rep(lam_im), rep(ldt),
      b_re.reshape(g, p * ch), b_im.reshape(g, p * ch))
    eye = jnp.eye(D_SETS * 4, dtype=F32)

    def pack_b(bb):
        bb = bb.reshape(D_SETS, 16, p, ch)
        return jnp.einsum('sgpi,gh->sgihp', bb, eye).reshape(D_SETS, D_SET_CH, D_SET_ST)

    def pack_c(c):
        c = c.reshape(D_SETS, 16, ch, p)
        return jnp.einsum('sgjp,gh->sgphj', c, eye).reshape(D_SETS, D_SET_ST, D_SET_CH)

    bd = jnp.concatenate([pack_b(bb_re), pack_b(bb_im)], axis=-1).astype(BF16)
    slab = (D_SETS, S5_SLAB, 128)
    return (a_re.reshape(slab), a_im.reshape(slab), bd,
            pack_c(c_re).astype(BF16), pack_c(c_im).astype(BF16))


S5_TM = 256
S5_SLAB = D_SET_ST // 128


def _s5_kernel(u_ref, z_ref, are_ref, aim_ref, bd_ref, cre_ref, cim_ref, dskip_ref, wglu_ref, bglu_ref,
               o_ref, hre_ref, him_ref, y_ref, *x_refs):
    @pl.when(pl.program_id(1) == 0)
    def _():
        hre_ref[...] = jnp.zeros(hre_ref.shape, F32)
        him_ref[...] = jnp.zeros(him_ref.shape, F32)

    xre, xim = x_refs[:D_SETS], x_refs[D_SETS:]
    tm = u_ref.shape[0]
    u = u_ref[...]
    ub = u.astype(BF16)
    for s in range(D_SETS):
        bu = _dot(ub[:, s * D_SET_CH:(s + 1) * D_SET_CH], bd_ref[s])
        for k in range(S5_SLAB):
            xre[s][pl.ds(k, tm, stride=S5_SLAB), :] = bu[:, k * 128:(k + 1) * 128]
            xim[s][pl.ds(k, tm, stride=S5_SLAB), :] = bu[:, D_SET_ST + k * 128:D_SET_ST + (k + 1) * 128]
    a_re = [are_ref[s] for s in range(D_SETS)]
    a_im = [aim_ref[s] for s in range(D_SETS)]

    def step(t, carry):
        rows = pl.ds(pl.multiple_of(t * S5_SLAB, S5_SLAB), S5_SLAB)
        out = []
        for s in range(D_SETS):
            hr, hi = carry[2 * s], carry[2 * s + 1]
            xr = xre[s][rows, :] + a_re[s] * hr - a_im[s] * hi
            xi = xim[s][rows, :] + a_re[s] * hi + a_im[s] * hr
            xre[s][rows, :] = xr
            xim[s][rows, :] = xi
            out += [xr, xi]
        return tuple(out)

    carry = []
    for s in range(D_SETS):
        carry += [hre_ref[s], him_ref[s]]
    carry = lax.fori_loop(0, tm, step, tuple(carry), unroll=8)
    for s in range(D_SETS):
        hre_ref[s] = carry[2 * s]
        him_ref[s] = carry[2 * s + 1]
        x_r = jnp.concatenate([xre[s][pl.ds(k, tm, stride=S5_SLAB), :] for k in range(S5_SLAB)], axis=1)
        x_i = jnp.concatenate([xim[s][pl.ds(k, tm, stride=S5_SLAB), :] for k in range(S5_SLAB)], axis=1)
        y_ref[:, s * D_SET_CH:(s + 1) * D_SET_CH] = (
            _dot(x_r.astype(BF16), cre_ref[s]) - _dot(x_i.astype(BF16), cim_ref[s]))
    y = y_ref[...] + dskip_ref[...] * u
    y = 0.5 * y * (1.0 + jnp.tanh(math.sqrt(2.0 / math.pi) * (y + 0.044715 * (y * y * y))))
    gate = _sigmoid(_dot(y.astype(BF16), wglu_ref[...]) + bglu_ref[...])
    o_ref[...] = (y * gate * _silu(z_ref[...])).astype(BF16)


def _s5(h, bsz, seq, a_re, a_im, bd, cd_re, cd_im, d_skip, w_glu, b_glu):
    tm = min(S5_TM, seq)
    nt = seq // tm
    full = lambda *shape: pl.BlockSpec(shape, lambda b, i: (0,) * len(shape))
    width = D_GROUPS * D_GROUP
    return pl.pallas_call(
        _s5_kernel,
        out_shape=jax.ShapeDtypeStruct((bsz * seq, width), BF16),
        grid=(bsz, nt),
        in_specs=[pl.BlockSpec((tm, width), lambda b, i: (b * nt + i, OD_DU // width)),
                  pl.BlockSpec((tm, width), lambda b, i: (b * nt + i, OD_DZ // width)),
                  full(D_SETS, S5_SLAB, 128), full(D_SETS, S5_SLAB, 128),
                  full(D_SETS, D_SET_CH, 2 * D_SET_ST),
                  full(D_SETS, D_SET_ST, D_SET_CH), full(D_SETS, D_SET_ST, D_SET_CH),
                  full(1, width), full(width, width), full(1, width)],
        out_specs=pl.BlockSpec((tm, width), lambda b, i: (b * nt + i, 0)),
        scratch_shapes=([pltpu.VMEM((D_SETS, S5_SLAB, 128), F32), pltpu.VMEM((D_SETS, S5_SLAB, 128), F32),
                         pltpu.VMEM((tm, width), F32)]
                        + [pltpu.VMEM((tm * S5_SLAB, 128), F32)] * (2 * D_SETS)),
        compiler_params=_cparams(("parallel", "arbitrary")),
        name="s5",
    )(h, h, a_re, a_im, bd, cd_re, cd_im, d_skip.reshape(1, -1), w_glu.astype(BF16),
      b_glu.reshape(1, -1))


def _pack_even_w(w_in):
    sizes = (1024, A_KV_RANK, A_IDX_HEADS * A_IDX_DIM, A_IDX_DIM, A_IDX_HEADS, 1024,
             3 * 1024, B_HEADS, B_HEADS, 1024)
    parts, start = [], 0
    for s in sizes:
        parts.append(w_in[:, start:start + s])
        start += s
    aq, ckv, qi, ki, wi, az, bqkv, ba, bb, bz = parts
    pad = jnp.zeros((w_in.shape[0], EV_WIDTH - start), w_in.dtype)
    return jnp.concatenate([aq, az, bqkv, bz, qi, ckv, ki, wi, ba, bb, pad], axis=1).astype(BF16)


def _even_layer(x2d, bsz, seq, bias, norm_g, w_in, a_q_norm, a_kv_norm, w_kv_up, a_k_norm,
                b_conv, b_a_log, b_dt_bias, b_out_norm, w_out):
    h = _norm_proj(x2d, norm_g, _pack_even_w(w_in), tn=768)
    qta, qit, wt, kidx, katt, vt = _dsa_prep(h, bsz * seq // QB, a_q_norm, a_kv_norm, w_kv_up, a_k_norm)
    mix_a = _dsa_attend(h, bsz, seq, qta, qit, wt, kidx, katt, vt, bias)
    u, w, qd, kd, attn, egl = _gdn_prep(h, bsz, seq, b_conv, b_a_log, b_dt_bias)
    mix_b = _gdn_scan(h, bsz, seq, u, w, qd, kd, attn, egl, b_out_norm)
    return _out_proj(x2d, mix_a, mix_b, w_out)


def _odd_layer(x2d, bsz, seq, cos2, sin2, norm_g, w_in, c_out_norm, lam_re, lam_im, log_dt,
               b_re, b_im, c_re, c_im, d_skip, w_glu, b_glu, w_out):
    h = _norm_proj(x2d, norm_g, w_in.astype(BF16), tn=1024)
    mix_c = _retention(h, bsz, seq, cos2, sin2, c_out_norm)
    s5p = _s5_params(lam_re, lam_im, log_dt, b_re, b_im, c_re, c_im)
    mix_d = _s5(h, bsz, seq, *s5p, d_skip, w_glu, b_glu)
    return _out_proj(x2d, mix_c, mix_d, w_out)


def kernel(x, rel_bias, ev_norm, ev_w_in, ev_a_q_norm, ev_a_kv_norm, ev_w_kv_up, ev_a_k_norm,
           ev_b_conv, ev_b_a_log, ev_b_dt_bias, ev_b_out_norm, ev_w_out,
           od_norm, od_w_in, od_c_out_norm, od_lam_re, od_lam_im, od_log_dt,
           od_b_re, od_b_im, od_c_re, od_c_im, od_d_skip, od_w_glu, od_b_glu, od_w_out):
    bsz, seq, d = x.shape
    depth = ev_norm.shape[0] + od_norm.shape[0]
    x2d = x.reshape(bsz * seq, d)
    bias = _bias_table(rel_bias)
    cos2, sin2 = _rope_tables(seq)
    for layer in range(depth):
        i = layer // 2
        if layer % 2 == 0:
            x2d = _even_layer(x2d, bsz, seq, bias, ev_norm[i], ev_w_in[i], ev_a_q_norm[i],
                              ev_a_kv_norm[i], ev_w_kv_up[i], ev_a_k_norm[i], ev_b_conv[i],
                              ev_b_a_log[i], ev_b_dt_bias[i], ev_b_out_norm[i], ev_w_out[i])
        else:
            x2d = _odd_layer(x2d, bsz, seq, cos2, sin2, od_norm[i], od_w_in[i], od_c_out_norm[i],
                             od_lam_re[i], od_lam_im[i], od_log_dt[i], od_b_re[i], od_b_im[i],
                             od_c_re[i], od_c_im[i], od_d_skip[i], od_w_glu[i], od_b_glu[i], od_w_out[i])
    return x2d.reshape(bsz, seq, d)
```

```python
import functools
import math

import numpy as np
import jax
import jax.numpy as jnp
from jax import lax
from jax.experimental import pallas as pl
from jax.experimental.pallas import tpu as pltpu

F32 = jnp.float32
BF16 = jnp.bfloat16
I32 = jnp.int32

D_MODEL = 1024
EPS = 1e-6
LOG2E = 1.4426950408889634
NEG_INF = float("-inf")
INT_MIN = -(2 ** 31)
HI16 = -(2 ** 16)

A_HEADS, A_HEAD_DIM, A_KV_RANK = 16, 64, 128
A_IDX_HEADS, A_IDX_DIM, A_TOPK_MAX = 8, 64, 256
QB = 128
REL_BUCKETS, REL_MAX_DIST = 32, 128
VT_ROWS = 80
SB = 2 * QB
SCORE_ROWS = 4 * QB
BIAS_ROWS = 5 * QB
MASKED = -(2.0 ** 100)
M_INIT = -(2.0 ** 60)
TIE_WALK_MAX = 8
B_HEADS, B_HEAD_DIM, CONV_K, B_CHUNK = 8, 128, 4, 64
C_HEADS, C_QK_DIM, C_V_DIM, C_CHUNK = 4, 128, 256, 128
ROPE_BASE = 10000.0
D_GROUP, D_STATE, D_GROUPS = 16, 64, 64
D_SETS, D_SET_CH, D_SET_ST = 4, 256, 1024

EV_AQ, EV_AZ, EV_BQ, EV_BK, EV_BV, EV_BZ, EV_QI, EV_CKV, EV_MISC = (
    0, 1024, 2048, 3072, 4096, 5120, 6144, 6656, 6784)
EV_WIDTH = 6912
MISC_KI, MISC_WI, MISC_BA, MISC_BB = 0, 64, 72, 80
OD_CQ, OD_CK, OD_CV, OD_CZ, OD_DU, OD_DZ = 0, 512, 1024, 2048, 3072, 4096
OD_WIDTH = 5120

VMEM_LIMIT = 48 * 1024 * 1024


def _cparams(sem):
    return pltpu.CompilerParams(dimension_semantics=sem, vmem_limit_bytes=VMEM_LIMIT)


def _dot(a, b):
    return jnp.dot(a, b, preferred_element_type=F32)


def _dot_nt(a, b):
    return lax.dot_general(a, b, (((1,), (1,)), ((), ())), preferred_element_type=F32)


def _split_bf16(x, n):
    parts = []
    for _ in range(n):
        p = x.astype(BF16)
        parts.append(p)
        x = x - p.astype(F32)
    return parts


def _dot_sel(sel, x):
    sel = sel.astype(BF16)
    hi, mid, lo = _split_bf16(x, 3)
    return _dot(sel, hi) + (_dot(sel, mid) + _dot(sel, lo))


def _dot_xsel(x, sel):
    sel = sel.astype(BF16)
    hi, mid, lo = _split_bf16(x, 3)
    return _dot(hi, sel) + (_dot(mid, sel) + _dot(lo, sel))


def _dot_x3(a, b):
    ah, al = _split_bf16(a, 2)
    bh, bl = _split_bf16(b, 2)
    return _dot(ah, bh) + (_dot(ah, bl) + _dot(al, bh))


def _sigmoid(x):
    return 1.0 / (1.0 + jnp.exp(-x))


def _silu(x):
    return x * _sigmoid(x)


def _norm_proj_kernel(x_ref, g_ref, w_ref, o_ref, xn_ref):
    @pl.when(pl.program_id(1) == 0)
    def _():
        x = x_ref[...]
        ms = jnp.mean(x * x, axis=-1, keepdims=True)
        xn_ref[...] = (x * lax.rsqrt(ms + EPS) * g_ref[...]).astype(BF16)

    o_ref[...] = _dot(xn_ref[...], w_ref[...])


def _norm_proj(x2d, gain, w_bf16, tn, tm=1024):
    t, d = x2d.shape
    n = w_bf16.shape[1]
    tm = min(tm, t)
    return pl.pallas_call(
        _norm_proj_kernel,
        out_shape=jax.ShapeDtypeStruct((t, n), F32),
        grid=(t // tm, n // tn),
        in_specs=[pl.BlockSpec((tm, d), lambda i, j: (i, 0)),
                  pl.BlockSpec((1, d), lambda i, j: (0, 0)),
                  pl.BlockSpec((d, tn), lambda i, j: (0, j))],
        out_specs=pl.BlockSpec((tm, tn), lambda i, j: (i, j)),
        scratch_shapes=[pltpu.VMEM((tm, d), BF16)],
        compiler_params=_cparams(("parallel", "arbitrary")),
        name="norm_proj",
    )(x2d, gain.reshape(1, d), w_bf16)


def _out_proj_kernel(x_ref, a_ref, b_ref, wa_ref, wb_ref, o_ref):
    o_ref[...] = x_ref[...] + _dot(a_ref[...], wa_ref[...]) + _dot(b_ref[...], wb_ref[...])


def _out_proj(x2d, mix_a, mix_b, w_out, tm=512):
    t, d = x2d.shape
    half = mix_a.shape[1]
    tm = min(tm, t)
    wa = w_out[:half].astype(BF16)
    wb = w_out[half:].astype(BF16)
    return pl.pallas_call(
        _out_proj_kernel,
        out_shape=jax.ShapeDtypeStruct((t, d), F32),
        grid=(t // tm,),
        in_specs=[pl.BlockSpec((tm, d), lambda i: (i, 0)),
                  pl.BlockSpec((tm, half), lambda i: (i, 0)),
                  pl.BlockSpec((tm, half), lambda i: (i, 0)),
                  pl.BlockSpec((half, d), lambda i: (0, 0)),
                  pl.BlockSpec((half, d), lambda i: (0, 0))],
        out_specs=pl.BlockSpec((tm, d), lambda i: (i, 0)),
        compiler_params=_cparams(("parallel",)),
        name="out_proj",
    )(x2d, mix_a, mix_b, wa, wb)


def _t5_bucket_starts():
    exact = REL_BUCKETS // 2
    n = np.arange(0, 4 * REL_MAX_DIST, dtype=np.int64)
    ratio = np.maximum(n, 1).astype(np.float32) / np.float32(exact)
    large = exact + (np.log(ratio).astype(np.float32) / np.float32(math.log(REL_MAX_DIST / exact))
                     * np.float32(REL_BUCKETS - exact)).astype(np.int32)
    bucket = np.where(n < exact, n, np.minimum(large, REL_BUCKETS - 1))
    starts = [int(np.argmax(bucket >= b)) for b in range(REL_BUCKETS)]
    assert all(bucket[s] == b for b, s in enumerate(starts)) and np.all(np.diff(bucket) >= 0)
    assert starts[-1] <= QB, "distances beyond one key block must share the last bucket"
    return starts


_BUCKET_STARTS = _t5_bucket_starts()


def _bias_table_kernel(rb_ref, o_ref):
    row = lax.broadcasted_iota(I32, (BIAS_ROWS, QB), 0)
    lane = lax.broadcasted_iota(I32, (BIAS_ROWS, QB), 1)
    dist = lane + 3 * QB - row
    for h in range(A_HEADS):
        val = jnp.full((BIAS_ROWS, QB), rb_ref[0, h], F32)
        for b in range(1, REL_BUCKETS):
            val = jnp.where(dist >= _BUCKET_STARTS[b], rb_ref[b, h], val)
        val = (val - rb_ref[REL_BUCKETS - 1, h]) * LOG2E
        o_ref[:, h * QB:(h + 1) * QB] = jnp.where(dist >= 0, val, 0.0)


def _bias_table(rel_bias):
    return pl.pallas_call(
        _bias_table_kernel,
        out_shape=jax.ShapeDtypeStruct((BIAS_ROWS, A_HEADS * QB), F32),
        in_specs=[pl.BlockSpec(memory_space=pltpu.SMEM)],
        out_specs=pl.BlockSpec(memory_space=pltpu.VMEM),
        name="dsa_bias_table",
    )(rel_bias)


def _dsa_prep_kernel(aq_ref, qi_ref, ckv_ref, misc_ref, gq_ref, gkv_ref, wkv_ref, gk_ref,
                     qta_ref, qit_ref, wt_ref, kidx_ref, katt_ref, vt_ref):
    zeros = jnp.zeros((QB, A_HEAD_DIM), F32)
    aq = aq_ref[...]
    gq = gq_ref[...]
    ident = jnp.where(lax.broadcasted_iota(I32, (QB, QB), 0) == lax.broadcasted_iota(I32, (QB, QB), 1),
                      1.0, 0.0).astype(BF16)
    for h in range(A_HEADS):
        q = aq[:, h * A_HEAD_DIM:(h + 1) * A_HEAD_DIM]
        ms = jnp.mean(q * q, axis=-1, keepdims=True)
        qn = q * lax.rsqrt(ms + EPS) * gq * (A_HEAD_DIM ** -0.5 * LOG2E)
        qta_ref[0:QB, h * QB:(h + 1) * QB] = jnp.concatenate([qn, zeros], axis=1).T.astype(BF16)
        qta_ref[QB:2 * QB, h * QB:(h + 1) * QB] = ident
    qi = qi_ref[...]
    for h in range(A_IDX_HEADS):
        qih = qi[:, h * A_IDX_DIM:(h + 1) * A_IDX_DIM]
        qit_ref[:, h * QB:(h + 1) * QB] = jnp.concatenate([qih, zeros], axis=1).T.astype(BF16)
    misc = misc_ref[...]
    wt_ref[...] = misc.T[MISC_WI:MISC_WI + A_IDX_HEADS, :] * (A_IDX_HEADS ** -0.5 * A_IDX_DIM ** -0.5)
    c = ckv_ref[...]
    cn = c * lax.rsqrt(jnp.mean(c * c, axis=-1, keepdims=True) + EPS) * gkv_ref[...]
    kv = _dot(cn.astype(BF16), wkv_ref[...])
    k = kv[:, :A_HEAD_DIM]
    kn = k * lax.rsqrt(jnp.mean(k * k, axis=-1, keepdims=True) + EPS) * gk_ref[...]
    kidx_ref[...] = jnp.concatenate([misc[:, MISC_KI:MISC_KI + A_IDX_DIM], zeros], axis=1).astype(BF16)
    katt_ref[...] = jnp.concatenate([kn, zeros], axis=1).astype(BF16)
    kvt = kv.T
    vt_ref[0:A_HEAD_DIM, :] = kvt[A_HEAD_DIM:, :].astype(BF16)
    ones_row = lax.broadcasted_iota(I32, (VT_ROWS - A_HEAD_DIM, QB), 0) == 0
    vt_ref[A_HEAD_DIM:, :] = jnp.where(ones_row, 1.0, 0.0).astype(BF16)


def _dsa_prep(h, nblk, a_q_norm, a_kv_norm, w_kv_up, a_k_norm):
    cb = lambda width, off: off // width
    return pl.pallas_call(
        _dsa_prep_kernel,
        out_shape=(jax.ShapeDtypeStruct((nblk, 2 * QB, A_HEADS * QB), BF16),
                   jax.ShapeDtypeStruct((nblk, QB, A_IDX_HEADS * QB), BF16),
                   jax.ShapeDtypeStruct((nblk, A_IDX_HEADS, QB), F32),
                   jax.ShapeDtypeStruct((nblk, QB, 128), BF16),
                   jax.ShapeDtypeStruct((nblk, QB, 128), BF16),
                   jax.ShapeDtypeStruct((nblk, VT_ROWS, QB), BF16)),
        grid=(nblk,),
        in_specs=[pl.BlockSpec((QB, 1024), lambda i: (i, cb(1024, EV_AQ))),
                  pl.BlockSpec((QB, 512), lambda i: (i, cb(512, EV_QI))),
                  pl.BlockSpec((QB, 128), lambda i: (i, cb(128, EV_CKV))),
                  pl.BlockSpec((QB, 128), lambda i: (i, cb(128, EV_MISC))),
                  pl.BlockSpec((1, A_HEAD_DIM), lambda i: (0, 0)),
                  pl.BlockSpec((1, A_KV_RANK), lambda i: (0, 0)),
                  pl.BlockSpec((A_KV_RANK, 2 * A_HEAD_DIM), lambda i: (0, 0)),
                  pl.BlockSpec((1, A_HEAD_DIM), lambda i: (0, 0))],
        out_specs=(pl.BlockSpec((None, 2 * QB, A_HEADS * QB), lambda i: (i, 0, 0)),
                   pl.BlockSpec((None, QB, A_IDX_HEADS * QB), lambda i: (i, 0, 0)),
                   pl.BlockSpec((None, A_IDX_HEADS, QB), lambda i: (i, 0, 0)),
                   pl.BlockSpec((None, QB, 128), lambda i: (i, 0, 0)),
                   pl.BlockSpec((None, QB, 128), lambda i: (i, 0, 0)),
                   pl.BlockSpec((None, VT_ROWS, QB), lambda i: (i, 0, 0))),
        compiler_params=_cparams(("parallel",)),
        name="dsa_prep",
    )(h, h, h, h, a_q_norm.reshape(1, -1), a_kv_norm.reshape(1, -1), w_kv_up.astype(BF16),
      a_k_norm.reshape(1, -1))


def _dsa_kernel(topk, qit_ref, wt_ref, qta_ref, kidx_ref, katt_ref, vt_ref, bias_ref, az_ref, o_ref,
                strip_ref, hi_ref, lo_ref, x_ref, m_ref, acc_ref, lg_ref, mx_ref, p_ref):
    qt = pl.program_id(1)
    t0 = qt * QB
    n_sc = qt // 4 + 1
    n_sb = qt // 2 + 1
    lane_s = lax.broadcasted_iota(I32, (SCORE_ROWS, QB), 1)
    row_s = lax.broadcasted_iota(I32, (SCORE_ROWS, QB), 0)

    w = wt_ref[...]

    def score_body(j, carry):
        kblk = kidx_ref[pl.ds(j * 4, 4)].reshape(SCORE_ROWS, 128)
        tot = None
        for hp in range(A_IDX_HEADS // 2):
            s = _dot(kblk, qit_ref[:, hp * SB:(hp + 1) * SB])
            for e in range(2):
                h = 2 * hp + e
                term = jnp.maximum(s[:, e * QB:(e + 1) * QB], 0.0) * w[h:h + 1, :]
                tot = term if tot is None else tot + term
        tot = jnp.where(j * SCORE_ROWS + row_s <= t0 + lane_s, tot, NEG_INF)
        bits = pltpu.bitcast(tot, I32)
        key = bits ^ ((bits >> 31) & 0x7FFFFFFF)
        rows = pl.ds(pl.multiple_of(j * SCORE_ROWS, SCORE_ROWS), SCORE_ROWS)
        strip_ref[rows, :] = key
        hi_ref[rows, :] = (key >> 16).astype(jnp.int16)
        return carry

    lax.fori_loop(0, n_sc, score_body, 0)

    def count(pred):
        def body(j, acc):
            r0 = pl.multiple_of(j * SCORE_ROWS, SCORE_ROWS)
            m = jnp.where(pred(strip_ref[pl.ds(r0, SCORE_ROWS), :], r0), 1, 0)
            return acc + m.reshape(SCORE_ROWS // 32, 32, QB).sum(axis=0)
        acc = lax.fori_loop(0, n_sc, body, jnp.zeros((32, QB), I32))
        return acc.reshape(4, 8, QB).sum(axis=0).sum(axis=0, keepdims=True)

    one_h = jnp.ones((), jnp.int16)
    zero_h = jnp.zeros((), jnp.int16)

    def count_half(ref, pred):
        def body(j, acc):
            r0 = pl.multiple_of(j * SCORE_ROWS, SCORE_ROWS)
            m = jnp.where(pred(ref[pl.ds(r0, SCORE_ROWS), :]), one_h, zero_h)
            parts = [m[r:r + 32, :] for r in range(0, SCORE_ROWS, 32)]
            while len(parts) > 1:
                parts = [parts[i] + parts[i + 1] for i in range(0, len(parts), 2)]
            return acc + parts[0]
        acc = lax.fori_loop(0, n_sc, body, jnp.zeros((32, QB), jnp.int16))
        return acc.astype(I32).reshape(4, 8, QB).sum(axis=0).sum(axis=0, keepdims=True)

    def count_hi(cand):
        cand_h = (cand >> 16).astype(jnp.int16)
        return count_half(hi_ref, lambda half: half >= cand_h)

    c0 = count_hi(jnp.zeros((1, QB), I32))
    nonneg = c0 >= topk
    thr0 = jnp.where(nonneg, 0, INT_MIN)
    cnt0 = jnp.where(nonneg, c0, n_sc * SCORE_ROWS)

    def bit_steps(counter, n):
        def one(_, carry):
            i, thr, cnt = carry
            cand = thr | lax.shift_left(jnp.int32(1), 30 - i)
            c = counter(cand)
            ok = c >= topk
            return i + 1, jnp.where(ok, cand, thr), jnp.where(ok, c, cnt)
        return lambda carry: lax.fori_loop(0, n, one, carry)

    state = bit_steps(count_hi, 15)((jnp.int32(0), thr0, cnt0))

    thr_hi = state[1] >> 16
    thr_hi_h = thr_hi.astype(jnp.int16)
    c_above = count_half(hi_ref, lambda half: half > thr_hi_h)

    def lo_body(j, carry):
        rows = pl.ds(pl.multiple_of(j * SCORE_ROWS, SCORE_ROWS), SCORE_ROWS)
        key = strip_ref[rows, :]
        lo_ref[rows, :] = jnp.where((key >> 16) == thr_hi, (key & 0xFFFF) - 2 ** 15, -(2 ** 15)).astype(jnp.int16)
        return carry

    lax.fori_loop(0, n_sc, lo_body, 0)

    def count_lo(cand):
        cand_l = ((cand & 0xFFFF) - 2 ** 15).astype(jnp.int16)
        return c_above + count_half(lo_ref, lambda half: half >= cand_l)

    _, thr, cnt = lax.while_loop(lambda c: (c[0] < 31) & (jnp.max(c[2]) > topk),
                                 bit_steps(count_lo, 4), state)

    x_ref[...] = jnp.full((1, QB), 2 ** 30, I32)
    tied = cnt > topk

    @pl.when(jnp.max(cnt) > topk)
    def _():
        above = count(lambda key, r0: key > thr)
        need = jnp.where(tied, topk - above, 0)
        max_need = jnp.max(need)

        @pl.when(max_need <= TIE_WALK_MAX)
        def _():
            def next_tie(prev):
                def body(j, acc):
                    r0 = pl.multiple_of(j * SCORE_ROWS, SCORE_ROWS)
                    idx = r0 + row_s
                    hit = (strip_ref[pl.ds(r0, SCORE_ROWS), :] == thr) & (idx > prev)
                    return jnp.minimum(acc, jnp.where(hit, idx, 2 ** 30)
                                       .reshape(SCORE_ROWS // 32, 32, QB).min(axis=0))
                acc = lax.fori_loop(0, n_sc, body, jnp.full((32, QB), 2 ** 30, I32))
                return acc.reshape(4, 8, QB).min(axis=0).min(axis=0, keepdims=True)

            last = lax.fori_loop(0, max_need, lambda r, prev: jnp.where(r < need, next_tie(prev), prev),
                                 jnp.full((1, QB), -1, I32))
            x_ref[...] = jnp.where(tied, last + 1, 2 ** 30)

        @pl.when(max_need > TIE_WALK_MAX)
        def _():
            def tie_body(i, xb):
                cand = xb | lax.shift_left(jnp.int32(1), 14 - i)
                c = count(lambda key, r0: (key == thr) & (r0 + row_s < cand))
                return jnp.where(c <= need, cand, xb)

            xb = lax.fori_loop(0, 15, tie_body, jnp.zeros((1, QB), I32))
            x_ref[...] = jnp.where(tied, xb, 2 ** 30)

    xb = x_ref[...]

    m_ref[...] = jnp.full(m_ref.shape, M_INIT, F32)
    acc_ref[...] = jnp.zeros(acc_ref.shape, F32)
    n_lt = A_HEADS * QB // SB
    row_a = lax.broadcasted_iota(I32, (SB, QB), 0)
    lane_a = lax.broadcasted_iota(I32, (SB, QB), 1)

    def logits(j, slot, near):
        r0 = pl.multiple_of(j * SB, SB)
        key = strip_ref[pl.ds(r0, SB), :]
        idx = r0 + row_a
        sel = (key > thr) | ((key == thr) & (idx < xb))
        if near:
            sel = sel & (idx <= t0 + lane_a)
            b0 = pl.multiple_of((2 * j - qt + 3) * QB, QB)
        pen = jnp.where(sel, 0.0, MASKED).astype(BF16)
        kaug = jnp.concatenate([katt_ref[pl.ds(2 * j, 2)].reshape(SB, 128), pen], axis=1)
        for lt in range(n_lt):
            cols = slice(lt * SB, (lt + 1) * SB)
            lg = _dot(kaug, qta_ref[:, cols])
            if near:
                lg = lg + bias_ref[pl.ds(b0, SB), cols]
            lg = lg.astype(BF16)
            lg_ref[slot, :, cols] = lg
            mx_ref[slot, :, cols] = jnp.max(lg, axis=0, keepdims=True).astype(F32)

    def accumulate(j, slot):
        vt = jnp.concatenate([vt_ref[2 * j], vt_ref[2 * j + 1]], axis=1)
        m_old = m_ref[...]
        m_new = jnp.maximum(m_old, mx_ref[slot])
        m_ref[...] = m_new
        m_b = m_new.astype(BF16)
        for lt in range(n_lt):
            cols = slice(lt * SB, (lt + 1) * SB)
            p_ref[:, cols] = jnp.exp2(lg_ref[slot, :, cols] - m_b[:, cols])
        acc_ref[...] = jnp.exp2(m_old - m_new) * acc_ref[...] + _dot(vt, p_ref[...])

    n_far = jnp.maximum(n_sb - 2, 0)
    n_pair = jnp.maximum(n_far - 1, 0) // 2
    n_rem = n_far - 2 * n_pair

    @pl.when(n_far > 0)
    def _():
        logits(0, 0, False)

    def pair_body(jj, carry):
        j = 2 * jj
        logits(j + 1, 1, False)
        accumulate(j, 0)
        logits(j + 2, 0, False)
        accumulate(j + 1, 1)
        return carry

    lax.fori_loop(0, n_pair, pair_body, 0)

    @pl.when(n_rem == 1)
    def _():
        accumulate(n_far - 1, 0)

    @pl.when(n_rem == 2)
    def _():
        logits(n_far - 1, 1, False)
        accumulate(n_far - 2, 0)
        accumulate(n_far - 1, 1)

    @pl.when(n_sb >= 2)
    def _():
        logits(n_sb - 2, 0, True)
        accumulate(n_sb - 2, 0)

    logits(n_sb - 1, 0, True)
    accumulate(n_sb - 1, 0)

    acc = acc_ref[...]
    o_t = acc[0:A_HEAD_DIM, :] / acc[A_HEAD_DIM:A_HEAD_DIM + 1, :]
    pad = jnp.zeros((QB - A_HEAD_DIM, QB), F32)
    pieces = []
    for h in range(A_HEADS):
        blk = jnp.concatenate([o_t[:, h * QB:(h + 1) * QB], pad], axis=0)
        pieces.append(blk.T[:, 0:A_HEAD_DIM])
    att = jnp.concatenate(pieces, axis=1)
    o_ref[...] = (att * _silu(az_ref[...])).astype(BF16)


def _dsa_attend(h, bsz, seq, qta, qit, wt, kidx, katt, vt, bias):
    nq = seq // QB
    topk = min(A_TOPK_MAX, seq // 4)
    blk = lambda b, q: (b * nq + q, 0, 0)
    per_batch = lambda b, q: (b, 0, 0)
    once = pl.Buffered(1)
    return pl.pallas_call(
        functools.partial(_dsa_kernel, topk),
        out_shape=jax.ShapeDtypeStruct((bsz * seq, A_HEADS * A_HEAD_DIM), BF16),
        grid=(bsz, nq),
        in_specs=[pl.BlockSpec((None, QB, A_IDX_HEADS * QB), blk),
                  pl.BlockSpec((None, A_IDX_HEADS, QB), blk),
                  pl.BlockSpec((None, 2 * QB, A_HEADS * QB), blk),
                  pl.BlockSpec((nq, QB, 128), per_batch, pipeline_mode=once),
                  pl.BlockSpec((nq, QB, 128), per_batch, pipeline_mode=once),
                  pl.BlockSpec((nq, VT_ROWS, QB), per_batch, pipeline_mode=once),
                  pl.BlockSpec((BIAS_ROWS, A_HEADS * QB), lambda b, q: (0, 0), pipeline_mode=once),
                  pl.BlockSpec((QB, 1024), lambda b, q: (b * nq + q, EV_AZ // 1024))],
        out_specs=pl.BlockSpec((QB, 1024), lambda b, q: (b * nq + q, 0)),
        scratch_shapes=[pltpu.VMEM((seq, QB), I32),
                        pltpu.VMEM((seq, QB), jnp.int16),
                        pltpu.VMEM((seq, QB), jnp.int16),
                        pltpu.VMEM((1, QB), I32),
                        pltpu.VMEM((1, A_HEADS * QB), F32),
                        pltpu.VMEM((VT_ROWS, A_HEADS * QB), F32),
                        pltpu.VMEM((2, SB, A_HEADS * QB), BF16),
                        pltpu.VMEM((2, 1, A_HEADS * QB), F32),
                        pltpu.VMEM((SB, A_HEADS * QB), BF16)],
        compiler_params=_cparams(("arbitrary", "arbitrary")),
        name="dsa_attend",
    )(qit, wt, qta, kidx, katt, vt, bias, h)


GDN_TM = 512
GDN_GROUP = 4
GDN_SCAN_HEADS = 4
GDN_PREP_HEADS = 2


def _gdn_prep_kernel(alog_ref, dtb_ref, q_ref, k_ref, v_ref, qh_ref, kh_ref, vh_ref, misc_ref,
                     cq_ref, ck_ref, cv_ref,
                     u_ref, w_ref, qd_ref, kdt_ref, attn_ref, egl_ref):
    i = pl.program_id(1)
    hp = pl.program_id(2)
    tm = q_ref.shape[0]
    nchunk = tm // B_CHUNK
    heads = range(GDN_PREP_HEADS)
    lanes = [slice(hh * 128, (hh + 1) * 128) for hh in heads]

    def conv_silu(x_ref, halo_ref, c_ref, ln):
        halo = jnp.where(i > 0, halo_ref[:, ln], 0.0)
        ext = jnp.concatenate([halo, x_ref[:, ln]], axis=0)
        cw = c_ref[:, ln]
        y = ext[8:, :] * cw[CONV_K - 1:CONV_K, :]
        for d in range(1, CONV_K):
            y = y + pltpu.roll(ext, d, 0)[8:, :] * cw[CONV_K - 1 - d:CONV_K - d, :]
        return _silu(y)

    q = [conv_silu(q_ref, qh_ref, cq_ref, ln) for ln in lanes]
    k = [conv_silu(k_ref, kh_ref, ck_ref, ln) for ln in lanes]
    v = [conv_silu(v_ref, vh_ref, cv_ref, ln) for ln in lanes]
    qn = [x * lax.rsqrt(jnp.sum(x * x, axis=-1, keepdims=True) + EPS) * (B_HEAD_DIM ** -0.5) for x in q]
    kn = [x * lax.rsqrt(jnp.sum(x * x, axis=-1, keepdims=True) + EPS) for x in k]

    misc = misc_ref[...]
    sel_r = lax.broadcasted_iota(I32, (128, 256), 0)
    sel_c = lax.broadcasted_iota(I32, (128, 256), 1)
    beta, g = [], []
    for hh in heads:
        hd = hp * GDN_PREP_HEADS + hh
        pick = jnp.where(sel_r == jnp.where(sel_c < 128, MISC_BA + hd, MISC_BB + hd), 1.0, 0.0)
        bab = _dot_xsel(misc, pick)
        beta.append(_sigmoid(bab[:, 128:256]))
        xg = bab[:, 0:128] + dtb_ref[hd]
        softplus = jnp.maximum(xg, 0.0) + jnp.log(1.0 + jnp.exp(-jnp.abs(xg)))
        g.append(-jnp.exp(jnp.full((1, 128), alog_ref[hd], F32)) * softplus)

    sl = 256
    r2 = lax.broadcasted_iota(I32, (sl, sl), 0)
    c2 = lax.broadcasted_iota(I32, (sl, sl), 1)
    same = (r2 // B_CHUNK) == (c2 // B_CHUNK)
    tri_blk = jnp.concatenate([jnp.where(same & (c2 <= r2), 1.0, 0.0), jnp.where(same, 1.0, 0.0)], axis=0)
    nsl = tm // sl
    gc, gl = [], []
    for hh in heads:
        both = _dot_sel(tri_blk, jnp.concatenate([g[hh][s * sl:(s + 1) * sl, :] for s in range(nsl)], axis=1))
        gc.append(jnp.concatenate([both[0:sl, s * 128:(s + 1) * 128] for s in range(nsl)], axis=0))
        gl.append(jnp.concatenate([both[sl:2 * sl, s * 128:(s + 1) * 128] for s in range(nsl)], axis=0))

    kb = [kn[hh] * beta[hh] for hh in heads]
    vb = [v[hh] * beta[hh] for hh in heads]
    egc = [jnp.exp(x) for x in gc]
    kbg = [kb[hh] * egc[hh] for hh in heads]
    for hh in heads:
        qd_ref[hh] = (qn[hh] * egc[hh]).astype(BF16)
        kd = kn[hh] * jnp.exp(gl[hh] - gc[hh])
        for m in range(tm // 128):
            kdt_ref[hh, :, m * 128:(m + 1) * 128] = kd[m * 128:(m + 1) * 128, :].T.astype(BF16)

    gw = GDN_GROUP * B_CHUNK
    ri = lax.broadcasted_iota(I32, (B_CHUNK, gw), 0)
    lj = lax.broadcasted_iota(I32, (B_CHUNK, gw), 1)
    lb = lj // B_CHUNK
    lj = lj % B_CHUNK
    bmask = (lax.broadcasted_iota(I32, (gw, gw), 0) // B_CHUNK
             == lax.broadcasted_iota(I32, (gw, gw), 1) // B_CHUNK)

    def fold(x):
        xm = jnp.where(bmask, x, 0.0)
        out = xm[0:B_CHUNK, :]
        for c in range(1, GDN_GROUP):
            out = out + xm[c * B_CHUNK:(c + 1) * B_CHUNK, :]
        return out

    def bdiag(cat):
        return jnp.where(bmask, jnp.concatenate([cat] * GDN_GROUP, axis=0), 0.0)

    probs = [(hh, grp) for grp in range(tm // gw) for hh in heads]
    a_cat = []
    for hh, grp in probs:
        rs = slice(grp * gw, (grp + 1) * gw)
        gcs = gc[hh][rs, :]
        col = jnp.concatenate([gcs[0:B_CHUNK, :]] * 2, axis=1)
        for c in range(1, GDN_GROUP):
            col = jnp.where(lb == c, jnp.concatenate([gcs[c * B_CHUNK:(c + 1) * B_CHUNK, :]] * 2, axis=1), col)
        rowf = jnp.concatenate([gcs[m * 128:(m + 1) * 128, :].T[0:B_CHUNK, :] for m in range(gw // 128)],
                               axis=1)
        decay = jnp.exp(jnp.where(ri >= lj, col - rowf, NEG_INF))
        kg = kn[hh][rs, :]
        a_cat.append(jnp.where(ri > lj, fold(_dot_nt(kb[hh][rs, :], kg)) * decay, 0.0))
        attn_ref[hh, grp] = jnp.where(ri >= lj, fold(_dot_nt(qn[hh][rs, :], kg)) * decay, 0.0).astype(BF16)
    eye = jnp.where(ri == lj, 1.0, 0.0)
    inv = [eye - a for a in a_cat]
    pw = [_dot_x3(a, bdiag(a)) for a in a_cat]
    for step in range(5):
        for p in range(len(probs)):
            if step < 4:
                both = _dot_x3(jnp.concatenate([inv[p], pw[p]], axis=0), bdiag(pw[p]))
                inv[p] = inv[p] + both[0:B_CHUNK, :]
                pw[p] = both[B_CHUNK:, :]
            else:
                inv[p] = inv[p] + _dot_x3(inv[p], bdiag(pw[p]))
    for p, (hh, grp) in enumerate(probs):
        rs = slice(grp * gw, (grp + 1) * gw)
        sol = _dot_x3(bdiag(inv[p]), jnp.concatenate([vb[hh][rs, :], kbg[hh][rs, :]], axis=1))
        u_ref[hh, rs, :] = sol[:, 0:B_HEAD_DIM]
        w_ref[hh, rs, :] = sol[:, B_HEAD_DIM:].astype(BF16)
    for hh in heads:
        for c in range(nchunk):
            egl_ref[hh, c:c + 1, :] = jnp.exp(gl[hh][c * B_CHUNK:c * B_CHUNK + 1, :])
        if nchunk < 8:
            egl_ref[hh, nchunk:, :] = jnp.zeros((8 - nchunk, 128), F32)


def _gdn_prep(h, bsz, seq, b_conv, b_a_log, b_dt_bias):
    tm = min(GDN_TM, seq)
    nt = seq // tm
    nrow = max(tm // B_CHUNK, 8)
    gw = GDN_GROUP * B_CHUNK
    hs = (bsz, B_HEADS, seq, B_HEAD_DIM)
    hg = GDN_PREP_HEADS
    hw = hg * 128
    cur = lambda off: (lambda b, i, hp: (b * nt + i, off // hw + hp))
    halo = lambda off: (lambda b, i, hp: (jnp.maximum((b * nt + i) * (tm // 8) - 1, 0), off // hw + hp))
    cw = lambda off: (lambda b, i, hp: (0, off // hw + hp))
    out = lambda b, i, hp: (b, hp, i, 0)
    smem = pl.BlockSpec(memory_space=pltpu.SMEM)
    return pl.pallas_call(
        _gdn_prep_kernel,
        out_shape=(jax.ShapeDtypeStruct(hs, F32), jax.ShapeDtypeStruct(hs, BF16),
                   jax.ShapeDtypeStruct(hs, BF16),
                   jax.ShapeDtypeStruct((bsz, B_HEADS, B_HEAD_DIM, seq), BF16),
                   jax.ShapeDtypeStruct((bsz, B_HEADS, seq // gw, B_CHUNK, gw), BF16),
                   jax.ShapeDtypeStruct((bsz, B_HEADS, nt * nrow, 128), F32)),
        grid=(bsz, nt, B_HEADS // hg),
        in_specs=[smem, smem,
                  pl.BlockSpec((tm, hw), cur(EV_BQ)), pl.BlockSpec((tm, hw), cur(EV_BK)),
                  pl.BlockSpec((tm, hw), cur(EV_BV)),
                  pl.BlockSpec((8, hw), halo(EV_BQ)), pl.BlockSpec((8, hw), halo(EV_BK)),
                  pl.BlockSpec((8, hw), halo(EV_BV)),
                  pl.BlockSpec((tm, 128), lambda b, i, hp: (b * nt + i, EV_MISC // 128)),
                  pl.BlockSpec((CONV_K, hw), cw(0)), pl.BlockSpec((CONV_K, hw), cw(1024)),
                  pl.BlockSpec((CONV_K, hw), cw(2048))],
        out_specs=(pl.BlockSpec((None, hg, tm, 128), out), pl.BlockSpec((None, hg, tm, 128), out),
                   pl.BlockSpec((None, hg, tm, 128), out),
                   pl.BlockSpec((None, hg, B_HEAD_DIM, tm), lambda b, i, hp: (b, hp, 0, i)),
                   pl.BlockSpec((None, hg, tm // gw, B_CHUNK, gw), lambda b, i, hp: (b, hp, i, 0, 0)),
                   pl.BlockSpec((None, hg, nrow, 128), out)),
        compiler_params=_cparams(("parallel", "parallel", "parallel")),
        name="gdn_prep",
    )(b_a_log, b_dt_bias, h, h, h, h, h, h, h, b_conv, b_conv, b_conv)


def _gdn_scan_kernel(u_ref, w_ref, qd_ref, kdt_ref, attn_ref, egl_ref, z_ref, gn_ref, o_ref, s_ref):
    @pl.when(pl.program_id(2) == 0)
    def _():
        s_ref[...] = jnp.zeros(s_ref.shape, F32)

    tm = u_ref.shape[1]
    gw = GDN_GROUP * B_CHUNK
    gain = gn_ref[...]
    zero = jnp.zeros((B_CHUNK, B_HEAD_DIM), BF16)
    states = [s_ref[hh] for hh in range(GDN_SCAN_HEADS)]
    for c in range(tm // B_CHUNK):
        grp, ci = divmod(c, GDN_GROUP)
        rs = slice(c * B_CHUNK, (c + 1) * B_CHUNK)
        for hh in range(GDN_SCAN_HEADS):
            sb = states[hh].astype(BF16)
            v_new = u_ref[hh, rs, :] - _dot(w_ref[hh, rs, :], sb)
            vpad = jnp.concatenate([zero] * ci + [v_new.astype(BF16)] + [zero] * (GDN_GROUP - 1 - ci), axis=0)
            o = _dot(qd_ref[hh, rs, :], sb) + _dot(attn_ref[hh, grp], vpad)
            states[hh] = (states[hh] * egl_ref[hh, c:c + 1, :]
                          + _dot(kdt_ref[hh, :, grp * gw:(grp + 1) * gw], vpad))
            on = o * lax.rsqrt(jnp.mean(o * o, axis=-1, keepdims=True) + EPS) * gain
            lanes = slice(hh * B_HEAD_DIM, (hh + 1) * B_HEAD_DIM)
            o_ref[rs, lanes] = (on * _silu(z_ref[rs, lanes])).astype(BF16)
    for hh in range(GDN_SCAN_HEADS):
        s_ref[hh] = states[hh]


def _gdn_scan(h, bsz, seq, u, w, qd, kd, attn, egl, b_out_norm):
    tm = min(GDN_TM, seq)
    nt = seq // tm
    nrow = max(tm // B_CHUNK, 8)
    gw = GDN_GROUP * B_CHUNK
    hg = GDN_SCAN_HEADS
    blk = lambda b, hp, i: (b, hp, i, 0)
    return pl.pallas_call(
        _gdn_scan_kernel,
        out_shape=jax.ShapeDtypeStruct((bsz * seq, B_HEADS * B_HEAD_DIM), BF16),
        grid=(bsz, B_HEADS // hg, nt),
        in_specs=[pl.BlockSpec((None, hg, tm, 128), blk), pl.BlockSpec((None, hg, tm, 128), blk),
                  pl.BlockSpec((None, hg, tm, 128), blk),
                  pl.BlockSpec((None, hg, B_HEAD_DIM, tm), lambda b, hp, i: (b, hp, 0, i)),
                  pl.BlockSpec((None, hg, tm // gw, B_CHUNK, gw), lambda b, hp, i: (b, hp, i, 0, 0)),
                  pl.BlockSpec((None, hg, nrow, 128), blk),
                  pl.BlockSpec((tm, hg * 128), lambda b, hp, i: (b * nt + i, EV_BZ // (hg * 128) + hp)),
                  pl.BlockSpec((1, 128), lambda b, hp, i: (0, 0))],
        out_specs=pl.BlockSpec((tm, hg * 128), lambda b, hp, i: (b * nt + i, hp)),
        scratch_shapes=[pltpu.VMEM((hg, B_HEAD_DIM, B_HEAD_DIM), F32)],
        compiler_params=_cparams(("parallel", "parallel", "arbitrary")),
        name="gdn_scan",
    )(u, w, qd, kd, attn, egl, h, b_out_norm.reshape(1, -1))


def _rope_kernel(inv_ref, cos_ref, sin_ref):
    tm = cos_ref.shape[0]
    pos = (pl.program_id(0) * tm + lax.broadcasted_iota(I32, (tm, 128), 0)).astype(F32)
    ang = pos * inv_ref[...]
    lane = lax.broadcasted_iota(I32, (tm, 128), 1)
    cos_ref[...] = jnp.cos(ang)
    sin_ref[...] = jnp.where(lane < 64, -jnp.sin(ang), jnp.sin(ang))


def _rope_tables(seq):
    half = C_QK_DIM // 2
    inv = 1.0 / (ROPE_BASE ** jnp.linspace(0.0, 1.0, half, dtype=F32))
    inv2 = jnp.concatenate([inv, inv]).reshape(1, C_QK_DIM)
    tm = min(1024, seq)
    return pl.pallas_call(
        _rope_kernel,
        out_shape=(jax.ShapeDtypeStruct((seq, C_QK_DIM), F32),) * 2,
        grid=(seq // tm,),
        in_specs=[pl.BlockSpec((1, C_QK_DIM), lambda i: (0, 0))],
        out_specs=(pl.BlockSpec((tm, C_QK_DIM), lambda i: (i, 0)),) * 2,
        compiler_params=_cparams(("parallel",)),
        name="rope_tables",
    )(inv2)


RET_TM = 512


def _retention_kernel(lg_ref, q_ref, k_ref, v_ref, z_ref, cos_ref, sin_ref, gn_ref, o_ref,
                      r_ref, dm_ref, zeta_ref, xi_ref):
    hd = pl.program_id(1)
    lg = lg_ref[0, hd]

    @pl.when(pl.program_id(2) == 0)
    def _():
        r_ref[...] = jnp.zeros(r_ref.shape, F32)
        ri = lax.broadcasted_iota(I32, (C_CHUNK, C_CHUNK), 0)
        ci = lax.broadcasted_iota(I32, (C_CHUNK, C_CHUNK), 1)
        diff = (ri - ci).astype(F32)
        dm_ref[...] = jnp.where(diff >= 0, jnp.exp(jnp.maximum(diff, 0.0) * lg), 0.0)
        zeta_ref[...] = jnp.exp((C_CHUNK - 1 - ri).astype(F32) * lg)
        rv = lax.broadcasted_iota(I32, (C_CHUNK, C_V_DIM), 0).astype(F32)
        xi_ref[...] = jnp.exp((rv + 1.0) * lg)

    g_chunk = lg_ref[1, hd]
    tm = q_ref.shape[0]
    state = r_ref[...]
    dmask = dm_ref[...]
    gain = gn_ref[...]
    for c in range(tm // C_CHUNK):
        rs = slice(c * C_CHUNK, (c + 1) * C_CHUNK)
        cos = cos_ref[rs, :]
        sin = sin_ref[rs, :]
        q = q_ref[rs, :]
        k = k_ref[rs, :]
        qr = q * cos + pltpu.roll(q, C_QK_DIM // 2, 1) * sin
        kr = (k * cos + pltpu.roll(k, C_QK_DIM // 2, 1) * sin) * (C_QK_DIM ** -0.5)
        vb = v_ref[rs, :].astype(BF16)
        qb = qr.astype(BF16)
        s = _dot_nt(qb, kr.astype(BF16)) * dmask
        o = _dot(s.astype(BF16), vb) + _dot(qb, state.astype(BF16)) * xi_ref[...]
        state = state * g_chunk + _dot((kr * zeta_ref[...]).T.astype(BF16), vb)
        mu = jnp.mean(o, axis=-1, keepdims=True)
        oc = o - mu
        var = jnp.mean(oc * oc, axis=-1, keepdims=True)
        y = oc * lax.rsqrt(var + EPS) * gain
        o_ref[rs, :] = (y * _silu(z_ref[rs, :])).astype(BF16)
    r_ref[...] = state


def _retention(h, bsz, seq, cos2, sin2, c_out_norm):
    tm = min(RET_TM, seq)
    nt = seq // tm
    lg = np.log1p(-np.exp2(-5.0 - np.arange(C_HEADS, dtype=np.float32))).astype(np.float32)
    lg = np.stack([lg, np.exp(np.float32(C_CHUNK) * lg).astype(np.float32)])
    return pl.pallas_call(
        _retention_kernel,
        out_shape=jax.ShapeDtypeStruct((bsz * seq, C_HEADS * C_V_DIM), BF16),
        grid=(bsz, C_HEADS, nt),
        in_specs=[pl.BlockSpec(memory_space=pltpu.SMEM),
                  pl.BlockSpec((tm, 128), lambda b, hd, i: (b * nt + i, OD_CQ // 128 + hd)),
                  pl.BlockSpec((tm, 128), lambda b, hd, i: (b * nt + i, OD_CK // 128 + hd)),
                  pl.BlockSpec((tm, 256), lambda b, hd, i: (b * nt + i, OD_CV // 256 + hd)),
                  pl.BlockSpec((tm, 256), lambda b, hd, i: (b * nt + i, OD_CZ // 256 + hd)),
                  pl.BlockSpec((tm, 128), lambda b, hd, i: (i, 0)),
                  pl.BlockSpec((tm, 128), lambda b, hd, i: (i, 0)),
                  pl.BlockSpec((1, C_V_DIM), lambda b, hd, i: (0, hd))],
        out_specs=pl.BlockSpec((tm, C_V_DIM), lambda b, hd, i: (b * nt + i, hd)),
        scratch_shapes=[pltpu.VMEM((C_QK_DIM, C_V_DIM), F32),
                        pltpu.VMEM((C_CHUNK, C_CHUNK), F32),
                        pltpu.VMEM((C_CHUNK, C_QK_DIM), F32),
                        pltpu.VMEM((C_CHUNK, C_V_DIM), F32)],
        compiler_params=_cparams(("parallel", "parallel", "arbitrary")),
        name="retention",
    )(jnp.asarray(lg), h, h, h, h, cos2, sin2, c_out_norm.reshape(1, -1))


def _s5_param_kernel(lre_ref, lim_ref, ldt_ref, lrex_ref, limx_ref, ldtx_ref, bre_ref, bim_ref,
                     are_ref, aim_ref, bbre_ref, bbim_ref):
    def disc(lre, lim, ldt):
        lr = jnp.minimum(lre, -1e-4)
        dt = jnp.exp(ldt)
        mag = jnp.exp(lr * dt)
        return lr, lim, mag * jnp.cos(lim * dt), mag * jnp.sin(lim * dt)

    _, _, a_re, a_im = disc(lre_ref[...], lim_ref[...], ldt_ref[...])
    are_ref[...] = a_re
    aim_ref[...] = a_im
    lr, li, ax_re, ax_im = disc(lrex_ref[...], limx_ref[...], ldtx_ref[...])
    den = lr * lr + li * li
    f_re = ((ax_re - 1.0) * lr + ax_im * li) / den
    f_im = (ax_im * lr - (ax_re - 1.0) * li) / den
    bbre_ref[...] = f_re * bre_ref[...] - f_im * bim_ref[...]
    bbim_ref[...] = f_re * bim_ref[...] + f_im * bre_ref[...]


def _s5_params(lam_re, lam_im, log_dt, b_re, b_im, c_re, c_im):
    g, p, ch = D_GROUPS, D_STATE, D_GROUP
    ldt = jnp.broadcast_to(log_dt[:, None], (g, p))
    rep = lambda a: jnp.repeat(a, ch, axis=1)
    vm = pl.BlockSpec(memory_space=pltpu.VMEM)
    a_re, a_im, bb_re, bb_im = pl.pallas_call(
        _s5_param_kernel,
        out_shape=(jax.ShapeDtypeStruct((g, p), F32), jax.ShapeDtypeStruct((g, p), F32),
                   jax.ShapeDtypeStruct((g, p * ch), F32), jax.ShapeDtypeStruct((g, p * ch), F32)),
        in_specs=[vm] * 8, out_specs=(vm,) * 4,
        name="s5_params",
    )(lam_re, lam_im, ldt, rep(lam_re), rep(lam_im), rep(ldt),
      b_re.reshape(g, p * ch), b_im.reshape(g, p * ch))
    eye = jnp.eye(D_SETS * 4, dtype=F32)

    def pack_b(bb):
        bb = bb.reshape(D_SETS, 16, p, ch)
        return jnp.einsum('sgpi,gh->sgihp', bb, eye).reshape(D_SETS, D_SET_CH, D_SET_ST)

    def pack_c(c):
        c = c.reshape(D_SETS, 16, ch, p)
        return jnp.einsum('sgjp,gh->sgphj', c, eye).reshape(D_SETS, D_SET_ST, D_SET_CH)

    bd = jnp.concatenate([pack_b(bb_re), pack_b(bb_im)], axis=-1).astype(BF16)
    slab = (D_SETS, S5_SLAB, 128)
    return (a_re.reshape(slab), a_im.reshape(slab), bd,
            pack_c(c_re).astype(BF16), pack_c(c_im).astype(BF16))


S5_TM = 256
S5_SLAB = D_SET_ST // 128


def _s5_kernel(u_ref, z_ref, are_ref, aim_ref, bd_ref, cre_ref, cim_ref, dskip_ref, wglu_ref, bglu_ref,
               o_ref, hre_ref, him_ref, y_ref, *x_refs):
    @pl.when(pl.program_id(1) == 0)
    def _():
        hre_ref[...] = jnp.zeros(hre_ref.shape, F32)
        him_ref[...] = jnp.zeros(him_ref.shape, F32)

    xre, xim = x_refs[:D_SETS], x_refs[D_SETS:]
    tm = u_ref.shape[0]
    u = u_ref[...]
    ub = u.astype(BF16)
    for s in range(D_SETS):
        bu = _dot(ub[:, s * D_SET_CH:(s + 1) * D_SET_CH], bd_ref[s])
        for k in range(S5_SLAB):
            xre[s][pl.ds(k, tm, stride=S5_SLAB), :] = bu[:, k * 128:(k + 1) * 128]
            xim[s][pl.ds(k, tm, stride=S5_SLAB), :] = bu[:, D_SET_ST + k * 128:D_SET_ST + (k + 1) * 128]
    a_re = [are_ref[s] for s in range(D_SETS)]
    a_im = [aim_ref[s] for s in range(D_SETS)]

    def step(t, carry):
        rows = pl.ds(pl.multiple_of(t * S5_SLAB, S5_SLAB), S5_SLAB)
        out = []
        for s in range(D_SETS):
            hr, hi = carry[2 * s], carry[2 * s + 1]
            xr = xre[s][rows, :] + a_re[s] * hr - a_im[s] * hi
            xi = xim[s][rows, :] + a_re[s] * hi + a_im[s] * hr
            xre[s][rows, :] = xr
            xim[s][rows, :] = xi
            out += [xr, xi]
        return tuple(out)

    carry = []
    for s in range(D_SETS):
        carry += [hre_ref[s], him_ref[s]]
    carry = lax.fori_loop(0, tm, step, tuple(carry), unroll=8)
    for s in range(D_SETS):
        hre_ref[s] = carry[2 * s]
        him_ref[s] = carry[2 * s + 1]
        x_r = jnp.concatenate([xre[s][pl.ds(k, tm, stride=S5_SLAB), :] for k in range(S5_SLAB)], axis=1)
        x_i = jnp.concatenate([xim[s][pl.ds(k, tm, stride=S5_SLAB), :] for k in range(S5_SLAB)], axis=1)
        y_ref[:, s * D_SET_CH:(s + 1) * D_SET_CH] = (
            _dot(x_r.astype(BF16), cre_ref[s]) - _dot(x_i.astype(BF16), cim_ref[s]))
    y = y_ref[...] + dskip_ref[...] * u
    y = 0.5 * y * (1.0 + jnp.tanh(math.sqrt(2.0 / math.pi) * (y + 0.044715 * (y * y * y))))
    gate = _sigmoid(_dot(y.astype(BF16), wglu_ref[...]) + bglu_ref[...])
    o_ref[...] = (y * gate * _silu(z_ref[...])).astype(BF16)


def _s5(h, bsz, seq, a_re, a_im, bd, cd_re, cd_im, d_skip, w_glu, b_glu):
    tm = min(S5_TM, seq)
    nt = seq // tm
    full = lambda *shape: pl.BlockSpec(shape, lambda b, i: (0,) * len(shape))
    width = D_GROUPS * D_GROUP
    return pl.pallas_call(
        _s5_kernel,
        out_shape=jax.ShapeDtypeStruct((bsz * seq, width), BF16),
        grid=(bsz, nt),
        in_specs=[pl.BlockSpec((tm, width), lambda b, i: (b * nt + i, OD_DU // width)),
                  pl.BlockSpec((tm, width), lambda b, i: (b * nt + i, OD_DZ // width)),
                  full(D_SETS, S5_SLAB, 128), full(D_SETS, S5_SLAB, 128),
                  full(D_SETS, D_SET_CH, 2 * D_SET_ST),
                  full(D_SETS, D_SET_ST, D_SET_CH), full(D_SETS, D_SET_ST, D_SET_CH),
                  full(1, width), full(width, width), full(1, width)],
        out_specs=pl.BlockSpec((tm, width), lambda b, i: (b * nt + i, 0)),
        scratch_shapes=([pltpu.VMEM((D_SETS, S5_SLAB, 128), F32), pltpu.VMEM((D_SETS, S5_SLAB, 128), F32),
                         pltpu.VMEM((tm, width), F32)]
                        + [pltpu.VMEM((tm * S5_SLAB, 128), F32)] * (2 * D_SETS)),
        compiler_params=_cparams(("parallel", "arbitrary")),
        name="s5",
    )(h, h, a_re, a_im, bd, cd_re, cd_im, d_skip.reshape(1, -1), w_glu.astype(BF16),
      b_glu.reshape(1, -1))


def _pack_even_w(w_in):
    sizes = (1024, A_KV_RANK, A_IDX_HEADS * A_IDX_DIM, A_IDX_DIM, A_IDX_HEADS, 1024,
             3 * 1024, B_HEADS, B_HEADS, 1024)
    parts, start = [], 0
    for s in sizes:
        parts.append(w_in[:, start:start + s])
        start += s
    aq, ckv, qi, ki, wi, az, bqkv, ba, bb, bz = parts
    pad = jnp.zeros((w_in.shape[0], EV_WIDTH - start), w_in.dtype)
    return jnp.concatenate([aq, az, bqkv, bz, qi, ckv, ki, wi, ba, bb, pad], axis=1).astype(BF16)


def _even_layer(x2d, bsz, seq, bias, norm_g, w_in, a_q_norm, a_kv_norm, w_kv_up, a_k_norm,
                b_conv, b_a_log, b_dt_bias, b_out_norm, w_out):
    h = _norm_proj(x2d, norm_g, _pack_even_w(w_in), tn=768)
    qta, qit, wt, kidx, katt, vt = _dsa_prep(h, bsz * seq // QB, a_q_norm, a_kv_norm, w_kv_up, a_k_norm)
    mix_a = _dsa_attend(h, bsz, seq, qta, qit, wt, kidx, katt, vt, bias)
    u, w, qd, kd, attn, egl = _gdn_prep(h, bsz, seq, b_conv, b_a_log, b_dt_bias)
    mix_b = _gdn_scan(h, bsz, seq, u, w, qd, kd, attn, egl, b_out_norm)
    return _out_proj(x2d, mix_a, mix_b, w_out)


def _odd_layer(x2d, bsz, seq, cos2, sin2, norm_g, w_in, c_out_norm, lam_re, lam_im, log_dt,
               b_re, b_im, c_re, c_im, d_skip, w_glu, b_glu, w_out):
    h = _norm_proj(x2d, norm_g, w_in.astype(BF16), tn=1024)
    mix_c = _retention(h, bsz, seq, cos2, sin2, c_out_norm)
    s5p = _s5_params(lam_re, lam_im, log_dt, b_re, b_im, c_re, c_im)
    mix_d = _s5(h, bsz, seq, *s5p, d_skip, w_glu, b_glu)
    return _out_proj(x2d, mix_c, mix_d, w_out)


def kernel(x, rel_bias, ev_norm, ev_w_in, ev_a_q_norm, ev_a_kv_norm, ev_w_kv_up, ev_a_k_norm,
           ev_b_conv, ev_b_a_log, ev_b_dt_bias, ev_b_out_norm, ev_w_out,
           od_norm, od_w_in, od_c_out_norm, od_lam_re, od_lam_im, od_log_dt,
           od_b_re, od_b_im, od_c_re, od_c_im, od_d_skip, od_w_glu, od_b_glu, od_w_out):
    bsz, seq, d = x.shape
    depth = ev_norm.shape[0] + od_norm.shape[0]
    x2d = x.reshape(bsz * seq, d)
    bias = _bias_table(rel_bias)
    cos2, sin2 = _rope_tables(seq)
    for layer in range(depth):
        i = layer // 2
        if layer % 2 == 0:
            x2d = _even_layer(x2d, bsz, seq, bias, ev_norm[i], ev_w_in[i], ev_a_q_norm[i],
                              ev_a_kv_norm[i], ev_w_kv_up[i], ev_a_k_norm[i], ev_b_conv[i],
                              ev_b_a_log[i], ev_b_dt_bias[i], ev_b_out_norm[i], ev_w_out[i])
        else:
            x2d = _odd_layer(x2d, bsz, seq, cos2, sin2, od_norm[i], od_w_in[i], od_c_out_norm[i],
                             od_lam_re[i], od_lam_im[i], od_log_dt[i], od_b_re[i], od_b_im[i],
                             od_c_re[i], od_c_im[i], od_d_skip[i], od_w_glu[i], od_b_glu[i], od_w_out[i])
    return x2d.reshape(bsz, seq, d)
```

```python
import functools
import math

import numpy as np
import jax
import jax.numpy as jnp
from jax import lax
from jax.experimental import pallas as pl
from jax.experimental.pallas import tpu as pltpu

F32 = jnp.float32
BF16 = jnp.bfloat16
I32 = jnp.int32

D_MODEL = 1024
EPS = 1e-6
LOG2E = 1.4426950408889634
NEG_INF = float("-inf")
INT_MIN = -(2 ** 31)

A_HEADS, A_HEAD_DIM, A_KV_RANK = 16, 64, 128
A_IDX_HEADS, A_IDX_DIM, A_TOPK_MAX = 8, 64, 256
QB = 128
REL_BUCKETS, REL_MAX_DIST = 32, 128
VT_ROWS = 80
SB = 2 * QB
SCORE_ROWS = 4 * QB
BIAS_ROWS = 5 * QB
MASKED = -(2.0 ** 100)
M_INIT = -(2.0 ** 60)
TIE_WALK_MAX = 8
B_HEADS, B_HEAD_DIM, CONV_K, B_CHUNK = 8, 128, 4, 64
C_HEADS, C_QK_DIM, C_V_DIM, C_CHUNK = 4, 128, 256, 128
ROPE_BASE = 10000.0
D_GROUP, D_STATE, D_GROUPS = 16, 64, 64
D_SETS, D_SET_CH, D_SET_ST = 4, 256, 1024

EV_AQ, EV_AZ, EV_BQ, EV_BK, EV_BV, EV_BZ, EV_QI, EV_CKV, EV_MISC = (
    0, 1024, 2048, 3072, 4096, 5120, 6144, 6656, 6784)
EV_WIDTH = 6912
MISC_KI, MISC_WI, MISC_BA, MISC_BB = 0, 64, 72, 80
OD_CQ, OD_CK, OD_CV, OD_CZ, OD_DU, OD_DZ = 0, 512, 1024, 2048, 3072, 4096
OD_WIDTH = 5120

VMEM_LIMIT = 48 * 1024 * 1024


def _cparams(sem):
    return pltpu.CompilerParams(dimension_semantics=sem, vmem_limit_bytes=VMEM_LIMIT)


def _dot(a, b):
    return jnp.dot(a, b, preferred_element_type=F32)


def _dot_nt(a, b):
    return lax.dot_general(a, b, (((1,), (1,)), ((), ())), preferred_element_type=F32)


def _split_bf16(x, n):
    parts = []
    for _ in range(n):
        p = x.astype(BF16)
        parts.append(p)
        x = x - p.astype(F32)
    return parts


def _dot_sel(sel, x):
    sel = sel.astype(BF16)
    hi, mid, lo = _split_bf16(x, 3)
    return _dot(sel, hi) + (_dot(sel, mid) + _dot(sel, lo))


def _dot_xsel(x, sel):
    sel = sel.astype(BF16)
    hi, mid, lo = _split_bf16(x, 3)
    return _dot(hi, sel) + (_dot(mid, sel) + _dot(lo, sel))


def _dot_x3(a, b):
    ah, al = _split_bf16(a, 2)
    bh, bl = _split_bf16(b, 2)
    return _dot(ah, bh) + (_dot(ah, bl) + _dot(al, bh))


def _sigmoid(x):
    return 1.0 / (1.0 + jnp.exp(-x))


def _silu(x):
    return x * _sigmoid(x)


def _norm_proj_kernel(x_ref, g_ref, w_ref, o_ref, xn_ref):
    @pl.when(pl.program_id(1) == 0)
    def _():
        x = x_ref[...]
        ms = jnp.mean(x * x, axis=-1, keepdims=True)
        xn_ref[...] = (x * lax.rsqrt(ms + EPS) * g_ref[...]).astype(BF16)

    o_ref[...] = _dot(xn_ref[...], w_ref[...])


def _norm_proj(x2d, gain, w_bf16, tn, tm=1024):
    t, d = x2d.shape
    n = w_bf16.shape[1]
    tm = min(tm, t)
    return pl.pallas_call(
        _norm_proj_kernel,
        out_shape=jax.ShapeDtypeStruct((t, n), F32),
        grid=(t // tm, n // tn),
        in_specs=[pl.BlockSpec((tm, d), lambda i, j: (i, 0)),
                  pl.BlockSpec((1, d), lambda i, j: (0, 0)),
                  pl.BlockSpec((d, tn), lambda i, j: (0, j))],
        out_specs=pl.BlockSpec((tm, tn), lambda i, j: (i, j)),
        scratch_shapes=[pltpu.VMEM((tm, d), BF16)],
        compiler_params=_cparams(("parallel", "arbitrary")),
        name="norm_proj",
    )(x2d, gain.reshape(1, d), w_bf16)


def _out_proj_kernel(x_ref, a_ref, b_ref, wa_ref, wb_ref, o_ref):
    o_ref[...] = x_ref[...] + _dot(a_ref[...], wa_ref[...]) + _dot(b_ref[...], wb_ref[...])


def _out_proj(x2d, mix_a, mix_b, w_out, tm=512):
    t, d = x2d.shape
    half = mix_a.shape[1]
    tm = min(tm, t)
    wa = w_out[:half].astype(BF16)
    wb = w_out[half:].astype(BF16)
    return pl.pallas_call(
        _out_proj_kernel,
        out_shape=jax.ShapeDtypeStruct((t, d), F32),
        grid=(t // tm,),
        in_specs=[pl.BlockSpec((tm, d), lambda i: (i, 0)),
                  pl.BlockSpec((tm, half), lambda i: (i, 0)),
                  pl.BlockSpec((tm, half), lambda i: (i, 0)),
                  pl.BlockSpec((half, d), lambda i: (0, 0)),
                  pl.BlockSpec((half, d), lambda i: (0, 0))],
        out_specs=pl.BlockSpec((tm, d), lambda i: (i, 0)),
        compiler_params=_cparams(("parallel",)),
        name="out_proj",
    )(x2d, mix_a, mix_b, wa, wb)


def _t5_bucket_starts():
    exact = REL_BUCKETS // 2
    n = np.arange(0, 4 * REL_MAX_DIST, dtype=np.int64)
    ratio = np.maximum(n, 1).astype(np.float32) / np.float32(exact)
    large = exact + (np.log(ratio).astype(np.float32) / np.float32(math.log(REL_MAX_DIST / exact))
                     * np.float32(REL_BUCKETS - exact)).astype(np.int32)
    bucket = np.where(n < exact, n, np.minimum(large, REL_BUCKETS - 1))
    starts = [int(np.argmax(bucket >= b)) for b in range(REL_BUCKETS)]
    assert all(bucket[s] == b for b, s in enumerate(starts)) and np.all(np.diff(bucket) >= 0)
    assert starts[-1] <= QB, "distances beyond one key block must share the last bucket"
    return starts


_BUCKET_STARTS = _t5_bucket_starts()


def _bias_table_kernel(rb_ref, o_ref):
    row = lax.broadcasted_iota(I32, (BIAS_ROWS, QB), 0)
    lane = lax.broadcasted_iota(I32, (BIAS_ROWS, QB), 1)
    dist = lane + 3 * QB - row
    for h in range(A_HEADS):
        val = jnp.full((BIAS_ROWS, QB), rb_ref[0, h], F32)
        for b in range(1, REL_BUCKETS):
            val = jnp.where(dist >= _BUCKET_STARTS[b], rb_ref[b, h], val)
        val = (val - rb_ref[REL_BUCKETS - 1, h]) * LOG2E
        o_ref[:, h * QB:(h + 1) * QB] = jnp.where(dist >= 0, val, 0.0)


def _bias_table(rel_bias):
    return pl.pallas_call(
        _bias_table_kernel,
        out_shape=jax.ShapeDtypeStruct((BIAS_ROWS, A_HEADS * QB), F32),
        in_specs=[pl.BlockSpec(memory_space=pltpu.SMEM)],
        out_specs=pl.BlockSpec(memory_space=pltpu.VMEM),
        name="dsa_bias_table",
    )(rel_bias)


def _dsa_prep_kernel(aq_ref, qi_ref, ckv_ref, misc_ref, gq_ref, gkv_ref, wkv_ref, gk_ref,
                     qta_ref, qit_ref, wt_ref, kidx_ref, katt_ref, vt_ref):
    zeros = jnp.zeros((QB, A_HEAD_DIM), F32)
    aq = aq_ref[...]
    gq = gq_ref[...]
    ident = jnp.where(lax.broadcasted_iota(I32, (QB, QB), 0) == lax.broadcasted_iota(I32, (QB, QB), 1),
                      1.0, 0.0).astype(BF16)
    for h in range(A_HEADS):
        q = aq[:, h * A_HEAD_DIM:(h + 1) * A_HEAD_DIM]
        ms = jnp.mean(q * q, axis=-1, keepdims=True)
        qn = q * lax.rsqrt(ms + EPS) * gq * (A_HEAD_DIM ** -0.5 * LOG2E)
        qta_ref[0:QB, h * QB:(h + 1) * QB] = jnp.concatenate([qn, zeros], axis=1).T.astype(BF16)
        qta_ref[QB:2 * QB, h * QB:(h + 1) * QB] = ident
    qi = qi_ref[...]
    for h in range(A_IDX_HEADS):
        qih = qi[:, h * A_IDX_DIM:(h + 1) * A_IDX_DIM]
        qit_ref[:, h * QB:(h + 1) * QB] = jnp.concatenate([qih, zeros], axis=1).T.astype(BF16)
    misc = misc_ref[...]
    wt_ref[...] = misc.T[MISC_WI:MISC_WI + A_IDX_HEADS, :] * (A_IDX_HEADS ** -0.5 * A_IDX_DIM ** -0.5)
    c = ckv_ref[...]
    cn = c * lax.rsqrt(jnp.mean(c * c, axis=-1, keepdims=True) + EPS) * gkv_ref[...]
    kv = _dot(cn.astype(BF16), wkv_ref[...])
    k = kv[:, :A_HEAD_DIM]
    kn = k * lax.rsqrt(jnp.mean(k * k, axis=-1, keepdims=True) + EPS) * gk_ref[...]
    kidx_ref[...] = jnp.concatenate([misc[:, MISC_KI:MISC_KI + A_IDX_DIM], zeros], axis=1).astype(BF16)
    katt_ref[...] = jnp.concatenate([kn, zeros], axis=1).astype(BF16)
    kvt = kv.T
    vt_ref[0:A_HEAD_DIM, :] = kvt[A_HEAD_DIM:, :].astype(BF16)
    ones_row = lax.broadcasted_iota(I32, (VT_ROWS - A_HEAD_DIM, QB), 0) == 0
    vt_ref[A_HEAD_DIM:, :] = jnp.where(ones_row, 1.0, 0.0).astype(BF16)


def _dsa_prep(h, nblk, a_q_norm, a_kv_norm, w_kv_up, a_k_norm):
    cb = lambda width, off: off // width
    return pl.pallas_call(
        _dsa_prep_kernel,
        out_shape=(jax.ShapeDtypeStruct((nblk, 2 * QB, A_HEADS * QB), BF16),
                   jax.ShapeDtypeStruct((nblk, QB, A_IDX_HEADS * QB), BF16),
                   jax.ShapeDtypeStruct((nblk, A_IDX_HEADS, QB), F32),
                   jax.ShapeDtypeStruct((nblk, QB, 128), BF16),
                   jax.ShapeDtypeStruct((nblk, QB, 128), BF16),
                   jax.ShapeDtypeStruct((nblk, VT_ROWS, QB), BF16)),
        grid=(nblk,),
        in_specs=[pl.BlockSpec((QB, 1024), lambda i: (i, cb(1024, EV_AQ))),
                  pl.BlockSpec((QB, 512), lambda i: (i, cb(512, EV_QI))),
                  pl.BlockSpec((QB, 128), lambda i: (i, cb(128, EV_CKV))),
                  pl.BlockSpec((QB, 128), lambda i: (i, cb(128, EV_MISC))),
                  pl.BlockSpec((1, A_HEAD_DIM), lambda i: (0, 0)),
                  pl.BlockSpec((1, A_KV_RANK), lambda i: (0, 0)),
                  pl.BlockSpec((A_KV_RANK, 2 * A_HEAD_DIM), lambda i: (0, 0)),
                  pl.BlockSpec((1, A_HEAD_DIM), lambda i: (0, 0))],
        out_specs=(pl.BlockSpec((None, 2 * QB, A_HEADS * QB), lambda i: (i, 0, 0)),
                   pl.BlockSpec((None, QB, A_IDX_HEADS * QB), lambda i: (i, 0, 0)),
                   pl.BlockSpec((None, A_IDX_HEADS, QB), lambda i: (i, 0, 0)),
                   pl.BlockSpec((None, QB, 128), lambda i: (i, 0, 0)),
                   pl.BlockSpec((None, QB, 128), lambda i: (i, 0, 0)),
                   pl.BlockSpec((None, VT_ROWS, QB), lambda i: (i, 0, 0))),
        compiler_params=_cparams(("parallel",)),
        name="dsa_prep",
    )(h, h, h, h, a_q_norm.reshape(1, -1), a_kv_norm.reshape(1, -1), w_kv_up.astype(BF16),
      a_k_norm.reshape(1, -1))


def _dsa_kernel(topk, qit_ref, wt_ref, qta_ref, kidx_ref, katt_ref, vt_ref, bias_ref, az_ref, o_ref,
                strip_ref, x_ref, m_ref, acc_ref, lg_ref, mx_ref, p_ref):
    qt = pl.program_id(1)
    t0 = qt * QB
    n_sc = qt // 4 + 1
    n_sb = qt // 2 + 1
    lane_s = lax.broadcasted_iota(I32, (SCORE_ROWS, QB), 1)
    row_s = lax.broadcasted_iota(I32, (SCORE_ROWS, QB), 0)

    w = wt_ref[...]

    def score_body(j, carry):
        kblk = kidx_ref[pl.ds(j * 4, 4)].reshape(SCORE_ROWS, 128)
        tot = None
        for hp in range(A_IDX_HEADS // 2):
            s = _dot(kblk, qit_ref[:, hp * SB:(hp + 1) * SB])
            for e in range(2):
                h = 2 * hp + e
                term = jnp.maximum(s[:, e * QB:(e + 1) * QB], 0.0) * w[h:h + 1, :]
                tot = term if tot is None else tot + term
        tot = jnp.where(j * SCORE_ROWS + row_s <= t0 + lane_s, tot, NEG_INF)
        bits = pltpu.bitcast(tot, I32)
        strip_ref[pl.ds(pl.multiple_of(j * SCORE_ROWS, SCORE_ROWS), SCORE_ROWS), :] = (
            bits ^ ((bits >> 31) & 0x7FFFFFFF))
        return carry

    lax.fori_loop(0, n_sc, score_body, 0)

    def count(pred):
        def body(j, acc):
            r0 = pl.multiple_of(j * SCORE_ROWS, SCORE_ROWS)
            m = jnp.where(pred(strip_ref[pl.ds(r0, SCORE_ROWS), :], r0), 1, 0)
            return acc + m.reshape(SCORE_ROWS // 32, 32, QB).sum(axis=0)
        acc = lax.fori_loop(0, n_sc, body, jnp.zeros((32, QB), I32))
        return acc.reshape(4, 8, QB).sum(axis=0).sum(axis=0, keepdims=True)

    count_ge = lambda cand: count(lambda key, r0: key >= cand)
    c0 = count_ge(jnp.zeros((1, QB), I32))
    nonneg = c0 >= topk
    thr0 = jnp.where(nonneg, 0, INT_MIN)
    cnt0 = jnp.where(nonneg, c0, n_sc * SCORE_ROWS)

    def bit_steps(counter, n):
        def one(_, carry):
            i, thr, cnt = carry
            cand = thr | lax.shift_left(jnp.int32(1), 30 - i)
            c = counter(cand)
            ok = c >= topk
            return i + 1, jnp.where(ok, cand, thr), jnp.where(ok, c, cnt)
        return lambda carry: lax.fori_loop(0, n, one, carry)

    state = bit_steps(count_ge, 15)((jnp.int32(0), thr0, cnt0))
    _, thr, cnt = lax.while_loop(lambda c: (c[0] < 31) & (jnp.max(c[2]) > topk),
                                 bit_steps(count_ge, 4), state)

    x_ref[...] = jnp.full((1, QB), 2 ** 30, I32)
    tied = cnt > topk

    @pl.when(jnp.max(cnt) > topk)
    def _():
        above = count(lambda key, r0: key > thr)
        need = jnp.where(tied, topk - above, 0)
        max_need = jnp.max(need)

        @pl.when(max_need <= TIE_WALK_MAX)
        def _():
            def next_tie(prev):
                def body(j, acc):
                    r0 = pl.multiple_of(j * SCORE_ROWS, SCORE_ROWS)
                    idx = r0 + row_s
                    hit = (strip_ref[pl.ds(r0, SCORE_ROWS), :] == thr) & (idx > prev)
                    return jnp.minimum(acc, jnp.where(hit, idx, 2 ** 30)
                                       .reshape(SCORE_ROWS // 32, 32, QB).min(axis=0))
                acc = lax.fori_loop(0, n_sc, body, jnp.full((32, QB), 2 ** 30, I32))
                return acc.reshape(4, 8, QB).min(axis=0).min(axis=0, keepdims=True)

            last = lax.fori_loop(0, max_need, lambda r, prev: jnp.where(r < need, next_tie(prev), prev),
                                 jnp.full((1, QB), -1, I32))
            x_ref[...] = jnp.where(tied, last + 1, 2 ** 30)

        @pl.when(max_need > TIE_WALK_MAX)
        def _():
            def tie_body(i, xb):
                cand = xb | lax.shift_left(jnp.int32(1), 14 - i)
                c = count(lambda key, r0: (key == thr) & (r0 + row_s < cand))
                return jnp.where(c <= need, cand, xb)

            xb = lax.fori_loop(0, 15, tie_body, jnp.zeros((1, QB), I32))
            x_ref[...] = jnp.where(tied, xb, 2 ** 30)

    xb = x_ref[...]

    m_ref[...] = jnp.full(m_ref.shape, M_INIT, F32)
    acc_ref[...] = jnp.zeros(acc_ref.shape, F32)
    n_lt = A_HEADS * QB // SB
    row_a = lax.broadcasted_iota(I32, (SB, QB), 0)
    lane_a = lax.broadcasted_iota(I32, (SB, QB), 1)

    def logits(j, slot, near):
        r0 = pl.multiple_of(j * SB, SB)
        key = strip_ref[pl.ds(r0, SB), :]
        idx = r0 + row_a
        sel = (key > thr) | ((key == thr) & (idx < xb))
        if near:
            sel = sel & (idx <= t0 + lane_a)
            b0 = pl.multiple_of((2 * j - qt + 3) * QB, QB)
        pen = jnp.where(sel, 0.0, MASKED).astype(BF16)
        kaug = jnp.concatenate([katt_ref[pl.ds(2 * j, 2)].reshape(SB, 128), pen], axis=1)
        for lt in range(n_lt):
            cols = slice(lt * SB, (lt + 1) * SB)
            lg = _dot(kaug, qta_ref[:, cols])
            if near:
                lg = lg + bias_ref[pl.ds(b0, SB), cols]
            lg = lg.astype(BF16)
            lg_ref[slot, :, cols] = lg
            mx_ref[slot, :, cols] = jnp.max(lg, axis=0, keepdims=True).astype(F32)

    def accumulate(j, slot):
        vt = jnp.concatenate([vt_ref[2 * j], vt_ref[2 * j + 1]], axis=1)
        m_old = m_ref[...]
        m_new = jnp.maximum(m_old, mx_ref[slot])
        m_ref[...] = m_new
        m_b = m_new.astype(BF16)
        for lt in range(n_lt):
            cols = slice(lt * SB, (lt + 1) * SB)
            p_ref[:, cols] = jnp.exp2(lg_ref[slot, :, cols] - m_b[:, cols])
        acc_ref[...] = jnp.exp2(m_old - m_new) * acc_ref[...] + _dot(vt, p_ref[...])

    n_far = jnp.maximum(n_sb - 2, 0)
    n_pair = jnp.maximum(n_far - 1, 0) // 2
    n_rem = n_far - 2 * n_pair

    @pl.when(n_far > 0)
    def _():
        logits(0, 0, False)

    def pair_body(jj, carry):
        j = 2 * jj
        logits(j + 1, 1, False)
        accumulate(j, 0)
        logits(j + 2, 0, False)
        accumulate(j + 1, 1)
        return carry

    lax.fori_loop(0, n_pair, pair_body, 0)

    @pl.when(n_rem == 1)
    def _():
        accumulate(n_far - 1, 0)

    @pl.when(n_rem == 2)
    def _():
        logits(n_far - 1, 1, False)
        accumulate(n_far - 2, 0)
        accumulate(n_far - 1, 1)

    @pl.when(n_sb >= 2)
    def _():
        logits(n_sb - 2, 0, True)
        accumulate(n_sb - 2, 0)

    logits(n_sb - 1, 0, True)
    accumulate(n_sb - 1, 0)

    acc = acc_ref[...]
    o_t = acc[0:A_HEAD_DIM, :] / acc[A_HEAD_DIM:A_HEAD_DIM + 1, :]
    pad = jnp.zeros((QB - A_HEAD_DIM, QB), F32)
    pieces = []
    for h in range(A_HEADS):
        blk = jnp.concatenate([o_t[:, h * QB:(h + 1) * QB], pad], axis=0)
        pieces.append(blk.T[:, 0:A_HEAD_DIM])
    att = jnp.concatenate(pieces, axis=1)
    o_ref[...] = (att * _silu(az_ref[...])).astype(BF16)


def _dsa_attend(h, bsz, seq, qta, qit, wt, kidx, katt, vt, bias):
    nq = seq // QB
    topk = min(A_TOPK_MAX, seq // 4)
    blk = lambda b, q: (b * nq + q, 0, 0)
    per_batch = lambda b, q: (b, 0, 0)
    once = pl.Buffered(1)
    return pl.pallas_call(
        functools.partial(_dsa_kernel, topk),
        out_shape=jax.ShapeDtypeStruct((bsz * seq, A_HEADS * A_HEAD_DIM), BF16),
        grid=(bsz, nq),
        in_specs=[pl.BlockSpec((None, QB, A_IDX_HEADS * QB), blk),
                  pl.BlockSpec((None, A_IDX_HEADS, QB), blk),
                  pl.BlockSpec((None, 2 * QB, A_HEADS * QB), blk),
                  pl.BlockSpec((nq, QB, 128), per_batch, pipeline_mode=once),
                  pl.BlockSpec((nq, QB, 128), per_batch, pipeline_mode=once),
                  pl.BlockSpec((nq, VT_ROWS, QB), per_batch, pipeline_mode=once),
                  pl.BlockSpec((BIAS_ROWS, A_HEADS * QB), lambda b, q: (0, 0), pipeline_mode=once),
                  pl.BlockSpec((QB, 1024), lambda b, q: (b * nq + q, EV_AZ // 1024))],
        out_specs=pl.BlockSpec((QB, 1024), lambda b, q: (b * nq + q, 0)),
        scratch_shapes=[pltpu.VMEM((seq, QB), I32),
                        pltpu.VMEM((1, QB), I32),
                        pltpu.VMEM((1, A_HEADS * QB), F32),
                        pltpu.VMEM((VT_ROWS, A_HEADS * QB), F32),
                        pltpu.VMEM((2, SB, A_HEADS * QB), BF16),
                        pltpu.VMEM((2, 1, A_HEADS * QB), F32),
                        pltpu.VMEM((SB, A_HEADS * QB), BF16)],
        compiler_params=_cparams(("arbitrary", "arbitrary")),
        name="dsa_attend",
    )(qit, wt, qta, kidx, katt, vt, bias, h)


GDN_TM = 512
GDN_GROUP = 4
GDN_SCAN_HEADS = 4
GDN_PREP_HEADS = 2


def _gdn_prep_kernel(alog_ref, dtb_ref, q_ref, k_ref, v_ref, qh_ref, kh_ref, vh_ref, misc_ref,
                     cq_ref, ck_ref, cv_ref,
                     u_ref, w_ref, qd_ref, kdt_ref, attn_ref, egl_ref):
    i = pl.program_id(1)
    hp = pl.program_id(2)
    tm = q_ref.shape[0]
    nchunk = tm // B_CHUNK
    heads = range(GDN_PREP_HEADS)
    lanes = [slice(hh * 128, (hh + 1) * 128) for hh in heads]

    def conv_silu(x_ref, halo_ref, c_ref, ln):
        halo = jnp.where(i > 0, halo_ref[:, ln], 0.0)
        ext = jnp.concatenate([halo, x_ref[:, ln]], axis=0)
        cw = c_ref[:, ln]
        y = ext[8:, :] * cw[CONV_K - 1:CONV_K, :]
        for d in range(1, CONV_K):
            y = y + pltpu.roll(ext, d, 0)[8:, :] * cw[CONV_K - 1 - d:CONV_K - d, :]
        return _silu(y)

    q = [conv_silu(q_ref, qh_ref, cq_ref, ln) for ln in lanes]
    k = [conv_silu(k_ref, kh_ref, ck_ref, ln) for ln in lanes]
    v = [conv_silu(v_ref, vh_ref, cv_ref, ln) for ln in lanes]
    qn = [x * lax.rsqrt(jnp.sum(x * x, axis=-1, keepdims=True) + EPS) * (B_HEAD_DIM ** -0.5) for x in q]
    kn = [x * lax.rsqrt(jnp.sum(x * x, axis=-1, keepdims=True) + EPS) for x in k]

    misc = misc_ref[...]
    sel_r = lax.broadcasted_iota(I32, (128, 256), 0)
    sel_c = lax.broadcasted_iota(I32, (128, 256), 1)
    beta, g = [], []
    for hh in heads:
        hd = hp * GDN_PREP_HEADS + hh
        pick = jnp.where(sel_r == jnp.where(sel_c < 128, MISC_BA + hd, MISC_BB + hd), 1.0, 0.0)
        bab = _dot_xsel(misc, pick)
        beta.append(_sigmoid(bab[:, 128:256]))
        xg = bab[:, 0:128] + dtb_ref[hd]
        softplus = jnp.maximum(xg, 0.0) + jnp.log(1.0 + jnp.exp(-jnp.abs(xg)))
        g.append(-jnp.exp(jnp.full((1, 128), alog_ref[hd], F32)) * softplus)

    sl = 256
    r2 = lax.broadcasted_iota(I32, (sl, sl), 0)
    c2 = lax.broadcasted_iota(I32, (sl, sl), 1)
    same = (r2 // B_CHUNK) == (c2 // B_CHUNK)
    tri_blk = jnp.concatenate([jnp.where(same & (c2 <= r2), 1.0, 0.0), jnp.where(same, 1.0, 0.0)], axis=0)
    nsl = tm // sl
    gc, gl = [], []
    for hh in heads:
        both = _dot_sel(tri_blk, jnp.concatenate([g[hh][s * sl:(s + 1) * sl, :] for s in range(nsl)], axis=1))
        gc.append(jnp.concatenate([both[0:sl, s * 128:(s + 1) * 128] for s in range(nsl)], axis=0))
        gl.append(jnp.concatenate([both[sl:2 * sl, s * 128:(s + 1) * 128] for s in range(nsl)], axis=0))

    kb = [kn[hh] * beta[hh] for hh in heads]
    vb = [v[hh] * beta[hh] for hh in heads]
    egc = [jnp.exp(x) for x in gc]
    kbg = [kb[hh] * egc[hh] for hh in heads]
    for hh in heads:
        qd_ref[hh] = (qn[hh] * egc[hh]).astype(BF16)
        kd = kn[hh] * jnp.exp(gl[hh] - gc[hh])
        for m in range(tm // 128):
            kdt_ref[hh, :, m * 128:(m + 1) * 128] = kd[m * 128:(m + 1) * 128, :].T.astype(BF16)

    gw = GDN_GROUP * B_CHUNK
    ri = lax.broadcasted_iota(I32, (B_CHUNK, gw), 0)
    lj = lax.broadcasted_iota(I32, (B_CHUNK, gw), 1)
    lb = lj // B_CHUNK
    lj = lj % B_CHUNK
    bmask = (lax.broadcasted_iota(I32, (gw, gw), 0) // B_CHUNK
             == lax.broadcasted_iota(I32, (gw, gw), 1) // B_CHUNK)

    def fold(x):
        xm = jnp.where(bmask, x, 0.0)
        out = xm[0:B_CHUNK, :]
        for c in range(1, GDN_GROUP):
            out = out + xm[c * B_CHUNK:(c + 1) * B_CHUNK, :]
        return out

    def bdiag(cat):
        return jnp.where(bmask, jnp.concatenate([cat] * GDN_GROUP, axis=0), 0.0)

    probs = [(hh, grp) for grp in range(tm // gw) for hh in heads]
    a_cat = []
    for hh, grp in probs:
        rs = slice(grp * gw, (grp + 1) * gw)
        gcs = gc[hh][rs, :]
        col = jnp.concatenate([gcs[0:B_CHUNK, :]] * 2, axis=1)
        for c in range(1, GDN_GROUP):
            col = jnp.where(lb == c, jnp.concatenate([gcs[c * B_CHUNK:(c + 1) * B_CHUNK, :]] * 2, axis=1), col)
        rowf = jnp.concatenate([gcs[m * 128:(m + 1) * 128, :].T[0:B_CHUNK, :] for m in range(gw // 128)],
                               axis=1)
        decay = jnp.exp(jnp.where(ri >= lj, col - rowf, NEG_INF))
        kg = kn[hh][rs, :]
        a_cat.append(jnp.where(ri > lj, fold(_dot_nt(kb[hh][rs, :], kg)) * decay, 0.0))
        attn_ref[hh, grp] = jnp.where(ri >= lj, fold(_dot_nt(qn[hh][rs, :], kg)) * decay, 0.0).astype(BF16)
    eye = jnp.where(ri == lj, 1.0, 0.0)
    inv = [eye - a for a in a_cat]
    pw = [_dot_x3(a, bdiag(a)) for a in a_cat]
    for step in range(5):
        for p in range(len(probs)):
            if step < 4:
                both = _dot_x3(jnp.concatenate([inv[p], pw[p]], axis=0), bdiag(pw[p]))
                inv[p] = inv[p] + both[0:B_CHUNK, :]
                pw[p] = both[B_CHUNK:, :]
            else:
                inv[p] = inv[p] + _dot_x3(inv[p], bdiag(pw[p]))
    for p, (hh, grp) in enumerate(probs):
        rs = slice(grp * gw, (grp + 1) * gw)
        sol = _dot_x3(bdiag(inv[p]), jnp.concatenate([vb[hh][rs, :], kbg[hh][rs, :]], axis=1))
        u_ref[hh, rs, :] = sol[:, 0:B_HEAD_DIM]
        w_ref[hh, rs, :] = sol[:, B_HEAD_DIM:].astype(BF16)
    for hh in heads:
        for c in range(nchunk):
            egl_ref[hh, c:c + 1, :] = jnp.exp(gl[hh][c * B_CHUNK:c * B_CHUNK + 1, :])
        if nchunk < 8:
            egl_ref[hh, nchunk:, :] = jnp.zeros((8 - nchunk, 128), F32)


def _gdn_prep(h, bsz, seq, b_conv, b_a_log, b_dt_bias):
    tm = min(GDN_TM, seq)
    nt = seq // tm
    nrow = max(tm // B_CHUNK, 8)
    gw = GDN_GROUP * B_CHUNK
    hs = (bsz, B_HEADS, seq, B_HEAD_DIM)
    hg = GDN_PREP_HEADS
    hw = hg * 128
    cur = lambda off: (lambda b, i, hp: (b * nt + i, off // hw + hp))
    halo = lambda off: (lambda b, i, hp: (jnp.maximum((b * nt + i) * (tm // 8) - 1, 0), off // hw + hp))
    cw = lambda off: (lambda b, i, hp: (0, off // hw + hp))
    out = lambda b, i, hp: (b, hp, i, 0)
    smem = pl.BlockSpec(memory_space=pltpu.SMEM)
    return pl.pallas_call(
        _gdn_prep_kernel,
        out_shape=(jax.ShapeDtypeStruct(hs, F32), jax.ShapeDtypeStruct(hs, BF16),
                   jax.ShapeDtypeStruct(hs, BF16),
                   jax.ShapeDtypeStruct((bsz, B_HEADS, B_HEAD_DIM, seq), BF16),
                   jax.ShapeDtypeStruct((bsz, B_HEADS, seq // gw, B_CHUNK, gw), BF16),
                   jax.ShapeDtypeStruct((bsz, B_HEADS, nt * nrow, 128), F32)),
        grid=(bsz, nt, B_HEADS // hg),
        in_specs=[smem, smem,
                  pl.BlockSpec((tm, hw), cur(EV_BQ)), pl.BlockSpec((tm, hw), cur(EV_BK)),
                  pl.BlockSpec((tm, hw), cur(EV_BV)),
                  pl.BlockSpec((8, hw), halo(EV_BQ)), pl.BlockSpec((8, hw), halo(EV_BK)),
                  pl.BlockSpec((8, hw), halo(EV_BV)),
                  pl.BlockSpec((tm, 128), lambda b, i, hp: (b * nt + i, EV_MISC // 128)),
                  pl.BlockSpec((CONV_K, hw), cw(0)), pl.BlockSpec((CONV_K, hw), cw(1024)),
                  pl.BlockSpec((CONV_K, hw), cw(2048))],
        out_specs=(pl.BlockSpec((None, hg, tm, 128), out), pl.BlockSpec((None, hg, tm, 128), out),
                   pl.BlockSpec((None, hg, tm, 128), out),
                   pl.BlockSpec((None, hg, B_HEAD_DIM, tm), lambda b, i, hp: (b, hp, 0, i)),
                   pl.BlockSpec((None, hg, tm // gw, B_CHUNK, gw), lambda b, i, hp: (b, hp, i, 0, 0)),
                   pl.BlockSpec((None, hg, nrow, 128), out)),
        compiler_params=_cparams(("parallel", "parallel", "parallel")),
        name="gdn_prep",
    )(b_a_log, b_dt_bias, h, h, h, h, h, h, h, b_conv, b_conv, b_conv)


def _gdn_scan_kernel(u_ref, w_ref, qd_ref, kdt_ref, attn_ref, egl_ref, z_ref, gn_ref, o_ref, s_ref):
    @pl.when(pl.program_id(2) == 0)
    def _():
        s_ref[...] = jnp.zeros(s_ref.shape, F32)

    tm = u_ref.shape[1]
    gw = GDN_GROUP * B_CHUNK
    gain = gn_ref[...]
    zero = jnp.zeros((B_CHUNK, B_HEAD_DIM), BF16)
    states = [s_ref[hh] for hh in range(GDN_SCAN_HEADS)]
    for c in range(tm // B_CHUNK):
        grp, ci = divmod(c, GDN_GROUP)
        rs = slice(c * B_CHUNK, (c + 1) * B_CHUNK)
        for hh in range(GDN_SCAN_HEADS):
            sb = states[hh].astype(BF16)
            v_new = u_ref[hh, rs, :] - _dot(w_ref[hh, rs, :], sb)
            vpad = jnp.concatenate([zero] * ci + [v_new.astype(BF16)] + [zero] * (GDN_GROUP - 1 - ci), axis=0)
            o = _dot(qd_ref[hh, rs, :], sb) + _dot(attn_ref[hh, grp], vpad)
            states[hh] = (states[hh] * egl_ref[hh, c:c + 1, :]
                          + _dot(kdt_ref[hh, :, grp * gw:(grp + 1) * gw], vpad))
            on = o * lax.rsqrt(jnp.mean(o * o, axis=-1, keepdims=True) + EPS) * gain
            lanes = slice(hh * B_HEAD_DIM, (hh + 1) * B_HEAD_DIM)
            o_ref[rs, lanes] = (on * _silu(z_ref[rs, lanes])).astype(BF16)
    for hh in range(GDN_SCAN_HEADS):
        s_ref[hh] = states[hh]


def _gdn_scan(h, bsz, seq, u, w, qd, kd, attn, egl, b_out_norm):
    tm = min(GDN_TM, seq)
    nt = seq // tm
    nrow = max(tm // B_CHUNK, 8)
    gw = GDN_GROUP * B_CHUNK
    hg = GDN_SCAN_HEADS
    blk = lambda b, hp, i: (b, hp, i, 0)
    return pl.pallas_call(
        _gdn_scan_kernel,
        out_shape=jax.ShapeDtypeStruct((bsz * seq, B_HEADS * B_HEAD_DIM), BF16),
        grid=(bsz, B_HEADS // hg, nt),
        in_specs=[pl.BlockSpec((None, hg, tm, 128), blk), pl.BlockSpec((None, hg, tm, 128), blk),
                  pl.BlockSpec((None, hg, tm, 128), blk),
                  pl.BlockSpec((None, hg, B_HEAD_DIM, tm), lambda b, hp, i: (b, hp, 0, i)),
                  pl.BlockSpec((None, hg, tm // gw, B_CHUNK, gw), lambda b, hp, i: (b, hp, i, 0, 0)),
                  pl.BlockSpec((None, hg, nrow, 128), blk),
                  pl.BlockSpec((tm, hg * 128), lambda b, hp, i: (b * nt + i, EV_BZ // (hg * 128) + hp)),
                  pl.BlockSpec((1, 128), lambda b, hp, i: (0, 0))],
        out_specs=pl.BlockSpec((tm, hg * 128), lambda b, hp, i: (b * nt + i, hp)),
        scratch_shapes=[pltpu.VMEM((hg, B_HEAD_DIM, B_HEAD_DIM), F32)],
        compiler_params=_cparams(("parallel", "parallel", "arbitrary")),
        name="gdn_scan",
    )(u, w, qd, kd, attn, egl, h, b_out_norm.reshape(1, -1))


def _rope_kernel(inv_ref, cos_ref, sin_ref):
    tm = cos_ref.shape[0]
    pos = (pl.program_id(0) * tm + lax.broadcasted_iota(I32, (tm, 128), 0)).astype(F32)
    ang = pos * inv_ref[...]
    lane = lax.broadcasted_iota(I32, (tm, 128), 1)
    cos_ref[...] = jnp.cos(ang)
    sin_ref[...] = jnp.where(lane < 64, -jnp.sin(ang), jnp.sin(ang))


def _rope_tables(seq):
    half = C_QK_DIM // 2
    inv = 1.0 / (ROPE_BASE ** jnp.linspace(0.0, 1.0, half, dtype=F32))
    inv2 = jnp.concatenate([inv, inv]).reshape(1, C_QK_DIM)
    tm = min(1024, seq)
    return pl.pallas_call(
        _rope_kernel,
        out_shape=(jax.ShapeDtypeStruct((seq, C_QK_DIM), F32),) * 2,
        grid=(seq // tm,),
        in_specs=[pl.BlockSpec((1, C_QK_DIM), lambda i: (0, 0))],
        out_specs=(pl.BlockSpec((tm, C_QK_DIM), lambda i: (i, 0)),) * 2,
        compiler_params=_cparams(("parallel",)),
        name="rope_tables",
    )(inv2)


RET_TM = 512


def _retention_kernel(lg_ref, q_ref, k_ref, v_ref, z_ref, cos_ref, sin_ref, gn_ref, o_ref,
                      r_ref, dm_ref, zeta_ref, xi_ref):
    hd = pl.program_id(1)
    lg = lg_ref[0, hd]

    @pl.when(pl.program_id(2) == 0)
    def _():
        r_ref[...] = jnp.zeros(r_ref.shape, F32)
        ri = lax.broadcasted_iota(I32, (C_CHUNK, C_CHUNK), 0)
        ci = lax.broadcasted_iota(I32, (C_CHUNK, C_CHUNK), 1)
        diff = (ri - ci).astype(F32)
        dm_ref[...] = jnp.where(diff >= 0, jnp.exp(jnp.maximum(diff, 0.0) * lg), 0.0)
        zeta_ref[...] = jnp.exp((C_CHUNK - 1 - ri).astype(F32) * lg)
        rv = lax.broadcasted_iota(I32, (C_CHUNK, C_V_DIM), 0).astype(F32)
        xi_ref[...] = jnp.exp((rv + 1.0) * lg)

    g_chunk = lg_ref[1, hd]
    tm = q_ref.shape[0]
    state = r_ref[...]
    dmask = dm_ref[...]
    gain = gn_ref[...]
    for c in range(tm // C_CHUNK):
        rs = slice(c * C_CHUNK, (c + 1) * C_CHUNK)
        cos = cos_ref[rs, :]
        sin = sin_ref[rs, :]
        q = q_ref[rs, :]
        k = k_ref[rs, :]
        qr = q * cos + pltpu.roll(q, C_QK_DIM // 2, 1) * sin
        kr = (k * cos + pltpu.roll(k, C_QK_DIM // 2, 1) * sin) * (C_QK_DIM ** -0.5)
        vb = v_ref[rs, :].astype(BF16)
        qb = qr.astype(BF16)
        s = _dot_nt(qb, kr.astype(BF16)) * dmask
        o = _dot(s.astype(BF16), vb) + _dot(qb, state.astype(BF16)) * xi_ref[...]
        state = state * g_chunk + _dot((kr * zeta_ref[...]).T.astype(BF16), vb)
        mu = jnp.mean(o, axis=-1, keepdims=True)
        oc = o - mu
        var = jnp.mean(oc * oc, axis=-1, keepdims=True)
        y = oc * lax.rsqrt(var + EPS) * gain
        o_ref[rs, :] = (y * _silu(z_ref[rs, :])).astype(BF16)
    r_ref[...] = state


def _retention(h, bsz, seq, cos2, sin2, c_out_norm):
    tm = min(RET_TM, seq)
    nt = seq // tm
    lg = np.log1p(-np.exp2(-5.0 - np.arange(C_HEADS, dtype=np.float32))).astype(np.float32)
    lg = np.stack([lg, np.exp(np.float32(C_CHUNK) * lg).astype(np.float32)])
    return pl.pallas_call(
        _retention_kernel,
        out_shape=jax.ShapeDtypeStruct((bsz * seq, C_HEADS * C_V_DIM), BF16),
        grid=(bsz, C_HEADS, nt),
        in_specs=[pl.BlockSpec(memory_space=pltpu.SMEM),
                  pl.BlockSpec((tm, 128), lambda b, hd, i: (b * nt + i, OD_CQ // 128 + hd)),
                  pl.BlockSpec((tm, 128), lambda b, hd, i: (b * nt + i, OD_CK // 128 + hd)),
                  pl.BlockSpec((tm, 256), lambda b, hd, i: (b * nt + i, OD_CV // 256 + hd)),
                  pl.BlockSpec((tm, 256), lambda b, hd, i: (b * nt + i, OD_CZ // 256 + hd)),
                  pl.BlockSpec((tm, 128), lambda b, hd, i: (i, 0)),
                  pl.BlockSpec((tm, 128), lambda b, hd, i: (i, 0)),
                  pl.BlockSpec((1, C_V_DIM), lambda b, hd, i: (0, hd))],
        out_specs=pl.BlockSpec((tm, C_V_DIM), lambda b, hd, i: (b * nt + i, hd)),
        scratch_shapes=[pltpu.VMEM((C_QK_DIM, C_V_DIM), F32),
                        pltpu.VMEM((C_CHUNK, C_CHUNK), F32),
                        pltpu.VMEM((C_CHUNK, C_QK_DIM), F32),
                        pltpu.VMEM((C_CHUNK, C_V_DIM), F32)],
        compiler_params=_cparams(("parallel", "parallel", "arbitrary")),
        name="retention",
    )(jnp.asarray(lg), h, h, h, h, cos2, sin2, c_out_norm.reshape(1, -1))


def _s5_param_kernel(lre_ref, lim_ref, ldt_ref, lrex_ref, limx_ref, ldtx_ref, bre_ref, bim_ref,
                     are_ref, aim_ref, bbre_ref, bbim_ref):
    def disc(lre, lim, ldt):
        lr = jnp.minimum(lre, -1e-4)
        dt = jnp.exp(ldt)
        mag = jnp.exp(lr * dt)
        return lr, lim, mag * jnp.cos(lim * dt), mag * jnp.sin(lim * dt)

    _, _, a_re, a_im = disc(lre_ref[...], lim_ref[...], ldt_ref[...])
    are_ref[...] = a_re
    aim_ref[...] = a_im
    lr, li, ax_re, ax_im = disc(lrex_ref[...], limx_ref[...], ldtx_ref[...])
    den = lr * lr + li * li
    f_re = ((ax_re - 1.0) * lr + ax_im * li) / den
    f_im = (ax_im * lr - (ax_re - 1.0) * li) / den
    bbre_ref[...] = f_re * bre_ref[...] - f_im * bim_ref[...]
    bbim_ref[...] = f_re * bim_ref[...] + f_im * bre_ref[...]


def _s5_params(lam_re, lam_im, log_dt, b_re, b_im, c_re, c_im):
    g, p, ch = D_GROUPS, D_STATE, D_GROUP
    ldt = jnp.broadcast_to(log_dt[:, None], (g, p))
    rep = lambda a: jnp.repeat(a, ch, axis=1)
    vm = pl.BlockSpec(memory_space=pltpu.VMEM)
    a_re, a_im, bb_re, bb_im = pl.pallas_call(
        _s5_param_kernel,
        out_shape=(jax.ShapeDtypeStruct((g, p), F32), jax.ShapeDtypeStruct((g, p), F32),
                   jax.ShapeDtypeStruct((g, p * ch), F32), jax.ShapeDtypeStruct((g, p * ch), F32)),
        in_specs=[vm] * 8, out_specs=(vm,) * 4,
        name="s5_params",
    )(lam_re, lam_im, ldt, rep(lam_re), rep(lam_im), rep(ldt),
      b_re.reshape(g, p * ch), b_im.reshape(g, p * ch))
    eye = jnp.eye(D_SETS * 4, dtype=F32)

    def pack_b(bb):
        bb = bb.reshape(D_SETS, 16, p, ch)
        return jnp.einsum('sgpi,gh->sgihp', bb, eye).reshape(D_SETS, D_SET_CH, D_SET_ST)

    def pack_c(c):
        c = c.reshape(D_SETS, 16, ch, p)
        return jnp.einsum('sgjp,gh->sgphj', c, eye).reshape(D_SETS, D_SET_ST, D_SET_CH)

    bd = jnp.concatenate([pack_b(bb_re), pack_b(bb_im)], axis=-1).astype(BF16)
    slab = (D_SETS, S5_SLAB, 128)
    return (a_re.reshape(slab), a_im.reshape(slab), bd,
            pack_c(c_re).astype(BF16), pack_c(c_im).astype(BF16))


S5_TM = 256
S5_SLAB = D_SET_ST // 128


def _s5_kernel(u_ref, z_ref, are_ref, aim_ref, bd_ref, cre_ref, cim_ref, dskip_ref, wglu_ref, bglu_ref,
               o_ref, hre_ref, him_ref, y_ref, *x_refs):
    @pl.when(pl.program_id(1) == 0)
    def _():
        hre_ref[...] = jnp.zeros(hre_ref.shape, F32)
        him_ref[...] = jnp.zeros(him_ref.shape, F32)

    xre, xim = x_refs[:D_SETS], x_refs[D_SETS:]
    tm = u_ref.shape[0]
    u = u_ref[...]
    ub = u.astype(BF16)
    for s in range(D_SETS):
        bu = _dot(ub[:, s * D_SET_CH:(s + 1) * D_SET_CH], bd_ref[s])
        for k in range(S5_SLAB):
            xre[s][pl.ds(k, tm, stride=S5_SLAB), :] = bu[:, k * 128:(k + 1) * 128]
            xim[s][pl.ds(k, tm, stride=S5_SLAB), :] = bu[:, D_SET_ST + k * 128:D_SET_ST + (k + 1) * 128]
    a_re = [are_ref[s] for s in range(D_SETS)]
    a_im = [aim_ref[s] for s in range(D_SETS)]

    def step(t, carry):
        rows = pl.ds(pl.multiple_of(t * S5_SLAB, S5_SLAB), S5_SLAB)
        out = []
        for s in range(D_SETS):
            hr, hi = carry[2 * s], carry[2 * s + 1]
            xr = xre[s][rows, :] + a_re[s] * hr - a_im[s] * hi
            xi = xim[s][rows, :] + a_re[s] * hi + a_im[s] * hr
            xre[s][rows, :] = xr
            xim[s][rows, :] = xi
            out += [xr, xi]
        return tuple(out)

    carry = []
    for s in range(D_SETS):
        carry += [hre_ref[s], him_ref[s]]
    carry = lax.fori_loop(0, tm, step, tuple(carry), unroll=8)
    for s in range(D_SETS):
        hre_ref[s] = carry[2 * s]
        him_ref[s] = carry[2 * s + 1]
        x_r = jnp.concatenate([xre[s][pl.ds(k, tm, stride=S5_SLAB), :] for k in range(S5_SLAB)], axis=1)
        x_i = jnp.concatenate([xim[s][pl.ds(k, tm, stride=S5_SLAB), :] for k in range(S5_SLAB)], axis=1)
        y_ref[:, s * D_SET_CH:(s + 1) * D_SET_CH] = (
            _dot(x_r.astype(BF16), cre_ref[s]) - _dot(x_i.astype(BF16), cim_ref[s]))
    y = y_ref[...] + dskip_ref[...] * u
    y = 0.5 * y * (1.0 + jnp.tanh(math.sqrt(2.0 / math.pi) * (y + 0.044715 * (y * y * y))))
    gate = _sigmoid(_dot(y.astype(BF16), wglu_ref[...]) + bglu_ref[...])
    o_ref[...] = (y * gate * _silu(z_ref[...])).astype(BF16)


def _s5(h, bsz, seq, a_re, a_im, bd, cd_re, cd_im, d_skip, w_glu, b_glu):
    tm = min(S5_TM, seq)
    nt = seq // tm
    full = lambda *shape: pl.BlockSpec(shape, lambda b, i: (0,) * len(shape))
    width = D_GROUPS * D_GROUP
    return pl.pallas_call(
        _s5_kernel,
        out_shape=jax.ShapeDtypeStruct((bsz * seq, width), BF16),
        grid=(bsz, nt),
        in_specs=[pl.BlockSpec((tm, width), lambda b, i: (b * nt + i, OD_DU // width)),
                  pl.BlockSpec((tm, width), lambda b, i: (b * nt + i, OD_DZ // width)),
                  full(D_SETS, S5_SLAB, 128), full(D_SETS, S5_SLAB, 128),
                  full(D_SETS, D_SET_CH, 2 * D_SET_ST),
                  full(D_SETS, D_SET_ST, D_SET_CH), full(D_SETS, D_SET_ST, D_SET_CH),
                  full(1, width), full(width, width), full(1, width)],
        out_specs=pl.BlockSpec((tm, width), lambda b, i: (b * nt + i, 0)),
        scratch_shapes=([pltpu.VMEM((D_SETS, S5_SLAB, 128), F32), pltpu.VMEM((D_SETS, S5_SLAB, 128), F32),
                         pltpu.VMEM((tm, width), F32)]
                        + [pltpu.VMEM((tm * S5_SLAB, 128), F32)] * (2 * D_SETS)),
        compiler_params=_cparams(("parallel", "arbitrary")),
        name="s5",
    )(h, h, a_re, a_im, bd, cd_re, cd_im, d_skip.reshape(1, -1), w_glu.astype(BF16),
      b_glu.reshape(1, -1))


def _pack_even_w(w_in):
    sizes = (1024, A_KV_RANK, A_IDX_HEADS * A_IDX_DIM, A_IDX_DIM, A_IDX_HEADS, 1024,
             3 * 1024, B_HEADS, B_HEADS, 1024)
    parts, start = [], 0
    for s in sizes:
        parts.append(w_in[:, start:start + s])
        start += s
    aq, ckv, qi, ki, wi, az, bqkv, ba, bb, bz = parts
    pad = jnp.zeros((w_in.shape[0], EV_WIDTH - start), w_in.dtype)
    return jnp.concatenate([aq, az, bqkv, bz, qi, ckv, ki, wi, ba, bb, pad], axis=1).astype(BF16)


def _even_layer(x2d, bsz, seq, bias, norm_g, w_in, a_q_norm, a_kv_norm, w_kv_up, a_k_norm,
                b_conv, b_a_log, b_dt_bias, b_out_norm, w_out):
    h = _norm_proj(x2d, norm_g, _pack_even_w(w_in), tn=768)
    qta, qit, wt, kidx, katt, vt = _dsa_prep(h, bsz * seq // QB, a_q_norm, a_kv_norm, w_kv_up, a_k_norm)
    mix_a = _dsa_attend(h, bsz, seq, qta, qit, wt, kidx, katt, vt, bias)
    u, w, qd, kd, attn, egl = _gdn_prep(h, bsz, seq, b_conv, b_a_log, b_dt_bias)
    mix_b = _gdn_scan(h, bsz, seq, u, w, qd, kd, attn, egl, b_out_norm)
    return _out_proj(x2d, mix_a, mix_b, w_out)


def _odd_layer(x2d, bsz, seq, cos2, sin2, norm_g, w_in, c_out_norm, lam_re, lam_im, log_dt,
               b_re, b_im, c_re, c_im, d_skip, w_glu, b_glu, w_out):
    h = _norm_proj(x2d, norm_g, w_in.astype(BF16), tn=1024)
    mix_c = _retention(h, bsz, seq, cos2, sin2, c_out_norm)
    s5p = _s5_params(lam_re, lam_im, log_dt, b_re, b_im, c_re, c_im)
    mix_d = _s5(h, bsz, seq, *s5p, d_skip, w_glu, b_glu)
    return _out_proj(x2d, mix_c, mix_d, w_out)


def kernel(x, rel_bias, ev_norm, ev_w_in, ev_a_q_norm, ev_a_kv_norm, ev_w_kv_up, ev_a_k_norm,
           ev_b_conv, ev_b_a_log, ev_b_dt_bias, ev_b_out_norm, ev_w_out,
           od_norm, od_w_in, od_c_out_norm, od_lam_re, od_lam_im, od_log_dt,
           od_b_re, od_b_im, od_c_re, od_c_im, od_d_skip, od_w_glu, od_b_glu, od_w_out):
    bsz, seq, d = x.shape
    depth = ev_norm.shape[0] + od_norm.shape[0]
    x2d = x.reshape(bsz * seq, d)
    bias = _bias_table(rel_bias)
    cos2, sin2 = _rope_tables(seq)
    for layer in range(depth):
        i = layer // 2
        if layer % 2 == 0:
            x2d = _even_layer(x2d, bsz, seq, bias, ev_norm[i], ev_w_in[i], ev_a_q_norm[i],
                              ev_a_kv_norm[i], ev_w_kv_up[i], ev_a_k_norm[i], ev_b_conv[i],
                              ev_b_a_log[i], ev_b_dt_bias[i], ev_b_out_norm[i], ev_w_out[i])
        else:
            x2d = _odd_layer(x2d, bsz, seq, cos2, sin2, od_norm[i], od_w_in[i], od_c_out_norm[i],
                             od_lam_re[i], od_lam_im[i], od_log_dt[i], od_b_re[i], od_b_im[i],
                             od_c_re[i], od_c_im[i], od_d_skip[i], od_w_glu[i], od_b_glu[i], od_w_out[i])
    return x2d.reshape(bsz, seq, d)
```

```python
import functools
import math

import numpy as np
import jax
import jax.numpy as jnp
from jax import lax
from jax.experimental import pallas as pl
from jax.experimental.pallas import tpu as pltpu

F32 = jnp.float32
BF16 = jnp.bfloat16
I32 = jnp.int32

D_MODEL = 1024
EPS = 1e-6
LOG2E = 1.4426950408889634
NEG_INF = float("-inf")
INT_MIN = -(2 ** 31)

A_HEADS, A_HEAD_DIM, A_KV_RANK = 16, 64, 128
A_IDX_HEADS, A_IDX_DIM, A_TOPK_MAX = 8, 64, 256
QB = 128
REL_BUCKETS, REL_MAX_DIST = 32, 128
VT_ROWS = 80
SB = 2 * QB
SCORE_ROWS = 4 * QB
BIAS_ROWS = 5 * QB
MASKED = -(2.0 ** 100)
M_INIT = -(2.0 ** 60)
TIE_WALK_MAX = 8
COUNT_UNROLL = 4
B_HEADS, B_HEAD_DIM, CONV_K, B_CHUNK = 8, 128, 4, 64
C_HEADS, C_QK_DIM, C_V_DIM, C_CHUNK = 4, 128, 256, 128
ROPE_BASE = 10000.0
D_GROUP, D_STATE, D_GROUPS = 16, 64, 64
D_SETS, D_SET_CH, D_SET_ST = 4, 256, 1024

EV_AQ, EV_AZ, EV_BQ, EV_BK, EV_BV, EV_BZ, EV_QI, EV_CKV, EV_MISC = (
    0, 1024, 2048, 3072, 4096, 5120, 6144, 6656, 6784)
EV_WIDTH = 6912
MISC_KI, MISC_WI, MISC_BA, MISC_BB = 0, 64, 72, 80
OD_CQ, OD_CK, OD_CV, OD_CZ, OD_DU, OD_DZ = 0, 512, 1024, 2048, 3072, 4096
OD_WIDTH = 5120

VMEM_LIMIT = 48 * 1024 * 1024


def _cparams(sem):
    return pltpu.CompilerParams(dimension_semantics=sem, vmem_limit_bytes=VMEM_LIMIT)


def _dot(a, b):
    return jnp.dot(a, b, preferred_element_type=F32)


def _dot_nt(a, b):
    return lax.dot_general(a, b, (((1,), (1,)), ((), ())), preferred_element_type=F32)


def _split_bf16(x, n):
    parts = []
    for _ in range(n):
        p = x.astype(BF16)
        parts.append(p)
        x = x - p.astype(F32)
    return parts


def _dot_sel(sel, x):
    sel = sel.astype(BF16)
    hi, mid, lo = _split_bf16(x, 3)
    return _dot(sel, hi) + (_dot(sel, mid) + _dot(sel, lo))


def _dot_xsel(x, sel):
    sel = sel.astype(BF16)
    hi, mid, lo = _split_bf16(x, 3)
    return _dot(hi, sel) + (_dot(mid, sel) + _dot(lo, sel))


def _dot_x3(a, b):
    ah, al = _split_bf16(a, 2)
    bh, bl = _split_bf16(b, 2)
    return _dot(ah, bh) + (_dot(ah, bl) + _dot(al, bh))


def _sigmoid(x):
    return 1.0 / (1.0 + jnp.exp(-x))


def _silu(x):
    return x * _sigmoid(x)


def _norm_proj_kernel(x_ref, g_ref, w_ref, o_ref, xn_ref):
    @pl.when(pl.program_id(1) == 0)
    def _():
        x = x_ref[...]
        ms = jnp.mean(x * x, axis=-1, keepdims=True)
        xn_ref[...] = (x * lax.rsqrt(ms + EPS) * g_ref[...]).astype(BF16)

    o_ref[...] = _dot(xn_ref[...], w_ref[...])


def _norm_proj(x2d, gain, w_bf16, tn, tm=1024):
    t, d = x2d.shape
    n = w_bf16.shape[1]
    tm = min(tm, t)
    return pl.pallas_call(
        _norm_proj_kernel,
        out_shape=jax.ShapeDtypeStruct((t, n), F32),
        grid=(t // tm, n // tn),
        in_specs=[pl.BlockSpec((tm, d), lambda i, j: (i, 0)),
                  pl.BlockSpec((1, d), lambda i, j: (0, 0)),
                  pl.BlockSpec((d, tn), lambda i, j: (0, j))],
        out_specs=pl.BlockSpec((tm, tn), lambda i, j: (i, j)),
        scratch_shapes=[pltpu.VMEM((tm, d), BF16)],
        compiler_params=_cparams(("parallel", "arbitrary")),
        name="norm_proj",
    )(x2d, gain.reshape(1, d), w_bf16)


def _out_proj_kernel(x_ref, a_ref, b_ref, wa_ref, wb_ref, o_ref):
    o_ref[...] = x_ref[...] + _dot(a_ref[...], wa_ref[...]) + _dot(b_ref[...], wb_ref[...])


def _out_proj(x2d, mix_a, mix_b, w_out, tm=512):
    t, d = x2d.shape
    half = mix_a.shape[1]
    tm = min(tm, t)
    wa = w_out[:half].astype(BF16)
    wb = w_out[half:].astype(BF16)
    return pl.pallas_call(
        _out_proj_kernel,
        out_shape=jax.ShapeDtypeStruct((t, d), F32),
        grid=(t // tm,),
        in_specs=[pl.BlockSpec((tm, d), lambda i: (i, 0)),
                  pl.BlockSpec((tm, half), lambda i: (i, 0)),
                  pl.BlockSpec((tm, half), lambda i: (i, 0)),
                  pl.BlockSpec((half, d), lambda i: (0, 0)),
                  pl.BlockSpec((half, d), lambda i: (0, 0))],
        out_specs=pl.BlockSpec((tm, d), lambda i: (i, 0)),
        compiler_params=_cparams(("parallel",)),
        name="out_proj",
    )(x2d, mix_a, mix_b, wa, wb)


def _t5_bucket_starts():
    exact = REL_BUCKETS // 2
    n = np.arange(0, 4 * REL_MAX_DIST, dtype=np.int64)
    ratio = np.maximum(n, 1).astype(np.float32) / np.float32(exact)
    large = exact + (np.log(ratio).astype(np.float32) / np.float32(math.log(REL_MAX_DIST / exact))
                     * np.float32(REL_BUCKETS - exact)).astype(np.int32)
    bucket = np.where(n < exact, n, np.minimum(large, REL_BUCKETS - 1))
    starts = [int(np.argmax(bucket >= b)) for b in range(REL_BUCKETS)]
    assert all(bucket[s] == b for b, s in enumerate(starts)) and np.all(np.diff(bucket) >= 0)
    assert starts[-1] <= QB, "distances beyond one key block must share the last bucket"
    return starts


_BUCKET_STARTS = _t5_bucket_starts()


def _bias_table_kernel(rb_ref, o_ref):
    row = lax.broadcasted_iota(I32, (BIAS_ROWS, QB), 0)
    lane = lax.broadcasted_iota(I32, (BIAS_ROWS, QB), 1)
    dist = lane + 3 * QB - row
    for h in range(A_HEADS):
        val = jnp.full((BIAS_ROWS, QB), rb_ref[0, h], F32)
        for b in range(1, REL_BUCKETS):
            val = jnp.where(dist >= _BUCKET_STARTS[b], rb_ref[b, h], val)
        val = (val - rb_ref[REL_BUCKETS - 1, h]) * LOG2E
        o_ref[:, h * QB:(h + 1) * QB] = jnp.where(dist >= 0, val, 0.0)


def _bias_table(rel_bias):
    return pl.pallas_call(
        _bias_table_kernel,
        out_shape=jax.ShapeDtypeStruct((BIAS_ROWS, A_HEADS * QB), F32),
        in_specs=[pl.BlockSpec(memory_space=pltpu.SMEM)],
        out_specs=pl.BlockSpec(memory_space=pltpu.VMEM),
        name="dsa_bias_table",
    )(rel_bias)


def _dsa_prep_kernel(aq_ref, qi_ref, ckv_ref, misc_ref, gq_ref, gkv_ref, wkv_ref, gk_ref,
                     qta_ref, qit_ref, wt_ref, kidx_ref, katt_ref, vt_ref):
    zeros = jnp.zeros((QB, A_HEAD_DIM), F32)
    aq = aq_ref[...]
    gq = gq_ref[...]
    ident = jnp.where(lax.broadcasted_iota(I32, (QB, QB), 0) == lax.broadcasted_iota(I32, (QB, QB), 1),
                      1.0, 0.0).astype(BF16)
    for h in range(A_HEADS):
        q = aq[:, h * A_HEAD_DIM:(h + 1) * A_HEAD_DIM]
        ms = jnp.mean(q * q, axis=-1, keepdims=True)
        qn = q * lax.rsqrt(ms + EPS) * gq * (A_HEAD_DIM ** -0.5 * LOG2E)
        qta_ref[0:QB, h * QB:(h + 1) * QB] = jnp.concatenate([qn, zeros], axis=1).T.astype(BF16)
        qta_ref[QB:2 * QB, h * QB:(h + 1) * QB] = ident
    qi = qi_ref[...]
    for h in range(A_IDX_HEADS):
        qih = qi[:, h * A_IDX_DIM:(h + 1) * A_IDX_DIM]
        qit_ref[:, h * QB:(h + 1) * QB] = jnp.concatenate([qih, zeros], axis=1).T.astype(BF16)
    misc = misc_ref[...]
    wt_ref[...] = misc.T[MISC_WI:MISC_WI + A_IDX_HEADS, :] * (A_IDX_HEADS ** -0.5 * A_IDX_DIM ** -0.5)
    c = ckv_ref[...]
    cn = c * lax.rsqrt(jnp.mean(c * c, axis=-1, keepdims=True) + EPS) * gkv_ref[...]
    kv = _dot(cn.astype(BF16), wkv_ref[...])
    k = kv[:, :A_HEAD_DIM]
    kn = k * lax.rsqrt(jnp.mean(k * k, axis=-1, keepdims=True) + EPS) * gk_ref[...]
    kidx_ref[...] = jnp.concatenate([misc[:, MISC_KI:MISC_KI + A_IDX_DIM], zeros], axis=1).astype(BF16)
    katt_ref[...] = jnp.concatenate([kn, zeros], axis=1).astype(BF16)
    kvt = kv.T
    vt_ref[0:A_HEAD_DIM, :] = kvt[A_HEAD_DIM:, :].astype(BF16)
    ones_row = lax.broadcasted_iota(I32, (VT_ROWS - A_HEAD_DIM, QB), 0) == 0
    vt_ref[A_HEAD_DIM:, :] = jnp.where(ones_row, 1.0, 0.0).astype(BF16)


def _dsa_prep(h, nblk, a_q_norm, a_kv_norm, w_kv_up, a_k_norm):
    cb = lambda width, off: off // width
    return pl.pallas_call(
        _dsa_prep_kernel,
        out_shape=(jax.ShapeDtypeStruct((nblk, 2 * QB, A_HEADS * QB), BF16),
                   jax.ShapeDtypeStruct((nblk, QB, A_IDX_HEADS * QB), BF16),
                   jax.ShapeDtypeStruct((nblk, A_IDX_HEADS, QB), F32),
                   jax.ShapeDtypeStruct((nblk, QB, 128), BF16),
                   jax.ShapeDtypeStruct((nblk, QB, 128), BF16),
                   jax.ShapeDtypeStruct((nblk, VT_ROWS, QB), BF16)),
        grid=(nblk,),
        in_specs=[pl.BlockSpec((QB, 1024), lambda i: (i, cb(1024, EV_AQ))),
                  pl.BlockSpec((QB, 512), lambda i: (i, cb(512, EV_QI))),
                  pl.BlockSpec((QB, 128), lambda i: (i, cb(128, EV_CKV))),
                  pl.BlockSpec((QB, 128), lambda i: (i, cb(128, EV_MISC))),
                  pl.BlockSpec((1, A_HEAD_DIM), lambda i: (0, 0)),
                  pl.BlockSpec((1, A_KV_RANK), lambda i: (0, 0)),
                  pl.BlockSpec((A_KV_RANK, 2 * A_HEAD_DIM), lambda i: (0, 0)),
                  pl.BlockSpec((1, A_HEAD_DIM), lambda i: (0, 0))],
        out_specs=(pl.BlockSpec((None, 2 * QB, A_HEADS * QB), lambda i: (i, 0, 0)),
                   pl.BlockSpec((None, QB, A_IDX_HEADS * QB), lambda i: (i, 0, 0)),
                   pl.BlockSpec((None, A_IDX_HEADS, QB), lambda i: (i, 0, 0)),
                   pl.BlockSpec((None, QB, 128), lambda i: (i, 0, 0)),
                   pl.BlockSpec((None, QB, 128), lambda i: (i, 0, 0)),
                   pl.BlockSpec((None, VT_ROWS, QB), lambda i: (i, 0, 0))),
        compiler_params=_cparams(("parallel",)),
        name="dsa_prep",
    )(h, h, h, h, a_q_norm.reshape(1, -1), a_kv_norm.reshape(1, -1), w_kv_up.astype(BF16),
      a_k_norm.reshape(1, -1))


def _dsa_kernel(topk, qit_ref, wt_ref, qta_ref, kidx_ref, katt_ref, vt_ref, bias_ref, az_ref, o_ref,
                strip_ref, x_ref, m_ref, acc_ref, lg_ref, mx_ref, p_ref):
    qt = pl.program_id(1)
    t0 = qt * QB
    n_sc = qt // 4 + 1
    n_sb = qt // 2 + 1
    lane_s = lax.broadcasted_iota(I32, (SCORE_ROWS, QB), 1)
    row_s = lax.broadcasted_iota(I32, (SCORE_ROWS, QB), 0)

    w = wt_ref[...]

    def score_body(j, carry):
        kblk = kidx_ref[pl.ds(j * 4, 4)].reshape(SCORE_ROWS, 128)
        tot = None
        for hp in range(A_IDX_HEADS // 2):
            s = _dot(kblk, qit_ref[:, hp * SB:(hp + 1) * SB])
            for e in range(2):
                h = 2 * hp + e
                term = jnp.maximum(s[:, e * QB:(e + 1) * QB], 0.0) * w[h:h + 1, :]
                tot = term if tot is None else tot + term
        tot = jnp.where(j * SCORE_ROWS + row_s <= t0 + lane_s, tot, NEG_INF)
        bits = pltpu.bitcast(tot, I32)
        strip_ref[pl.ds(pl.multiple_of(j * SCORE_ROWS, SCORE_ROWS), SCORE_ROWS), :] = (
            bits ^ ((bits >> 31) & 0x7FFFFFFF))
        return carry

    lax.fori_loop(0, n_sc, score_body, 0)

    def count(pred):
        def one(j, acc):
            r0 = pl.multiple_of(j * SCORE_ROWS, SCORE_ROWS)
            m = jnp.where(pred(strip_ref[pl.ds(r0, SCORE_ROWS), :], r0), 1, 0)
            return acc + m.reshape(SCORE_ROWS // 32, 32, QB).sum(axis=0)

        def body(j, acc):
            for sub in range(COUNT_UNROLL):
                acc = one(COUNT_UNROLL * j + sub, acc)
            return acc
        n_main = n_sc // COUNT_UNROLL
        acc = lax.fori_loop(0, n_main, body, jnp.zeros((32, QB), I32))
        acc = lax.fori_loop(n_main * COUNT_UNROLL, n_sc, one, acc)
        return acc.reshape(4, 8, QB).sum(axis=0).sum(axis=0, keepdims=True)

    count_ge = lambda cand: count(lambda key, r0: key >= cand)
    c0 = count_ge(jnp.zeros((1, QB), I32))
    nonneg = c0 >= topk
    thr0 = jnp.where(nonneg, 0, INT_MIN)
    cnt0 = jnp.where(nonneg, c0, n_sc * SCORE_ROWS)

    def bit_steps(counter, n):
        def one(_, carry):
            i, thr, cnt = carry
            cand = thr | lax.shift_left(jnp.int32(1), 30 - i)
            c = counter(cand)
            ok = c >= topk
            return i + 1, jnp.where(ok, cand, thr), jnp.where(ok, c, cnt)
        return lambda carry: lax.fori_loop(0, n, one, carry)

    state = bit_steps(count_ge, 15)((jnp.int32(0), thr0, cnt0))
    _, thr, cnt = lax.while_loop(lambda c: (c[0] < 31) & (jnp.max(c[2]) > topk),
                                 bit_steps(count_ge, 4), state)

    x_ref[...] = jnp.full((1, QB), 2 ** 30, I32)
    tied = cnt > topk

    @pl.when(jnp.max(cnt) > topk)
    def _():
        above = count(lambda key, r0: key > thr)
        need = jnp.where(tied, topk - above, 0)
        max_need = jnp.max(need)

        @pl.when(max_need <= TIE_WALK_MAX)
        def _():
            def next_tie(prev):
                def body(j, acc):
                    r0 = pl.multiple_of(j * SCORE_ROWS, SCORE_ROWS)
                    idx = r0 + row_s
                    hit = (strip_ref[pl.ds(r0, SCORE_ROWS), :] == thr) & (idx > prev)
                    return jnp.minimum(acc, jnp.where(hit, idx, 2 ** 30)
                                       .reshape(SCORE_ROWS // 32, 32, QB).min(axis=0))
                acc = lax.fori_loop(0, n_sc, body, jnp.full((32, QB), 2 ** 30, I32))
                return acc.reshape(4, 8, QB).min(axis=0).min(axis=0, keepdims=True)

            last = lax.fori_loop(0, max_need, lambda r, prev: jnp.where(r < need, next_tie(prev), prev),
                                 jnp.full((1, QB), -1, I32))
            x_ref[...] = jnp.where(tied, last + 1, 2 ** 30)

        @pl.when(max_need > TIE_WALK_MAX)
        def _():
            def tie_body(i, xb):
                cand = xb | lax.shift_left(jnp.int32(1), 14 - i)
                c = count(lambda key, r0: (key == thr) & (r0 + row_s < cand))
                return jnp.where(c <= need, cand, xb)

            xb = lax.fori_loop(0, 15, tie_body, jnp.zeros((1, QB), I32))
            x_ref[...] = jnp.where(tied, xb, 2 ** 30)

    xb = x_ref[...]

    m_ref[...] = jnp.full(m_ref.shape, M_INIT, F32)
    acc_ref[...] = jnp.zeros(acc_ref.shape, F32)
    n_lt = A_HEADS * QB // SB
    row_a = lax.broadcasted_iota(I32, (SB, QB), 0)
    lane_a = lax.broadcasted_iota(I32, (SB, QB), 1)

    def logits(j, slot, near):
        r0 = pl.multiple_of(j * SB, SB)
        key = strip_ref[pl.ds(r0, SB), :]
        idx = r0 + row_a
        sel = (key > thr) | ((key == thr) & (idx < xb))
        if near:
            sel = sel & (idx <= t0 + lane_a)
            b0 = pl.multiple_of((2 * j - qt + 3) * QB, QB)
        pen = jnp.where(sel, 0.0, MASKED).astype(BF16)
        kaug = jnp.concatenate([katt_ref[pl.ds(2 * j, 2)].reshape(SB, 128), pen], axis=1)
        for lt in range(n_lt):
            cols = slice(lt * SB, (lt + 1) * SB)
            lg = _dot(kaug, qta_ref[:, cols])
            if near:
                lg = lg + bias_ref[pl.ds(b0, SB), cols]
            lg = lg.astype(BF16)
            lg_ref[slot, :, cols] = lg
            mx_ref[slot, :, cols] = jnp.max(lg, axis=0, keepdims=True).astype(F32)

    def accumulate(j, slot):
        vt = jnp.concatenate([vt_ref[2 * j], vt_ref[2 * j + 1]], axis=1)
        m_old = m_ref[...]
        m_new = jnp.maximum(m_old, mx_ref[slot])
        m_ref[...] = m_new
        m_b = m_new.astype(BF16)
        for lt in range(n_lt):
            cols = slice(lt * SB, (lt + 1) * SB)
            p_ref[:, cols] = jnp.exp2(lg_ref[slot, :, cols] - m_b[:, cols])
        acc_ref[...] = jnp.exp2(m_old - m_new) * acc_ref[...] + _dot(vt, p_ref[...])

    n_far = jnp.maximum(n_sb - 2, 0)
    n_pair = jnp.maximum(n_far - 1, 0) // 2
    n_rem = n_far - 2 * n_pair

    @pl.when(n_far > 0)
    def _():
        logits(0, 0, False)

    def pair_body(jj, carry):
        j = 2 * jj
        logits(j + 1, 1, False)
        accumulate(j, 0)
        logits(j + 2, 0, False)
        accumulate(j + 1, 1)
        return carry

    lax.fori_loop(0, n_pair, pair_body, 0)

    @pl.when(n_rem == 1)
    def _():
        accumulate(n_far - 1, 0)

    @pl.when(n_rem == 2)
    def _():
        logits(n_far - 1, 1, False)
        accumulate(n_far - 2, 0)
        accumulate(n_far - 1, 1)

    @pl.when(n_sb >= 2)
    def _():
        logits(n_sb - 2, 0, True)
        accumulate(n_sb - 2, 0)

    logits(n_sb - 1, 0, True)
    accumulate(n_sb - 1, 0)

    acc = acc_ref[...]
    o_t = acc[0:A_HEAD_DIM, :] / acc[A_HEAD_DIM:A_HEAD_DIM + 1, :]
    pad = jnp.zeros((QB - A_HEAD_DIM, QB), F32)
    pieces = []
    for h in range(A_HEADS):
        blk = jnp.concatenate([o_t[:, h * QB:(h + 1) * QB], pad], axis=0)
        pieces.append(blk.T[:, 0:A_HEAD_DIM])
    att = jnp.concatenate(pieces, axis=1)
    o_ref[...] = (att * _silu(az_ref[...])).astype(BF16)


def _dsa_attend(h, bsz, seq, qta, qit, wt, kidx, katt, vt, bias):
    nq = seq // QB
    topk = min(A_TOPK_MAX, seq // 4)
    blk = lambda b, q: (b * nq + q, 0, 0)
    per_batch = lambda b, q: (b, 0, 0)
    once = pl.Buffered(1)
    return pl.pallas_call(
        functools.partial(_dsa_kernel, topk),
        out_shape=jax.ShapeDtypeStruct((bsz * seq, A_HEADS * A_HEAD_DIM), BF16),
        grid=(bsz, nq),
        in_specs=[pl.BlockSpec((None, QB, A_IDX_HEADS * QB), blk),
                  pl.BlockSpec((None, A_IDX_HEADS, QB), blk),
                  pl.BlockSpec((None, 2 * QB, A_HEADS * QB), blk),
                  pl.BlockSpec((nq, QB, 128), per_batch, pipeline_mode=once),
                  pl.BlockSpec((nq, QB, 128), per_batch, pipeline_mode=once),
                  pl.BlockSpec((nq, VT_ROWS, QB), per_batch, pipeline_mode=once),
                  pl.BlockSpec((BIAS_ROWS, A_HEADS * QB), lambda b, q: (0, 0), pipeline_mode=once),
                  pl.BlockSpec((QB, 1024), lambda b, q: (b * nq + q, EV_AZ // 1024))],
        out_specs=pl.BlockSpec((QB, 1024), lambda b, q: (b * nq + q, 0)),
        scratch_shapes=[pltpu.VMEM((seq, QB), I32),
                        pltpu.VMEM((1, QB), I32),
                        pltpu.VMEM((1, A_HEADS * QB), F32),
                        pltpu.VMEM((VT_ROWS, A_HEADS * QB), F32),
                        pltpu.VMEM((2, SB, A_HEADS * QB), BF16),
                        pltpu.VMEM((2, 1, A_HEADS * QB), F32),
                        pltpu.VMEM((SB, A_HEADS * QB), BF16)],
        compiler_params=_cparams(("arbitrary", "arbitrary")),
        name="dsa_attend",
    )(qit, wt, qta, kidx, katt, vt, bias, h)


GDN_TM = 512
GDN_GROUP = 4
GDN_SCAN_HEADS = 4
GDN_PREP_HEADS = 2


def _gdn_prep_kernel(alog_ref, dtb_ref, q_ref, k_ref, v_ref, qh_ref, kh_ref, vh_ref, misc_ref,
                     cq_ref, ck_ref, cv_ref,
                     u_ref, w_ref, qd_ref, kdt_ref, attn_ref, egl_ref):
    i = pl.program_id(1)
    hp = pl.program_id(2)
    tm = q_ref.shape[0]
    nchunk = tm // B_CHUNK
    heads = range(GDN_PREP_HEADS)
    lanes = [slice(hh * 128, (hh + 1) * 128) for hh in heads]

    def conv_silu(x_ref, halo_ref, c_ref, ln):
        halo = jnp.where(i > 0, halo_ref[:, ln], 0.0)
        ext = jnp.concatenate([halo, x_ref[:, ln]], axis=0)
        cw = c_ref[:, ln]
        y = ext[8:, :] * cw[CONV_K - 1:CONV_K, :]
        for d in range(1, CONV_K):
            y = y + pltpu.roll(ext, d, 0)[8:, :] * cw[CONV_K - 1 - d:CONV_K - d, :]
        return _silu(y)

    q = [conv_silu(q_ref, qh_ref, cq_ref, ln) for ln in lanes]
    k = [conv_silu(k_ref, kh_ref, ck_ref, ln) for ln in lanes]
    v = [conv_silu(v_ref, vh_ref, cv_ref, ln) for ln in lanes]
    qn = [x * lax.rsqrt(jnp.sum(x * x, axis=-1, keepdims=True) + EPS) * (B_HEAD_DIM ** -0.5) for x in q]
    kn = [x * lax.rsqrt(jnp.sum(x * x, axis=-1, keepdims=True) + EPS) for x in k]

    misc = misc_ref[...]
    sel_r = lax.broadcasted_iota(I32, (128, 256), 0)
    sel_c = lax.broadcasted_iota(I32, (128, 256), 1)
    beta, g = [], []
    for hh in heads:
        hd = hp * GDN_PREP_HEADS + hh
        pick = jnp.where(sel_r == jnp.where(sel_c < 128, MISC_BA + hd, MISC_BB + hd), 1.0, 0.0)
        bab = _dot_xsel(misc, pick)
        beta.append(_sigmoid(bab[:, 128:256]))
        xg = bab[:, 0:128] + dtb_ref[hd]
        softplus = jnp.maximum(xg, 0.0) + jnp.log(1.0 + jnp.exp(-jnp.abs(xg)))
        g.append(-jnp.exp(jnp.full((1, 128), alog_ref[hd], F32)) * softplus)

    sl = 256
    r2 = lax.broadcasted_iota(I32, (sl, sl), 0)
    c2 = lax.broadcasted_iota(I32, (sl, sl), 1)
    same = (r2 // B_CHUNK) == (c2 // B_CHUNK)
    tri_blk = jnp.concatenate([jnp.where(same & (c2 <= r2), 1.0, 0.0), jnp.where(same, 1.0, 0.0)], axis=0)
    nsl = tm // sl
    gc, gl = [], []
    for hh in heads:
        both = _dot_sel(tri_blk, jnp.concatenate([g[hh][s * sl:(s + 1) * sl, :] for s in range(nsl)], axis=1))
        gc.append(jnp.concatenate([both[0:sl, s * 128:(s + 1) * 128] for s in range(nsl)], axis=0))
        gl.append(jnp.concatenate([both[sl:2 * sl, s * 128:(s + 1) * 128] for s in range(nsl)], axis=0))

    kb = [kn[hh] * beta[hh] for hh in heads]
    vb = [v[hh] * beta[hh] for hh in heads]
    egc = [jnp.exp(x) for x in gc]
    kbg = [kb[hh] * egc[hh] for hh in heads]
    for hh in heads:
        qd_ref[hh] = (qn[hh] * egc[hh]).astype(BF16)
        kd = kn[hh] * jnp.exp(gl[hh] - gc[hh])
        for m in range(tm // 128):
            kdt_ref[hh, :, m * 128:(m + 1) * 128] = kd[m * 128:(m + 1) * 128, :].T.astype(BF16)

    gw = GDN_GROUP * B_CHUNK
    ri = lax.broadcasted_iota(I32, (B_CHUNK, gw), 0)
    lj = lax.broadcasted_iota(I32, (B_CHUNK, gw), 1)
    lb = lj // B_CHUNK
    lj = lj % B_CHUNK
    bmask = (lax.broadcasted_iota(I32, (gw, gw), 0) // B_CHUNK
             == lax.broadcasted_iota(I32, (gw, gw), 1) // B_CHUNK)

    def fold(x):
        xm = jnp.where(bmask, x, 0.0)
        out = xm[0:B_CHUNK, :]
        for c in range(1, GDN_GROUP):
            out = out + xm[c * B_CHUNK:(c + 1) * B_CHUNK, :]
        return out

    def bdiag(cat):
        return jnp.where(bmask, jnp.concatenate([cat] * GDN_GROUP, axis=0), 0.0)

    probs = [(hh, grp) for grp in range(tm // gw) for hh in heads]
    a_cat = []
    for hh, grp in probs:
        rs = slice(grp * gw, (grp + 1) * gw)
        gcs = gc[hh][rs, :]
        col = jnp.concatenate([gcs[0:B_CHUNK, :]] * 2, axis=1)
        for c in range(1, GDN_GROUP):
            col = jnp.where(lb == c, jnp.concatenate([gcs[c * B_CHUNK:(c + 1) * B_CHUNK, :]] * 2, axis=1), col)
        rowf = jnp.concatenate([gcs[m * 128:(m + 1) * 128, :].T[0:B_CHUNK, :] for m in range(gw // 128)],
                               axis=1)
        decay = jnp.exp(jnp.where(ri >= lj, col - rowf, NEG_INF))
        kg = kn[hh][rs, :]
        a_cat.append(jnp.where(ri > lj, fold(_dot_nt(kb[hh][rs, :], kg)) * decay, 0.0))
        attn_ref[hh, grp] = jnp.where(ri >= lj, fold(_dot_nt(qn[hh][rs, :], kg)) * decay, 0.0).astype(BF16)
    eye = jnp.where(ri == lj, 1.0, 0.0)
    inv = [eye - a for a in a_cat]
    pw = [_dot_x3(a, bdiag(a)) for a in a_cat]
    for step in range(5):
        for p in range(len(probs)):
            if step < 4:
                both = _dot_x3(jnp.concatenate([inv[p], pw[p]], axis=0), bdiag(pw[p]))
                inv[p] = inv[p] + both[0:B_CHUNK, :]
                pw[p] = both[B_CHUNK:, :]
            else:
                inv[p] = inv[p] + _dot_x3(inv[p], bdiag(pw[p]))
    for p, (hh, grp) in enumerate(probs):
        rs = slice(grp * gw, (grp + 1) * gw)
        sol = _dot_x3(bdiag(inv[p]), jnp.concatenate([vb[hh][rs, :], kbg[hh][rs, :]], axis=1))
        u_ref[hh, rs, :] = sol[:, 0:B_HEAD_DIM]
        w_ref[hh, rs, :] = sol[:, B_HEAD_DIM:].astype(BF16)
    for hh in heads:
        for c in range(nchunk):
            egl_ref[hh, c:c + 1, :] = jnp.exp(gl[hh][c * B_CHUNK:c * B_CHUNK + 1, :])
        if nchunk < 8:
            egl_ref[hh, nchunk:, :] = jnp.zeros((8 - nchunk, 128), F32)


def _gdn_prep(h, bsz, seq, b_conv, b_a_log, b_dt_bias):
    tm = min(GDN_TM, seq)
    nt = seq // tm
    nrow = max(tm // B_CHUNK, 8)
    gw = GDN_GROUP * B_CHUNK
    hs = (bsz, B_HEADS, seq, B_HEAD_DIM)
    hg = GDN_PREP_HEADS
    hw = hg * 128
    cur = lambda off: (lambda b, i, hp: (b * nt + i, off // hw + hp))
    halo = lambda off: (lambda b, i, hp: (jnp.maximum((b * nt + i) * (tm // 8) - 1, 0), off // hw + hp))
    cw = lambda off: (lambda b, i, hp: (0, off // hw + hp))
    out = lambda b, i, hp: (b, hp, i, 0)
    smem = pl.BlockSpec(memory_space=pltpu.SMEM)
    return pl.pallas_call(
        _gdn_prep_kernel,
        out_shape=(jax.ShapeDtypeStruct(hs, F32), jax.ShapeDtypeStruct(hs, BF16),
                   jax.ShapeDtypeStruct(hs, BF16),
                   jax.ShapeDtypeStruct((bsz, B_HEADS, B_HEAD_DIM, seq), BF16),
                   jax.ShapeDtypeStruct((bsz, B_HEADS, seq // gw, B_CHUNK, gw), BF16),
                   jax.ShapeDtypeStruct((bsz, B_HEADS, nt * nrow, 128), F32)),
        grid=(bsz, nt, B_HEADS // hg),
        in_specs=[smem, smem,
                  pl.BlockSpec((tm, hw), cur(EV_BQ)), pl.BlockSpec((tm, hw), cur(EV_BK)),
                  pl.BlockSpec((tm, hw), cur(EV_BV)),
                  pl.BlockSpec((8, hw), halo(EV_BQ)), pl.BlockSpec((8, hw), halo(EV_BK)),
                  pl.BlockSpec((8, hw), halo(EV_BV)),
                  pl.BlockSpec((tm, 128), lambda b, i, hp: (b * nt + i, EV_MISC // 128)),
                  pl.BlockSpec((CONV_K, hw), cw(0)), pl.BlockSpec((CONV_K, hw), cw(1024)),
                  pl.BlockSpec((CONV_K, hw), cw(2048))],
        out_specs=(pl.BlockSpec((None, hg, tm, 128), out), pl.BlockSpec((None, hg, tm, 128), out),
                   pl.BlockSpec((None, hg, tm, 128), out),
                   pl.BlockSpec((None, hg, B_HEAD_DIM, tm), lambda b, i, hp: (b, hp, 0, i)),
                   pl.BlockSpec((None, hg, tm // gw, B_CHUNK, gw), lambda b, i, hp: (b, hp, i, 0, 0)),
                   pl.BlockSpec((None, hg, nrow, 128), out)),
        compiler_params=_cparams(("parallel", "parallel", "parallel")),
        name="gdn_prep",
    )(b_a_log, b_dt_bias, h, h, h, h, h, h, h, b_conv, b_conv, b_conv)


def _gdn_scan_kernel(u_ref, w_ref, qd_ref, kdt_ref, attn_ref, egl_ref, z_ref, gn_ref, o_ref, s_ref):
    @pl.when(pl.program_id(2) == 0)
    def _():
        s_ref[...] = jnp.zeros(s_ref.shape, F32)

    tm = u_ref.shape[1]
    gw = GDN_GROUP * B_CHUNK
    gain = gn_ref[...]
    zero = jnp.zeros((B_CHUNK, B_HEAD_DIM), BF16)
    states = [s_ref[hh] for hh in range(GDN_SCAN_HEADS)]
    for c in range(tm // B_CHUNK):
        grp, ci = divmod(c, GDN_GROUP)
        rs = slice(c * B_CHUNK, (c + 1) * B_CHUNK)
        for hh in range(GDN_SCAN_HEADS):
            sb = states[hh].astype(BF16)
            v_new = u_ref[hh, rs, :] - _dot(w_ref[hh, rs, :], sb)
            vpad = jnp.concatenate([zero] * ci + [v_new.astype(BF16)] + [zero] * (GDN_GROUP - 1 - ci), axis=0)
            o = _dot(qd_ref[hh, rs, :], sb) + _dot(attn_ref[hh, grp], vpad)
            states[hh] = (states[hh] * egl_ref[hh, c:c + 1, :]
                          + _dot(kdt_ref[hh, :, grp * gw:(grp + 1) * gw], vpad))
            on = o * lax.rsqrt(jnp.mean(o * o, axis=-1, keepdims=True) + EPS) * gain
            lanes = slice(hh * B_HEAD_DIM, (hh + 1) * B_HEAD_DIM)
            o_ref[rs, lanes] = (on * _silu(z_ref[rs, lanes])).astype(BF16)
    for hh in range(GDN_SCAN_HEADS):
        s_ref[hh] = states[hh]


def _gdn_scan(h, bsz, seq, u, w, qd, kd, attn, egl, b_out_norm):
    tm = min(GDN_TM, seq)
    nt = seq // tm
    nrow = max(tm // B_CHUNK, 8)
    gw = GDN_GROUP * B_CHUNK
    hg = GDN_SCAN_HEADS
    blk = lambda b, hp, i: (b, hp, i, 0)
    return pl.pallas_call(
        _gdn_scan_kernel,
        out_shape=jax.ShapeDtypeStruct((bsz * seq, B_HEADS * B_HEAD_DIM), BF16),
        grid=(bsz, B_HEADS // hg, nt),
        in_specs=[pl.BlockSpec((None, hg, tm, 128), blk), pl.BlockSpec((None, hg, tm, 128), blk),
                  pl.BlockSpec((None, hg, tm, 128), blk),
                  pl.BlockSpec((None, hg, B_HEAD_DIM, tm), lambda b, hp, i: (b, hp, 0, i)),
                  pl.BlockSpec((None, hg, tm // gw, B_CHUNK, gw), lambda b, hp, i: (b, hp, i, 0, 0)),
                  pl.BlockSpec((None, hg, nrow, 128), blk),
                  pl.BlockSpec((tm, hg * 128), lambda b, hp, i: (b * nt + i, EV_BZ // (hg * 128) + hp)),
                  pl.BlockSpec((1, 128), lambda b, hp, i: (0, 0))],
        out_specs=pl.BlockSpec((tm, hg * 128), lambda b, hp, i: (b * nt + i, hp)),
        scratch_shapes=[pltpu.VMEM((hg, B_HEAD_DIM, B_HEAD_DIM), F32)],
        compiler_params=_cparams(("parallel", "parallel", "arbitrary")),
        name="gdn_scan",
    )(u, w, qd, kd, attn, egl, h, b_out_norm.reshape(1, -1))


def _rope_kernel(inv_ref, cos_ref, sin_ref):
    tm = cos_ref.shape[0]
    pos = (pl.program_id(0) * tm + lax.broadcasted_iota(I32, (tm, 128), 0)).astype(F32)
    ang = pos * inv_ref[...]
    lane = lax.broadcasted_iota(I32, (tm, 128), 1)
    cos_ref[...] = jnp.cos(ang)
    sin_ref[...] = jnp.where(lane < 64, -jnp.sin(ang), jnp.sin(ang))


def _rope_tables(seq):
    half = C_QK_DIM // 2
    inv = 1.0 / (ROPE_BASE ** jnp.linspace(0.0, 1.0, half, dtype=F32))
    inv2 = jnp.concatenate([inv, inv]).reshape(1, C_QK_DIM)
    tm = min(1024, seq)
    return pl.pallas_call(
        _rope_kernel,
        out_shape=(jax.ShapeDtypeStruct((seq, C_QK_DIM), F32),) * 2,
        grid=(seq // tm,),
        in_specs=[pl.BlockSpec((1, C_QK_DIM), lambda i: (0, 0))],
        out_specs=(pl.BlockSpec((tm, C_QK_DIM), lambda i: (i, 0)),) * 2,
        compiler_params=_cparams(("parallel",)),
        name="rope_tables",
    )(inv2)


RET_TM = 512


def _retention_kernel(lg_ref, q_ref, k_ref, v_ref, z_ref, cos_ref, sin_ref, gn_ref, o_ref,
                      r_ref, dm_ref, zeta_ref, xi_ref):
    hd = pl.program_id(1)
    lg = lg_ref[0, hd]

    @pl.when(pl.program_id(2) == 0)
    def _():
        r_ref[...] = jnp.zeros(r_ref.shape, F32)
        ri = lax.broadcasted_iota(I32, (C_CHUNK, C_CHUNK), 0)
        ci = lax.broadcasted_iota(I32, (C_CHUNK, C_CHUNK), 1)
        diff = (ri - ci).astype(F32)
        dm_ref[...] = jnp.where(diff >= 0, jnp.exp(jnp.maximum(diff, 0.0) * lg), 0.0)
        zeta_ref[...] = jnp.exp((C_CHUNK - 1 - ri).astype(F32) * lg)
        rv = lax.broadcasted_iota(I32, (C_CHUNK, C_V_DIM), 0).astype(F32)
        xi_ref[...] = jnp.exp((rv + 1.0) * lg)

    g_chunk = lg_ref[1, hd]
    tm = q_ref.shape[0]
    state = r_ref[...]
    dmask = dm_ref[...]
    gain = gn_ref[...]
    for c in range(tm // C_CHUNK):
        rs = slice(c * C_CHUNK, (c + 1) * C_CHUNK)
        cos = cos_ref[rs, :]
        sin = sin_ref[rs, :]
        q = q_ref[rs, :]
        k = k_ref[rs, :]
        qr = q * cos + pltpu.roll(q, C_QK_DIM // 2, 1) * sin
        kr = (k * cos + pltpu.roll(k, C_QK_DIM // 2, 1) * sin) * (C_QK_DIM ** -0.5)
        vb = v_ref[rs, :].astype(BF16)
        qb = qr.astype(BF16)
        s = _dot_nt(qb, kr.astype(BF16)) * dmask
        o = _dot(s.astype(BF16), vb) + _dot(qb, state.astype(BF16)) * xi_ref[...]
        state = state * g_chunk + _dot((kr * zeta_ref[...]).T.astype(BF16), vb)
        mu = jnp.mean(o, axis=-1, keepdims=True)
        oc = o - mu
        var = jnp.mean(oc * oc, axis=-1, keepdims=True)
        y = oc * lax.rsqrt(var + EPS) * gain
        o_ref[rs, :] = (y * _silu(z_ref[rs, :])).astype(BF16)
    r_ref[...] = state


def _retention(h, bsz, seq, cos2, sin2, c_out_norm):
    tm = min(RET_TM, seq)
    nt = seq // tm
    lg = np.log1p(-np.exp2(-5.0 - np.arange(C_HEADS, dtype=np.float32))).astype(np.float32)
    lg = np.stack([lg, np.exp(np.float32(C_CHUNK) * lg).astype(np.float32)])
    return pl.pallas_call(
        _retention_kernel,
        out_shape=jax.ShapeDtypeStruct((bsz * seq, C_HEADS * C_V_DIM), BF16),
        grid=(bsz, C_HEADS, nt),
        in_specs=[pl.BlockSpec(memory_space=pltpu.SMEM),
                  pl.BlockSpec((tm, 128), lambda b, hd, i: (b * nt + i, OD_CQ // 128 + hd)),
                  pl.BlockSpec((tm, 128), lambda b, hd, i: (b * nt + i, OD_CK // 128 + hd)),
                  pl.BlockSpec((tm, 256), lambda b, hd, i: (b * nt + i, OD_CV // 256 + hd)),
                  pl.BlockSpec((tm, 256), lambda b, hd, i: (b * nt + i, OD_CZ // 256 + hd)),
                  pl.BlockSpec((tm, 128), lambda b, hd, i: (i, 0)),
                  pl.BlockSpec((tm, 128), lambda b, hd, i: (i, 0)),
                  pl.BlockSpec((1, C_V_DIM), lambda b, hd, i: (0, hd))],
        out_specs=pl.BlockSpec((tm, C_V_DIM), lambda b, hd, i: (b * nt + i, hd)),
        scratch_shapes=[pltpu.VMEM((C_QK_DIM, C_V_DIM), F32),
                        pltpu.VMEM((C_CHUNK, C_CHUNK), F32),
                        pltpu.VMEM((C_CHUNK, C_QK_DIM), F32),
                        pltpu.VMEM((C_CHUNK, C_V_DIM), F32)],
        compiler_params=_cparams(("parallel", "parallel", "arbitrary")),
        name="retention",
    )(jnp.asarray(lg), h, h, h, h, cos2, sin2, c_out_norm.reshape(1, -1))


def _s5_param_kernel(lre_ref, lim_ref, ldt_ref, lrex_ref, limx_ref, ldtx_ref, bre_ref, bim_ref,
                     are_ref, aim_ref, bbre_ref, bbim_ref):
    def disc(lre, lim, ldt):
        lr = jnp.minimum(lre, -1e-4)
        dt = jnp.exp(ldt)
        mag = jnp.exp(lr * dt)
        return lr, lim, mag * jnp.cos(lim * dt), mag * jnp.sin(lim * dt)

    _, _, a_re, a_im = disc(lre_ref[...], lim_ref[...], ldt_ref[...])
    are_ref[...] = a_re
    aim_ref[...] = a_im
    lr, li, ax_re, ax_im = disc(lrex_ref[...], limx_ref[...], ldtx_ref[...])
    den = lr * lr + li * li
    f_re = ((ax_re - 1.0) * lr + ax_im * li) / den
    f_im = (ax_im * lr - (ax_re - 1.0) * li) / den
    bbre_ref[...] = f_re * bre_ref[...] - f_im * bim_ref[...]
    bbim_ref[...] = f_re * bim_ref[...] + f_im * bre_ref[...]


def _s5_params(lam_re, lam_im, log_dt, b_re, b_im, c_re, c_im):
    g, p, ch = D_GROUPS, D_STATE, D_GROUP
    ldt = jnp.broadcast_to(log_dt[:, None], (g, p))
    rep = lambda a: jnp.repeat(a, ch, axis=1)
    vm = pl.BlockSpec(memory_space=pltpu.VMEM)
    a_re, a_im, bb_re, bb_im = pl.pallas_call(
        _s5_param_kernel,
        out_shape=(jax.ShapeDtypeStruct((g, p), F32), jax.ShapeDtypeStruct((g, p), F32),
                   jax.ShapeDtypeStruct((g, p * ch), F32), jax.ShapeDtypeStruct((g, p * ch), F32)),
        in_specs=[vm] * 8, out_specs=(vm,) * 4,
        name="s5_params",
    )(lam_re, lam_im, ldt, rep(lam_re), rep(lam_im), rep(ldt),
      b_re.reshape(g, p * ch), b_im.reshape(g, p * ch))
    eye = jnp.eye(D_SETS * 4, dtype=F32)

    def pack_b(bb):
        bb = bb.reshape(D_SETS, 16, p, ch)
        return jnp.einsum('sgpi,gh->sgihp', bb, eye).reshape(D_SETS, D_SET_CH, D_SET_ST)

    def pack_c(c):
        c = c.reshape(D_SETS, 16, ch, p)
        return jnp.einsum('sgjp,gh->sgphj', c, eye).reshape(D_SETS, D_SET_ST, D_SET_CH)

    bd = jnp.concatenate([pack_b(bb_re), pack_b(bb_im)], axis=-1).astype(BF16)
    slab = (D_SETS, S5_SLAB, 128)
    return (a_re.reshape(slab), a_im.reshape(slab), bd,
            pack_c(c_re).astype(BF16), pack_c(c_im).astype(BF16))


S5_TM = 256
S5_SLAB = D_SET_ST // 128


def _s5_kernel(u_ref, z_ref, are_ref, aim_ref, bd_ref, cre_ref, cim_ref, dskip_ref, wglu_ref, bglu_ref,
               o_ref, hre_ref, him_ref, y_ref, *x_refs):
    @pl.when(pl.program_id(1) == 0)
    def _():
        hre_ref[...] = jnp.zeros(hre_ref.shape, F32)
        him_ref[...] = jnp.zeros(him_ref.shape, F32)

    xre, xim = x_refs[:D_SETS], x_refs[D_SETS:]
    tm = u_ref.shape[0]
    u = u_ref[...]
    ub = u.astype(BF16)
    for s in range(D_SETS):
        bu = _dot(ub[:, s * D_SET_CH:(s + 1) * D_SET_CH], bd_ref[s])
        for k in range(S5_SLAB):
            xre[s][pl.ds(k, tm, stride=S5_SLAB), :] = bu[:, k * 128:(k + 1) * 128]
            xim[s][pl.ds(k, tm, stride=S5_SLAB), :] = bu[:, D_SET_ST + k * 128:D_SET_ST + (k + 1) * 128]
    a_re = [are_ref[s] for s in range(D_SETS)]
    a_im = [aim_ref[s] for s in range(D_SETS)]

    def step(t, carry):
        rows = pl.ds(pl.multiple_of(t * S5_SLAB, S5_SLAB), S5_SLAB)
        out = []
        for s in range(D_SETS):
            hr, hi = carry[2 * s], carry[2 * s + 1]
            xr = xre[s][rows, :] + a_re[s] * hr - a_im[s] * hi
            xi = xim[s][rows, :] + a_re[s] * hi + a_im[s] * hr
            xre[s][rows, :] = xr
            xim[s][rows, :] = xi
            out += [xr, xi]
        return tuple(out)

    carry = []
    for s in range(D_SETS):
        carry += [hre_ref[s], him_ref[s]]
    carry = lax.fori_loop(0, tm, step, tuple(carry), unroll=8)
    for s in range(D_SETS):
        hre_ref[s] = carry[2 * s]
        him_ref[s] = carry[2 * s + 1]
        x_r = jnp.concatenate([xre[s][pl.ds(k, tm, stride=S5_SLAB), :] for k in range(S5_SLAB)], axis=1)
        x_i = jnp.concatenate([xim[s][pl.ds(k, tm, stride=S5_SLAB), :] for k in range(S5_SLAB)], axis=1)
        y_ref[:, s * D_SET_CH:(s + 1) * D_SET_CH] = (
            _dot(x_r.astype(BF16), cre_ref[s]) - _dot(x_i.astype(BF16), cim_ref[s]))
    y = y_ref[...] + dskip_ref[...] * u
    y = 0.5 * y * (1.0 + jnp.tanh(math.sqrt(2.0 / math.pi) * (y + 0.044715 * (y * y * y))))
    gate = _sigmoid(_dot(y.astype(BF16), wglu_ref[...]) + bglu_ref[...])
    o_ref[...] = (y * gate * _silu(z_ref[...])).astype(BF16)


def _s5(h, bsz, seq, a_re, a_im, bd, cd_re, cd_im, d_skip, w_glu, b_glu):
    tm = min(S5_TM, seq)
    nt = seq // tm
    full = lambda *shape: pl.BlockSpec(shape, lambda b, i: (0,) * len(shape))
    width = D_GROUPS * D_GROUP
    return pl.pallas_call(
        _s5_kernel,
        out_shape=jax.ShapeDtypeStruct((bsz * seq, width), BF16),
        grid=(bsz, nt),
        in_specs=[pl.BlockSpec((tm, width), lambda b, i: (b * nt + i, OD_DU // width)),
                  pl.BlockSpec((tm, width), lambda b, i: (b * nt + i, OD_DZ // width)),
                  full(D_SETS, S5_SLAB, 128), full(D_SETS, S5_SLAB, 128),
                  full(D_SETS, D_SET_CH, 2 * D_SET_ST),
                  full(D_SETS, D_SET_ST, D_SET_CH), full(D_SETS, D_SET_ST, D_SET_CH),
                  full(1, width), full(width, width), full(1, width)],
        out_specs=pl.BlockSpec((tm, width), lambda b, i: (b * nt + i, 0)),
        scratch_shapes=([pltpu.VMEM((D_SETS, S5_SLAB, 128), F32), pltpu.VMEM((D_SETS, S5_SLAB, 128), F32),
                         pltpu.VMEM((tm, width), F32)]
                        + [pltpu.VMEM((tm * S5_SLAB, 128), F32)] * (2 * D_SETS)),
        compiler_params=_cparams(("parallel", "arbitrary")),
        name="s5",
    )(h, h, a_re, a_im, bd, cd_re, cd_im, d_skip.reshape(1, -1), w_glu.astype(BF16),
      b_glu.reshape(1, -1))


def _pack_even_w(w_in):
    sizes = (1024, A_KV_RANK, A_IDX_HEADS * A_IDX_DIM, A_IDX_DIM, A_IDX_HEADS, 1024,
             3 * 1024, B_HEADS, B_HEADS, 1024)
    parts, start = [], 0
    for s in sizes:
        parts.append(w_in[:, start:start + s])
        start += s
    aq, ckv, qi, ki, wi, az, bqkv, ba, bb, bz = parts
    pad = jnp.zeros((w_in.shape[0], EV_WIDTH - start), w_in.dtype)
    return jnp.concatenate([aq, az, bqkv, bz, qi, ckv, ki, wi, ba, bb, pad], axis=1).astype(BF16)


def _even_layer(x2d, bsz, seq, bias, norm_g, w_in, a_q_norm, a_kv_norm, w_kv_up, a_k_norm,
                b_conv, b_a_log, b_dt_bias, b_out_norm, w_out):
    h = _norm_proj(x2d, norm_g, _pack_even_w(w_in), tn=768)
    qta, qit, wt, kidx, katt, vt = _dsa_prep(h, bsz * seq // QB, a_q_norm, a_kv_norm, w_kv_up, a_k_norm)
    mix_a = _dsa_attend(h, bsz, seq, qta, qit, wt, kidx, katt, vt, bias)
    u, w, qd, kd, attn, egl = _gdn_prep(h, bsz, seq, b_conv, b_a_log, b_dt_bias)
    mix_b = _gdn_scan(h, bsz, seq, u, w, qd, kd, attn, egl, b_out_norm)
    return _out_proj(x2d, mix_a, mix_b, w_out)


def _odd_layer(x2d, bsz, seq, cos2, sin2, norm_g, w_in, c_out_norm, lam_re, lam_im, log_dt,
               b_re, b_im, c_re, c_im, d_skip, w_glu, b_glu, w_out):
    h = _norm_proj(x2d, norm_g, w_in.astype(BF16), tn=1024)
    mix_c = _retention(h, bsz, seq, cos2, sin2, c_out_norm)
    s5p = _s5_params(lam_re, lam_im, log_dt, b_re, b_im, c_re, c_im)
    mix_d = _s5(h, bsz, seq, *s5p, d_skip, w_glu, b_glu)
    return _out_proj(x2d, mix_c, mix_d, w_out)


def kernel(x, rel_bias, ev_norm, ev_w_in, ev_a_q_norm, ev_a_kv_norm, ev_w_kv_up, ev_a_k_norm,
           ev_b_conv, ev_b_a_log, ev_b_dt_bias, ev_b_out_norm, ev_w_out,
           od_norm, od_w_in, od_c_out_norm, od_lam_re, od_lam_im, od_log_dt,
           od_b_re, od_b_im, od_c_re, od_c_im, od_d_skip, od_w_glu, od_b_glu, od_w_out):
    bsz, seq, d = x.shape
    depth = ev_norm.shape[0] + od_norm.shape[0]
    x2d = x.reshape(bsz * seq, d)
    bias = _bias_table(rel_bias)
    cos2, sin2 = _rope_tables(seq)
    for layer in range(depth):
        i = layer // 2
        if layer % 2 == 0:
            x2d = _even_layer(x2d, bsz, seq, bias, ev_norm[i], ev_w_in[i], ev_a_q_norm[i],
                              ev_a_kv_norm[i], ev_w_kv_up[i], ev_a_k_norm[i], ev_b_conv[i],
                              ev_b_a_log[i], ev_b_dt_bias[i], ev_b_out_norm[i], ev_w_out[i])
        else:
            x2d = _odd_layer(x2d, bsz, seq, cos2, sin2, od_norm[i], od_w_in[i], od_c_out_norm[i],
                             od_lam_re[i], od_lam_im[i], od_log_dt[i], od_b_re[i], od_b_im[i],
                             od_c_re[i], od_c_im[i], od_d_skip[i], od_w_glu[i], od_b_glu[i], od_w_out[i])
    return x2d.reshape(bsz, seq, d)
```

```python
import functools
import math

import numpy as np
import jax
import jax.numpy as jnp
from jax import lax
from jax.experimental import pallas as pl
from jax.experimental.pallas import tpu as pltpu

F32 = jnp.float32
BF16 = jnp.bfloat16
I32 = jnp.int32

D_MODEL = 1024
EPS = 1e-6
LOG2E = 1.4426950408889634
NEG_INF = float("-inf")
INT_MIN = -(2 ** 31)

A_HEADS, A_HEAD_DIM, A_KV_RANK = 16, 64, 128
A_IDX_HEADS, A_IDX_DIM, A_TOPK_MAX = 8, 64, 256
QB = 128
REL_BUCKETS, REL_MAX_DIST = 32, 128
VT_ROWS = 80
SB = 2 * QB
SCORE_ROWS = 4 * QB
BIAS_ROWS = 5 * QB
MASKED = -(2.0 ** 100)
M_INIT = -(2.0 ** 60)
TIE_WALK_MAX = 8
SCORE_UNROLL = 4
COUNT_UNROLL = 4
B_HEADS, B_HEAD_DIM, CONV_K, B_CHUNK = 8, 128, 4, 64
C_HEADS, C_QK_DIM, C_V_DIM, C_CHUNK = 4, 128, 256, 128
ROPE_BASE = 10000.0
D_GROUP, D_STATE, D_GROUPS = 16, 64, 64
D_SETS, D_SET_CH, D_SET_ST = 4, 256, 1024

EV_AQ, EV_AZ, EV_BQ, EV_BK, EV_BV, EV_BZ, EV_QI, EV_CKV, EV_MISC = (
    0, 1024, 2048, 3072, 4096, 5120, 6144, 6656, 6784)
EV_WIDTH = 6912
MISC_KI, MISC_WI, MISC_BA, MISC_BB = 0, 64, 72, 80
OD_CQ, OD_CK, OD_CV, OD_CZ, OD_DU, OD_DZ = 0, 512, 1024, 2048, 3072, 4096
OD_WIDTH = 5120

VMEM_LIMIT = 48 * 1024 * 1024


def _cparams(sem):
    return pltpu.CompilerParams(dimension_semantics=sem, vmem_limit_bytes=VMEM_LIMIT)


def _dot(a, b):
    return jnp.dot(a, b, preferred_element_type=F32)


def _dot_nt(a, b):
    return lax.dot_general(a, b, (((1,), (1,)), ((), ())), preferred_element_type=F32)


def _split_bf16(x, n):
    parts = []
    for _ in range(n):
        p = x.astype(BF16)
        parts.append(p)
        x = x - p.astype(F32)
    return parts


def _dot_sel(sel, x):
    sel = sel.astype(BF16)
    hi, mid, lo = _split_bf16(x, 3)
    return _dot(sel, hi) + (_dot(sel, mid) + _dot(sel, lo))


def _dot_xsel(x, sel):
    sel = sel.astype(BF16)
    hi, mid, lo = _split_bf16(x, 3)
    return _dot(hi, sel) + (_dot(mid, sel) + _dot(lo, sel))


def _dot_x3(a, b):
    ah, al = _split_bf16(a, 2)
    bh, bl = _split_bf16(b, 2)
    return _dot(ah, bh) + (_dot(ah, bl) + _dot(al, bh))


def _sigmoid(x):
    return 1.0 / (1.0 + jnp.exp(-x))


def _silu(x):
    return x * _sigmoid(x)


def _norm_proj_kernel(x_ref, g_ref, w_ref, o_ref, xn_ref):
    @pl.when(pl.program_id(1) == 0)
    def _():
        x = x_ref[...]
        ms = jnp.mean(x * x, axis=-1, keepdims=True)
        xn_ref[...] = (x * lax.rsqrt(ms + EPS) * g_ref[...]).astype(BF16)

    o_ref[...] = _dot(xn_ref[...], w_ref[...])


def _norm_proj(x2d, gain, w_bf16, tn, tm=1024):
    t, d = x2d.shape
    n = w_bf16.shape[1]
    tm = min(tm, t)
    return pl.pallas_call(
        _norm_proj_kernel,
        out_shape=jax.ShapeDtypeStruct((t, n), F32),
        grid=(t // tm, n // tn),
        in_specs=[pl.BlockSpec((tm, d), lambda i, j: (i, 0)),
                  pl.BlockSpec((1, d), lambda i, j: (0, 0)),
                  pl.BlockSpec((d, tn), lambda i, j: (0, j))],
        out_specs=pl.BlockSpec((tm, tn), lambda i, j: (i, j)),
        scratch_shapes=[pltpu.VMEM((tm, d), BF16)],
        compiler_params=_cparams(("parallel", "arbitrary")),
        name="norm_proj",
    )(x2d, gain.reshape(1, d), w_bf16)


def _out_proj_kernel(x_ref, a_ref, b_ref, wa_ref, wb_ref, o_ref):
    o_ref[...] = x_ref[...] + _dot(a_ref[...], wa_ref[...]) + _dot(b_ref[...], wb_ref[...])


def _out_proj(x2d, mix_a, mix_b, w_out, tm=512):
    t, d = x2d.shape
    half = mix_a.shape[1]
    tm = min(tm, t)
    wa = w_out[:half].astype(BF16)
    wb = w_out[half:].astype(BF16)
    return pl.pallas_call(
        _out_proj_kernel,
        out_shape=jax.ShapeDtypeStruct((t, d), F32),
        grid=(t // tm,),
        in_specs=[pl.BlockSpec((tm, d), lambda i: (i, 0)),
                  pl.BlockSpec((tm, half), lambda i: (i, 0)),
                  pl.BlockSpec((tm, half), lambda i: (i, 0)),
                  pl.BlockSpec((half, d), lambda i: (0, 0)),
                  pl.BlockSpec((half, d), lambda i: (0, 0))],
        out_specs=pl.BlockSpec((tm, d), lambda i: (i, 0)),
        compiler_params=_cparams(("parallel",)),
        name="out_proj",
    )(x2d, mix_a, mix_b, wa, wb)


def _t5_bucket_starts():
    exact = REL_BUCKETS // 2
    n = np.arange(0, 4 * REL_MAX_DIST, dtype=np.int64)
    ratio = np.maximum(n, 1).astype(np.float32) / np.float32(exact)
    large = exact + (np.log(ratio).astype(np.float32) / np.float32(math.log(REL_MAX_DIST / exact))
                     * np.float32(REL_BUCKETS - exact)).astype(np.int32)
    bucket = np.where(n < exact, n, np.minimum(large, REL_BUCKETS - 1))
    starts = [int(np.argmax(bucket >= b)) for b in range(REL_BUCKETS)]
    assert all(bucket[s] == b for b, s in enumerate(starts)) and np.all(np.diff(bucket) >= 0)
    assert starts[-1] <= QB, "distances beyond one key block must share the last bucket"
    return starts


_BUCKET_STARTS = _t5_bucket_starts()


def _bias_table_kernel(rb_ref, o_ref):
    row = lax.broadcasted_iota(I32, (BIAS_ROWS, QB), 0)
    lane = lax.broadcasted_iota(I32, (BIAS_ROWS, QB), 1)
    dist = lane + 3 * QB - row
    for h in range(A_HEADS):
        val = jnp.full((BIAS_ROWS, QB), rb_ref[0, h], F32)
        for b in range(1, REL_BUCKETS):
            val = jnp.where(dist >= _BUCKET_STARTS[b], rb_ref[b, h], val)
        val = (val - rb_ref[REL_BUCKETS - 1, h]) * LOG2E
        o_ref[:, h * QB:(h + 1) * QB] = jnp.where(dist >= 0, val, 0.0)


def _bias_table(rel_bias):
    return pl.pallas_call(
        _bias_table_kernel,
        out_shape=jax.ShapeDtypeStruct((BIAS_ROWS, A_HEADS * QB), F32),
        in_specs=[pl.BlockSpec(memory_space=pltpu.SMEM)],
        out_specs=pl.BlockSpec(memory_space=pltpu.VMEM),
        name="dsa_bias_table",
    )(rel_bias)


def _dsa_prep_kernel(aq_ref, qi_ref, ckv_ref, misc_ref, gq_ref, gkv_ref, wkv_ref, gk_ref,
                     qta_ref, qit_ref, wt_ref, kidx_ref, katt_ref, vt_ref):
    zeros = jnp.zeros((QB, A_HEAD_DIM), F32)
    aq = aq_ref[...]
    gq = gq_ref[...]
    ident = jnp.where(lax.broadcasted_iota(I32, (QB, QB), 0) == lax.broadcasted_iota(I32, (QB, QB), 1),
                      1.0, 0.0).astype(BF16)
    for h in range(A_HEADS):
        q = aq[:, h * A_HEAD_DIM:(h + 1) * A_HEAD_DIM]
        ms = jnp.mean(q * q, axis=-1, keepdims=True)
        qn = q * lax.rsqrt(ms + EPS) * gq * (A_HEAD_DIM ** -0.5 * LOG2E)
        qta_ref[0:QB, h * QB:(h + 1) * QB] = jnp.concatenate([qn, zeros], axis=1).T.astype(BF16)
        qta_ref[QB:2 * QB, h * QB:(h + 1) * QB] = ident
    qi = qi_ref[...]
    for h in range(A_IDX_HEADS):
        qih = qi[:, h * A_IDX_DIM:(h + 1) * A_IDX_DIM]
        qit_ref[:, h * QB:(h + 1) * QB] = jnp.concatenate([qih, zeros], axis=1).T.astype(BF16)
    misc = misc_ref[...]
    wt_ref[...] = misc.T[MISC_WI:MISC_WI + A_IDX_HEADS, :] * (A_IDX_HEADS ** -0.5 * A_IDX_DIM ** -0.5)
    c = ckv_ref[...]
    cn = c * lax.rsqrt(jnp.mean(c * c, axis=-1, keepdims=True) + EPS) * gkv_ref[...]
    kv = _dot(cn.astype(BF16), wkv_ref[...])
    k = kv[:, :A_HEAD_DIM]
    kn = k * lax.rsqrt(jnp.mean(k * k, axis=-1, keepdims=True) + EPS) * gk_ref[...]
    kidx_ref[...] = jnp.concatenate([misc[:, MISC_KI:MISC_KI + A_IDX_DIM], zeros], axis=1).astype(BF16)
    katt_ref[...] = jnp.concatenate([kn, zeros], axis=1).astype(BF16)
    kvt = kv.T
    vt_ref[0:A_HEAD_DIM, :] = kvt[A_HEAD_DIM:, :].astype(BF16)
    ones_row = lax.broadcasted_iota(I32, (VT_ROWS - A_HEAD_DIM, QB), 0) == 0
    vt_ref[A_HEAD_DIM:, :] = jnp.where(ones_row, 1.0, 0.0).astype(BF16)


def _dsa_prep(h, nblk, a_q_norm, a_kv_norm, w_kv_up, a_k_norm):
    cb = lambda width, off: off // width
    return pl.pallas_call(
        _dsa_prep_kernel,
        out_shape=(jax.ShapeDtypeStruct((nblk, 2 * QB, A_HEADS * QB), BF16),
                   jax.ShapeDtypeStruct((nblk, QB, A_IDX_HEADS * QB), BF16),
                   jax.ShapeDtypeStruct((nblk, A_IDX_HEADS, QB), F32),
                   jax.ShapeDtypeStruct((nblk, QB, 128), BF16),
                   jax.ShapeDtypeStruct((nblk, QB, 128), BF16),
                   jax.ShapeDtypeStruct((nblk, VT_ROWS, QB), BF16)),
        grid=(nblk,),
        in_specs=[pl.BlockSpec((QB, 1024), lambda i: (i, cb(1024, EV_AQ))),
                  pl.BlockSpec((QB, 512), lambda i: (i, cb(512, EV_QI))),
                  pl.BlockSpec((QB, 128), lambda i: (i, cb(128, EV_CKV))),
                  pl.BlockSpec((QB, 128), lambda i: (i, cb(128, EV_MISC))),
                  pl.BlockSpec((1, A_HEAD_DIM), lambda i: (0, 0)),
                  pl.BlockSpec((1, A_KV_RANK), lambda i: (0, 0)),
                  pl.BlockSpec((A_KV_RANK, 2 * A_HEAD_DIM), lambda i: (0, 0)),
                  pl.BlockSpec((1, A_HEAD_DIM), lambda i: (0, 0))],
        out_specs=(pl.BlockSpec((None, 2 * QB, A_HEADS * QB), lambda i: (i, 0, 0)),
                   pl.BlockSpec((None, QB, A_IDX_HEADS * QB), lambda i: (i, 0, 0)),
                   pl.BlockSpec((None, A_IDX_HEADS, QB), lambda i: (i, 0, 0)),
                   pl.BlockSpec((None, QB, 128), lambda i: (i, 0, 0)),
                   pl.BlockSpec((None, QB, 128), lambda i: (i, 0, 0)),
                   pl.BlockSpec((None, VT_ROWS, QB), lambda i: (i, 0, 0))),
        compiler_params=_cparams(("parallel",)),
        name="dsa_prep",
    )(h, h, h, h, a_q_norm.reshape(1, -1), a_kv_norm.reshape(1, -1), w_kv_up.astype(BF16),
      a_k_norm.reshape(1, -1))


def _dsa_kernel(topk, qit_ref, wt_ref, qta_ref, kidx_ref, katt_ref, vt_ref, bias_ref, az_ref, o_ref,
                strip_ref, x_ref, m_ref, acc_ref, lg_ref, mx_ref, p_ref):
    qt = pl.program_id(1)
    t0 = qt * QB
    n_sc = qt // 4 + 1
    n_sb = qt // 2 + 1
    lane_s = lax.broadcasted_iota(I32, (SCORE_ROWS, QB), 1)
    row_s = lax.broadcasted_iota(I32, (SCORE_ROWS, QB), 0)

    w = wt_ref[...]

    def score_body(j, carry):
        kblk = kidx_ref[pl.ds(j * 4, 4)].reshape(SCORE_ROWS, 128)
        tot = None
        for hp in range(A_IDX_HEADS // 2):
            s = _dot(kblk, qit_ref[:, hp * SB:(hp + 1) * SB])
            for e in range(2):
                h = 2 * hp + e
                term = jnp.maximum(s[:, e * QB:(e + 1) * QB], 0.0) * w[h:h + 1, :]
                tot = term if tot is None else tot + term
        tot = jnp.where(j * SCORE_ROWS + row_s <= t0 + lane_s, tot, NEG_INF)
        bits = pltpu.bitcast(tot, I32)
        strip_ref[pl.ds(pl.multiple_of(j * SCORE_ROWS, SCORE_ROWS), SCORE_ROWS), :] = (
            bits ^ ((bits >> 31) & 0x7FFFFFFF))
        return carry

    def score_group(jj, carry):
        for sub in range(SCORE_UNROLL):
            score_body(SCORE_UNROLL * jj + sub, carry)
        return carry

    lax.fori_loop(0, n_sc // SCORE_UNROLL, score_group, 0)
    lax.fori_loop(SCORE_UNROLL * (n_sc // SCORE_UNROLL), n_sc, score_body, 0)

    def count(pred):
        def one(j, acc):
            r0 = pl.multiple_of(j * SCORE_ROWS, SCORE_ROWS)
            m = jnp.where(pred(strip_ref[pl.ds(r0, SCORE_ROWS), :], r0), 1, 0)
            return acc + m.reshape(SCORE_ROWS // 32, 32, QB).sum(axis=0)

        def body(j, acc):
            for sub in range(COUNT_UNROLL):
                acc = one(COUNT_UNROLL * j + sub, acc)
            return acc
        n_main = n_sc // COUNT_UNROLL
        acc = lax.fori_loop(0, n_main, body, jnp.zeros((32, QB), I32))
        acc = lax.fori_loop(n_main * COUNT_UNROLL, n_sc, one, acc)
        return acc.reshape(4, 8, QB).sum(axis=0).sum(axis=0, keepdims=True)

    count_ge = lambda cand: count(lambda key, r0: key >= cand)
    c0 = count_ge(jnp.zeros((1, QB), I32))
    nonneg = c0 >= topk
    thr0 = jnp.where(nonneg, 0, INT_MIN)
    cnt0 = jnp.where(nonneg, c0, n_sc * SCORE_ROWS)

    def bit_steps(counter, n):
        def one(_, carry):
            i, thr, cnt = carry
            cand = thr | lax.shift_left(jnp.int32(1), 30 - i)
            c = counter(cand)
            ok = c >= topk
            return i + 1, jnp.where(ok, cand, thr), jnp.where(ok, c, cnt)
        return lambda carry: lax.fori_loop(0, n, one, carry)

    state = bit_steps(count_ge, 15)((jnp.int32(0), thr0, cnt0))
    _, thr, cnt = lax.while_loop(lambda c: (c[0] < 31) & (jnp.max(c[2]) > topk),
                                 bit_steps(count_ge, 4), state)

    x_ref[...] = jnp.full((1, QB), 2 ** 30, I32)
    tied = cnt > topk

    @pl.when(jnp.max(cnt) > topk)
    def _():
        above = count(lambda key, r0: key > thr)
        need = jnp.where(tied, topk - above, 0)
        max_need = jnp.max(need)

        @pl.when(max_need <= TIE_WALK_MAX)
        def _():
            def next_tie(prev):
                def body(j, acc):
                    r0 = pl.multiple_of(j * SCORE_ROWS, SCORE_ROWS)
                    idx = r0 + row_s
                    hit = (strip_ref[pl.ds(r0, SCORE_ROWS), :] == thr) & (idx > prev)
                    return jnp.minimum(acc, jnp.where(hit, idx, 2 ** 30)
                                       .reshape(SCORE_ROWS // 32, 32, QB).min(axis=0))
                acc = lax.fori_loop(0, n_sc, body, jnp.full((32, QB), 2 ** 30, I32))
                return acc.reshape(4, 8, QB).min(axis=0).min(axis=0, keepdims=True)

            last = lax.fori_loop(0, max_need, lambda r, prev: jnp.where(r < need, next_tie(prev), prev),
                                 jnp.full((1, QB), -1, I32))
            x_ref[...] = jnp.where(tied, last + 1, 2 ** 30)

        @pl.when(max_need > TIE_WALK_MAX)
        def _():
            def tie_body(i, xb):
                cand = xb | lax.shift_left(jnp.int32(1), 14 - i)
                c = count(lambda key, r0: (key == thr) & (r0 + row_s < cand))
                return jnp.where(c <= need, cand, xb)

            xb = lax.fori_loop(0, 15, tie_body, jnp.zeros((1, QB), I32))
            x_ref[...] = jnp.where(tied, xb, 2 ** 30)

    xb = x_ref[...]

    m_ref[...] = jnp.full(m_ref.shape, M_INIT, F32)
    acc_ref[...] = jnp.zeros(acc_ref.shape, F32)
    n_lt = A_HEADS * QB // SB
    row_a = lax.broadcasted_iota(I32, (SB, QB), 0)
    lane_a = lax.broadcasted_iota(I32, (SB, QB), 1)

    def logits(j, slot, near):
        r0 = pl.multiple_of(j * SB, SB)
        key = strip_ref[pl.ds(r0, SB), :]
        idx = r0 + row_a
        sel = (key > thr) | ((key == thr) & (idx < xb))
        if near:
            sel = sel & (idx <= t0 + lane_a)
            b0 = pl.multiple_of((2 * j - qt + 3) * QB, QB)
        pen = jnp.where(sel, 0.0, MASKED).astype(BF16)
        kaug = jnp.concatenate([katt_ref[pl.ds(2 * j, 2)].reshape(SB, 128), pen], axis=1)
        for lt in range(n_lt):
            cols = slice(lt * SB, (lt + 1) * SB)
            lg = _dot(kaug, qta_ref[:, cols])
            if near:
                lg = lg + bias_ref[pl.ds(b0, SB), cols]
            lg = lg.astype(BF16)
            lg_ref[slot, :, cols] = lg
            mx_ref[slot, :, cols] = jnp.max(lg, axis=0, keepdims=True).astype(F32)

    def accumulate(j, slot):
        vt = jnp.concatenate([vt_ref[2 * j], vt_ref[2 * j + 1]], axis=1)
        m_old = m_ref[...]
        m_new = jnp.maximum(m_old, mx_ref[slot])
        m_ref[...] = m_new
        m_b = m_new.astype(BF16)
        for lt in range(n_lt):
            cols = slice(lt * SB, (lt + 1) * SB)
            p_ref[:, cols] = jnp.exp2(lg_ref[slot, :, cols] - m_b[:, cols])
        acc_ref[...] = jnp.exp2(m_old - m_new) * acc_ref[...] + _dot(vt, p_ref[...])

    n_far = jnp.maximum(n_sb - 2, 0)
    n_pair = jnp.maximum(n_far - 1, 0) // 2
    n_rem = n_far - 2 * n_pair

    @pl.when(n_far > 0)
    def _():
        logits(0, 0, False)

    def fused(j, slot):
        nxt = 1 - slot
        r0 = pl.multiple_of((j + 1) * SB, SB)
        key = strip_ref[pl.ds(r0, SB), :]
        sel = (key > thr) | ((key == thr) & (r0 + row_a < xb))
        pen = jnp.where(sel, 0.0, MASKED).astype(BF16)
        kaug = jnp.concatenate([katt_ref[pl.ds(2 * j + 2, 2)].reshape(SB, 128), pen], axis=1)
        vt = jnp.concatenate([vt_ref[2 * j], vt_ref[2 * j + 1]], axis=1)
        m_old = m_ref[...]
        m_new = jnp.maximum(m_old, mx_ref[slot])
        m_ref[...] = m_new
        m_b = m_new.astype(BF16)
        alpha = jnp.exp2(m_old - m_new)
        for lt in range(n_lt):
            cols = slice(lt * SB, (lt + 1) * SB)
            lg = _dot(kaug, qta_ref[:, cols]).astype(BF16)
            lg_ref[nxt, :, cols] = lg
            mx_ref[nxt, :, cols] = jnp.max(lg, axis=0, keepdims=True).astype(F32)
            p = jnp.exp2(lg_ref[slot, :, cols] - m_b[:, cols])
            acc_ref[:, cols] = alpha[:, cols] * acc_ref[:, cols] + _dot(vt, p)

    def pair_body(jj, carry):
        j = 2 * jj
        fused(j, 0)
        fused(j + 1, 1)
        return carry

    lax.fori_loop(0, n_pair, pair_body, 0)

    @pl.when(n_rem == 1)
    def _():
        accumulate(n_far - 1, 0)

    @pl.when(n_rem == 2)
    def _():
        logits(n_far - 1, 1, False)
        accumulate(n_far - 2, 0)
        accumulate(n_far - 1, 1)

    @pl.when(n_sb >= 2)
    def _():
        logits(n_sb - 2, 0, True)
        accumulate(n_sb - 2, 0)

    logits(n_sb - 1, 0, True)
    accumulate(n_sb - 1, 0)

    acc = acc_ref[...]
    o_t = acc[0:A_HEAD_DIM, :] / acc[A_HEAD_DIM:A_HEAD_DIM + 1, :]
    pad = jnp.zeros((QB - A_HEAD_DIM, QB), F32)
    pieces = []
    for h in range(A_HEADS):
        blk = jnp.concatenate([o_t[:, h * QB:(h + 1) * QB], pad], axis=0)
        pieces.append(blk.T[:, 0:A_HEAD_DIM])
    att = jnp.concatenate(pieces, axis=1)
    o_ref[...] = (att * _silu(az_ref[...])).astype(BF16)


def _dsa_attend(h, bsz, seq, qta, qit, wt, kidx, katt, vt, bias):
    nq = seq // QB
    topk = min(A_TOPK_MAX, seq // 4)
    blk = lambda b, q: (b * nq + q, 0, 0)
    per_batch = lambda b, q: (b, 0, 0)
    once = pl.Buffered(1)
    return pl.pallas_call(
        functools.partial(_dsa_kernel, topk),
        out_shape=jax.ShapeDtypeStruct((bsz * seq, A_HEADS * A_HEAD_DIM), BF16),
        grid=(bsz, nq),
        in_specs=[pl.BlockSpec((None, QB, A_IDX_HEADS * QB), blk),
                  pl.BlockSpec((None, A_IDX_HEADS, QB), blk),
                  pl.BlockSpec((None, 2 * QB, A_HEADS * QB), blk),
                  pl.BlockSpec((nq, QB, 128), per_batch, pipeline_mode=once),
                  pl.BlockSpec((nq, QB, 128), per_batch, pipeline_mode=once),
                  pl.BlockSpec((nq, VT_ROWS, QB), per_batch, pipeline_mode=once),
                  pl.BlockSpec((BIAS_ROWS, A_HEADS * QB), lambda b, q: (0, 0), pipeline_mode=once),
                  pl.BlockSpec((QB, 1024), lambda b, q: (b * nq + q, EV_AZ // 1024))],
        out_specs=pl.BlockSpec((QB, 1024), lambda b, q: (b * nq + q, 0)),
        scratch_shapes=[pltpu.VMEM((seq, QB), I32),
                        pltpu.VMEM((1, QB), I32),
                        pltpu.VMEM((1, A_HEADS * QB), F32),
                        pltpu.VMEM((VT_ROWS, A_HEADS * QB), F32),
                        pltpu.VMEM((2, SB, A_HEADS * QB), BF16),
                        pltpu.VMEM((2, 1, A_HEADS * QB), F32),
                        pltpu.VMEM((SB, A_HEADS * QB), BF16)],
        compiler_params=_cparams(("arbitrary", "arbitrary")),
        name="dsa_attend",
    )(qit, wt, qta, kidx, katt, vt, bias, h)


GDN_TM = 512
GDN_GROUP = 4
GDN_SCAN_HEADS = 4
GDN_PREP_HEADS = 2


def _gdn_prep_kernel(alog_ref, dtb_ref, q_ref, k_ref, v_ref, qh_ref, kh_ref, vh_ref, misc_ref,
                     cq_ref, ck_ref, cv_ref,
                     u_ref, w_ref, qd_ref, kdt_ref, attn_ref, egl_ref):
    i = pl.program_id(1)
    hp = pl.program_id(2)
    tm = q_ref.shape[0]
    nchunk = tm // B_CHUNK
    heads = range(GDN_PREP_HEADS)
    lanes = [slice(hh * 128, (hh + 1) * 128) for hh in heads]

    def conv_silu(x_ref, halo_ref, c_ref, ln):
        halo = jnp.where(i > 0, halo_ref[:, ln], 0.0)
        ext = jnp.concatenate([halo, x_ref[:, ln]], axis=0)
        cw = c_ref[:, ln]
        y = ext[8:, :] * cw[CONV_K - 1:CONV_K, :]
        for d in range(1, CONV_K):
            y = y + pltpu.roll(ext, d, 0)[8:, :] * cw[CONV_K - 1 - d:CONV_K - d, :]
        return _silu(y)

    q = [conv_silu(q_ref, qh_ref, cq_ref, ln) for ln in lanes]
    k = [conv_silu(k_ref, kh_ref, ck_ref, ln) for ln in lanes]
    v = [conv_silu(v_ref, vh_ref, cv_ref, ln) for ln in lanes]
    qn = [x * lax.rsqrt(jnp.sum(x * x, axis=-1, keepdims=True) + EPS) * (B_HEAD_DIM ** -0.5) for x in q]
    kn = [x * lax.rsqrt(jnp.sum(x * x, axis=-1, keepdims=True) + EPS) for x in k]

    misc = misc_ref[...]
    sel_r = lax.broadcasted_iota(I32, (128, 256), 0)
    sel_c = lax.broadcasted_iota(I32, (128, 256), 1)
    beta, g = [], []
    for hh in heads:
        hd = hp * GDN_PREP_HEADS + hh
        pick = jnp.where(sel_r == jnp.where(sel_c < 128, MISC_BA + hd, MISC_BB + hd), 1.0, 0.0)
        bab = _dot_xsel(misc, pick)
        beta.append(_sigmoid(bab[:, 128:256]))
        xg = bab[:, 0:128] + dtb_ref[hd]
        softplus = jnp.maximum(xg, 0.0) + jnp.log(1.0 + jnp.exp(-jnp.abs(xg)))
        g.append(-jnp.exp(jnp.full((1, 128), alog_ref[hd], F32)) * softplus)

    sl = 256
    r2 = lax.broadcasted_iota(I32, (sl, sl), 0)
    c2 = lax.broadcasted_iota(I32, (sl, sl), 1)
    same = (r2 // B_CHUNK) == (c2 // B_CHUNK)
    tri_blk = jnp.concatenate([jnp.where(same & (c2 <= r2), 1.0, 0.0), jnp.where(same, 1.0, 0.0)], axis=0)
    nsl = tm // sl
    gc, gl = [], []
    for hh in heads:
        both = _dot_sel(tri_blk, jnp.concatenate([g[hh][s * sl:(s + 1) * sl, :] for s in range(nsl)], axis=1))
        gc.append(jnp.concatenate([both[0:sl, s * 128:(s + 1) * 128] for s in range(nsl)], axis=0))
        gl.append(jnp.concatenate([both[sl:2 * sl, s * 128:(s + 1) * 128] for s in range(nsl)], axis=0))

    kb = [kn[hh] * beta[hh] for hh in heads]
    vb = [v[hh] * beta[hh] for hh in heads]
    egc = [jnp.exp(x) for x in gc]
    kbg = [kb[hh] * egc[hh] for hh in heads]
    for hh in heads:
        qd_ref[hh] = (qn[hh] * egc[hh]).astype(BF16)
        kd = kn[hh] * jnp.exp(gl[hh] - gc[hh])
        for m in range(tm // 128):
            kdt_ref[hh, :, m * 128:(m + 1) * 128] = kd[m * 128:(m + 1) * 128, :].T.astype(BF16)

    gw = GDN_GROUP * B_CHUNK
    ri = lax.broadcasted_iota(I32, (B_CHUNK, gw), 0)
    lj = lax.broadcasted_iota(I32, (B_CHUNK, gw), 1)
    lb = lj // B_CHUNK
    lj = lj % B_CHUNK
    bmask = (lax.broadcasted_iota(I32, (gw, gw), 0) // B_CHUNK
             == lax.broadcasted_iota(I32, (gw, gw), 1) // B_CHUNK)

    def fold(x):
        xm = jnp.where(bmask, x, 0.0)
        out = xm[0:B_CHUNK, :]
        for c in range(1, GDN_GROUP):
            out = out + xm[c * B_CHUNK:(c + 1) * B_CHUNK, :]
        return out

    def bdiag(cat):
        return jnp.where(bmask, jnp.concatenate([cat] * GDN_GROUP, axis=0), 0.0)

    probs = [(hh, grp) for grp in range(tm // gw) for hh in heads]
    a_cat = []
    for hh, grp in probs:
        rs = slice(grp * gw, (grp + 1) * gw)
        gcs = gc[hh][rs, :]
        col = jnp.concatenate([gcs[0:B_CHUNK, :]] * 2, axis=1)
        for c in range(1, GDN_GROUP):
            col = jnp.where(lb == c, jnp.concatenate([gcs[c * B_CHUNK:(c + 1) * B_CHUNK, :]] * 2, axis=1), col)
        rowf = jnp.concatenate([gcs[m * 128:(m + 1) * 128, :].T[0:B_CHUNK, :] for m in range(gw // 128)],
                               axis=1)
        decay = jnp.exp(jnp.where(ri >= lj, col - rowf, NEG_INF))
        kg = kn[hh][rs, :]
        a_cat.append(jnp.where(ri > lj, fold(_dot_nt(kb[hh][rs, :], kg)) * decay, 0.0))
        attn_ref[hh, grp] = jnp.where(ri >= lj, fold(_dot_nt(qn[hh][rs, :], kg)) * decay, 0.0).astype(BF16)
    eye = jnp.where(ri == lj, 1.0, 0.0)
    inv = [eye - a for a in a_cat]
    pw = [_dot_x3(a, bdiag(a)) for a in a_cat]
    for step in range(5):
        for p in range(len(probs)):
            if step < 4:
                both = _dot_x3(jnp.concatenate([inv[p], pw[p]], axis=0), bdiag(pw[p]))
                inv[p] = inv[p] + both[0:B_CHUNK, :]
                pw[p] = both[B_CHUNK:, :]
            else:
                inv[p] = inv[p] + _dot_x3(inv[p], bdiag(pw[p]))
    for p, (hh, grp) in enumerate(probs):
        rs = slice(grp * gw, (grp + 1) * gw)
        sol = _dot_x3(bdiag(inv[p]), jnp.concatenate([vb[hh][rs, :], kbg[hh][rs, :]], axis=1))
        u_ref[hh, rs, :] = sol[:, 0:B_HEAD_DIM]
        w_ref[hh, rs, :] = sol[:, B_HEAD_DIM:].astype(BF16)
    for hh in heads:
        for c in range(nchunk):
            egl_ref[hh, c:c + 1, :] = jnp.exp(gl[hh][c * B_CHUNK:c * B_CHUNK + 1, :])
        if nchunk < 8:
            egl_ref[hh, nchunk:, :] = jnp.zeros((8 - nchunk, 128), F32)


def _gdn_prep(h, bsz, seq, b_conv, b_a_log, b_dt_bias):
    tm = min(GDN_TM, seq)
    nt = seq // tm
    nrow = max(tm // B_CHUNK, 8)
    gw = GDN_GROUP * B_CHUNK
    hs = (bsz, B_HEADS, seq, B_HEAD_DIM)
    hg = GDN_PREP_HEADS
    hw = hg * 128
    cur = lambda off: (lambda b, i, hp: (b * nt + i, off // hw + hp))
    halo = lambda off: (lambda b, i, hp: (jnp.maximum((b * nt + i) * (tm // 8) - 1, 0), off // hw + hp))
    cw = lambda off: (lambda b, i, hp: (0, off // hw + hp))
    out = lambda b, i, hp: (b, hp, i, 0)
    smem = pl.BlockSpec(memory_space=pltpu.SMEM)
    return pl.pallas_call(
        _gdn_prep_kernel,
        out_shape=(jax.ShapeDtypeStruct(hs, F32), jax.ShapeDtypeStruct(hs, BF16),
                   jax.ShapeDtypeStruct(hs, BF16),
                   jax.ShapeDtypeStruct((bsz, B_HEADS, B_HEAD_DIM, seq), BF16),
                   jax.ShapeDtypeStruct((bsz, B_HEADS, seq // gw, B_CHUNK, gw), BF16),
                   jax.ShapeDtypeStruct((bsz, B_HEADS, nt * nrow, 128), F32)),
        grid=(bsz, nt, B_HEADS // hg),
        in_specs=[smem, smem,
                  pl.BlockSpec((tm, hw), cur(EV_BQ)), pl.BlockSpec((tm, hw), cur(EV_BK)),
                  pl.BlockSpec((tm, hw), cur(EV_BV)),
                  pl.BlockSpec((8, hw), halo(EV_BQ)), pl.BlockSpec((8, hw), halo(EV_BK)),
                  pl.BlockSpec((8, hw), halo(EV_BV)),
                  pl.BlockSpec((tm, 128), lambda b, i, hp: (b * nt + i, EV_MISC // 128)),
                  pl.BlockSpec((CONV_K, hw), cw(0)), pl.BlockSpec((CONV_K, hw), cw(1024)),
                  pl.BlockSpec((CONV_K, hw), cw(2048))],
        out_specs=(pl.BlockSpec((None, hg, tm, 128), out), pl.BlockSpec((None, hg, tm, 128), out),
                   pl.BlockSpec((None, hg, tm, 128), out),
                   pl.BlockSpec((None, hg, B_HEAD_DIM, tm), lambda b, i, hp: (b, hp, 0, i)),
                   pl.BlockSpec((None, hg, tm // gw, B_CHUNK, gw), lambda b, i, hp: (b, hp, i, 0, 0)),
                   pl.BlockSpec((None, hg, nrow, 128), out)),
        compiler_params=_cparams(("parallel", "parallel", "parallel")),
        name="gdn_prep",
    )(b_a_log, b_dt_bias, h, h, h, h, h, h, h, b_conv, b_conv, b_conv)


def _gdn_scan_kernel(u_ref, w_ref, qd_ref, kdt_ref, attn_ref, egl_ref, z_ref, gn_ref, o_ref, s_ref):
    @pl.when(pl.program_id(2) == 0)
    def _():
        s_ref[...] = jnp.zeros(s_ref.shape, F32)

    tm = u_ref.shape[1]
    gw = GDN_GROUP * B_CHUNK
    gain = gn_ref[...]
    zero = jnp.zeros((B_CHUNK, B_HEAD_DIM), BF16)
    states = [s_ref[hh] for hh in range(GDN_SCAN_HEADS)]
    for c in range(tm // B_CHUNK):
        grp, ci = divmod(c, GDN_GROUP)
        rs = slice(c * B_CHUNK, (c + 1) * B_CHUNK)
        for hh in range(GDN_SCAN_HEADS):
            sb = states[hh].astype(BF16)
            v_new = u_ref[hh, rs, :] - _dot(w_ref[hh, rs, :], sb)
            vpad = jnp.concatenate([zero] * ci + [v_new.astype(BF16)] + [zero] * (GDN_GROUP - 1 - ci), axis=0)
            o = _dot(qd_ref[hh, rs, :], sb) + _dot(attn_ref[hh, grp], vpad)
            states[hh] = (states[hh] * egl_ref[hh, c:c + 1, :]
                          + _dot(kdt_ref[hh, :, grp * gw:(grp + 1) * gw], vpad))
            on = o * lax.rsqrt(jnp.mean(o * o, axis=-1, keepdims=True) + EPS) * gain
            lanes = slice(hh * B_HEAD_DIM, (hh + 1) * B_HEAD_DIM)
            o_ref[rs, lanes] = (on * _silu(z_ref[rs, lanes])).astype(BF16)
    for hh in range(GDN_SCAN_HEADS):
        s_ref[hh] = states[hh]


def _gdn_scan(h, bsz, seq, u, w, qd, kd, attn, egl, b_out_norm):
    tm = min(GDN_TM, seq)
    nt = seq // tm
    nrow = max(tm // B_CHUNK, 8)
    gw = GDN_GROUP * B_CHUNK
    hg = GDN_SCAN_HEADS
    blk = lambda b, hp, i: (b, hp, i, 0)
    return pl.pallas_call(
        _gdn_scan_kernel,
        out_shape=jax.ShapeDtypeStruct((bsz * seq, B_HEADS * B_HEAD_DIM), BF16),
        grid=(bsz, B_HEADS // hg, nt),
        in_specs=[pl.BlockSpec((None, hg, tm, 128), blk), pl.BlockSpec((None, hg, tm, 128), blk),
                  pl.BlockSpec((None, hg, tm, 128), blk),
                  pl.BlockSpec((None, hg, B_HEAD_DIM, tm), lambda b, hp, i: (b, hp, 0, i)),
                  pl.BlockSpec((None, hg, tm // gw, B_CHUNK, gw), lambda b, hp, i: (b, hp, i, 0, 0)),
                  pl.BlockSpec((None, hg, nrow, 128), blk),
                  pl.BlockSpec((tm, hg * 128), lambda b, hp, i: (b * nt + i, EV_BZ // (hg * 128) + hp)),
                  pl.BlockSpec((1, 128), lambda b, hp, i: (0, 0))],
        out_specs=pl.BlockSpec((tm, hg * 128), lambda b, hp, i: (b * nt + i, hp)),
        scratch_shapes=[pltpu.VMEM((hg, B_HEAD_DIM, B_HEAD_DIM), F32)],
        compiler_params=_cparams(("parallel", "parallel", "arbitrary")),
        name="gdn_scan",
    )(u, w, qd, kd, attn, egl, h, b_out_norm.reshape(1, -1))


def _rope_kernel(inv_ref, cos_ref, sin_ref):
    tm = cos_ref.shape[0]
    pos = (pl.program_id(0) * tm + lax.broadcasted_iota(I32, (tm, 128), 0)).astype(F32)
    ang = pos * inv_ref[...]
    lane = lax.broadcasted_iota(I32, (tm, 128), 1)
    cos_ref[...] = jnp.cos(ang)
    sin_ref[...] = jnp.where(lane < 64, -jnp.sin(ang), jnp.sin(ang))


def _rope_tables(seq):
    half = C_QK_DIM // 2
    inv = 1.0 / (ROPE_BASE ** jnp.linspace(0.0, 1.0, half, dtype=F32))
    inv2 = jnp.concatenate([inv, inv]).reshape(1, C_QK_DIM)
    tm = min(1024, seq)
    return pl.pallas_call(
        _rope_kernel,
        out_shape=(jax.ShapeDtypeStruct((seq, C_QK_DIM), F32),) * 2,
        grid=(seq // tm,),
        in_specs=[pl.BlockSpec((1, C_QK_DIM), lambda i: (0, 0))],
        out_specs=(pl.BlockSpec((tm, C_QK_DIM), lambda i: (i, 0)),) * 2,
        compiler_params=_cparams(("parallel",)),
        name="rope_tables",
    )(inv2)


RET_TM = 512


def _retention_kernel(lg_ref, q_ref, k_ref, v_ref, z_ref, cos_ref, sin_ref, gn_ref, o_ref,
                      r_ref, dm_ref, zeta_ref, xi_ref):
    hd = pl.program_id(1)
    lg = lg_ref[0, hd]

    @pl.when(pl.program_id(2) == 0)
    def _():
        r_ref[...] = jnp.zeros(r_ref.shape, F32)
        ri = lax.broadcasted_iota(I32, (C_CHUNK, C_CHUNK), 0)
        ci = lax.broadcasted_iota(I32, (C_CHUNK, C_CHUNK), 1)
        diff = (ri - ci).astype(F32)
        dm_ref[...] = jnp.where(diff >= 0, jnp.exp(jnp.maximum(diff, 0.0) * lg), 0.0)
        zeta_ref[...] = jnp.exp((C_CHUNK - 1 - ri).astype(F32) * lg)
        rv = lax.broadcasted_iota(I32, (C_CHUNK, C_V_DIM), 0).astype(F32)
        xi_ref[...] = jnp.exp((rv + 1.0) * lg)

    g_chunk = lg_ref[1, hd]
    tm = q_ref.shape[0]
    state = r_ref[...]
    dmask = dm_ref[...]
    gain = gn_ref[...]
    for c in range(tm // C_CHUNK):
        rs = slice(c * C_CHUNK, (c + 1) * C_CHUNK)
        cos = cos_ref[rs, :]
        sin = sin_ref[rs, :]
        q = q_ref[rs, :]
        k = k_ref[rs, :]
        qr = q * cos + pltpu.roll(q, C_QK_DIM // 2, 1) * sin
        kr = (k * cos + pltpu.roll(k, C_QK_DIM // 2, 1) * sin) * (C_QK_DIM ** -0.5)
        vb = v_ref[rs, :].astype(BF16)
        qb = qr.astype(BF16)
        s = _dot_nt(qb, kr.astype(BF16)) * dmask
        o = _dot(s.astype(BF16), vb) + _dot(qb, state.astype(BF16)) * xi_ref[...]
        state = state * g_chunk + _dot((kr * zeta_ref[...]).T.astype(BF16), vb)
        mu = jnp.mean(o, axis=-1, keepdims=True)
        oc = o - mu
        var = jnp.mean(oc * oc, axis=-1, keepdims=True)
        y = oc * lax.rsqrt(var + EPS) * gain
        o_ref[rs, :] = (y * _silu(z_ref[rs, :])).astype(BF16)
    r_ref[...] = state


def _retention(h, bsz, seq, cos2, sin2, c_out_norm):
    tm = min(RET_TM, seq)
    nt = seq // tm
    lg = np.log1p(-np.exp2(-5.0 - np.arange(C_HEADS, dtype=np.float32))).astype(np.float32)
    lg = np.stack([lg, np.exp(np.float32(C_CHUNK) * lg).astype(np.float32)])
    return pl.pallas_call(
        _retention_kernel,
        out_shape=jax.ShapeDtypeStruct((bsz * seq, C_HEADS * C_V_DIM), BF16),
        grid=(bsz, C_HEADS, nt),
        in_specs=[pl.BlockSpec(memory_space=pltpu.SMEM),
                  pl.BlockSpec((tm, 128), lambda b, hd, i: (b * nt + i, OD_CQ // 128 + hd)),
                  pl.BlockSpec((tm, 128), lambda b, hd, i: (b * nt + i, OD_CK // 128 + hd)),
                  pl.BlockSpec((tm, 256), lambda b, hd, i: (b * nt + i, OD_CV // 256 + hd)),
                  pl.BlockSpec((tm, 256), lambda b, hd, i: (b * nt + i, OD_CZ // 256 + hd)),
                  pl.BlockSpec((tm, 128), lambda b, hd, i: (i, 0)),
                  pl.BlockSpec((tm, 128), lambda b, hd, i: (i, 0)),
                  pl.BlockSpec((1, C_V_DIM), lambda b, hd, i: (0, hd))],
        out_specs=pl.BlockSpec((tm, C_V_DIM), lambda b, hd, i: (b * nt + i, hd)),
        scratch_shapes=[pltpu.VMEM((C_QK_DIM, C_V_DIM), F32),
                        pltpu.VMEM((C_CHUNK, C_CHUNK), F32),
                        pltpu.VMEM((C_CHUNK, C_QK_DIM), F32),
                        pltpu.VMEM((C_CHUNK, C_V_DIM), F32)],
        compiler_params=_cparams(("parallel", "parallel", "arbitrary")),
        name="retention",
    )(jnp.asarray(lg), h, h, h, h, cos2, sin2, c_out_norm.reshape(1, -1))


def _s5_param_kernel(lre_ref, lim_ref, ldt_ref, lrex_ref, limx_ref, ldtx_ref, bre_ref, bim_ref,
                     are_ref, aim_ref, bbre_ref, bbim_ref):
    def disc(lre, lim, ldt):
        lr = jnp.minimum(lre, -1e-4)
        dt = jnp.exp(ldt)
        mag = jnp.exp(lr * dt)
        return lr, lim, mag * jnp.cos(lim * dt), mag * jnp.sin(lim * dt)

    _, _, a_re, a_im = disc(lre_ref[...], lim_ref[...], ldt_ref[...])
    are_ref[...] = a_re
    aim_ref[...] = a_im
    lr, li, ax_re, ax_im = disc(lrex_ref[...], limx_ref[...], ldtx_ref[...])
    den = lr * lr + li * li
    f_re = ((ax_re - 1.0) * lr + ax_im * li) / den
    f_im = (ax_im * lr - (ax_re - 1.0) * li) / den
    bbre_ref[...] = f_re * bre_ref[...] - f_im * bim_ref[...]
    bbim_ref[...] = f_re * bim_ref[...] + f_im * bre_ref[...]


def _s5_params(lam_re, lam_im, log_dt, b_re, b_im, c_re, c_im):
    g, p, ch = D_GROUPS, D_STATE, D_GROUP
    ldt = jnp.broadcast_to(log_dt[:, None], (g, p))
    rep = lambda a: jnp.repeat(a, ch, axis=1)
    vm = pl.BlockSpec(memory_space=pltpu.VMEM)
    a_re, a_im, bb_re, bb_im = pl.pallas_call(
        _s5_param_kernel,
        out_shape=(jax.ShapeDtypeStruct((g, p), F32), jax.ShapeDtypeStruct((g, p), F32),
                   jax.ShapeDtypeStruct((g, p * ch), F32), jax.ShapeDtypeStruct((g, p * ch), F32)),
        in_specs=[vm] * 8, out_specs=(vm,) * 4,
        name="s5_params",
    )(lam_re, lam_im, ldt, rep(lam_re), rep(lam_im), rep(ldt),
      b_re.reshape(g, p * ch), b_im.reshape(g, p * ch))
    eye = jnp.eye(D_SETS * 4, dtype=F32)

    def pack_b(bb):
        bb = bb.reshape(D_SETS, 16, p, ch)
        return jnp.einsum('sgpi,gh->sgihp', bb, eye).reshape(D_SETS, D_SET_CH, D_SET_ST)

    def pack_c(c):
        c = c.reshape(D_SETS, 16, ch, p)
        return jnp.einsum('sgjp,gh->sgphj', c, eye).reshape(D_SETS, D_SET_ST, D_SET_CH)

    bd = jnp.concatenate([pack_b(bb_re), pack_b(bb_im)], axis=-1).astype(BF16)
    slab = (D_SETS, S5_SLAB, 128)
    return (a_re.reshape(slab), a_im.reshape(slab), bd,
            pack_c(c_re).astype(BF16), pack_c(c_im).astype(BF16))


S5_TM = 256
S5_SLAB = D_SET_ST // 128


def _s5_kernel(u_ref, z_ref, are_ref, aim_ref, bd_ref, cre_ref, cim_ref, dskip_ref, wglu_ref, bglu_ref,
               o_ref, hre_ref, him_ref, y_ref, *x_refs):
    @pl.when(pl.program_id(1) == 0)
    def _():
        hre_ref[...] = jnp.zeros(hre_ref.shape, F32)
        him_ref[...] = jnp.zeros(him_ref.shape, F32)

    xre, xim = x_refs[:D_SETS], x_refs[D_SETS:]
    tm = u_ref.shape[0]
    u = u_ref[...]
    ub = u.astype(BF16)
    for s in range(D_SETS):
        bu = _dot(ub[:, s * D_SET_CH:(s + 1) * D_SET_CH], bd_ref[s])
        for k in range(S5_SLAB):
            xre[s][pl.ds(k, tm, stride=S5_SLAB), :] = bu[:, k * 128:(k + 1) * 128]
            xim[s][pl.ds(k, tm, stride=S5_SLAB), :] = bu[:, D_SET_ST + k * 128:D_SET_ST + (k + 1) * 128]
    a_re = [are_ref[s] for s in range(D_SETS)]
    a_im = [aim_ref[s] for s in range(D_SETS)]

    def step(t, carry):
        rows = pl.ds(pl.multiple_of(t * S5_SLAB, S5_SLAB), S5_SLAB)
        out = []
        for s in range(D_SETS):
            hr, hi = carry[2 * s], carry[2 * s + 1]
            xr = xre[s][rows, :] + a_re[s] * hr - a_im[s] * hi
            xi = xim[s][rows, :] + a_re[s] * hi + a_im[s] * hr
            xre[s][rows, :] = xr
            xim[s][rows, :] = xi
            out += [xr, xi]
        return tuple(out)

    carry = []
    for s in range(D_SETS):
        carry += [hre_ref[s], him_ref[s]]
    carry = lax.fori_loop(0, tm, step, tuple(carry), unroll=8)
    for s in range(D_SETS):
        hre_ref[s] = carry[2 * s]
        him_ref[s] = carry[2 * s + 1]
        x_r = jnp.concatenate([xre[s][pl.ds(k, tm, stride=S5_SLAB), :] for k in range(S5_SLAB)], axis=1)
        x_i = jnp.concatenate([xim[s][pl.ds(k, tm, stride=S5_SLAB), :] for k in range(S5_SLAB)], axis=1)
        y_ref[:, s * D_SET_CH:(s + 1) * D_SET_CH] = (
            _dot(x_r.astype(BF16), cre_ref[s]) - _dot(x_i.astype(BF16), cim_ref[s]))
    y = y_ref[...] + dskip_ref[...] * u
    y = 0.5 * y * (1.0 + jnp.tanh(math.sqrt(2.0 / math.pi) * (y + 0.044715 * (y * y * y))))
    gate = _sigmoid(_dot(y.astype(BF16), wglu_ref[...]) + bglu_ref[...])
    o_ref[...] = (y * gate * _silu(z_ref[...])).astype(BF16)


def _s5(h, bsz, seq, a_re, a_im, bd, cd_re, cd_im, d_skip, w_glu, b_glu):
    tm = min(S5_TM, seq)
    nt = seq // tm
    full = lambda *shape: pl.BlockSpec(shape, lambda b, i: (0,) * len(shape))
    width = D_GROUPS * D_GROUP
    return pl.pallas_call(
        _s5_kernel,
        out_shape=jax.ShapeDtypeStruct((bsz * seq, width), BF16),
        grid=(bsz, nt),
        in_specs=[pl.BlockSpec((tm, width), lambda b, i: (b * nt + i, OD_DU // width)),
                  pl.BlockSpec((tm, width), lambda b, i: (b * nt + i, OD_DZ // width)),
                  full(D_SETS, S5_SLAB, 128), full(D_SETS, S5_SLAB, 128),
                  full(D_SETS, D_SET_CH, 2 * D_SET_ST),
                  full(D_SETS, D_SET_ST, D_SET_CH), full(D_SETS, D_SET_ST, D_SET_CH),
                  full(1, width), full(width, width), full(1, width)],
        out_specs=pl.BlockSpec((tm, width), lambda b, i: (b * nt + i, 0)),
        scratch_shapes=([pltpu.VMEM((D_SETS, S5_SLAB, 128), F32), pltpu.VMEM((D_SETS, S5_SLAB, 128), F32),
                         pltpu.VMEM((tm, width), F32)]
                        + [pltpu.VMEM((tm * S5_SLAB, 128), F32)] * (2 * D_SETS)),
        compiler_params=_cparams(("parallel", "arbitrary")),
        name="s5",
    )(h, h, a_re, a_im, bd, cd_re, cd_im, d_skip.reshape(1, -1), w_glu.astype(BF16),
      b_glu.reshape(1, -1))


def _pack_even_w(w_in):
    sizes = (1024, A_KV_RANK, A_IDX_HEADS * A_IDX_DIM, A_IDX_DIM, A_IDX_HEADS, 1024,
             3 * 1024, B_HEADS, B_HEADS, 1024)
    parts, start = [], 0
    for s in sizes:
        parts.append(w_in[:, start:start + s])
        start += s
    aq, ckv, qi, ki, wi, az, bqkv, ba, bb, bz = parts
    pad = jnp.zeros((w_in.shape[0], EV_WIDTH - start), w_in.dtype)
    return jnp.concatenate([aq, az, bqkv, bz, qi, ckv, ki, wi, ba, bb, pad], axis=1).astype(BF16)


def _even_layer(x2d, bsz, seq, bias, norm_g, w_in, a_q_norm, a_kv_norm, w_kv_up, a_k_norm,
                b_conv, b_a_log, b_dt_bias, b_out_norm, w_out):
    h = _norm_proj(x2d, norm_g, _pack_even_w(w_in), tn=768)
    qta, qit, wt, kidx, katt, vt = _dsa_prep(h, bsz * seq // QB, a_q_norm, a_kv_norm, w_kv_up, a_k_norm)
    mix_a = _dsa_attend(h, bsz, seq, qta, qit, wt, kidx, katt, vt, bias)
    u, w, qd, kd, attn, egl = _gdn_prep(h, bsz, seq, b_conv, b_a_log, b_dt_bias)
    mix_b = _gdn_scan(h, bsz, seq, u, w, qd, kd, attn, egl, b_out_norm)
    return _out_proj(x2d, mix_a, mix_b, w_out)


def _odd_layer(x2d, bsz, seq, cos2, sin2, norm_g, w_in, c_out_norm, lam_re, lam_im, log_dt,
               b_re, b_im, c_re, c_im, d_skip, w_glu, b_glu, w_out):
    h = _norm_proj(x2d, norm_g, w_in.astype(BF16), tn=1024)
    mix_c = _retention(h, bsz, seq, cos2, sin2, c_out_norm)
    s5p = _s5_params(lam_re, lam_im, log_dt, b_re, b_im, c_re, c_im)
    mix_d = _s5(h, bsz, seq, *s5p, d_skip, w_glu, b_glu)
    return _out_proj(x2d, mix_c, mix_d, w_out)


def kernel(x, rel_bias, ev_norm, ev_w_in, ev_a_q_norm, ev_a_kv_norm, ev_w_kv_up, ev_a_k_norm,
           ev_b_conv, ev_b_a_log, ev_b_dt_bias, ev_b_out_norm, ev_w_out,
           od_norm, od_w_in, od_c_out_norm, od_lam_re, od_lam_im, od_log_dt,
           od_b_re, od_b_im, od_c_re, od_c_im, od_d_skip, od_w_glu, od_b_glu, od_w_out):
    bsz, seq, d = x.shape
    depth = ev_norm.shape[0] + od_norm.shape[0]
    x2d = x.reshape(bsz * seq, d)
    bias = _bias_table(rel_bias)
    cos2, sin2 = _rope_tables(seq)
    for layer in range(depth):
        i = layer // 2
        if layer % 2 == 0:
            x2d = _even_layer(x2d, bsz, seq, bias, ev_norm[i], ev_w_in[i], ev_a_q_norm[i],
                              ev_a_kv_norm[i], ev_w_kv_up[i], ev_a_k_norm[i], ev_b_conv[i],
                              ev_b_a_log[i], ev_b_dt_bias[i], ev_b_out_norm[i], ev_w_out[i])
        else:
            x2d = _odd_layer(x2d, bsz, seq, cos2, sin2, od_norm[i], od_w_in[i], od_c_out_norm[i],
                             od_lam_re[i], od_lam_im[i], od_log_dt[i], od_b_re[i], od_b_im[i],
                             od_c_re[i], od_c_im[i], od_d_skip[i], od_w_glu[i], od_b_glu[i], od_w_out[i])
    return x2d.reshape(bsz, seq, d)
```

```python
import functools
import math

import numpy as np
import jax
import jax.numpy as jnp
from jax import lax
from jax.experimental import pallas as pl
from jax.experimental.pallas import tpu as pltpu

F32 = jnp.float32
BF16 = jnp.bfloat16
I32 = jnp.int32

D_MODEL = 1024
EPS = 1e-6
LOG2E = 1.4426950408889634
NEG_INF = float("-inf")
INT_MIN = -(2 ** 31)

A_HEADS, A_HEAD_DIM, A_KV_RANK = 16, 64, 128
A_IDX_HEADS, A_IDX_DIM, A_TOPK_MAX = 8, 64, 256
QB = 128
REL_BUCKETS, REL_MAX_DIST = 32, 128
VT_ROWS = 80
SB = 2 * QB
FAR_BLOCKS = 4
SCORE_ROWS = 4 * QB
BIAS_ROWS = 5 * QB
MASKED = -(2.0 ** 100)
M_INIT = -(2.0 ** 60)
TIE_WALK_MAX = 8
SCORE_UNROLL = 4
COUNT_UNROLL = 4
B_HEADS, B_HEAD_DIM, CONV_K, B_CHUNK = 8, 128, 4, 64
C_HEADS, C_QK_DIM, C_V_DIM, C_CHUNK = 4, 128, 256, 128
ROPE_BASE = 10000.0
D_GROUP, D_STATE, D_GROUPS = 16, 64, 64
D_SETS, D_SET_CH, D_SET_ST = 4, 256, 1024

EV_AQ, EV_AZ, EV_BQ, EV_BK, EV_BV, EV_BZ, EV_QI, EV_CKV, EV_MISC = (
    0, 1024, 2048, 3072, 4096, 5120, 6144, 6656, 6784)
EV_WIDTH = 6912
MISC_KI, MISC_WI, MISC_BA, MISC_BB = 0, 64, 72, 80
OD_CQ, OD_CK, OD_CV, OD_CZ, OD_DU, OD_DZ = 0, 512, 1024, 2048, 3072, 4096
OD_WIDTH = 5120

VMEM_LIMIT = 48 * 1024 * 1024


def _cparams(sem):
    return pltpu.CompilerParams(dimension_semantics=sem, vmem_limit_bytes=VMEM_LIMIT)


def _dot(a, b):
    return jnp.dot(a, b, preferred_element_type=F32)


def _dot_nt(a, b):
    return lax.dot_general(a, b, (((1,), (1,)), ((), ())), preferred_element_type=F32)


def _split_bf16(x, n):
    parts = []
    for _ in range(n):
        p = x.astype(BF16)
        parts.append(p)
        x = x - p.astype(F32)
    return parts


def _dot_sel(sel, x):
    sel = sel.astype(BF16)
    hi, mid, lo = _split_bf16(x, 3)
    return _dot(sel, hi) + (_dot(sel, mid) + _dot(sel, lo))


def _dot_xsel(x, sel):
    sel = sel.astype(BF16)
    hi, mid, lo = _split_bf16(x, 3)
    return _dot(hi, sel) + (_dot(mid, sel) + _dot(lo, sel))


def _dot_x3(a, b):
    ah, al = _split_bf16(a, 2)
    bh, bl = _split_bf16(b, 2)
    return _dot(ah, bh) + (_dot(ah, bl) + _dot(al, bh))


def _sigmoid(x):
    return 1.0 / (1.0 + jnp.exp(-x))


def _silu(x):
    return x * _sigmoid(x)


def _norm_proj_kernel(x_ref, g_ref, w_ref, o_ref, xn_ref):
    @pl.when(pl.program_id(1) == 0)
    def _():
        x = x_ref[...]
        ms = jnp.mean(x * x, axis=-1, keepdims=True)
        xn_ref[...] = (x * lax.rsqrt(ms + EPS) * g_ref[...]).astype(BF16)

    o_ref[...] = _dot(xn_ref[...], w_ref[...])


def _norm_proj(x2d, gain, w_bf16, tn, tm=2048):
    t, d = x2d.shape
    n = w_bf16.shape[1]
    tm = min(tm, t)
    return pl.pallas_call(
        _norm_proj_kernel,
        out_shape=jax.ShapeDtypeStruct((t, n), F32),
        grid=(t // tm, n // tn),
        in_specs=[pl.BlockSpec((tm, d), lambda i, j: (i, 0)),
                  pl.BlockSpec((1, d), lambda i, j: (0, 0)),
                  pl.BlockSpec((d, tn), lambda i, j: (0, j))],
        out_specs=pl.BlockSpec((tm, tn), lambda i, j: (i, j)),
        scratch_shapes=[pltpu.VMEM((tm, d), BF16)],
        compiler_params=_cparams(("parallel", "arbitrary")),
        name="norm_proj",
    )(x2d, gain.reshape(1, d), w_bf16)


def _out_proj_kernel(x_ref, a_ref, b_ref, wa_ref, wb_ref, o_ref):
    o_ref[...] = x_ref[...] + _dot(a_ref[...], wa_ref[...]) + _dot(b_ref[...], wb_ref[...])


def _out_proj(x2d, mix_a, mix_b, w_out, tm=512):
    t, d = x2d.shape
    half = mix_a.shape[1]
    tm = min(tm, t)
    wa = w_out[:half].astype(BF16)
    wb = w_out[half:].astype(BF16)
    return pl.pallas_call(
        _out_proj_kernel,
        out_shape=jax.ShapeDtypeStruct((t, d), F32),
        grid=(t // tm,),
        in_specs=[pl.BlockSpec((tm, d), lambda i: (i, 0)),
                  pl.BlockSpec((tm, half), lambda i: (i, 0)),
                  pl.BlockSpec((tm, half), lambda i: (i, 0)),
                  pl.BlockSpec((half, d), lambda i: (0, 0)),
                  pl.BlockSpec((half, d), lambda i: (0, 0))],
        out_specs=pl.BlockSpec((tm, d), lambda i: (i, 0)),
        compiler_params=_cparams(("parallel",)),
        name="out_proj",
    )(x2d, mix_a, mix_b, wa, wb)


def _t5_bucket_starts():
    exact = REL_BUCKETS // 2
    n = np.arange(0, 4 * REL_MAX_DIST, dtype=np.int64)
    ratio = np.maximum(n, 1).astype(np.float32) / np.float32(exact)
    large = exact + (np.log(ratio).astype(np.float32) / np.float32(math.log(REL_MAX_DIST / exact))
                     * np.float32(REL_BUCKETS - exact)).astype(np.int32)
    bucket = np.where(n < exact, n, np.minimum(large, REL_BUCKETS - 1))
    starts = [int(np.argmax(bucket >= b)) for b in range(REL_BUCKETS)]
    assert all(bucket[s] == b for b, s in enumerate(starts)) and np.all(np.diff(bucket) >= 0)
    assert starts[-1] <= QB, "distances beyond one key block must share the last bucket"
    return starts


_BUCKET_STARTS = _t5_bucket_starts()


def _bias_table_kernel(rb_ref, o_ref):
    row = lax.broadcasted_iota(I32, (BIAS_ROWS, QB), 0)
    lane = lax.broadcasted_iota(I32, (BIAS_ROWS, QB), 1)
    dist = lane + 3 * QB - row
    for h in range(A_HEADS):
        val = jnp.full((BIAS_ROWS, QB), rb_ref[0, h], F32)
        for b in range(1, REL_BUCKETS):
            val = jnp.where(dist >= _BUCKET_STARTS[b], rb_ref[b, h], val)
        val = (val - rb_ref[REL_BUCKETS - 1, h]) * LOG2E
        o_ref[:, h * QB:(h + 1) * QB] = jnp.where(dist >= 0, val, 0.0)


def _bias_table(rel_bias):
    return pl.pallas_call(
        _bias_table_kernel,
        out_shape=jax.ShapeDtypeStruct((BIAS_ROWS, A_HEADS * QB), F32),
        in_specs=[pl.BlockSpec(memory_space=pltpu.SMEM)],
        out_specs=pl.BlockSpec(memory_space=pltpu.VMEM),
        name="dsa_bias_table",
    )(rel_bias)


def _dsa_prep_kernel(aq_ref, qi_ref, ckv_ref, misc_ref, gq_ref, gkv_ref, wkv_ref, gk_ref,
                     qta_ref, qit_ref, wt_ref, kidx_ref, katt_ref, vt_ref):
    zeros = jnp.zeros((QB, A_HEAD_DIM), F32)
    aq = aq_ref[...]
    gq = gq_ref[...]
    ident = jnp.where(lax.broadcasted_iota(I32, (QB, QB), 0) == lax.broadcasted_iota(I32, (QB, QB), 1),
                      1.0, 0.0).astype(BF16)
    for h in range(A_HEADS):
        q = aq[:, h * A_HEAD_DIM:(h + 1) * A_HEAD_DIM]
        ms = jnp.mean(q * q, axis=-1, keepdims=True)
        qn = q * lax.rsqrt(ms + EPS) * gq * (A_HEAD_DIM ** -0.5 * LOG2E)
        qta_ref[0:QB, h * QB:(h + 1) * QB] = jnp.concatenate([qn, zeros], axis=1).T.astype(BF16)
        qta_ref[QB:2 * QB, h * QB:(h + 1) * QB] = ident
    qi = qi_ref[...]
    for h in range(A_IDX_HEADS):
        qih = qi[:, h * A_IDX_DIM:(h + 1) * A_IDX_DIM]
        qit_ref[:, h * QB:(h + 1) * QB] = jnp.concatenate([qih, zeros], axis=1).T.astype(BF16)
    misc = misc_ref[...]
    wt_ref[...] = misc.T[MISC_WI:MISC_WI + A_IDX_HEADS, :] * (A_IDX_HEADS ** -0.5 * A_IDX_DIM ** -0.5)
    c = ckv_ref[...]
    cn = c * lax.rsqrt(jnp.mean(c * c, axis=-1, keepdims=True) + EPS) * gkv_ref[...]
    kv = _dot(cn.astype(BF16), wkv_ref[...])
    k = kv[:, :A_HEAD_DIM]
    kn = k * lax.rsqrt(jnp.mean(k * k, axis=-1, keepdims=True) + EPS) * gk_ref[...]
    kidx_ref[...] = jnp.concatenate([misc[:, MISC_KI:MISC_KI + A_IDX_DIM], zeros], axis=1).astype(BF16)
    katt_ref[...] = jnp.concatenate([kn, zeros], axis=1).astype(BF16)
    kvt = kv.T
    vt_ref[0:A_HEAD_DIM, :] = kvt[A_HEAD_DIM:, :].astype(BF16)
    ones_row = lax.broadcasted_iota(I32, (VT_ROWS - A_HEAD_DIM, QB), 0) == 0
    vt_ref[A_HEAD_DIM:, :] = jnp.where(ones_row, 1.0, 0.0).astype(BF16)


def _dsa_prep(h, nblk, a_q_norm, a_kv_norm, w_kv_up, a_k_norm):
    cb = lambda width, off: off // width
    return pl.pallas_call(
        _dsa_prep_kernel,
        out_shape=(jax.ShapeDtypeStruct((nblk, 2 * QB, A_HEADS * QB), BF16),
                   jax.ShapeDtypeStruct((nblk, QB, A_IDX_HEADS * QB), BF16),
                   jax.ShapeDtypeStruct((nblk, A_IDX_HEADS, QB), F32),
                   jax.ShapeDtypeStruct((nblk, QB, 128), BF16),
                   jax.ShapeDtypeStruct((nblk, QB, 128), BF16),
                   jax.ShapeDtypeStruct((nblk, VT_ROWS, QB), BF16)),
        grid=(nblk,),
        in_specs=[pl.BlockSpec((QB, 1024), lambda i: (i, cb(1024, EV_AQ))),
                  pl.BlockSpec((QB, 512), lambda i: (i, cb(512, EV_QI))),
                  pl.BlockSpec((QB, 128), lambda i: (i, cb(128, EV_CKV))),
                  pl.BlockSpec((QB, 128), lambda i: (i, cb(128, EV_MISC))),
                  pl.BlockSpec((1, A_HEAD_DIM), lambda i: (0, 0)),
                  pl.BlockSpec((1, A_KV_RANK), lambda i: (0, 0)),
                  pl.BlockSpec((A_KV_RANK, 2 * A_HEAD_DIM), lambda i: (0, 0)),
                  pl.BlockSpec((1, A_HEAD_DIM), lambda i: (0, 0))],
        out_specs=(pl.BlockSpec((None, 2 * QB, A_HEADS * QB), lambda i: (i, 0, 0)),
                   pl.BlockSpec((None, QB, A_IDX_HEADS * QB), lambda i: (i, 0, 0)),
                   pl.BlockSpec((None, A_IDX_HEADS, QB), lambda i: (i, 0, 0)),
                   pl.BlockSpec((None, QB, 128), lambda i: (i, 0, 0)),
                   pl.BlockSpec((None, QB, 128), lambda i: (i, 0, 0)),
                   pl.BlockSpec((None, VT_ROWS, QB), lambda i: (i, 0, 0))),
        compiler_params=_cparams(("parallel",)),
        name="dsa_prep",
    )(h, h, h, h, a_q_norm.reshape(1, -1), a_kv_norm.reshape(1, -1), w_kv_up.astype(BF16),
      a_k_norm.reshape(1, -1))


def _dsa_kernel(topk, qit_ref, wt_ref, qta_ref, kidx_ref, katt_ref, vt_ref, bias_ref, az_ref, o_ref,
                strip_ref, x_ref, m_ref, acc_ref, lg_ref, mx_ref, p_ref):
    qt = pl.program_id(1)
    t0 = qt * QB
    n_sc = qt // 4 + 1
    n_sb = qt // 2 + 1
    lane_s = lax.broadcasted_iota(I32, (SCORE_ROWS, QB), 1)
    row_s = lax.broadcasted_iota(I32, (SCORE_ROWS, QB), 0)

    w = wt_ref[...]

    def score_body(j, carry):
        kblk = kidx_ref[pl.ds(j * 4, 4)].reshape(SCORE_ROWS, 128)
        tot = None
        for hp in range(A_IDX_HEADS // 2):
            s = _dot(kblk, qit_ref[:, hp * SB:(hp + 1) * SB])
            for e in range(2):
                h = 2 * hp + e
                term = jnp.maximum(s[:, e * QB:(e + 1) * QB], 0.0) * w[h:h + 1, :]
                tot = term if tot is None else tot + term
        tot = jnp.where(j * SCORE_ROWS + row_s <= t0 + lane_s, tot, NEG_INF)
        bits = pltpu.bitcast(tot, I32)
        strip_ref[pl.ds(pl.multiple_of(j * SCORE_ROWS, SCORE_ROWS), SCORE_ROWS), :] = (
            bits ^ ((bits >> 31) & 0x7FFFFFFF))
        return carry

    def score_group(jj, carry):
        for sub in range(SCORE_UNROLL):
            score_body(SCORE_UNROLL * jj + sub, carry)
        return carry

    lax.fori_loop(0, n_sc // SCORE_UNROLL, score_group, 0)
    lax.fori_loop(SCORE_UNROLL * (n_sc // SCORE_UNROLL), n_sc, score_body, 0)

    def count(pred):
        def one(j, acc):
            r0 = pl.multiple_of(j * SCORE_ROWS, SCORE_ROWS)
            m = jnp.where(pred(strip_ref[pl.ds(r0, SCORE_ROWS), :], r0), 1, 0)
            return acc + m.reshape(SCORE_ROWS // 32, 32, QB).sum(axis=0)

        def body(j, acc):
            for sub in range(COUNT_UNROLL):
                acc = one(COUNT_UNROLL * j + sub, acc)
            return acc
        n_main = n_sc // COUNT_UNROLL
        acc = lax.fori_loop(0, n_main, body, jnp.zeros((32, QB), I32))
        acc = lax.fori_loop(n_main * COUNT_UNROLL, n_sc, one, acc)
        return acc.reshape(4, 8, QB).sum(axis=0).sum(axis=0, keepdims=True)

    count_ge = lambda cand: count(lambda key, r0: key >= cand)
    c0 = count_ge(jnp.zeros((1, QB), I32))
    nonneg = c0 >= topk
    thr0 = jnp.where(nonneg, 0, INT_MIN)
    cnt0 = jnp.where(nonneg, c0, n_sc * SCORE_ROWS)

    def bit_steps(counter, n):
        def one(_, carry):
            i, thr, cnt = carry
            cand = thr | lax.shift_left(jnp.int32(1), 30 - i)
            c = counter(cand)
            ok = c >= topk
            return i + 1, jnp.where(ok, cand, thr), jnp.where(ok, c, cnt)
        return lambda carry: lax.fori_loop(0, n, one, carry)

    state = bit_steps(count_ge, 15)((jnp.int32(0), thr0, cnt0))
    _, thr, cnt = lax.while_loop(lambda c: (c[0] < 31) & (jnp.max(c[2]) > topk),
                                 bit_steps(count_ge, 4), state)

    x_ref[...] = jnp.full((1, QB), 2 ** 30, I32)
    tied = cnt > topk

    @pl.when(jnp.max(cnt) > topk)
    def _():
        above = count(lambda key, r0: key > thr)
        need = jnp.where(tied, topk - above, 0)
        max_need = jnp.max(need)

        @pl.when(max_need <= TIE_WALK_MAX)
        def _():
            def next_tie(prev):
                def body(j, acc):
                    r0 = pl.multiple_of(j * SCORE_ROWS, SCORE_ROWS)
                    idx = r0 + row_s
                    hit = (strip_ref[pl.ds(r0, SCORE_ROWS), :] == thr) & (idx > prev)
                    return jnp.minimum(acc, jnp.where(hit, idx, 2 ** 30)
                                       .reshape(SCORE_ROWS // 32, 32, QB).min(axis=0))
                acc = lax.fori_loop(0, n_sc, body, jnp.full((32, QB), 2 ** 30, I32))
                return acc.reshape(4, 8, QB).min(axis=0).min(axis=0, keepdims=True)

            last = lax.fori_loop(0, max_need, lambda r, prev: jnp.where(r < need, next_tie(prev), prev),
                                 jnp.full((1, QB), -1, I32))
            x_ref[...] = jnp.where(tied, last + 1, 2 ** 30)

        @pl.when(max_need > TIE_WALK_MAX)
        def _():
            def tie_body(i, xb):
                cand = xb | lax.shift_left(jnp.int32(1), 14 - i)
                c = count(lambda key, r0: (key == thr) & (r0 + row_s < cand))
                return jnp.where(c <= need, cand, xb)

            xb = lax.fori_loop(0, 15, tie_body, jnp.zeros((1, QB), I32))
            x_ref[...] = jnp.where(tied, xb, 2 ** 30)

    xb = x_ref[...]

    m_ref[...] = jnp.full(m_ref.shape, M_INIT, F32)
    acc_ref[...] = jnp.zeros(acc_ref.shape, F32)
    n_lt = A_HEADS * QB // SB
    row_a = {nk: lax.broadcasted_iota(I32, (nk * QB, QB), 0) for nk in (2, FAR_BLOCKS)}
    lane_a = lax.broadcasted_iota(I32, (SB, QB), 1)

    def key_operand(j, nk, near):
        rows = nk * QB
        r0 = pl.multiple_of(j * rows, rows)
        key = strip_ref[pl.ds(r0, rows), :]
        idx = r0 + row_a[nk]
        sel = (key > thr) | ((key == thr) & (idx < xb))
        if near:
            sel = sel & (idx <= t0 + lane_a)
        pen = jnp.where(sel, 0.0, MASKED).astype(BF16)
        return jnp.concatenate([katt_ref[pl.ds(nk * j, nk)].reshape(rows, 128), pen], axis=1)

    def values(j, nk):
        return jnp.concatenate([vt_ref[nk * j + i] for i in range(nk)], axis=1)

    def new_max(slot):
        m_old = m_ref[...]
        m_new = jnp.maximum(m_old, mx_ref[slot])
        m_ref[...] = m_new
        return m_new.astype(BF16), jnp.exp2(m_old - m_new)

    def logits(j, slot, near, nk=2):
        rows = nk * QB
        kaug = key_operand(j, nk, near)
        if near:
            b0 = pl.multiple_of((2 * j - qt + 3) * QB, QB)
        for lt in range(n_lt):
            cols = slice(lt * SB, (lt + 1) * SB)
            lg = _dot(kaug, qta_ref[:, cols])
            if near:
                lg = lg + bias_ref[pl.ds(b0, SB), cols]
            lg = lg.astype(BF16)
            lg_ref[slot, 0:rows, cols] = lg
            mx_ref[slot, :, cols] = jnp.max(lg, axis=0, keepdims=True).astype(F32)

    def accumulate(j, slot, nk=2):
        rows = nk * QB
        vt = values(j, nk)
        m_b, alpha = new_max(slot)
        for lt in range(n_lt):
            cols = slice(lt * SB, (lt + 1) * SB)
            p_ref[0:rows, cols] = jnp.exp2(lg_ref[slot, 0:rows, cols] - m_b[:, cols])
        acc_ref[...] = alpha * acc_ref[...] + _dot(vt, p_ref[0:rows, :])

    nk = FAR_BLOCKS
    n_far = jnp.maximum(n_sb - 2, 0)
    n_big = n_far // (nk // 2)
    n_pair = jnp.maximum(n_big - 1, 0) // 2
    n_rem = n_big - 2 * n_pair

    @pl.when(n_big > 0)
    def _():
        logits(0, 0, False, nk)

    def fused(j, slot):
        nxt = 1 - slot
        rows = nk * QB
        kaug = key_operand(j + 1, nk, False)
        vt = values(j, nk)
        m_b, alpha = new_max(slot)
        for lt in range(n_lt):
            cols = slice(lt * SB, (lt + 1) * SB)
            lg = _dot(kaug, qta_ref[:, cols]).astype(BF16)
            lg_ref[nxt, 0:rows, cols] = lg
            mx_ref[nxt, :, cols] = jnp.max(lg, axis=0, keepdims=True).astype(F32)
            p = jnp.exp2(lg_ref[slot, 0:rows, cols] - m_b[:, cols])
            acc_ref[:, cols] = alpha[:, cols] * acc_ref[:, cols] + _dot(vt, p)

    def pair_body(jj, carry):
        j = 2 * jj
        fused(j, 0)
        fused(j + 1, 1)
        return carry

    lax.fori_loop(0, n_pair, pair_body, 0)

    @pl.when(n_rem == 1)
    def _():
        accumulate(n_big - 1, 0, nk)

    @pl.when(n_rem == 2)
    def _():
        fused(n_big - 2, 0)
        accumulate(n_big - 1, 1, nk)

    @pl.when(n_far % (nk // 2) == 1)
    def _():
        logits(n_far - 1, 0, False)
        accumulate(n_far - 1, 0)

    @pl.when(n_sb >= 2)
    def _():
        logits(n_sb - 2, 0, True)
        accumulate(n_sb - 2, 0)

    logits(n_sb - 1, 0, True)
    accumulate(n_sb - 1, 0)

    acc = acc_ref[...]
    o_t = acc[0:A_HEAD_DIM, :] / acc[A_HEAD_DIM:A_HEAD_DIM + 1, :]
    pad = jnp.zeros((QB - A_HEAD_DIM, QB), F32)
    pieces = []
    for h in range(A_HEADS):
        blk = jnp.concatenate([o_t[:, h * QB:(h + 1) * QB], pad], axis=0)
        pieces.append(blk.T[:, 0:A_HEAD_DIM])
    att = jnp.concatenate(pieces, axis=1)
    o_ref[...] = (att * _silu(az_ref[...])).astype(BF16)


def _dsa_attend(h, bsz, seq, qta, qit, wt, kidx, katt, vt, bias):
    nq = seq // QB
    topk = min(A_TOPK_MAX, seq // 4)
    blk = lambda b, q: (b * nq + q, 0, 0)
    per_batch = lambda b, q: (b, 0, 0)
    once = pl.Buffered(1)
    return pl.pallas_call(
        functools.partial(_dsa_kernel, topk),
        out_shape=jax.ShapeDtypeStruct((bsz * seq, A_HEADS * A_HEAD_DIM), BF16),
        grid=(bsz, nq),
        in_specs=[pl.BlockSpec((None, QB, A_IDX_HEADS * QB), blk),
                  pl.BlockSpec((None, A_IDX_HEADS, QB), blk),
                  pl.BlockSpec((None, 2 * QB, A_HEADS * QB), blk),
                  pl.BlockSpec((nq, QB, 128), per_batch, pipeline_mode=once),
                  pl.BlockSpec((nq, QB, 128), per_batch, pipeline_mode=once),
                  pl.BlockSpec((nq, VT_ROWS, QB), per_batch, pipeline_mode=once),
                  pl.BlockSpec((BIAS_ROWS, A_HEADS * QB), lambda b, q: (0, 0), pipeline_mode=once),
                  pl.BlockSpec((QB, 1024), lambda b, q: (b * nq + q, EV_AZ // 1024))],
        out_specs=pl.BlockSpec((QB, 1024), lambda b, q: (b * nq + q, 0)),
        scratch_shapes=[pltpu.VMEM((seq, QB), I32),
                        pltpu.VMEM((1, QB), I32),
                        pltpu.VMEM((1, A_HEADS * QB), F32),
                        pltpu.VMEM((VT_ROWS, A_HEADS * QB), F32),
                        pltpu.VMEM((2, FAR_BLOCKS * QB, A_HEADS * QB), BF16),
                        pltpu.VMEM((2, 1, A_HEADS * QB), F32),
                        pltpu.VMEM((FAR_BLOCKS * QB, A_HEADS * QB), BF16)],
        compiler_params=_cparams(("arbitrary", "arbitrary")),
        name="dsa_attend",
    )(qit, wt, qta, kidx, katt, vt, bias, h)


GDN_TM = 512
GDN_GROUP = 4
GDN_SCAN_HEADS = 4
GDN_PREP_HEADS = 2


def _gdn_prep_kernel(alog_ref, dtb_ref, q_ref, k_ref, v_ref, qh_ref, kh_ref, vh_ref, misc_ref,
                     cq_ref, ck_ref, cv_ref,
                     u_ref, w_ref, qd_ref, kdt_ref, attn_ref, egl_ref):
    i = pl.program_id(1)
    hp = pl.program_id(2)
    tm = q_ref.shape[0]
    nchunk = tm // B_CHUNK
    heads = range(GDN_PREP_HEADS)
    lanes = [slice(hh * 128, (hh + 1) * 128) for hh in heads]

    def conv_silu(x_ref, halo_ref, c_ref, ln):
        halo = jnp.where(i > 0, halo_ref[:, ln], 0.0)
        ext = jnp.concatenate([halo, x_ref[:, ln]], axis=0)
        cw = c_ref[:, ln]
        y = ext[8:, :] * cw[CONV_K - 1:CONV_K, :]
        for d in range(1, CONV_K):
            y = y + pltpu.roll(ext, d, 0)[8:, :] * cw[CONV_K - 1 - d:CONV_K - d, :]
        return _silu(y)

    q = [conv_silu(q_ref, qh_ref, cq_ref, ln) for ln in lanes]
    k = [conv_silu(k_ref, kh_ref, ck_ref, ln) for ln in lanes]
    v = [conv_silu(v_ref, vh_ref, cv_ref, ln) for ln in lanes]
    qn = [x * lax.rsqrt(jnp.sum(x * x, axis=-1, keepdims=True) + EPS) * (B_HEAD_DIM ** -0.5) for x in q]
    kn = [x * lax.rsqrt(jnp.sum(x * x, axis=-1, keepdims=True) + EPS) for x in k]

    misc = misc_ref[...]
    sel_r = lax.broadcasted_iota(I32, (128, 256), 0)
    sel_c = lax.broadcasted_iota(I32, (128, 256), 1)
    beta, g = [], []
    for hh in heads:
        hd = hp * GDN_PREP_HEADS + hh
        pick = jnp.where(sel_r == jnp.where(sel_c < 128, MISC_BA + hd, MISC_BB + hd), 1.0, 0.0)
        bab = _dot_xsel(misc, pick)
        beta.append(_sigmoid(bab[:, 128:256]))
        xg = bab[:, 0:128] + dtb_ref[hd]
        softplus = jnp.maximum(xg, 0.0) + jnp.log(1.0 + jnp.exp(-jnp.abs(xg)))
        g.append(-jnp.exp(jnp.full((1, 128), alog_ref[hd], F32)) * softplus)

    sl = 256
    r2 = lax.broadcasted_iota(I32, (sl, sl), 0)
    c2 = lax.broadcasted_iota(I32, (sl, sl), 1)
    same = (r2 // B_CHUNK) == (c2 // B_CHUNK)
    tri_blk = jnp.concatenate([jnp.where(same & (c2 <= r2), 1.0, 0.0), jnp.where(same, 1.0, 0.0)], axis=0)
    nsl = tm // sl
    gc, gl = [], []
    for hh in heads:
        both = _dot_sel(tri_blk, jnp.concatenate([g[hh][s * sl:(s + 1) * sl, :] for s in range(nsl)], axis=1))
        gc.append(jnp.concatenate([both[0:sl, s * 128:(s + 1) * 128] for s in range(nsl)], axis=0))
        gl.append(jnp.concatenate([both[sl:2 * sl, s * 128:(s + 1) * 128] for s in range(nsl)], axis=0))

    kb = [kn[hh] * beta[hh] for hh in heads]
    vb = [v[hh] * beta[hh] for hh in heads]
    egc = [jnp.exp(x) for x in gc]
    kbg = [kb[hh] * egc[hh] for hh in heads]
    for hh in heads:
        qd_ref[hh] = (qn[hh] * egc[hh]).astype(BF16)
        kd = kn[hh] * jnp.exp(gl[hh] - gc[hh])
        for m in range(tm // 128):
            kdt_ref[hh, :, m * 128:(m + 1) * 128] = kd[m * 128:(m + 1) * 128, :].T.astype(BF16)

    gw = GDN_GROUP * B_CHUNK
    ri = lax.broadcasted_iota(I32, (B_CHUNK, gw), 0)
    lj = lax.broadcasted_iota(I32, (B_CHUNK, gw), 1)
    lb = lj // B_CHUNK
    lj = lj % B_CHUNK
    bmask = (lax.broadcasted_iota(I32, (gw, gw), 0) // B_CHUNK
             == lax.broadcasted_iota(I32, (gw, gw), 1) // B_CHUNK)

    def fold(x):
        xm = jnp.where(bmask, x, 0.0)
        out = xm[0:B_CHUNK, :]
        for c in range(1, GDN_GROUP):
            out = out + xm[c * B_CHUNK:(c + 1) * B_CHUNK, :]
        return out

    def bdiag(cat):
        return jnp.where(bmask, jnp.concatenate([cat] * GDN_GROUP, axis=0), 0.0)

    probs = [(hh, grp) for grp in range(tm // gw) for hh in heads]
    a_cat = []
    for hh, grp in probs:
        rs = slice(grp * gw, (grp + 1) * gw)
        gcs = gc[hh][rs, :]
        col = jnp.concatenate([gcs[0:B_CHUNK, :]] * 2, axis=1)
        for c in range(1, GDN_GROUP):
            col = jnp.where(lb == c, jnp.concatenate([gcs[c * B_CHUNK:(c + 1) * B_CHUNK, :]] * 2, axis=1), col)
        rowf = jnp.concatenate([gcs[m * 128:(m + 1) * 128, :].T[0:B_CHUNK, :] for m in range(gw // 128)],
                               axis=1)
        decay = jnp.exp(jnp.where(ri >= lj, col - rowf, NEG_INF))
        kg = kn[hh][rs, :]
        a_cat.append(jnp.where(ri > lj, fold(_dot_nt(kb[hh][rs, :], kg)) * decay, 0.0))
        attn_ref[hh, grp] = jnp.where(ri >= lj, fold(_dot_nt(qn[hh][rs, :], kg)) * decay, 0.0).astype(BF16)
    eye = jnp.where(ri == lj, 1.0, 0.0)
    inv = [eye - a for a in a_cat]
    pw = [_dot_x3(a, bdiag(a)) for a in a_cat]
    for step in range(5):
        for p in range(len(probs)):
            if step < 4:
                both = _dot_x3(jnp.concatenate([inv[p], pw[p]], axis=0), bdiag(pw[p]))
                inv[p] = inv[p] + both[0:B_CHUNK, :]
                pw[p] = both[B_CHUNK:, :]
            else:
                inv[p] = inv[p] + _dot_x3(inv[p], bdiag(pw[p]))
    for p, (hh, grp) in enumerate(probs):
        rs = slice(grp * gw, (grp + 1) * gw)
        sol = _dot_x3(bdiag(inv[p]), jnp.concatenate([vb[hh][rs, :], kbg[hh][rs, :]], axis=1))
        u_ref[hh, rs, :] = sol[:, 0:B_HEAD_DIM]
        w_ref[hh, rs, :] = sol[:, B_HEAD_DIM:].astype(BF16)
    for hh in heads:
        for c in range(nchunk):
            egl_ref[hh, c:c + 1, :] = jnp.exp(gl[hh][c * B_CHUNK:c * B_CHUNK + 1, :])
        if nchunk < 8:
            egl_ref[hh, nchunk:, :] = jnp.zeros((8 - nchunk, 128), F32)


def _gdn_prep(h, bsz, seq, b_conv, b_a_log, b_dt_bias):
    tm = min(GDN_TM, seq)
    nt = seq // tm
    nrow = max(tm // B_CHUNK, 8)
    gw = GDN_GROUP * B_CHUNK
    hs = (bsz, B_HEADS, seq, B_HEAD_DIM)
    hg = GDN_PREP_HEADS
    hw = hg * 128
    cur = lambda off: (lambda b, i, hp: (b * nt + i, off // hw + hp))
    halo = lambda off: (lambda b, i, hp: (jnp.maximum((b * nt + i) * (tm // 8) - 1, 0), off // hw + hp))
    cw = lambda off: (lambda b, i, hp: (0, off // hw + hp))
    out = lambda b, i, hp: (b, hp, i, 0)
    smem = pl.BlockSpec(memory_space=pltpu.SMEM)
    return pl.pallas_call(
        _gdn_prep_kernel,
        out_shape=(jax.ShapeDtypeStruct(hs, F32), jax.ShapeDtypeStruct(hs, BF16),
                   jax.ShapeDtypeStruct(hs, BF16),
                   jax.ShapeDtypeStruct((bsz, B_HEADS, B_HEAD_DIM, seq), BF16),
                   jax.ShapeDtypeStruct((bsz, B_HEADS, seq // gw, B_CHUNK, gw), BF16),
                   jax.ShapeDtypeStruct((bsz, B_HEADS, nt * nrow, 128), F32)),
        grid=(bsz, nt, B_HEADS // hg),
        in_specs=[smem, smem,
                  pl.BlockSpec((tm, hw), cur(EV_BQ)), pl.BlockSpec((tm, hw), cur(EV_BK)),
                  pl.BlockSpec((tm, hw), cur(EV_BV)),
                  pl.BlockSpec((8, hw), halo(EV_BQ)), pl.BlockSpec((8, hw), halo(EV_BK)),
                  pl.BlockSpec((8, hw), halo(EV_BV)),
                  pl.BlockSpec((tm, 128), lambda b, i, hp: (b * nt + i, EV_MISC // 128)),
                  pl.BlockSpec((CONV_K, hw), cw(0)), pl.BlockSpec((CONV_K, hw), cw(1024)),
                  pl.BlockSpec((CONV_K, hw), cw(2048))],
        out_specs=(pl.BlockSpec((None, hg, tm, 128), out), pl.BlockSpec((None, hg, tm, 128), out),
                   pl.BlockSpec((None, hg, tm, 128), out),
                   pl.BlockSpec((None, hg, B_HEAD_DIM, tm), lambda b, i, hp: (b, hp, 0, i)),
                   pl.BlockSpec((None, hg, tm // gw, B_CHUNK, gw), lambda b, i, hp: (b, hp, i, 0, 0)),
                   pl.BlockSpec((None, hg, nrow, 128), out)),
        compiler_params=_cparams(("parallel", "parallel", "parallel")),
        name="gdn_prep",
    )(b_a_log, b_dt_bias, h, h, h, h, h, h, h, b_conv, b_conv, b_conv)


def _gdn_scan_kernel(u_ref, w_ref, qd_ref, kdt_ref, attn_ref, egl_ref, z_ref, gn_ref, o_ref, s_ref):
    @pl.when(pl.program_id(2) == 0)
    def _():
        s_ref[...] = jnp.zeros(s_ref.shape, F32)

    tm = u_ref.shape[1]
    gw = GDN_GROUP * B_CHUNK
    gain = gn_ref[...]
    zero = jnp.zeros((B_CHUNK, B_HEAD_DIM), BF16)
    states = [s_ref[hh] for hh in range(GDN_SCAN_HEADS)]
    for c in range(tm // B_CHUNK):
        grp, ci = divmod(c, GDN_GROUP)
        rs = slice(c * B_CHUNK, (c + 1) * B_CHUNK)
        for hh in range(GDN_SCAN_HEADS):
            sb = states[hh].astype(BF16)
            v_new = u_ref[hh, rs, :] - _dot(w_ref[hh, rs, :], sb)
            vpad = jnp.concatenate([zero] * ci + [v_new.astype(BF16)] + [zero] * (GDN_GROUP - 1 - ci), axis=0)
            o = _dot(qd_ref[hh, rs, :], sb) + _dot(attn_ref[hh, grp], vpad)
            states[hh] = (states[hh] * egl_ref[hh, c:c + 1, :]
                          + _dot(kdt_ref[hh, :, grp * gw:(grp + 1) * gw], vpad))
            on = o * lax.rsqrt(jnp.mean(o * o, axis=-1, keepdims=True) + EPS) * gain
            lanes = slice(hh * B_HEAD_DIM, (hh + 1) * B_HEAD_DIM)
            o_ref[rs, lanes] = (on * _silu(z_ref[rs, lanes])).astype(BF16)
    for hh in range(GDN_SCAN_HEADS):
        s_ref[hh] = states[hh]


def _gdn_scan(h, bsz, seq, u, w, qd, kd, attn, egl, b_out_norm):
    tm = min(GDN_TM, seq)
    nt = seq // tm
    nrow = max(tm // B_CHUNK, 8)
    gw = GDN_GROUP * B_CHUNK
    hg = GDN_SCAN_HEADS
    blk = lambda b, hp, i: (b, hp, i, 0)
    return pl.pallas_call(
        _gdn_scan_kernel,
        out_shape=jax.ShapeDtypeStruct((bsz * seq, B_HEADS * B_HEAD_DIM), BF16),
        grid=(bsz, B_HEADS // hg, nt),
        in_specs=[pl.BlockSpec((None, hg, tm, 128), blk), pl.BlockSpec((None, hg, tm, 128), blk),
                  pl.BlockSpec((None, hg, tm, 128), blk),
                  pl.BlockSpec((None, hg, B_HEAD_DIM, tm), lambda b, hp, i: (b, hp, 0, i)),
                  pl.BlockSpec((None, hg, tm // gw, B_CHUNK, gw), lambda b, hp, i: (b, hp, i, 0, 0)),
                  pl.BlockSpec((None, hg, nrow, 128), blk),
                  pl.BlockSpec((tm, hg * 128), lambda b, hp, i: (b * nt + i, EV_BZ // (hg * 128) + hp)),
                  pl.BlockSpec((1, 128), lambda b, hp, i: (0, 0))],
        out_specs=pl.BlockSpec((tm, hg * 128), lambda b, hp, i: (b * nt + i, hp)),
        scratch_shapes=[pltpu.VMEM((hg, B_HEAD_DIM, B_HEAD_DIM), F32)],
        compiler_params=_cparams(("parallel", "parallel", "arbitrary")),
        name="gdn_scan",
    )(u, w, qd, kd, attn, egl, h, b_out_norm.reshape(1, -1))


def _rope_kernel(inv_ref, cos_ref, sin_ref):
    tm = cos_ref.shape[0]
    pos = (pl.program_id(0) * tm + lax.broadcasted_iota(I32, (tm, 128), 0)).astype(F32)
    ang = pos * inv_ref[...]
    lane = lax.broadcasted_iota(I32, (tm, 128), 1)
    cos_ref[...] = jnp.cos(ang)
    sin_ref[...] = jnp.where(lane < 64, -jnp.sin(ang), jnp.sin(ang))


def _rope_tables(seq):
    half = C_QK_DIM // 2
    inv = 1.0 / (ROPE_BASE ** jnp.linspace(0.0, 1.0, half, dtype=F32))
    inv2 = jnp.concatenate([inv, inv]).reshape(1, C_QK_DIM)
    tm = min(1024, seq)
    return pl.pallas_call(
        _rope_kernel,
        out_shape=(jax.ShapeDtypeStruct((seq, C_QK_DIM), F32),) * 2,
        grid=(seq // tm,),
        in_specs=[pl.BlockSpec((1, C_QK_DIM), lambda i: (0, 0))],
        out_specs=(pl.BlockSpec((tm, C_QK_DIM), lambda i: (i, 0)),) * 2,
        compiler_params=_cparams(("parallel",)),
        name="rope_tables",
    )(inv2)


RET_TM = 512


def _retention_kernel(lg_ref, q_ref, k_ref, v_ref, z_ref, cos_ref, sin_ref, gn_ref, o_ref,
                      r_ref, dm_ref, zeta_ref, xi_ref):
    hd = pl.program_id(1)
    lg = lg_ref[0, hd]

    @pl.when(pl.program_id(2) == 0)
    def _():
        r_ref[...] = jnp.zeros(r_ref.shape, F32)
        ri = lax.broadcasted_iota(I32, (C_CHUNK, C_CHUNK), 0)
        ci = lax.broadcasted_iota(I32, (C_CHUNK, C_CHUNK), 1)
        diff = (ri - ci).astype(F32)
        dm_ref[...] = jnp.where(diff >= 0, jnp.exp(jnp.maximum(diff, 0.0) * lg), 0.0)
        zeta_ref[...] = jnp.exp((C_CHUNK - 1 - ri).astype(F32) * lg)
        rv = lax.broadcasted_iota(I32, (C_CHUNK, C_V_DIM), 0).astype(F32)
        xi_ref[...] = jnp.exp((rv + 1.0) * lg)

    g_chunk = lg_ref[1, hd]
    tm = q_ref.shape[0]
    state = r_ref[...]
    dmask = dm_ref[...]
    gain = gn_ref[...]
    for c in range(tm // C_CHUNK):
        rs = slice(c * C_CHUNK, (c + 1) * C_CHUNK)
        cos = cos_ref[rs, :]
        sin = sin_ref[rs, :]
        q = q_ref[rs, :]
        k = k_ref[rs, :]
        qr = q * cos + pltpu.roll(q, C_QK_DIM // 2, 1) * sin
        kr = (k * cos + pltpu.roll(k, C_QK_DIM // 2, 1) * sin) * (C_QK_DIM ** -0.5)
        vb = v_ref[rs, :].astype(BF16)
        qb = qr.astype(BF16)
        s = _dot_nt(qb, kr.astype(BF16)) * dmask
        o = _dot(s.astype(BF16), vb) + _dot(qb, state.astype(BF16)) * xi_ref[...]
        state = state * g_chunk + _dot((kr * zeta_ref[...]).T.astype(BF16), vb)
        mu = jnp.mean(o, axis=-1, keepdims=True)
        oc = o - mu
        var = jnp.mean(oc * oc, axis=-1, keepdims=True)
        y = oc * lax.rsqrt(var + EPS) * gain
        o_ref[rs, :] = (y * _silu(z_ref[rs, :])).astype(BF16)
    r_ref[...] = state


def _retention(h, bsz, seq, cos2, sin2, c_out_norm):
    tm = min(RET_TM, seq)
    nt = seq // tm
    lg = np.log1p(-np.exp2(-5.0 - np.arange(C_HEADS, dtype=np.float32))).astype(np.float32)
    lg = np.stack([lg, np.exp(np.float32(C_CHUNK) * lg).astype(np.float32)])
    return pl.pallas_call(
        _retention_kernel,
        out_shape=jax.ShapeDtypeStruct((bsz * seq, C_HEADS * C_V_DIM), BF16),
        grid=(bsz, C_HEADS, nt),
        in_specs=[pl.BlockSpec(memory_space=pltpu.SMEM),
                  pl.BlockSpec((tm, 128), lambda b, hd, i: (b * nt + i, OD_CQ // 128 + hd)),
                  pl.BlockSpec((tm, 128), lambda b, hd, i: (b * nt + i, OD_CK // 128 + hd)),
                  pl.BlockSpec((tm, 256), lambda b, hd, i: (b * nt + i, OD_CV // 256 + hd)),
                  pl.BlockSpec((tm, 256), lambda b, hd, i: (b * nt + i, OD_CZ // 256 + hd)),
                  pl.BlockSpec((tm, 128), lambda b, hd, i: (i, 0)),
                  pl.BlockSpec((tm, 128), lambda b, hd, i: (i, 0)),
                  pl.BlockSpec((1, C_V_DIM), lambda b, hd, i: (0, hd))],
        out_specs=pl.BlockSpec((tm, C_V_DIM), lambda b, hd, i: (b * nt + i, hd)),
        scratch_shapes=[pltpu.VMEM((C_QK_DIM, C_V_DIM), F32),
                        pltpu.VMEM((C_CHUNK, C_CHUNK), F32),
                        pltpu.VMEM((C_CHUNK, C_QK_DIM), F32),
                        pltpu.VMEM((C_CHUNK, C_V_DIM), F32)],
        compiler_params=_cparams(("parallel", "parallel", "arbitrary")),
        name="retention",
    )(jnp.asarray(lg), h, h, h, h, cos2, sin2, c_out_norm.reshape(1, -1))


def _s5_param_kernel(lre_ref, lim_ref, ldt_ref, lrex_ref, limx_ref, ldtx_ref, bre_ref, bim_ref,
                     are_ref, aim_ref, bbre_ref, bbim_ref):
    def disc(lre, lim, ldt):
        lr = jnp.minimum(lre, -1e-4)
        dt = jnp.exp(ldt)
        mag = jnp.exp(lr * dt)
        return lr, lim, mag * jnp.cos(lim * dt), mag * jnp.sin(lim * dt)

    _, _, a_re, a_im = disc(lre_ref[...], lim_ref[...], ldt_ref[...])
    are_ref[...] = a_re
    aim_ref[...] = a_im
    lr, li, ax_re, ax_im = disc(lrex_ref[...], limx_ref[...], ldtx_ref[...])
    den = lr * lr + li * li
    f_re = ((ax_re - 1.0) * lr + ax_im * li) / den
    f_im = (ax_im * lr - (ax_re - 1.0) * li) / den
    bbre_ref[...] = f_re * bre_ref[...] - f_im * bim_ref[...]
    bbim_ref[...] = f_re * bim_ref[...] + f_im * bre_ref[...]


def _s5_params(lam_re, lam_im, log_dt, b_re, b_im, c_re, c_im):
    g, p, ch = D_GROUPS, D_STATE, D_GROUP
    ldt = jnp.broadcast_to(log_dt[:, None], (g, p))
    rep = lambda a: jnp.repeat(a, ch, axis=1)
    vm = pl.BlockSpec(memory_space=pltpu.VMEM)
    a_re, a_im, bb_re, bb_im = pl.pallas_call(
        _s5_param_kernel,
        out_shape=(jax.ShapeDtypeStruct((g, p), F32), jax.ShapeDtypeStruct((g, p), F32),
                   jax.ShapeDtypeStruct((g, p * ch), F32), jax.ShapeDtypeStruct((g, p * ch), F32)),
        in_specs=[vm] * 8, out_specs=(vm,) * 4,
        name="s5_params",
    )(lam_re, lam_im, ldt, rep(lam_re), rep(lam_im), rep(ldt),
      b_re.reshape(g, p * ch), b_im.reshape(g, p * ch))
    eye = jnp.eye(D_SETS * 4, dtype=F32)

    def pack_b(bb):
        bb = bb.reshape(D_SETS, 16, p, ch)
        return jnp.einsum('sgpi,gh->sgihp', bb, eye).reshape(D_SETS, D_SET_CH, D_SET_ST)

    def pack_c(c):
        c = c.reshape(D_SETS, 16, ch, p)
        return jnp.einsum('sgjp,gh->sgphj', c, eye).reshape(D_SETS, D_SET_ST, D_SET_CH)

    bd = jnp.concatenate([pack_b(bb_re), pack_b(bb_im)], axis=-1).astype(BF16)
    slab = (D_SETS, S5_SLAB, 128)
    return (a_re.reshape(slab), a_im.reshape(slab), bd,
            pack_c(c_re).astype(BF16), pack_c(c_im).astype(BF16))


S5_TM = 256
S5_SLAB = D_SET_ST // 128


def _s5_kernel(u_ref, z_ref, are_ref, aim_ref, bd_ref, cre_ref, cim_ref, dskip_ref, wglu_ref, bglu_ref,
               o_ref, hre_ref, him_ref, y_ref, *x_refs):
    @pl.when(pl.program_id(1) == 0)
    def _():
        hre_ref[...] = jnp.zeros(hre_ref.shape, F32)
        him_ref[...] = jnp.zeros(him_ref.shape, F32)

    xre, xim = x_refs[:D_SETS], x_refs[D_SETS:]
    tm = u_ref.shape[0]
    u = u_ref[...]
    ub = u.astype(BF16)
    for s in range(D_SETS):
        bu = _dot(ub[:, s * D_SET_CH:(s + 1) * D_SET_CH], bd_ref[s])
        for k in range(S5_SLAB):
            xre[s][pl.ds(k, tm, stride=S5_SLAB), :] = bu[:, k * 128:(k + 1) * 128]
            xim[s][pl.ds(k, tm, stride=S5_SLAB), :] = bu[:, D_SET_ST + k * 128:D_SET_ST + (k + 1) * 128]
    a_re = [are_ref[s] for s in range(D_SETS)]
    a_im = [aim_ref[s] for s in range(D_SETS)]

    def step(t, carry):
        rows = pl.ds(pl.multiple_of(t * S5_SLAB, S5_SLAB), S5_SLAB)
        out = []
        for s in range(D_SETS):
            hr, hi = carry[2 * s], carry[2 * s + 1]
            xr = xre[s][rows, :] + a_re[s] * hr - a_im[s] * hi
            xi = xim[s][rows, :] + a_re[s] * hi + a_im[s] * hr
            xre[s][rows, :] = xr
            xim[s][rows, :] = xi
            out += [xr, xi]
        return tuple(out)

    carry = []
    for s in range(D_SETS):
        carry += [hre_ref[s], him_ref[s]]
    carry = lax.fori_loop(0, tm, step, tuple(carry), unroll=8)
    for s in range(D_SETS):
        hre_ref[s] = carry[2 * s]
        him_ref[s] = carry[2 * s + 1]
        x_r = jnp.concatenate([xre[s][pl.ds(k, tm, stride=S5_SLAB), :] for k in range(S5_SLAB)], axis=1)
        x_i = jnp.concatenate([xim[s][pl.ds(k, tm, stride=S5_SLAB), :] for k in range(S5_SLAB)], axis=1)
        y_ref[:, s * D_SET_CH:(s + 1) * D_SET_CH] = (
            _dot(x_r.astype(BF16), cre_ref[s]) - _dot(x_i.astype(BF16), cim_ref[s]))
    y = y_ref[...] + dskip_ref[...] * u
    y = 0.5 * y * (1.0 + jnp.tanh(math.sqrt(2.0 / math.pi) * (y + 0.044715 * (y * y * y))))
    gate = _sigmoid(_dot(y.astype(BF16), wglu_ref[...]) + bglu_ref[...])
    o_ref[...] = (y * gate * _silu(z_ref[...])).astype(BF16)


def _s5(h, bsz, seq, a_re, a_im, bd, cd_re, cd_im, d_skip, w_glu, b_glu):
    tm = min(S5_TM, seq)
    nt = seq // tm
    full = lambda *shape: pl.BlockSpec(shape, lambda b, i: (0,) * len(shape))
    width = D_GROUPS * D_GROUP
    return pl.pallas_call(
        _s5_kernel,
        out_shape=jax.ShapeDtypeStruct((bsz * seq, width), BF16),
        grid=(bsz, nt),
        in_specs=[pl.BlockSpec((tm, width), lambda b, i: (b * nt + i, OD_DU // width)),
                  pl.BlockSpec((tm, width), lambda b, i: (b * nt + i, OD_DZ // width)),
                  full(D_SETS, S5_SLAB, 128), full(D_SETS, S5_SLAB, 128),
                  full(D_SETS, D_SET_CH, 2 * D_SET_ST),
                  full(D_SETS, D_SET_ST, D_SET_CH), full(D_SETS, D_SET_ST, D_SET_CH),
                  full(1, width), full(width, width), full(1, width)],
        out_specs=pl.BlockSpec((tm, width), lambda b, i: (b * nt + i, 0)),
        scratch_shapes=([pltpu.VMEM((D_SETS, S5_SLAB, 128), F32), pltpu.VMEM((D_SETS, S5_SLAB, 128), F32),
                         pltpu.VMEM((tm, width), F32)]
                        + [pltpu.VMEM((tm * S5_SLAB, 128), F32)] * (2 * D_SETS)),
        compiler_params=_cparams(("parallel", "arbitrary")),
        name="s5",
    )(h, h, a_re, a_im, bd, cd_re, cd_im, d_skip.reshape(1, -1), w_glu.astype(BF16),
      b_glu.reshape(1, -1))


def _pack_even_w(w_in):
    sizes = (1024, A_KV_RANK, A_IDX_HEADS * A_IDX_DIM, A_IDX_DIM, A_IDX_HEADS, 1024,
             3 * 1024, B_HEADS, B_HEADS, 1024)
    parts, start = [], 0
    for s in sizes:
        parts.append(w_in[:, start:start + s])
        start += s
    aq, ckv, qi, ki, wi, az, bqkv, ba, bb, bz = parts
    pad = jnp.zeros((w_in.shape[0], EV_WIDTH - start), w_in.dtype)
    return jnp.concatenate([aq, az, bqkv, bz, qi, ckv, ki, wi, ba, bb, pad], axis=1).astype(BF16)


def _even_layer(x2d, bsz, seq, bias, norm_g, w_in, a_q_norm, a_kv_norm, w_kv_up, a_k_norm,
                b_conv, b_a_log, b_dt_bias, b_out_norm, w_out):
    h = _norm_proj(x2d, norm_g, _pack_even_w(w_in), tn=768)
    qta, qit, wt, kidx, katt, vt = _dsa_prep(h, bsz * seq // QB, a_q_norm, a_kv_norm, w_kv_up, a_k_norm)
    mix_a = _dsa_attend(h, bsz, seq, qta, qit, wt, kidx, katt, vt, bias)
    u, w, qd, kd, attn, egl = _gdn_prep(h, bsz, seq, b_conv, b_a_log, b_dt_bias)
    mix_b = _gdn_scan(h, bsz, seq, u, w, qd, kd, attn, egl, b_out_norm)
    return _out_proj(x2d, mix_a, mix_b, w_out)


def _odd_layer(x2d, bsz, seq, cos2, sin2, norm_g, w_in, c_out_norm, lam_re, lam_im, log_dt,
               b_re, b_im, c_re, c_im, d_skip, w_glu, b_glu, w_out):
    h = _norm_proj(x2d, norm_g, w_in.astype(BF16), tn=1024)
    mix_c = _retention(h, bsz, seq, cos2, sin2, c_out_norm)
    s5p = _s5_params(lam_re, lam_im, log_dt, b_re, b_im, c_re, c_im)
    mix_d = _s5(h, bsz, seq, *s5p, d_skip, w_glu, b_glu)
    return _out_proj(x2d, mix_c, mix_d, w_out)


def kernel(x, rel_bias, ev_norm, ev_w_in, ev_a_q_norm, ev_a_kv_norm, ev_w_kv_up, ev_a_k_norm,
           ev_b_conv, ev_b_a_log, ev_b_dt_bias, ev_b_out_norm, ev_w_out,
           od_norm, od_w_in, od_c_out_norm, od_lam_re, od_lam_im, od_log_dt,
           od_b_re, od_b_im, od_c_re, od_c_im, od_d_skip, od_w_glu, od_b_glu, od_w_out):
    bsz, seq, d = x.shape
    depth = ev_norm.shape[0] + od_norm.shape[0]
    x2d = x.reshape(bsz * seq, d)
    bias = _bias_table(rel_bias)
    cos2, sin2 = _rope_tables(seq)
    for layer in range(depth):
        i = layer // 2
        if layer % 2 == 0:
            x2d = _even_layer(x2d, bsz, seq, bias, ev_norm[i], ev_w_in[i], ev_a_q_norm[i],
                              ev_a_kv_norm[i], ev_w_kv_up[i], ev_a_k_norm[i], ev_b_conv[i],
                              ev_b_a_log[i], ev_b_dt_bias[i], ev_b_out_norm[i], ev_w_out[i])
        else:
            x2d = _odd_layer(x2d, bsz, seq, cos2, sin2, od_norm[i], od_w_in[i], od_c_out_norm[i],
                             od_lam_re[i], od_lam_im[i], od_log_dt[i], od_b_re[i], od_b_im[i],
                             od_c_re[i], od_c_im[i], od_d_skip[i], od_w_glu[i], od_b_glu[i], od_w_out[i])
    return x2d.reshape(bsz, seq, d)
```

```python
import functools
import math

import numpy as np
import jax
import jax.numpy as jnp
from jax import lax
from jax.experimental import pallas as pl
from jax.experimental.pallas import tpu as pltpu

F32 = jnp.float32
BF16 = jnp.bfloat16
I32 = jnp.int32

D_MODEL = 1024
EPS = 1e-6
LOG2E = 1.4426950408889634
NEG_INF = float("-inf")
INT_MIN = -(2 ** 31)

A_HEADS, A_HEAD_DIM, A_KV_RANK = 16, 64, 128
A_IDX_HEADS, A_IDX_DIM, A_TOPK_MAX = 8, 64, 256
QB = 128
REL_BUCKETS, REL_MAX_DIST = 32, 128
VT_ROWS = 80
SB = 2 * QB
FAR_BLOCKS = 4
SCORE_ROWS = 4 * QB
BIAS_ROWS = 5 * QB
MASKED = -(2.0 ** 100)
M_INIT = -(2.0 ** 60)
TIE_WALK_MAX = 8
SCORE_UNROLL = 4
COUNT_UNROLL = 4
B_HEADS, B_HEAD_DIM, CONV_K, B_CHUNK = 8, 128, 4, 64
C_HEADS, C_QK_DIM, C_V_DIM, C_CHUNK = 4, 128, 256, 128
ROPE_BASE = 10000.0
D_GROUP, D_STATE, D_GROUPS = 16, 64, 64
D_SETS, D_SET_CH, D_SET_ST = 4, 256, 1024

EV_AQ, EV_AZ, EV_BQ, EV_BK, EV_BV, EV_BZ, EV_QI, EV_CKV, EV_MISC = (
    0, 1024, 2048, 3072, 4096, 5120, 6144, 6656, 6784)
EV_WIDTH = 6912
MISC_KI, MISC_WI, MISC_BA, MISC_BB = 0, 64, 72, 80
OD_CQ, OD_CK, OD_CV, OD_CZ, OD_DU, OD_DZ = 0, 512, 1024, 2048, 3072, 4096
OD_WIDTH = 5120

VMEM_LIMIT = 48 * 1024 * 1024


def _cparams(sem):
    return pltpu.CompilerParams(dimension_semantics=sem, vmem_limit_bytes=VMEM_LIMIT)


def _dot(a, b):
    return jnp.dot(a, b, preferred_element_type=F32)


def _dot_nt(a, b):
    return lax.dot_general(a, b, (((1,), (1,)), ((), ())), preferred_element_type=F32)


def _split_bf16(x, n):
    parts = []
    for term in range(n):
        p = x.astype(BF16)
        parts.append(p)
        if term + 1 < n:
            x = x - p.astype(F32)
    return parts


def _dot_sel(sel, x):
    sel = sel.astype(BF16)
    hi, mid, lo = _split_bf16(x, 3)
    return _dot(sel, hi) + (_dot(sel, mid) + _dot(sel, lo))


def _dot_xsel(x, sel):
    sel = sel.astype(BF16)
    hi, mid, lo = _split_bf16(x, 3)
    return _dot(hi, sel) + (_dot(mid, sel) + _dot(lo, sel))


def _dot_x3(a, b, expand_a=None, expand_b=None):
    ah, al = _split_bf16(a, 2)
    bh, bl = _split_bf16(b, 2)
    if expand_a is not None:
        ah, al = expand_a(ah), expand_a(al)
    if expand_b is not None:
        bh, bl = expand_b(bh), expand_b(bl)
    return _dot(ah, bh) + (_dot(ah, bl) + _dot(al, bh))


def _sigmoid(x):
    return 1.0 / (1.0 + jnp.exp(-x))


def _silu(x):
    return x * _sigmoid(x)


def _norm_proj_kernel(x_ref, g_ref, w_ref, o_ref, xn_ref):
    @pl.when(pl.program_id(1) == 0)
    def _():
        x = x_ref[...]
        ms = jnp.mean(x * x, axis=-1, keepdims=True)
        xn_ref[...] = (x * lax.rsqrt(ms + EPS) * g_ref[...]).astype(BF16)

    o_ref[...] = _dot(xn_ref[...], w_ref[...])


def _norm_proj(x2d, gain, w_bf16, tn, tm=2048):
    t, d = x2d.shape
    n = w_bf16.shape[1]
    tm = min(tm, t)
    return pl.pallas_call(
        _norm_proj_kernel,
        out_shape=jax.ShapeDtypeStruct((t, n), F32),
        grid=(t // tm, n // tn),
        in_specs=[pl.BlockSpec((tm, d), lambda i, j: (i, 0)),
                  pl.BlockSpec((1, d), lambda i, j: (0, 0)),
                  pl.BlockSpec((d, tn), lambda i, j: (0, j))],
        out_specs=pl.BlockSpec((tm, tn), lambda i, j: (i, j)),
        scratch_shapes=[pltpu.VMEM((tm, d), BF16)],
        compiler_params=_cparams(("parallel", "arbitrary")),
        name="norm_proj",
    )(x2d, gain.reshape(1, d), w_bf16)


def _out_proj_kernel(x_ref, a_ref, b_ref, wa_ref, wb_ref, o_ref):
    o_ref[...] = x_ref[...] + _dot(a_ref[...], wa_ref[...]) + _dot(b_ref[...], wb_ref[...])


def _out_proj(x2d, mix_a, mix_b, w_out, tm=512):
    t, d = x2d.shape
    half = mix_a.shape[1]
    tm = min(tm, t)
    wa = w_out[:half].astype(BF16)
    wb = w_out[half:].astype(BF16)
    return pl.pallas_call(
        _out_proj_kernel,
        out_shape=jax.ShapeDtypeStruct((t, d), F32),
        grid=(t // tm,),
        in_specs=[pl.BlockSpec((tm, d), lambda i: (i, 0)),
                  pl.BlockSpec((tm, half), lambda i: (i, 0)),
                  pl.BlockSpec((tm, half), lambda i: (i, 0)),
                  pl.BlockSpec((half, d), lambda i: (0, 0)),
                  pl.BlockSpec((half, d), lambda i: (0, 0))],
        out_specs=pl.BlockSpec((tm, d), lambda i: (i, 0)),
        compiler_params=_cparams(("parallel",)),
        name="out_proj",
    )(x2d, mix_a, mix_b, wa, wb)


def _t5_bucket_starts():
    exact = REL_BUCKETS // 2
    n = np.arange(0, 4 * REL_MAX_DIST, dtype=np.int64)
    ratio = np.maximum(n, 1).astype(np.float32) / np.float32(exact)
    large = exact + (np.log(ratio).astype(np.float32) / np.float32(math.log(REL_MAX_DIST / exact))
                     * np.float32(REL_BUCKETS - exact)).astype(np.int32)
    bucket = np.where(n < exact, n, np.minimum(large, REL_BUCKETS - 1))
    starts = [int(np.argmax(bucket >= b)) for b in range(REL_BUCKETS)]
    assert all(bucket[s] == b for b, s in enumerate(starts)) and np.all(np.diff(bucket) >= 0)
    assert starts[-1] <= QB, "distances beyond one key block must share the last bucket"
    return starts


_BUCKET_STARTS = _t5_bucket_starts()


def _bias_table_kernel(rb_ref, o_ref):
    row = lax.broadcasted_iota(I32, (BIAS_ROWS, QB), 0)
    lane = lax.broadcasted_iota(I32, (BIAS_ROWS, QB), 1)
    dist = lane + 3 * QB - row
    for h in range(A_HEADS):
        val = jnp.full((BIAS_ROWS, QB), rb_ref[0, h], F32)
        for b in range(1, REL_BUCKETS):
            val = jnp.where(dist >= _BUCKET_STARTS[b], rb_ref[b, h], val)
        val = (val - rb_ref[REL_BUCKETS - 1, h]) * LOG2E
        o_ref[:, h * QB:(h + 1) * QB] = jnp.where(dist >= 0, val, 0.0)


def _bias_table(rel_bias):
    return pl.pallas_call(
        _bias_table_kernel,
        out_shape=jax.ShapeDtypeStruct((BIAS_ROWS, A_HEADS * QB), F32),
        in_specs=[pl.BlockSpec(memory_space=pltpu.SMEM)],
        out_specs=pl.BlockSpec(memory_space=pltpu.VMEM),
        name="dsa_bias_table",
    )(rel_bias)


def _dsa_prep_kernel(aq_ref, qi_ref, ckv_ref, misc_ref, gq_ref, gkv_ref, wkv_ref, gk_ref,
                     qta_ref, qit_ref, wt_ref, kidx_ref, katt_ref, vt_ref):
    zeros = jnp.zeros((QB, A_HEAD_DIM), F32)
    aq = aq_ref[...]
    gq = gq_ref[...]
    ident = jnp.where(lax.broadcasted_iota(I32, (QB, QB), 0) == lax.broadcasted_iota(I32, (QB, QB), 1),
                      1.0, 0.0).astype(BF16)
    for h in range(A_HEADS):
        q = aq[:, h * A_HEAD_DIM:(h + 1) * A_HEAD_DIM]
        ms = jnp.mean(q * q, axis=-1, keepdims=True)
        qn = q * lax.rsqrt(ms + EPS) * gq * (A_HEAD_DIM ** -0.5 * LOG2E)
        qta_ref[0:QB, h * QB:(h + 1) * QB] = jnp.concatenate([qn, zeros], axis=1).T.astype(BF16)
        qta_ref[QB:2 * QB, h * QB:(h + 1) * QB] = ident
    qi = qi_ref[...]
    for h in range(A_IDX_HEADS):
        qih = qi[:, h * A_IDX_DIM:(h + 1) * A_IDX_DIM]
        qit_ref[:, h * QB:(h + 1) * QB] = jnp.concatenate([qih, zeros], axis=1).T.astype(BF16)
    misc = misc_ref[...]
    wt_ref[...] = misc.T[MISC_WI:MISC_WI + A_IDX_HEADS, :] * (A_IDX_HEADS ** -0.5 * A_IDX_DIM ** -0.5)
    c = ckv_ref[...]
    cn = c * lax.rsqrt(jnp.mean(c * c, axis=-1, keepdims=True) + EPS) * gkv_ref[...]
    kv = _dot(cn.astype(BF16), wkv_ref[...])
    k = kv[:, :A_HEAD_DIM]
    kn = k * lax.rsqrt(jnp.mean(k * k, axis=-1, keepdims=True) + EPS) * gk_ref[...]
    kidx_ref[...] = jnp.concatenate([misc[:, MISC_KI:MISC_KI + A_IDX_DIM], zeros], axis=1).astype(BF16)
    katt_ref[...] = jnp.concatenate([kn, zeros], axis=1).astype(BF16)
    kvt = kv.T
    vt_ref[0:A_HEAD_DIM, :] = kvt[A_HEAD_DIM:, :].astype(BF16)
    ones_row = lax.broadcasted_iota(I32, (VT_ROWS - A_HEAD_DIM, QB), 0) == 0
    vt_ref[A_HEAD_DIM:, :] = jnp.where(ones_row, 1.0, 0.0).astype(BF16)


def _dsa_prep(h, nblk, a_q_norm, a_kv_norm, w_kv_up, a_k_norm):
    cb = lambda width, off: off // width
    return pl.pallas_call(
        _dsa_prep_kernel,
        out_shape=(jax.ShapeDtypeStruct((nblk, 2 * QB, A_HEADS * QB), BF16),
                   jax.ShapeDtypeStruct((nblk, QB, A_IDX_HEADS * QB), BF16),
                   jax.ShapeDtypeStruct((nblk, A_IDX_HEADS, QB), F32),
                   jax.ShapeDtypeStruct((nblk, QB, 128), BF16),
                   jax.ShapeDtypeStruct((nblk, QB, 128), BF16),
                   jax.ShapeDtypeStruct((nblk, VT_ROWS, QB), BF16)),
        grid=(nblk,),
        in_specs=[pl.BlockSpec((QB, 1024), lambda i: (i, cb(1024, EV_AQ))),
                  pl.BlockSpec((QB, 512), lambda i: (i, cb(512, EV_QI))),
                  pl.BlockSpec((QB, 128), lambda i: (i, cb(128, EV_CKV))),
                  pl.BlockSpec((QB, 128), lambda i: (i, cb(128, EV_MISC))),
                  pl.BlockSpec((1, A_HEAD_DIM), lambda i: (0, 0)),
                  pl.BlockSpec((1, A_KV_RANK), lambda i: (0, 0)),
                  pl.BlockSpec((A_KV_RANK, 2 * A_HEAD_DIM), lambda i: (0, 0)),
                  pl.BlockSpec((1, A_HEAD_DIM), lambda i: (0, 0))],
        out_specs=(pl.BlockSpec((None, 2 * QB, A_HEADS * QB), lambda i: (i, 0, 0)),
                   pl.BlockSpec((None, QB, A_IDX_HEADS * QB), lambda i: (i, 0, 0)),
                   pl.BlockSpec((None, A_IDX_HEADS, QB), lambda i: (i, 0, 0)),
                   pl.BlockSpec((None, QB, 128), lambda i: (i, 0, 0)),
                   pl.BlockSpec((None, QB, 128), lambda i: (i, 0, 0)),
                   pl.BlockSpec((None, VT_ROWS, QB), lambda i: (i, 0, 0))),
        compiler_params=_cparams(("parallel",)),
        name="dsa_prep",
    )(h, h, h, h, a_q_norm.reshape(1, -1), a_kv_norm.reshape(1, -1), w_kv_up.astype(BF16),
      a_k_norm.reshape(1, -1))


def _dsa_kernel(topk, qit_ref, wt_ref, qta_ref, kidx_ref, katt_ref, vt_ref, bias_ref, az_ref, o_ref,
                strip_ref, x_ref, m_ref, acc_ref, lg_ref, mx_ref, p_ref):
    qt = pl.program_id(1)
    t0 = qt * QB
    n_sc = qt // 4 + 1
    n_sb = qt // 2 + 1
    lane_s = lax.broadcasted_iota(I32, (SCORE_ROWS, QB), 1)
    row_s = lax.broadcasted_iota(I32, (SCORE_ROWS, QB), 0)

    w = wt_ref[...]

    def score_body(j, carry):
        kblk = kidx_ref[pl.ds(j * 4, 4)].reshape(SCORE_ROWS, 128)
        tot = None
        for hp in range(A_IDX_HEADS // 2):
            s = _dot(kblk, qit_ref[:, hp * SB:(hp + 1) * SB])
            for e in range(2):
                h = 2 * hp + e
                term = jnp.maximum(s[:, e * QB:(e + 1) * QB], 0.0) * w[h:h + 1, :]
                tot = term if tot is None else tot + term
        tot = jnp.where(j * SCORE_ROWS + row_s <= t0 + lane_s, tot, NEG_INF)
        bits = pltpu.bitcast(tot, I32)
        strip_ref[pl.ds(pl.multiple_of(j * SCORE_ROWS, SCORE_ROWS), SCORE_ROWS), :] = (
            bits ^ ((bits >> 31) & 0x7FFFFFFF))
        return carry

    def score_group(jj, carry):
        for sub in range(SCORE_UNROLL):
            score_body(SCORE_UNROLL * jj + sub, carry)
        return carry

    lax.fori_loop(0, n_sc // SCORE_UNROLL, score_group, 0)
    lax.fori_loop(SCORE_UNROLL * (n_sc // SCORE_UNROLL), n_sc, score_body, 0)

    def count(pred):
        def one(j, acc):
            r0 = pl.multiple_of(j * SCORE_ROWS, SCORE_ROWS)
            m = jnp.where(pred(strip_ref[pl.ds(r0, SCORE_ROWS), :], r0), 1, 0)
            return acc + m.reshape(SCORE_ROWS // 32, 32, QB).sum(axis=0)

        def body(j, acc):
            for sub in range(COUNT_UNROLL):
                acc = one(COUNT_UNROLL * j + sub, acc)
            return acc
        n_main = n_sc // COUNT_UNROLL
        acc = lax.fori_loop(0, n_main, body, jnp.zeros((32, QB), I32))
        acc = lax.fori_loop(n_main * COUNT_UNROLL, n_sc, one, acc)
        return acc.reshape(4, 8, QB).sum(axis=0).sum(axis=0, keepdims=True)

    count_ge = lambda cand: count(lambda key, r0: key >= cand)
    c0 = count_ge(jnp.zeros((1, QB), I32))
    nonneg = c0 >= topk
    thr0 = jnp.where(nonneg, 0, INT_MIN)
    cnt0 = jnp.where(nonneg, c0, n_sc * SCORE_ROWS)

    def bit_steps(counter, n):
        def one(_, carry):
            i, thr, cnt = carry
            cand = thr | lax.shift_left(jnp.int32(1), 30 - i)
            c = counter(cand)
            ok = c >= topk
            return i + 1, jnp.where(ok, cand, thr), jnp.where(ok, c, cnt)
        return lambda carry: lax.fori_loop(0, n, one, carry)

    state = bit_steps(count_ge, 15)((jnp.int32(0), thr0, cnt0))
    _, thr, cnt = lax.while_loop(lambda c: (c[0] < 31) & (jnp.max(c[2]) > topk),
                                 bit_steps(count_ge, 4), state)

    x_ref[...] = jnp.full((1, QB), 2 ** 30, I32)
    tied = cnt > topk

    drop = jnp.where(tied, cnt - topk, 0)
    max_drop = jnp.max(drop)

    @pl.when((max_drop > 0) & (max_drop <= TIE_WALK_MAX))
    def _():
        def prev_tie(bound):
            def body(j, acc):
                r0 = pl.multiple_of(j * SCORE_ROWS, SCORE_ROWS)
                idx = r0 + row_s
                hit = (strip_ref[pl.ds(r0, SCORE_ROWS), :] == thr) & (idx < bound)
                return jnp.maximum(acc, jnp.where(hit, idx, -1)
                                   .reshape(SCORE_ROWS // 32, 32, QB).max(axis=0))
            acc = lax.fori_loop(0, n_sc, body, jnp.full((32, QB), -1, I32))
            return acc.reshape(4, 8, QB).max(axis=0).max(axis=0, keepdims=True)

        x_ref[...] = lax.fori_loop(0, max_drop, lambda r, bound: jnp.where(r < drop, prev_tie(bound), bound),
                                   jnp.full((1, QB), 2 ** 30, I32))

    @pl.when(max_drop > TIE_WALK_MAX)
    def _():
        above = count(lambda key, r0: key > thr)
        need = jnp.where(tied, topk - above, 0)

        def tie_body(i, xb):
            cand = xb | lax.shift_left(jnp.int32(1), 14 - i)
            c = count(lambda key, r0: (key == thr) & (r0 + row_s < cand))
            return jnp.where(c <= need, cand, xb)

        xb = lax.fori_loop(0, 15, tie_body, jnp.zeros((1, QB), I32))
        x_ref[...] = jnp.where(tied, xb, 2 ** 30)

    xb = x_ref[...]

    m_ref[...] = jnp.full(m_ref.shape, M_INIT, F32)
    acc_ref[...] = jnp.zeros(acc_ref.shape, F32)
    n_lt = A_HEADS * QB // SB
    row_a = {nk: lax.broadcasted_iota(I32, (nk * QB, QB), 0) for nk in (2, FAR_BLOCKS)}
    lane_a = lax.broadcasted_iota(I32, (SB, QB), 1)

    def key_operand(j, nk, near):
        rows = nk * QB
        r0 = pl.multiple_of(j * rows, rows)
        key = strip_ref[pl.ds(r0, rows), :]
        idx = r0 + row_a[nk]
        sel = (key > thr) | ((key == thr) & (idx < xb))
        if near:
            sel = sel & (idx <= t0 + lane_a)
        pen = jnp.where(sel, 0.0, MASKED).astype(BF16)
        return jnp.concatenate([katt_ref[pl.ds(nk * j, nk)].reshape(rows, 128), pen], axis=1)

    def values(j, nk):
        return jnp.concatenate([vt_ref[nk * j + i] for i in range(nk)], axis=1)

    def new_max(slot):
        m_old = m_ref[...]
        m_new = jnp.maximum(m_old, mx_ref[slot])
        m_ref[...] = m_new
        return m_new.astype(BF16), jnp.exp2(m_old - m_new)

    def logits(j, slot, near, nk=2):
        rows = nk * QB
        kaug = key_operand(j, nk, near)
        if near:
            b0 = pl.multiple_of((2 * j - qt + 3) * QB, QB)
        for lt in range(n_lt):
            cols = slice(lt * SB, (lt + 1) * SB)
            lg = _dot(kaug, qta_ref[:, cols])
            if near:
                lg = lg + bias_ref[pl.ds(b0, SB), cols]
            lg = lg.astype(BF16)
            lg_ref[slot, 0:rows, cols] = lg
            mx_ref[slot, :, cols] = jnp.max(lg, axis=0, keepdims=True).astype(F32)

    def accumulate(j, slot, nk=2):
        rows = nk * QB
        vt = values(j, nk)
        m_b, alpha = new_max(slot)
        for lt in range(n_lt):
            cols = slice(lt * SB, (lt + 1) * SB)
            p_ref[0:rows, cols] = jnp.exp2(lg_ref[slot, 0:rows, cols] - m_b[:, cols])
        acc_ref[...] = alpha * acc_ref[...] + _dot(vt, p_ref[0:rows, :])

    nk = FAR_BLOCKS
    n_far = jnp.maximum(n_sb - 2, 0)
    n_big = n_far // (nk // 2)
    n_pair = jnp.maximum(n_big - 1, 0) // 2
    n_rem = n_big - 2 * n_pair

    @pl.when(n_big > 0)
    def _():
        logits(0, 0, False, nk)

    def fused(j, slot):
        nxt = 1 - slot
        rows = nk * QB
        kaug = key_operand(j + 1, nk, False)
        vt = values(j, nk)
        m_b, alpha = new_max(slot)
        for lt in range(n_lt):
            cols = slice(lt * SB, (lt + 1) * SB)
            lg = _dot(kaug, qta_ref[:, cols]).astype(BF16)
            lg_ref[nxt, 0:rows, cols] = lg
            mx_ref[nxt, :, cols] = jnp.max(lg, axis=0, keepdims=True).astype(F32)
            p = jnp.exp2(lg_ref[slot, 0:rows, cols] - m_b[:, cols])
            acc_ref[:, cols] = alpha[:, cols] * acc_ref[:, cols] + _dot(vt, p)

    def pair_body(jj, carry):
        j = 2 * jj
        fused(j, 0)
        fused(j + 1, 1)
        return carry

    lax.fori_loop(0, n_pair, pair_body, 0)

    @pl.when(n_rem == 1)
    def _():
        accumulate(n_big - 1, 0, nk)

    @pl.when(n_rem == 2)
    def _():
        fused(n_big - 2, 0)
        accumulate(n_big - 1, 1, nk)

    @pl.when(n_far % (nk // 2) == 1)
    def _():
        logits(n_far - 1, 0, False)
        accumulate(n_far - 1, 0)

    @pl.when(n_sb >= 2)
    def _():
        logits(n_sb - 2, 0, True)
        accumulate(n_sb - 2, 0)

    logits(n_sb - 1, 0, True)
    accumulate(n_sb - 1, 0)

    acc = acc_ref[...]
    o_t = acc[0:A_HEAD_DIM, :] / acc[A_HEAD_DIM:A_HEAD_DIM + 1, :]
    pad = jnp.zeros((QB - A_HEAD_DIM, QB), F32)
    pieces = []
    for h in range(A_HEADS):
        blk = jnp.concatenate([o_t[:, h * QB:(h + 1) * QB], pad], axis=0)
        pieces.append(blk.T[:, 0:A_HEAD_DIM])
    att = jnp.concatenate(pieces, axis=1)
    o_ref[...] = (att * _silu(az_ref[...])).astype(BF16)


def _dsa_attend(h, bsz, seq, qta, qit, wt, kidx, katt, vt, bias):
    nq = seq // QB
    topk = min(A_TOPK_MAX, seq // 4)
    blk = lambda b, q: (b * nq + q, 0, 0)
    per_batch = lambda b, q: (b, 0, 0)
    once = pl.Buffered(1)
    return pl.pallas_call(
        functools.partial(_dsa_kernel, topk),
        out_shape=jax.ShapeDtypeStruct((bsz * seq, A_HEADS * A_HEAD_DIM), BF16),
        grid=(bsz, nq),
        in_specs=[pl.BlockSpec((None, QB, A_IDX_HEADS * QB), blk),
                  pl.BlockSpec((None, A_IDX_HEADS, QB), blk),
                  pl.BlockSpec((None, 2 * QB, A_HEADS * QB), blk),
                  pl.BlockSpec((nq, QB, 128), per_batch, pipeline_mode=once),
                  pl.BlockSpec((nq, QB, 128), per_batch, pipeline_mode=once),
                  pl.BlockSpec((nq, VT_ROWS, QB), per_batch, pipeline_mode=once),
                  pl.BlockSpec((BIAS_ROWS, A_HEADS * QB), lambda b, q: (0, 0), pipeline_mode=once),
                  pl.BlockSpec((QB, 1024), lambda b, q: (b * nq + q, EV_AZ // 1024))],
        out_specs=pl.BlockSpec((QB, 1024), lambda b, q: (b * nq + q, 0)),
        scratch_shapes=[pltpu.VMEM((seq, QB), I32),
                        pltpu.VMEM((1, QB), I32),
                        pltpu.VMEM((1, A_HEADS * QB), F32),
                        pltpu.VMEM((VT_ROWS, A_HEADS * QB), F32),
                        pltpu.VMEM((2, FAR_BLOCKS * QB, A_HEADS * QB), BF16),
                        pltpu.VMEM((2, 1, A_HEADS * QB), F32),
                        pltpu.VMEM((FAR_BLOCKS * QB, A_HEADS * QB), BF16)],
        compiler_params=_cparams(("arbitrary", "arbitrary")),
        name="dsa_attend",
    )(qit, wt, qta, kidx, katt, vt, bias, h)


GDN_TM = 512
GDN_GROUP = 4
GDN_SCAN_HEADS = 8
GDN_PREP_HEADS = 2


def _gdn_prep_kernel(alog_ref, dtb_ref, q_ref, k_ref, v_ref, qh_ref, kh_ref, vh_ref, misc_ref,
                     cq_ref, ck_ref, cv_ref,
                     u_ref, w_ref, qd_ref, kdt_ref, attn_ref, egl_ref):
    i = pl.program_id(1)
    hp = pl.program_id(2)
    tm = q_ref.shape[0]
    nchunk = tm // B_CHUNK
    heads = range(GDN_PREP_HEADS)
    lanes = [slice(hh * 128, (hh + 1) * 128) for hh in heads]

    def conv_silu(x_ref, halo_ref, c_ref, ln):
        halo = jnp.where(i > 0, halo_ref[:, ln], 0.0)
        ext = jnp.concatenate([halo, x_ref[:, ln]], axis=0)
        cw = c_ref[:, ln]
        y = ext[8:, :] * cw[CONV_K - 1:CONV_K, :]
        for d in range(1, CONV_K):
            y = y + pltpu.roll(ext, d, 0)[8:, :] * cw[CONV_K - 1 - d:CONV_K - d, :]
        return _silu(y)

    q = [conv_silu(q_ref, qh_ref, cq_ref, ln) for ln in lanes]
    k = [conv_silu(k_ref, kh_ref, ck_ref, ln) for ln in lanes]
    v = [conv_silu(v_ref, vh_ref, cv_ref, ln) for ln in lanes]
    qn = [x * lax.rsqrt(jnp.sum(x * x, axis=-1, keepdims=True) + EPS) * (B_HEAD_DIM ** -0.5) for x in q]
    kn = [x * lax.rsqrt(jnp.sum(x * x, axis=-1, keepdims=True) + EPS) for x in k]

    misc = misc_ref[...]
    sel_r = lax.broadcasted_iota(I32, (128, 256), 0)
    sel_c = lax.broadcasted_iota(I32, (128, 256), 1)
    beta, g = [], []
    for hh in heads:
        hd = hp * GDN_PREP_HEADS + hh
        pick = jnp.where(sel_r == jnp.where(sel_c < 128, MISC_BA + hd, MISC_BB + hd), 1.0, 0.0)
        bab = _dot_xsel(misc, pick)
        beta.append(_sigmoid(bab[:, 128:256]))
        xg = bab[:, 0:128] + dtb_ref[hd]
        softplus = jnp.maximum(xg, 0.0) + jnp.log(1.0 + jnp.exp(-jnp.abs(xg)))
        g.append(-jnp.exp(jnp.full((1, 128), alog_ref[hd], F32)) * softplus)

    sl = 256
    r2 = lax.broadcasted_iota(I32, (sl, sl), 0)
    c2 = lax.broadcasted_iota(I32, (sl, sl), 1)
    same = (r2 // B_CHUNK) == (c2 // B_CHUNK)
    tri_blk = jnp.concatenate([jnp.where(same & (c2 <= r2), 1.0, 0.0), jnp.where(same, 1.0, 0.0)], axis=0)
    nsl = tm // sl
    gc, gl = [], []
    for hh in heads:
        both = _dot_sel(tri_blk, jnp.concatenate([g[hh][s * sl:(s + 1) * sl, :] for s in range(nsl)], axis=1))
        gc.append(jnp.concatenate([both[0:sl, s * 128:(s + 1) * 128] for s in range(nsl)], axis=0))
        gl.append(jnp.concatenate([both[sl:2 * sl, s * 128:(s + 1) * 128] for s in range(nsl)], axis=0))

    kb = [kn[hh] * beta[hh] for hh in heads]
    vb = [v[hh] * beta[hh] for hh in heads]
    egc = [jnp.exp(x) for x in gc]
    kbg = [kb[hh] * egc[hh] for hh in heads]
    for hh in heads:
        qd_ref[hh] = (qn[hh] * egc[hh]).astype(BF16)
        kd = kn[hh] * jnp.exp(gl[hh] - gc[hh])
        for m in range(tm // 128):
            kdt_ref[hh, :, m * 128:(m + 1) * 128] = kd[m * 128:(m + 1) * 128, :].T.astype(BF16)

    gw = GDN_GROUP * B_CHUNK
    ri = lax.broadcasted_iota(I32, (B_CHUNK, gw), 0)
    lj = lax.broadcasted_iota(I32, (B_CHUNK, gw), 1)
    lb = lj // B_CHUNK
    lj = lj % B_CHUNK
    bmask = (lax.broadcasted_iota(I32, (gw, gw), 0) // B_CHUNK
             == lax.broadcasted_iota(I32, (gw, gw), 1) // B_CHUNK)

    def fold(x):
        xm = jnp.where(bmask, x, 0.0)
        out = xm[0:B_CHUNK, :]
        for c in range(1, GDN_GROUP):
            out = out + xm[c * B_CHUNK:(c + 1) * B_CHUNK, :]
        return out

    bmask_b = jnp.where(bmask, 1.0, 0.0).astype(BF16)

    def bdiag(cat):
        return jnp.concatenate([cat] * GDN_GROUP, axis=0) * bmask_b

    probs = [(hh, grp) for grp in range(tm // gw) for hh in heads]
    a_cat = []
    for hh, grp in probs:
        rs = slice(grp * gw, (grp + 1) * gw)
        gcs = gc[hh][rs, :]
        col = jnp.concatenate([gcs[0:B_CHUNK, :]] * 2, axis=1)
        for c in range(1, GDN_GROUP):
            col = jnp.where(lb == c, jnp.concatenate([gcs[c * B_CHUNK:(c + 1) * B_CHUNK, :]] * 2, axis=1), col)
        rowf = jnp.concatenate([gcs[m * 128:(m + 1) * 128, :].T[0:B_CHUNK, :] for m in range(gw // 128)],
                               axis=1)
        decay = jnp.exp(jnp.where(ri >= lj, col - rowf, NEG_INF))
        kg = kn[hh][rs, :]
        a_cat.append(jnp.where(ri > lj, fold(_dot_nt(kb[hh][rs, :], kg)) * decay, 0.0))
        attn_ref[hh, grp] = jnp.where(ri >= lj, fold(_dot_nt(qn[hh][rs, :], kg)) * decay, 0.0).astype(BF16)
    eye = jnp.where(ri == lj, 1.0, 0.0)
    inv = [eye - a for a in a_cat]
    pw = [_dot_x3(a, a, expand_b=bdiag) for a in a_cat]
    for step in range(5):
        for p in range(len(probs)):
            if step < 4:
                both = _dot_x3(jnp.concatenate([inv[p], pw[p]], axis=0), pw[p], expand_b=bdiag)
                inv[p] = inv[p] + both[0:B_CHUNK, :]
                pw[p] = both[B_CHUNK:, :]
            else:
                inv[p] = inv[p] + _dot_x3(inv[p], pw[p], expand_b=bdiag)
    for p, (hh, grp) in enumerate(probs):
        rs = slice(grp * gw, (grp + 1) * gw)
        sol = _dot_x3(inv[p], jnp.concatenate([vb[hh][rs, :], kbg[hh][rs, :]], axis=1), expand_a=bdiag)
        u_ref[hh, rs, :] = sol[:, 0:B_HEAD_DIM]
        w_ref[hh, rs, :] = sol[:, B_HEAD_DIM:].astype(BF16)
    for hh in heads:
        for c in range(nchunk):
            egl_ref[hh, c:c + 1, :] = jnp.exp(gl[hh][c * B_CHUNK:c * B_CHUNK + 1, :])
        if nchunk < 8:
            egl_ref[hh, nchunk:, :] = jnp.zeros((8 - nchunk, 128), F32)


def _gdn_prep(h, bsz, seq, b_conv, b_a_log, b_dt_bias):
    tm = min(GDN_TM, seq)
    nt = seq // tm
    nrow = max(tm // B_CHUNK, 8)
    gw = GDN_GROUP * B_CHUNK
    hs = (bsz, B_HEADS, seq, B_HEAD_DIM)
    hg = GDN_PREP_HEADS
    hw = hg * 128
    cur = lambda off: (lambda b, i, hp: (b * nt + i, off // hw + hp))
    halo = lambda off: (lambda b, i, hp: (jnp.maximum((b * nt + i) * (tm // 8) - 1, 0), off // hw + hp))
    cw = lambda off: (lambda b, i, hp: (0, off // hw + hp))
    out = lambda b, i, hp: (b, hp, i, 0)
    smem = pl.BlockSpec(memory_space=pltpu.SMEM)
    return pl.pallas_call(
        _gdn_prep_kernel,
        out_shape=(jax.ShapeDtypeStruct(hs, F32), jax.ShapeDtypeStruct(hs, BF16),
                   jax.ShapeDtypeStruct(hs, BF16),
                   jax.ShapeDtypeStruct((bsz, B_HEADS, B_HEAD_DIM, seq), BF16),
                   jax.ShapeDtypeStruct((bsz, B_HEADS, seq // gw, B_CHUNK, gw), BF16),
                   jax.ShapeDtypeStruct((bsz, B_HEADS, nt * nrow, 128), F32)),
        grid=(bsz, nt, B_HEADS // hg),
        in_specs=[smem, smem,
                  pl.BlockSpec((tm, hw), cur(EV_BQ)), pl.BlockSpec((tm, hw), cur(EV_BK)),
                  pl.BlockSpec((tm, hw), cur(EV_BV)),
                  pl.BlockSpec((8, hw), halo(EV_BQ)), pl.BlockSpec((8, hw), halo(EV_BK)),
                  pl.BlockSpec((8, hw), halo(EV_BV)),
                  pl.BlockSpec((tm, 128), lambda b, i, hp: (b * nt + i, EV_MISC // 128)),
                  pl.BlockSpec((CONV_K, hw), cw(0)), pl.BlockSpec((CONV_K, hw), cw(1024)),
                  pl.BlockSpec((CONV_K, hw), cw(2048))],
        out_specs=(pl.BlockSpec((None, hg, tm, 128), out), pl.BlockSpec((None, hg, tm, 128), out),
                   pl.BlockSpec((None, hg, tm, 128), out),
                   pl.BlockSpec((None, hg, B_HEAD_DIM, tm), lambda b, i, hp: (b, hp, 0, i)),
                   pl.BlockSpec((None, hg, tm // gw, B_CHUNK, gw), lambda b, i, hp: (b, hp, i, 0, 0)),
                   pl.BlockSpec((None, hg, nrow, 128), out)),
        compiler_params=_cparams(("parallel", "parallel", "parallel")),
        name="gdn_prep",
    )(b_a_log, b_dt_bias, h, h, h, h, h, h, h, b_conv, b_conv, b_conv)


def _gdn_scan_kernel(u_ref, w_ref, qd_ref, kdt_ref, attn_ref, egl_ref, z_ref, gn_ref, o_ref, s_ref):
    @pl.when(pl.program_id(2) == 0)
    def _():
        s_ref[...] = jnp.zeros(s_ref.shape, F32)

    tm = u_ref.shape[1]
    gw = GDN_GROUP * B_CHUNK
    gain = gn_ref[...]
    zero = jnp.zeros((B_CHUNK, B_HEAD_DIM), BF16)
    states = [s_ref[hh] for hh in range(GDN_SCAN_HEADS)]
    for c in range(tm // B_CHUNK):
        grp, ci = divmod(c, GDN_GROUP)
        rs = slice(c * B_CHUNK, (c + 1) * B_CHUNK)
        for hh in range(GDN_SCAN_HEADS):
            sb = states[hh].astype(BF16)
            v_new = u_ref[hh, rs, :] - _dot(w_ref[hh, rs, :], sb)
            vpad = jnp.concatenate([zero] * ci + [v_new.astype(BF16)] + [zero] * (GDN_GROUP - 1 - ci), axis=0)
            o = _dot(qd_ref[hh, rs, :], sb) + _dot(attn_ref[hh, grp], vpad)
            states[hh] = (states[hh] * egl_ref[hh, c:c + 1, :]
                          + _dot(kdt_ref[hh, :, grp * gw:(grp + 1) * gw], vpad))
            on = o * lax.rsqrt(jnp.mean(o * o, axis=-1, keepdims=True) + EPS) * gain
            lanes = slice(hh * B_HEAD_DIM, (hh + 1) * B_HEAD_DIM)
            o_ref[rs, lanes] = (on * _silu(z_ref[rs, lanes])).astype(BF16)
    for hh in range(GDN_SCAN_HEADS):
        s_ref[hh] = states[hh]


def _gdn_scan(h, bsz, seq, u, w, qd, kd, attn, egl, b_out_norm):
    tm = min(GDN_TM, seq)
    nt = seq // tm
    nrow = max(tm // B_CHUNK, 8)
    gw = GDN_GROUP * B_CHUNK
    hg = GDN_SCAN_HEADS
    blk = lambda b, hp, i: (b, hp, i, 0)
    return pl.pallas_call(
        _gdn_scan_kernel,
        out_shape=jax.ShapeDtypeStruct((bsz * seq, B_HEADS * B_HEAD_DIM), BF16),
        grid=(bsz, B_HEADS // hg, nt),
        in_specs=[pl.BlockSpec((None, hg, tm, 128), blk), pl.BlockSpec((None, hg, tm, 128), blk),
                  pl.BlockSpec((None, hg, tm, 128), blk),
                  pl.BlockSpec((None, hg, B_HEAD_DIM, tm), lambda b, hp, i: (b, hp, 0, i)),
                  pl.BlockSpec((None, hg, tm // gw, B_CHUNK, gw), lambda b, hp, i: (b, hp, i, 0, 0)),
                  pl.BlockSpec((None, hg, nrow, 128), blk),
                  pl.BlockSpec((tm, hg * 128), lambda b, hp, i: (b * nt + i, EV_BZ // (hg * 128) + hp)),
                  pl.BlockSpec((1, 128), lambda b, hp, i: (0, 0))],
        out_specs=pl.BlockSpec((tm, hg * 128), lambda b, hp, i: (b * nt + i, hp)),
        scratch_shapes=[pltpu.VMEM((hg, B_HEAD_DIM, B_HEAD_DIM), F32)],
        compiler_params=_cparams(("parallel", "parallel", "arbitrary")),
        name="gdn_scan",
    )(u, w, qd, kd, attn, egl, h, b_out_norm.reshape(1, -1))


def _rope_kernel(inv_ref, cos_ref, sin_ref):
    tm = cos_ref.shape[0]
    pos = (pl.program_id(0) * tm + lax.broadcasted_iota(I32, (tm, 128), 0)).astype(F32)
    ang = pos * inv_ref[...]
    lane = lax.broadcasted_iota(I32, (tm, 128), 1)
    cos_ref[...] = jnp.cos(ang)
    sin_ref[...] = jnp.where(lane < 64, -jnp.sin(ang), jnp.sin(ang))


def _rope_tables(seq):
    half = C_QK_DIM // 2
    inv = 1.0 / (ROPE_BASE ** jnp.linspace(0.0, 1.0, half, dtype=F32))
    inv2 = jnp.concatenate([inv, inv]).reshape(1, C_QK_DIM)
    tm = min(1024, seq)
    return pl.pallas_call(
        _rope_kernel,
        out_shape=(jax.ShapeDtypeStruct((seq, C_QK_DIM), F32),) * 2,
        grid=(seq // tm,),
        in_specs=[pl.BlockSpec((1, C_QK_DIM), lambda i: (0, 0))],
        out_specs=(pl.BlockSpec((tm, C_QK_DIM), lambda i: (i, 0)),) * 2,
        compiler_params=_cparams(("parallel",)),
        name="rope_tables",
    )(inv2)


RET_TM = 512


def _retention_kernel(lg_ref, q_ref, k_ref, v_ref, z_ref, cos_ref, sin_ref, gn_ref, o_ref,
                      r_ref, dm_ref, zeta_ref, xi_ref):
    hd = pl.program_id(1)
    lg = lg_ref[0, hd]

    @pl.when(pl.program_id(2) == 0)
    def _():
        r_ref[...] = jnp.zeros(r_ref.shape, F32)
        ri = lax.broadcasted_iota(I32, (C_CHUNK, C_CHUNK), 0)
        ci = lax.broadcasted_iota(I32, (C_CHUNK, C_CHUNK), 1)
        diff = (ri - ci).astype(F32)
        dm_ref[...] = jnp.where(diff >= 0, jnp.exp(jnp.maximum(diff, 0.0) * lg), 0.0)
        zeta_ref[...] = jnp.exp((C_CHUNK - 1 - ri).astype(F32) * lg)
        rv = lax.broadcasted_iota(I32, (C_CHUNK, C_V_DIM), 0).astype(F32)
        xi_ref[...] = jnp.exp((rv + 1.0) * lg)

    g_chunk = lg_ref[1, hd]
    tm = q_ref.shape[0]
    state = r_ref[...]
    dmask = dm_ref[...]
    gain = gn_ref[...]
    for c in range(tm // C_CHUNK):
        rs = slice(c * C_CHUNK, (c + 1) * C_CHUNK)
        cos = cos_ref[rs, :]
        sin = sin_ref[rs, :]
        q = q_ref[rs, :]
        k = k_ref[rs, :]
        qr = q * cos + pltpu.roll(q, C_QK_DIM // 2, 1) * sin
        kr = (k * cos + pltpu.roll(k, C_QK_DIM // 2, 1) * sin) * (C_QK_DIM ** -0.5)
        vb = v_ref[rs, :].astype(BF16)
        qb = qr.astype(BF16)
        s = _dot_nt(qb, kr.astype(BF16)) * dmask
        o = _dot(s.astype(BF16), vb) + _dot(qb, state.astype(BF16)) * xi_ref[...]
        state = state * g_chunk + _dot((kr * zeta_ref[...]).T.astype(BF16), vb)
        mu = jnp.mean(o, axis=-1, keepdims=True)
        oc = o - mu
        var = jnp.mean(oc * oc, axis=-1, keepdims=True)
        y = oc * lax.rsqrt(var + EPS) * gain
        o_ref[rs, :] = (y * _silu(z_ref[rs, :])).astype(BF16)
    r_ref[...] = state


def _retention(h, bsz, seq, cos2, sin2, c_out_norm):
    tm = min(RET_TM, seq)
    nt = seq // tm
    lg = np.log1p(-np.exp2(-5.0 - np.arange(C_HEADS, dtype=np.float32))).astype(np.float32)
    lg = np.stack([lg, np.exp(np.float32(C_CHUNK) * lg).astype(np.float32)])
    return pl.pallas_call(
        _retention_kernel,
        out_shape=jax.ShapeDtypeStruct((bsz * seq, C_HEADS * C_V_DIM), BF16),
        grid=(bsz, C_HEADS, nt),
        in_specs=[pl.BlockSpec(memory_space=pltpu.SMEM),
                  pl.BlockSpec((tm, 128), lambda b, hd, i: (b * nt + i, OD_CQ // 128 + hd)),
                  pl.BlockSpec((tm, 128), lambda b, hd, i: (b * nt + i, OD_CK // 128 + hd)),
                  pl.BlockSpec((tm, 256), lambda b, hd, i: (b * nt + i, OD_CV // 256 + hd)),
                  pl.BlockSpec((tm, 256), lambda b, hd, i: (b * nt + i, OD_CZ // 256 + hd)),
                  pl.BlockSpec((tm, 128), lambda b, hd, i: (i, 0)),
                  pl.BlockSpec((tm, 128), lambda b, hd, i: (i, 0)),
                  pl.BlockSpec((1, C_V_DIM), lambda b, hd, i: (0, hd))],
        out_specs=pl.BlockSpec((tm, C_V_DIM), lambda b, hd, i: (b * nt + i, hd)),
        scratch_shapes=[pltpu.VMEM((C_QK_DIM, C_V_DIM), F32),
                        pltpu.VMEM((C_CHUNK, C_CHUNK), F32),
                        pltpu.VMEM((C_CHUNK, C_QK_DIM), F32),
                        pltpu.VMEM((C_CHUNK, C_V_DIM), F32)],
        compiler_params=_cparams(("parallel", "parallel", "arbitrary")),
        name="retention",
    )(jnp.asarray(lg), h, h, h, h, cos2, sin2, c_out_norm.reshape(1, -1))


def _s5_param_kernel(lre_ref, lim_ref, ldt_ref, lrex_ref, limx_ref, ldtx_ref, bre_ref, bim_ref,
                     are_ref, aim_ref, bbre_ref, bbim_ref):
    def disc(lre, lim, ldt):
        lr = jnp.minimum(lre, -1e-4)
        dt = jnp.exp(ldt)
        mag = jnp.exp(lr * dt)
        return lr, lim, mag * jnp.cos(lim * dt), mag * jnp.sin(lim * dt)

    _, _, a_re, a_im = disc(lre_ref[...], lim_ref[...], ldt_ref[...])
    are_ref[...] = a_re
    aim_ref[...] = a_im
    lr, li, ax_re, ax_im = disc(lrex_ref[...], limx_ref[...], ldtx_ref[...])
    den = lr * lr + li * li
    f_re = ((ax_re - 1.0) * lr + ax_im * li) / den
    f_im = (ax_im * lr - (ax_re - 1.0) * li) / den
    bbre_ref[...] = f_re * bre_ref[...] - f_im * bim_ref[...]
    bbim_ref[...] = f_re * bim_ref[...] + f_im * bre_ref[...]


def _s5_params(lam_re, lam_im, log_dt, b_re, b_im, c_re, c_im):
    g, p, ch = D_GROUPS, D_STATE, D_GROUP
    ldt = jnp.broadcast_to(log_dt[:, None], (g, p))
    rep = lambda a: jnp.repeat(a, ch, axis=1)
    vm = pl.BlockSpec(memory_space=pltpu.VMEM)
    a_re, a_im, bb_re, bb_im = pl.pallas_call(
        _s5_param_kernel,
        out_shape=(jax.ShapeDtypeStruct((g, p), F32), jax.ShapeDtypeStruct((g, p), F32),
                   jax.ShapeDtypeStruct((g, p * ch), F32), jax.ShapeDtypeStruct((g, p * ch), F32)),
        in_specs=[vm] * 8, out_specs=(vm,) * 4,
        name="s5_params",
    )(lam_re, lam_im, ldt, rep(lam_re), rep(lam_im), rep(ldt),
      b_re.reshape(g, p * ch), b_im.reshape(g, p * ch))
    eye = jnp.eye(D_SETS * 4, dtype=F32)

    def pack_b(bb):
        bb = bb.reshape(D_SETS, 16, p, ch)
        return jnp.einsum('sgpi,gh->sgihp', bb, eye).reshape(D_SETS, D_SET_CH, D_SET_ST)

    def pack_c(c):
        c = c.reshape(D_SETS, 16, ch, p)
        return jnp.einsum('sgjp,gh->sgphj', c, eye).reshape(D_SETS, D_SET_ST, D_SET_CH)

    bd = jnp.concatenate([pack_b(bb_re), pack_b(bb_im)], axis=-1).astype(BF16)
    slab = (D_SETS, S5_SLAB, 128)
    return (a_re.reshape(slab), a_im.reshape(slab), bd,
            pack_c(c_re).astype(BF16), pack_c(c_im).astype(BF16))


S5_TM = 256
S5_SLAB = D_SET_ST // 128


def _s5_kernel(u_ref, z_ref, are_ref, aim_ref, bd_ref, cre_ref, cim_ref, dskip_ref, wglu_ref, bglu_ref,
               o_ref, hre_ref, him_ref, y_ref, *x_refs):
    @pl.when(pl.program_id(1) == 0)
    def _():
        hre_ref[...] = jnp.zeros(hre_ref.shape, F32)
        him_ref[...] = jnp.zeros(him_ref.shape, F32)

    xre, xim = x_refs[:D_SETS], x_refs[D_SETS:]
    tm = u_ref.shape[0]
    u = u_ref[...]
    ub = u.astype(BF16)
    for s in range(D_SETS):
        bu = _dot(ub[:, s * D_SET_CH:(s + 1) * D_SET_CH], bd_ref[s])
        for k in range(S5_SLAB):
            xre[s][pl.ds(k, tm, stride=S5_SLAB), :] = bu[:, k * 128:(k + 1) * 128]
            xim[s][pl.ds(k, tm, stride=S5_SLAB), :] = bu[:, D_SET_ST + k * 128:D_SET_ST + (k + 1) * 128]
    a_re = [are_ref[s] for s in range(D_SETS)]
    a_im = [aim_ref[s] for s in range(D_SETS)]

    def step(t, carry):
        rows = pl.ds(pl.multiple_of(t * S5_SLAB, S5_SLAB), S5_SLAB)
        out = []
        for s in range(D_SETS):
            hr, hi = carry[2 * s], carry[2 * s + 1]
            xr = xre[s][rows, :] + a_re[s] * hr - a_im[s] * hi
            xi = xim[s][rows, :] + a_re[s] * hi + a_im[s] * hr
            xre[s][rows, :] = xr
            xim[s][rows, :] = xi
            out += [xr, xi]
        return tuple(out)

    carry = []
    for s in range(D_SETS):
        carry += [hre_ref[s], him_ref[s]]
    carry = lax.fori_loop(0, tm, step, tuple(carry), unroll=8)
    for s in range(D_SETS):
        hre_ref[s] = carry[2 * s]
        him_ref[s] = carry[2 * s + 1]
        x_r = jnp.concatenate([xre[s][pl.ds(k, tm, stride=S5_SLAB), :] for k in range(S5_SLAB)], axis=1)
        x_i = jnp.concatenate([xim[s][pl.ds(k, tm, stride=S5_SLAB), :] for k in range(S5_SLAB)], axis=1)
        y_ref[:, s * D_SET_CH:(s + 1) * D_SET_CH] = (
            _dot(x_r.astype(BF16), cre_ref[s]) - _dot(x_i.astype(BF16), cim_ref[s]))
    y = y_ref[...] + dskip_ref[...] * u
    y = 0.5 * y * (1.0 + jnp.tanh(math.sqrt(2.0 / math.pi) * (y + 0.044715 * (y * y * y))))
    gate = _sigmoid(_dot(y.astype(BF16), wglu_ref[...]) + bglu_ref[...])
    o_ref[...] = (y * gate * _silu(z_ref[...])).astype(BF16)


def _s5(h, bsz, seq, a_re, a_im, bd, cd_re, cd_im, d_skip, w_glu, b_glu):
    tm = min(S5_TM, seq)
    nt = seq // tm
    full = lambda *shape: pl.BlockSpec(shape, lambda b, i: (0,) * len(shape))
    width = D_GROUPS * D_GROUP
    return pl.pallas_call(
        _s5_kernel,
        out_shape=jax.ShapeDtypeStruct((bsz * seq, width), BF16),
        grid=(bsz, nt),
        in_specs=[pl.BlockSpec((tm, width), lambda b, i: (b * nt + i, OD_DU // width)),
                  pl.BlockSpec((tm, width), lambda b, i: (b * nt + i, OD_DZ // width)),
                  full(D_SETS, S5_SLAB, 128), full(D_SETS, S5_SLAB, 128),
                  full(D_SETS, D_SET_CH, 2 * D_SET_ST),
                  full(D_SETS, D_SET_ST, D_SET_CH), full(D_SETS, D_SET_ST, D_SET_CH),
                  full(1, width), full(width, width), full(1, width)],
        out_specs=pl.BlockSpec((tm, width), lambda b, i: (b * nt + i, 0)),
        scratch_shapes=([pltpu.VMEM((D_SETS, S5_SLAB, 128), F32), pltpu.VMEM((D_SETS, S5_SLAB, 128), F32),
                         pltpu.VMEM((tm, width), F32)]
                        + [pltpu.VMEM((tm * S5_SLAB, 128), F32)] * (2 * D_SETS)),
        compiler_params=_cparams(("parallel", "arbitrary")),
        name="s5",
    )(h, h, a_re, a_im, bd, cd_re, cd_im, d_skip.reshape(1, -1), w_glu.astype(BF16),
      b_glu.reshape(1, -1))


def _pack_even_w(w_in):
    sizes = (1024, A_KV_RANK, A_IDX_HEADS * A_IDX_DIM, A_IDX_DIM, A_IDX_HEADS, 1024,
             3 * 1024, B_HEADS, B_HEADS, 1024)
    parts, start = [], 0
    for s in sizes:
        parts.append(w_in[:, start:start + s])
        start += s
    aq, ckv, qi, ki, wi, az, bqkv, ba, bb, bz = parts
    pad = jnp.zeros((w_in.shape[0], EV_WIDTH - start), w_in.dtype)
    return jnp.concatenate([aq, az, bqkv, bz, qi, ckv, ki, wi, ba, bb, pad], axis=1).astype(BF16)


def _even_layer(x2d, bsz, seq, bias, norm_g, w_in, a_q_norm, a_kv_norm, w_kv_up, a_k_norm,
                b_conv, b_a_log, b_dt_bias, b_out_norm, w_out):
    h = _norm_proj(x2d, norm_g, _pack_even_w(w_in), tn=768)
    qta, qit, wt, kidx, katt, vt = _dsa_prep(h, bsz * seq // QB, a_q_norm, a_kv_norm, w_kv_up, a_k_norm)
    mix_a = _dsa_attend(h, bsz, seq, qta, qit, wt, kidx, katt, vt, bias)
    u, w, qd, kd, attn, egl = _gdn_prep(h, bsz, seq, b_conv, b_a_log, b_dt_bias)
    mix_b = _gdn_scan(h, bsz, seq, u, w, qd, kd, attn, egl, b_out_norm)
    return _out_proj(x2d, mix_a, mix_b, w_out)


def _odd_layer(x2d, bsz, seq, cos2, sin2, norm_g, w_in, c_out_norm, lam_re, lam_im, log_dt,
               b_re, b_im, c_re, c_im, d_skip, w_glu, b_glu, w_out):
    h = _norm_proj(x2d, norm_g, w_in.astype(BF16), tn=1024)
    mix_c = _retention(h, bsz, seq, cos2, sin2, c_out_norm)
    s5p = _s5_params(lam_re, lam_im, log_dt, b_re, b_im, c_re, c_im)
    mix_d = _s5(h, bsz, seq, *s5p, d_skip, w_glu, b_glu)
    return _out_proj(x2d, mix_c, mix_d, w_out)


def kernel(x, rel_bias, ev_norm, ev_w_in, ev_a_q_norm, ev_a_kv_norm, ev_w_kv_up, ev_a_k_norm,
           ev_b_conv, ev_b_a_log, ev_b_dt_bias, ev_b_out_norm, ev_w_out,
           od_norm, od_w_in, od_c_out_norm, od_lam_re, od_lam_im, od_log_dt,
           od_b_re, od_b_im, od_c_re, od_c_im, od_d_skip, od_w_glu, od_b_glu, od_w_out):
    bsz, seq, d = x.shape
    depth = ev_norm.shape[0] + od_norm.shape[0]
    x2d = x.reshape(bsz * seq, d)
    bias = _bias_table(rel_bias)
    cos2, sin2 = _rope_tables(seq)
    for layer in range(depth):
        i = layer // 2
        if layer % 2 == 0:
            x2d = _even_layer(x2d, bsz, seq, bias, ev_norm[i], ev_w_in[i], ev_a_q_norm[i],
                              ev_a_kv_norm[i], ev_w_kv_up[i], ev_a_k_norm[i], ev_b_conv[i],
                              ev_b_a_log[i], ev_b_dt_bias[i], ev_b_out_norm[i], ev_w_out[i])
        else:
            x2d = _odd_layer(x2d, bsz, seq, cos2, sin2, od_norm[i], od_w_in[i], od_c_out_norm[i],
                             od_lam_re[i], od_lam_im[i], od_log_dt[i], od_b_re[i], od_b_im[i],
                             od_c_re[i], od_c_im[i], od_d_skip[i], od_w_glu[i], od_b_glu[i], od_w_out[i])
    return x2d.reshape(bsz, seq, d)
```

```python
import functools
import math

import numpy as np
import jax
import jax.numpy as jnp
from jax import lax
from jax.experimental import pallas as pl
from jax.experimental.pallas import tpu as pltpu

F32 = jnp.float32
BF16 = jnp.bfloat16
I32 = jnp.int32

D_MODEL = 1024
EPS = 1e-6
LOG2E = 1.4426950408889634
NEG_INF = float("-inf")
INT_MIN = -(2 ** 31)

A_HEADS, A_HEAD_DIM, A_KV_RANK = 16, 64, 128
A_IDX_HEADS, A_IDX_DIM, A_TOPK_MAX = 8, 64, 256
QB = 128
REL_BUCKETS, REL_MAX_DIST = 32, 128
VT_ROWS = 80
SB = 2 * QB
FAR_BLOCKS = 4
DSA_PREP_TILES = 2
SCORE_ROWS = 4 * QB
BIAS_ROWS = 5 * QB
MASKED = -(2.0 ** 100)
M_INIT = -(2.0 ** 60)
TIE_WALK_MAX = 8
SCORE_UNROLL = 4
COUNT_UNROLL = 4
B_HEADS, B_HEAD_DIM, CONV_K, B_CHUNK = 8, 128, 4, 64
C_HEADS, C_QK_DIM, C_V_DIM, C_CHUNK = 4, 128, 256, 128
ROPE_BASE = 10000.0
D_GROUP, D_STATE, D_GROUPS = 16, 64, 64
D_SETS, D_SET_CH, D_SET_ST = 4, 256, 1024

EV_AQ, EV_AZ, EV_BQ, EV_BK, EV_BV, EV_BZ, EV_QI, EV_CKV, EV_MISC = (
    0, 1024, 2048, 3072, 4096, 5120, 6144, 6656, 6784)
EV_WIDTH = 6912
MISC_KI, MISC_WI, MISC_BA, MISC_BB = 0, 64, 72, 80
OD_CQ, OD_CK, OD_CV, OD_CZ, OD_DU, OD_DZ = 0, 512, 1024, 2048, 3072, 4096
OD_WIDTH = 5120

VMEM_LIMIT = 48 * 1024 * 1024


def _cparams(sem):
    return pltpu.CompilerParams(dimension_semantics=sem, vmem_limit_bytes=VMEM_LIMIT)


def _dot(a, b):
    return jnp.dot(a, b, preferred_element_type=F32)


def _dot_nt(a, b):
    return lax.dot_general(a, b, (((1,), (1,)), ((), ())), preferred_element_type=F32)


def _split_bf16(x, n):
    parts = []
    for term in range(n):
        p = x.astype(BF16)
        parts.append(p)
        if term + 1 < n:
            x = x - p.astype(F32)
    return parts


def _dot_sel(sel, x):
    sel = sel.astype(BF16)
    hi, mid, lo = _split_bf16(x, 3)
    return _dot(sel, hi) + (_dot(sel, mid) + _dot(sel, lo))


def _dot_xsel(x, sel):
    sel = sel.astype(BF16)
    hi, mid, lo = _split_bf16(x, 3)
    return _dot(hi, sel) + (_dot(mid, sel) + _dot(lo, sel))


def _dot_x3(a, b, expand_a=None, expand_b=None):
    ah, al = _split_bf16(a, 2)
    bh, bl = _split_bf16(b, 2)
    if expand_a is not None:
        ah, al = expand_a(ah), expand_a(al)
    if expand_b is not None:
        bh, bl = expand_b(bh), expand_b(bl)
    return _dot(ah, bh) + (_dot(ah, bl) + _dot(al, bh))


def _sigmoid(x):
    return 1.0 / (1.0 + jnp.exp(-x))


def _silu(x):
    return x * _sigmoid(x)


def _norm_proj_kernel(x_ref, g_ref, w_ref, o_ref, xn_ref):
    @pl.when(pl.program_id(1) == 0)
    def _():
        x = x_ref[...]
        ms = jnp.mean(x * x, axis=-1, keepdims=True)
        xn_ref[...] = (x * lax.rsqrt(ms + EPS) * g_ref[...]).astype(BF16)

    o_ref[...] = _dot(xn_ref[...], w_ref[...])


def _norm_proj(x2d, gain, w_bf16, tn, tm=2048):
    t, d = x2d.shape
    n = w_bf16.shape[1]
    tm = min(tm, t)
    return pl.pallas_call(
        _norm_proj_kernel,
        out_shape=jax.ShapeDtypeStruct((t, n), F32),
        grid=(t // tm, n // tn),
        in_specs=[pl.BlockSpec((tm, d), lambda i, j: (i, 0)),
                  pl.BlockSpec((1, d), lambda i, j: (0, 0)),
                  pl.BlockSpec((d, tn), lambda i, j: (0, j))],
        out_specs=pl.BlockSpec((tm, tn), lambda i, j: (i, j)),
        scratch_shapes=[pltpu.VMEM((tm, d), BF16)],
        compiler_params=_cparams(("parallel", "arbitrary")),
        name="norm_proj",
    )(x2d, gain.reshape(1, d), w_bf16)


def _out_proj_kernel(x_ref, a_ref, b_ref, wa_ref, wb_ref, o_ref):
    o_ref[...] = x_ref[...] + _dot(a_ref[...], wa_ref[...]) + _dot(b_ref[...], wb_ref[...])


def _out_proj(x2d, mix_a, mix_b, w_out, tm=512):
    t, d = x2d.shape
    half = mix_a.shape[1]
    tm = min(tm, t)
    wa = w_out[:half].astype(BF16)
    wb = w_out[half:].astype(BF16)
    return pl.pallas_call(
        _out_proj_kernel,
        out_shape=jax.ShapeDtypeStruct((t, d), F32),
        grid=(t // tm,),
        in_specs=[pl.BlockSpec((tm, d), lambda i: (i, 0)),
                  pl.BlockSpec((tm, half), lambda i: (i, 0)),
                  pl.BlockSpec((tm, half), lambda i: (i, 0)),
                  pl.BlockSpec((half, d), lambda i: (0, 0)),
                  pl.BlockSpec((half, d), lambda i: (0, 0))],
        out_specs=pl.BlockSpec((tm, d), lambda i: (i, 0)),
        compiler_params=_cparams(("parallel",)),
        name="out_proj",
    )(x2d, mix_a, mix_b, wa, wb)


def _t5_bucket_starts():
    exact = REL_BUCKETS // 2
    n = np.arange(0, 4 * REL_MAX_DIST, dtype=np.int64)
    ratio = np.maximum(n, 1).astype(np.float32) / np.float32(exact)
    large = exact + (np.log(ratio).astype(np.float32) / np.float32(math.log(REL_MAX_DIST / exact))
                     * np.float32(REL_BUCKETS - exact)).astype(np.int32)
    bucket = np.where(n < exact, n, np.minimum(large, REL_BUCKETS - 1))
    starts = [int(np.argmax(bucket >= b)) for b in range(REL_BUCKETS)]
    assert all(bucket[s] == b for b, s in enumerate(starts)) and np.all(np.diff(bucket) >= 0)
    assert starts[-1] <= QB, "distances beyond one key block must share the last bucket"
    return starts


_BUCKET_STARTS = _t5_bucket_starts()


def _bias_table_kernel(rb_ref, o_ref):
    row = lax.broadcasted_iota(I32, (BIAS_ROWS, QB), 0)
    lane = lax.broadcasted_iota(I32, (BIAS_ROWS, QB), 1)
    dist = lane + 3 * QB - row
    for h in range(A_HEADS):
        val = jnp.full((BIAS_ROWS, QB), rb_ref[0, h], F32)
        for b in range(1, REL_BUCKETS):
            val = jnp.where(dist >= _BUCKET_STARTS[b], rb_ref[b, h], val)
        val = (val - rb_ref[REL_BUCKETS - 1, h]) * LOG2E
        o_ref[:, h * QB:(h + 1) * QB] = jnp.where(dist >= 0, val, 0.0)


def _bias_table(rel_bias):
    return pl.pallas_call(
        _bias_table_kernel,
        out_shape=jax.ShapeDtypeStruct((BIAS_ROWS, A_HEADS * QB), F32),
        in_specs=[pl.BlockSpec(memory_space=pltpu.SMEM)],
        out_specs=pl.BlockSpec(memory_space=pltpu.VMEM),
        name="dsa_bias_table",
    )(rel_bias)


def _dsa_prep_kernel(aq_ref, qi_ref, ckv_ref, misc_ref, gq_ref, hsum_ref, gkv_ref, wkv_ref, gk_ref,
                     qta_ref, qit_ref, wt_ref, kidx_ref, katt_ref, vt_ref):
    zeros = jnp.zeros((QB, A_HEAD_DIM), F32)
    gq = gq_ref[...]
    ident = jnp.where(lax.broadcasted_iota(I32, (QB, QB), 0) == lax.broadcasted_iota(I32, (QB, QB), 1),
                      1.0, 0.0).astype(BF16)
    ones_row = lax.broadcasted_iota(I32, (VT_ROWS - A_HEAD_DIM, QB), 0) == 0

    def transpose_b(x):
        return _dot_nt(ident, x.astype(BF16)).astype(BF16)

    for t in range(DSA_PREP_TILES):
        rs = slice(t * QB, (t + 1) * QB)
        aq = aq_ref[rs, :]
        sq_hi, sq_lo = _split_bf16(aq * aq, 2)
        ssq = _dot(sq_hi, hsum_ref[...]) + _dot(sq_lo, hsum_ref[...])
        qn_all = aq * lax.rsqrt(ssq * (1.0 / A_HEAD_DIM) + EPS) * gq * (A_HEAD_DIM ** -0.5 * LOG2E)
        for h in range(A_HEADS):
            qn = qn_all[:, h * A_HEAD_DIM:(h + 1) * A_HEAD_DIM]
            qta_ref[t, :, h * QB:(h + 1) * QB] = transpose_b(jnp.concatenate([qn, zeros], axis=1))
        qi = qi_ref[rs, :]
        for h in range(A_IDX_HEADS):
            qih = qi[:, h * A_IDX_DIM:(h + 1) * A_IDX_DIM]
            qit_ref[t, :, h * QB:(h + 1) * QB] = transpose_b(jnp.concatenate([qih, zeros], axis=1))
        misc = misc_ref[rs, :]
        wt_ref[t] = misc.T[MISC_WI:MISC_WI + A_IDX_HEADS, :] * (A_IDX_HEADS ** -0.5 * A_IDX_DIM ** -0.5)
        c = ckv_ref[rs, :]
        cn = c * lax.rsqrt(jnp.mean(c * c, axis=-1, keepdims=True) + EPS) * gkv_ref[...]
        kv = _dot(cn.astype(BF16), wkv_ref[...])
        k = kv[:, :A_HEAD_DIM]
        kn = k * lax.rsqrt(jnp.mean(k * k, axis=-1, keepdims=True) + EPS) * gk_ref[...]
        kidx_ref[t] = jnp.concatenate([misc[:, MISC_KI:MISC_KI + A_IDX_DIM], zeros], axis=1).astype(BF16)
        katt_ref[t] = jnp.concatenate([kn, zeros], axis=1).astype(BF16)
        vt_ref[t, 0:A_HEAD_DIM, :] = transpose_b(kv)[A_HEAD_DIM:, :]
        vt_ref[t, A_HEAD_DIM:, :] = jnp.where(ones_row, 1.0, 0.0).astype(BF16)


def _dsa_prep(h, nblk, a_q_norm, a_kv_norm, w_kv_up, a_k_norm):
    cb = lambda width, off: off // width
    nt = DSA_PREP_TILES
    width = A_HEADS * A_HEAD_DIM
    head_of = np.arange(width) // A_HEAD_DIM
    same_head = (head_of[:, None] == head_of[None, :]).astype(np.float32)
    return pl.pallas_call(
        _dsa_prep_kernel,
        out_shape=(jax.ShapeDtypeStruct((nblk, QB, A_HEADS * QB), BF16),
                   jax.ShapeDtypeStruct((nblk, QB, A_IDX_HEADS * QB), BF16),
                   jax.ShapeDtypeStruct((nblk, A_IDX_HEADS, QB), F32),
                   jax.ShapeDtypeStruct((nblk, QB, 128), BF16),
                   jax.ShapeDtypeStruct((nblk, QB, 128), BF16),
                   jax.ShapeDtypeStruct((nblk, VT_ROWS, QB), BF16)),
        grid=(nblk // nt,),
        in_specs=[pl.BlockSpec((nt * QB, 1024), lambda i: (i, cb(1024, EV_AQ))),
                  pl.BlockSpec((nt * QB, 512), lambda i: (i, cb(512, EV_QI))),
                  pl.BlockSpec((nt * QB, 128), lambda i: (i, cb(128, EV_CKV))),
                  pl.BlockSpec((nt * QB, 128), lambda i: (i, cb(128, EV_MISC))),
                  pl.BlockSpec((1, width), lambda i: (0, 0)),
                  pl.BlockSpec((width, width), lambda i: (0, 0)),
                  pl.BlockSpec((1, A_KV_RANK), lambda i: (0, 0)),
                  pl.BlockSpec((A_KV_RANK, 2 * A_HEAD_DIM), lambda i: (0, 0)),
                  pl.BlockSpec((1, A_HEAD_DIM), lambda i: (0, 0))],
        out_specs=(pl.BlockSpec((nt, QB, A_HEADS * QB), lambda i: (i, 0, 0)),
                   pl.BlockSpec((nt, QB, A_IDX_HEADS * QB), lambda i: (i, 0, 0)),
                   pl.BlockSpec((nt, A_IDX_HEADS, QB), lambda i: (i, 0, 0)),
                   pl.BlockSpec((nt, QB, 128), lambda i: (i, 0, 0)),
                   pl.BlockSpec((nt, QB, 128), lambda i: (i, 0, 0)),
                   pl.BlockSpec((nt, VT_ROWS, QB), lambda i: (i, 0, 0))),
        compiler_params=_cparams(("parallel",)),
        name="dsa_prep",
    )(h, h, h, h, jnp.tile(a_q_norm, A_HEADS).reshape(1, width), jnp.asarray(same_head, BF16),
      a_kv_norm.reshape(1, -1), w_kv_up.astype(BF16), a_k_norm.reshape(1, -1))


def _dsa_kernel(topk, qit_ref, wt_ref, qta_ref, kidx_ref, katt_ref, vt_ref, bias_ref, az_ref, o_ref,
                strip_ref, x_ref, m_ref, acc_ref, lg_ref, mx_ref, p_ref, rhs_ref):
    qt = pl.program_id(1)
    t0 = qt * QB
    rhs_ref[0:QB, :] = qta_ref[...]

    @pl.when((pl.program_id(0) == 0) & (qt == 0))
    def _():
        ident = jnp.where(lax.broadcasted_iota(I32, (QB, QB), 0) == lax.broadcasted_iota(I32, (QB, QB), 1),
                          1.0, 0.0).astype(BF16)
        for h in range(A_HEADS):
            rhs_ref[QB:2 * QB, h * QB:(h + 1) * QB] = ident

    n_sc = qt // 4 + 1
    n_sb = qt // 2 + 1
    lane_s = lax.broadcasted_iota(I32, (SCORE_ROWS, QB), 1)
    row_s = lax.broadcasted_iota(I32, (SCORE_ROWS, QB), 0)

    w = wt_ref[...]

    def score_body(j, carry):
        kblk = kidx_ref[pl.ds(j * 4, 4)].reshape(SCORE_ROWS, 128)
        tot = None
        for hp in range(A_IDX_HEADS // 2):
            s = _dot(kblk, qit_ref[:, hp * SB:(hp + 1) * SB])
            for e in range(2):
                h = 2 * hp + e
                term = jnp.maximum(s[:, e * QB:(e + 1) * QB], 0.0) * w[h:h + 1, :]
                tot = term if tot is None else tot + term
        tot = jnp.where(j * SCORE_ROWS + row_s <= t0 + lane_s, tot, NEG_INF)
        bits = pltpu.bitcast(tot, I32)
        strip_ref[pl.ds(pl.multiple_of(j * SCORE_ROWS, SCORE_ROWS), SCORE_ROWS), :] = (
            bits ^ ((bits >> 31) & 0x7FFFFFFF))
        return carry

    def score_group(jj, carry):
        for sub in range(SCORE_UNROLL):
            score_body(SCORE_UNROLL * jj + sub, carry)
        return carry

    lax.fori_loop(0, n_sc // SCORE_UNROLL, score_group, 0)
    lax.fori_loop(SCORE_UNROLL * (n_sc // SCORE_UNROLL), n_sc, score_body, 0)

    def count(pred):
        def one(j, acc):
            r0 = pl.multiple_of(j * SCORE_ROWS, SCORE_ROWS)
            m = jnp.where(pred(strip_ref[pl.ds(r0, SCORE_ROWS), :], r0), 1, 0)
            return acc + m.reshape(SCORE_ROWS // 32, 32, QB).sum(axis=0)

        def body(j, acc):
            for sub in range(COUNT_UNROLL):
                acc = one(COUNT_UNROLL * j + sub, acc)
            return acc
        n_main = n_sc // COUNT_UNROLL
        acc = lax.fori_loop(0, n_main, body, jnp.zeros((32, QB), I32))
        acc = lax.fori_loop(n_main * COUNT_UNROLL, n_sc, one, acc)
        return acc.reshape(4, 8, QB).sum(axis=0).sum(axis=0, keepdims=True)

    count_ge = lambda cand: count(lambda key, r0: key >= cand)
    c0 = count_ge(jnp.zeros((1, QB), I32))
    nonneg = c0 >= topk
    thr0 = jnp.where(nonneg, 0, INT_MIN)
    cnt0 = jnp.where(nonneg, c0, n_sc * SCORE_ROWS)

    def bit_steps(counter, n):
        def one(_, carry):
            i, thr, cnt = carry
            cand = thr | lax.shift_left(jnp.int32(1), 30 - i)
            c = counter(cand)
            ok = c >= topk
            return i + 1, jnp.where(ok, cand, thr), jnp.where(ok, c, cnt)
        return lambda carry: lax.fori_loop(0, n, one, carry)

    state = bit_steps(count_ge, 15)((jnp.int32(0), thr0, cnt0))
    _, thr, cnt = lax.while_loop(lambda c: (c[0] < 31) & (jnp.max(c[2]) > topk),
                                 bit_steps(count_ge, 4), state)

    x_ref[...] = jnp.full((1, QB), 2 ** 30, I32)
    tied = cnt > topk

    drop = jnp.where(tied, cnt - topk, 0)
    max_drop = jnp.max(drop)

    @pl.when((max_drop > 0) & (max_drop <= TIE_WALK_MAX))
    def _():
        def prev_tie(bound):
            def body(j, acc):
                r0 = pl.multiple_of(j * SCORE_ROWS, SCORE_ROWS)
                idx = r0 + row_s
                hit = (strip_ref[pl.ds(r0, SCORE_ROWS), :] == thr) & (idx < bound)
                return jnp.maximum(acc, jnp.where(hit, idx, -1)
                                   .reshape(SCORE_ROWS // 32, 32, QB).max(axis=0))
            acc = lax.fori_loop(0, n_sc, body, jnp.full((32, QB), -1, I32))
            return acc.reshape(4, 8, QB).max(axis=0).max(axis=0, keepdims=True)

        x_ref[...] = lax.fori_loop(0, max_drop, lambda r, bound: jnp.where(r < drop, prev_tie(bound), bound),
                                   jnp.full((1, QB), 2 ** 30, I32))

    @pl.when(max_drop > TIE_WALK_MAX)
    def _():
        above = count(lambda key, r0: key > thr)
        need = jnp.where(tied, topk - above, 0)

        def tie_body(i, xb):
            cand = xb | lax.shift_left(jnp.int32(1), 14 - i)
            c = count(lambda key, r0: (key == thr) & (r0 + row_s < cand))
            return jnp.where(c <= need, cand, xb)

        xb = lax.fori_loop(0, 15, tie_body, jnp.zeros((1, QB), I32))
        x_ref[...] = jnp.where(tied, xb, 2 ** 30)

    xb = x_ref[...]

    m_ref[...] = jnp.full(m_ref.shape, M_INIT, F32)
    acc_ref[...] = jnp.zeros(acc_ref.shape, F32)
    n_lt = A_HEADS * QB // SB
    row_a = {nk: lax.broadcasted_iota(I32, (nk * QB, QB), 0) for nk in (2, FAR_BLOCKS)}
    lane_a = lax.broadcasted_iota(I32, (SB, QB), 1)

    def key_operand(j, nk, near):
        rows = nk * QB
        r0 = pl.multiple_of(j * rows, rows)
        key = strip_ref[pl.ds(r0, rows), :]
        idx = r0 + row_a[nk]
        sel = (key > thr) | ((key == thr) & (idx < xb))
        if near:
            sel = sel & (idx <= t0 + lane_a)
        pen = jnp.where(sel, 0.0, MASKED).astype(BF16)
        return jnp.concatenate([katt_ref[pl.ds(nk * j, nk)].reshape(rows, 128), pen], axis=1)

    def values(j, nk):
        return jnp.concatenate([vt_ref[nk * j + i] for i in range(nk)], axis=1)

    def new_max(slot):
        m_old = m_ref[...]
        m_new = jnp.maximum(m_old, mx_ref[slot])
        m_ref[...] = m_new
        return m_new.astype(BF16), jnp.exp2(m_old - m_new)

    def logits(j, slot, near, nk=2):
        rows = nk * QB
        kaug = key_operand(j, nk, near)
        if near:
            b0 = pl.multiple_of((2 * j - qt + 3) * QB, QB)
        for lt in range(n_lt):
            cols = slice(lt * SB, (lt + 1) * SB)
            lg = _dot(kaug, rhs_ref[:, cols])
            if near:
                lg = lg + bias_ref[pl.ds(b0, SB), cols]
            lg = lg.astype(BF16)
            lg_ref[slot, 0:rows, cols] = lg
            mx_ref[slot, :, cols] = jnp.max(lg, axis=0, keepdims=True).astype(F32)

    def accumulate(j, slot, nk=2):
        rows = nk * QB
        vt = values(j, nk)
        m_b, alpha = new_max(slot)
        for lt in range(n_lt):
            cols = slice(lt * SB, (lt + 1) * SB)
            p_ref[0:rows, cols] = jnp.exp2(lg_ref[slot, 0:rows, cols] - m_b[:, cols])
        acc_ref[...] = alpha * acc_ref[...] + _dot(vt, p_ref[0:rows, :])

    nk = FAR_BLOCKS
    n_far = jnp.maximum(n_sb - 2, 0)
    n_big = n_far // (nk // 2)
    n_pair = jnp.maximum(n_big - 1, 0) // 2
    n_rem = n_big - 2 * n_pair

    @pl.when(n_big > 0)
    def _():
        logits(0, 0, False, nk)

    def fused(j, slot):
        nxt = 1 - slot
        rows = nk * QB
        kaug = key_operand(j + 1, nk, False)
        vt = values(j, nk)
        m_b, alpha = new_max(slot)
        for lt in range(n_lt):
            cols = slice(lt * SB, (lt + 1) * SB)
            lg = _dot(kaug, rhs_ref[:, cols]).astype(BF16)
            lg_ref[nxt, 0:rows, cols] = lg
            mx_ref[nxt, :, cols] = jnp.max(lg, axis=0, keepdims=True).astype(F32)
            p = jnp.exp2(lg_ref[slot, 0:rows, cols] - m_b[:, cols])
            acc_ref[:, cols] = alpha[:, cols] * acc_ref[:, cols] + _dot(vt, p)

    def pair_body(jj, carry):
        j = 2 * jj
        fused(j, 0)
        fused(j + 1, 1)
        return carry

    lax.fori_loop(0, n_pair, pair_body, 0)

    @pl.when(n_rem == 1)
    def _():
        accumulate(n_big - 1, 0, nk)

    @pl.when(n_rem == 2)
    def _():
        fused(n_big - 2, 0)
        accumulate(n_big - 1, 1, nk)

    @pl.when(n_far % (nk // 2) == 1)
    def _():
        logits(n_far - 1, 0, False)
        accumulate(n_far - 1, 0)

    @pl.when(n_sb >= 2)
    def _():
        logits(n_sb - 2, 0, True)
        accumulate(n_sb - 2, 0)

    logits(n_sb - 1, 0, True)
    accumulate(n_sb - 1, 0)

    acc = acc_ref[...]
    o_t = acc[0:A_HEAD_DIM, :] / acc[A_HEAD_DIM:A_HEAD_DIM + 1, :]
    pad = jnp.zeros((QB - A_HEAD_DIM, QB), F32)
    pieces = []
    for h in range(A_HEADS):
        blk = jnp.concatenate([o_t[:, h * QB:(h + 1) * QB], pad], axis=0)
        pieces.append(blk.T[:, 0:A_HEAD_DIM])
    att = jnp.concatenate(pieces, axis=1)
    o_ref[...] = (att * _silu(az_ref[...])).astype(BF16)


def _dsa_attend(h, bsz, seq, qta, qit, wt, kidx, katt, vt, bias):
    nq = seq // QB
    topk = min(A_TOPK_MAX, seq // 4)
    blk = lambda b, q: (b * nq + q, 0, 0)
    per_batch = lambda b, q: (b, 0, 0)
    once = pl.Buffered(1)
    return pl.pallas_call(
        functools.partial(_dsa_kernel, topk),
        out_shape=jax.ShapeDtypeStruct((bsz * seq, A_HEADS * A_HEAD_DIM), BF16),
        grid=(bsz, nq),
        in_specs=[pl.BlockSpec((None, QB, A_IDX_HEADS * QB), blk),
                  pl.BlockSpec((None, A_IDX_HEADS, QB), blk),
                  pl.BlockSpec((None, QB, A_HEADS * QB), blk),
                  pl.BlockSpec((nq, QB, 128), per_batch, pipeline_mode=once),
                  pl.BlockSpec((nq, QB, 128), per_batch, pipeline_mode=once),
                  pl.BlockSpec((nq, VT_ROWS, QB), per_batch, pipeline_mode=once),
                  pl.BlockSpec((BIAS_ROWS, A_HEADS * QB), lambda b, q: (0, 0), pipeline_mode=once),
                  pl.BlockSpec((QB, 1024), lambda b, q: (b * nq + q, EV_AZ // 1024))],
        out_specs=pl.BlockSpec((QB, 1024), lambda b, q: (b * nq + q, 0)),
        scratch_shapes=[pltpu.VMEM((seq, QB), I32),
                        pltpu.VMEM((1, QB), I32),
                        pltpu.VMEM((1, A_HEADS * QB), F32),
                        pltpu.VMEM((VT_ROWS, A_HEADS * QB), F32),
                        pltpu.VMEM((2, FAR_BLOCKS * QB, A_HEADS * QB), BF16),
                        pltpu.VMEM((2, 1, A_HEADS * QB), F32),
                        pltpu.VMEM((FAR_BLOCKS * QB, A_HEADS * QB), BF16),
                        pltpu.VMEM((2 * QB, A_HEADS * QB), BF16)],
        compiler_params=_cparams(("arbitrary", "arbitrary")),
        name="dsa_attend",
    )(qit, wt, qta, kidx, katt, vt, bias, h)


GDN_TM = 1024
GDN_GROUP = 4
GDN_SCAN_HEADS = 8
GDN_PREP_HEADS = 2


def _gdn_prep_kernel(alog_ref, dtb_ref, q_ref, k_ref, v_ref, qh_ref, kh_ref, vh_ref, misc_ref,
                     cq_ref, ck_ref, cv_ref,
                     u_ref, w_ref, qd_ref, kdt_ref, attn_ref, egl_ref):
    i = pl.program_id(1)
    hp = pl.program_id(2)
    tm = q_ref.shape[0]
    nchunk = tm // B_CHUNK
    heads = range(GDN_PREP_HEADS)
    lanes = [slice(hh * 128, (hh + 1) * 128) for hh in heads]

    def conv_silu(x_ref, halo_ref, c_ref, ln):
        halo = jnp.where(i > 0, halo_ref[:, ln], 0.0)
        ext = jnp.concatenate([halo, x_ref[:, ln]], axis=0)
        cw = c_ref[:, ln]
        y = ext[8:, :] * cw[CONV_K - 1:CONV_K, :]
        for d in range(1, CONV_K):
            y = y + pltpu.roll(ext, d, 0)[8:, :] * cw[CONV_K - 1 - d:CONV_K - d, :]
        return _silu(y)

    q = [conv_silu(q_ref, qh_ref, cq_ref, ln) for ln in lanes]
    k = [conv_silu(k_ref, kh_ref, ck_ref, ln) for ln in lanes]
    v = [conv_silu(v_ref, vh_ref, cv_ref, ln) for ln in lanes]
    qn = [x * lax.rsqrt(jnp.sum(x * x, axis=-1, keepdims=True) + EPS) * (B_HEAD_DIM ** -0.5) for x in q]
    kn = [x * lax.rsqrt(jnp.sum(x * x, axis=-1, keepdims=True) + EPS) for x in k]

    misc = misc_ref[...]
    sel_r = lax.broadcasted_iota(I32, (128, 256), 0)
    sel_c = lax.broadcasted_iota(I32, (128, 256), 1)
    beta, g = [], []
    for hh in heads:
        hd = hp * GDN_PREP_HEADS + hh
        pick = jnp.where(sel_r == jnp.where(sel_c < 128, MISC_BA + hd, MISC_BB + hd), 1.0, 0.0)
        bab = _dot_xsel(misc, pick)
        beta.append(_sigmoid(bab[:, 128:256]))
        xg = bab[:, 0:128] + dtb_ref[hd]
        softplus = jnp.maximum(xg, 0.0) + jnp.log(1.0 + jnp.exp(-jnp.abs(xg)))
        g.append(-jnp.exp(jnp.full((1, 128), alog_ref[hd], F32)) * softplus)

    sl = 256
    r2 = lax.broadcasted_iota(I32, (sl, sl), 0)
    c2 = lax.broadcasted_iota(I32, (sl, sl), 1)
    same = (r2 // B_CHUNK) == (c2 // B_CHUNK)
    tri_blk = jnp.concatenate([jnp.where(same & (c2 <= r2), 1.0, 0.0), jnp.where(same, 1.0, 0.0)], axis=0)
    nsl = tm // sl
    gc, gl = [], []
    for hh in heads:
        both = _dot_sel(tri_blk, jnp.concatenate([g[hh][s * sl:(s + 1) * sl, :] for s in range(nsl)], axis=1))
        gc.append(jnp.concatenate([both[0:sl, s * 128:(s + 1) * 128] for s in range(nsl)], axis=0))
        gl.append(jnp.concatenate([both[sl:2 * sl, s * 128:(s + 1) * 128] for s in range(nsl)], axis=0))

    kb = [kn[hh] * beta[hh] for hh in heads]
    vb = [v[hh] * beta[hh] for hh in heads]
    egc = [jnp.exp(x) for x in gc]
    kbg = [kb[hh] * egc[hh] for hh in heads]
    for hh in heads:
        qd_ref[hh] = (qn[hh] * egc[hh]).astype(BF16)
        kd = kn[hh] * jnp.exp(gl[hh] - gc[hh])
        for m in range(tm // 128):
            kdt_ref[hh, :, m * 128:(m + 1) * 128] = kd[m * 128:(m + 1) * 128, :].T.astype(BF16)

    gw = GDN_GROUP * B_CHUNK
    ri = lax.broadcasted_iota(I32, (B_CHUNK, gw), 0)
    lj = lax.broadcasted_iota(I32, (B_CHUNK, gw), 1)
    lb = lj // B_CHUNK
    lj = lj % B_CHUNK
    bmask = (lax.broadcasted_iota(I32, (gw, gw), 0) // B_CHUNK
             == lax.broadcasted_iota(I32, (gw, gw), 1) // B_CHUNK)

    def fold(x):
        xm = jnp.where(bmask, x, 0.0)
        out = xm[0:B_CHUNK, :]
        for c in range(1, GDN_GROUP):
            out = out + xm[c * B_CHUNK:(c + 1) * B_CHUNK, :]
        return out

    bmask_b = jnp.where(bmask, 1.0, 0.0).astype(BF16)

    def bdiag(cat):
        return jnp.concatenate([cat] * GDN_GROUP, axis=0) * bmask_b

    probs = [(hh, grp) for grp in range(tm // gw) for hh in heads]
    a_cat = []
    for hh, grp in probs:
        rs = slice(grp * gw, (grp + 1) * gw)
        gcs = gc[hh][rs, :]
        col = jnp.concatenate([gcs[0:B_CHUNK, :]] * 2, axis=1)
        for c in range(1, GDN_GROUP):
            col = jnp.where(lb == c, jnp.concatenate([gcs[c * B_CHUNK:(c + 1) * B_CHUNK, :]] * 2, axis=1), col)
        rowf = jnp.concatenate([gcs[m * 128:(m + 1) * 128, :].T[0:B_CHUNK, :] for m in range(gw // 128)],
                               axis=1)
        decay = jnp.exp(jnp.where(ri >= lj, col - rowf, NEG_INF))
        kg = kn[hh][rs, :]
        a_cat.append(jnp.where(ri > lj, fold(_dot_nt(kb[hh][rs, :], kg)) * decay, 0.0))
        attn_ref[hh, grp] = jnp.where(ri >= lj, fold(_dot_nt(qn[hh][rs, :], kg)) * decay, 0.0).astype(BF16)
    eye = jnp.where(ri == lj, 1.0, 0.0)
    inv = [eye - a for a in a_cat]
    pw = [_dot_x3(a, a, expand_b=bdiag) for a in a_cat]
    for step in range(5):
        for p in range(len(probs)):
            if step < 4:
                both = _dot_x3(jnp.concatenate([inv[p], pw[p]], axis=0), pw[p], expand_b=bdiag)
                inv[p] = inv[p] + both[0:B_CHUNK, :]
                pw[p] = both[B_CHUNK:, :]
            else:
                inv[p] = inv[p] + _dot_x3(inv[p], pw[p], expand_b=bdiag)
    for p, (hh, grp) in enumerate(probs):
        rs = slice(grp * gw, (grp + 1) * gw)
        sol = _dot_x3(inv[p], jnp.concatenate([vb[hh][rs, :], kbg[hh][rs, :]], axis=1), expand_a=bdiag)
        u_ref[hh, rs, :] = sol[:, 0:B_HEAD_DIM]
        w_ref[hh, rs, :] = sol[:, B_HEAD_DIM:].astype(BF16)
    for hh in heads:
        for c in range(nchunk):
            egl_ref[hh, c:c + 1, :] = jnp.exp(gl[hh][c * B_CHUNK:c * B_CHUNK + 1, :])
        if nchunk < 8:
            egl_ref[hh, nchunk:, :] = jnp.zeros((8 - nchunk, 128), F32)


def _gdn_prep(h, bsz, seq, b_conv, b_a_log, b_dt_bias):
    tm = min(GDN_TM, seq)
    nt = seq // tm
    nrow = max(tm // B_CHUNK, 8)
    gw = GDN_GROUP * B_CHUNK
    hs = (bsz, B_HEADS, seq, B_HEAD_DIM)
    hg = GDN_PREP_HEADS
    hw = hg * 128
    cur = lambda off: (lambda b, i, hp: (b * nt + i, off // hw + hp))
    halo = lambda off: (lambda b, i, hp: (jnp.maximum((b * nt + i) * (tm // 8) - 1, 0), off // hw + hp))
    cw = lambda off: (lambda b, i, hp: (0, off // hw + hp))
    out = lambda b, i, hp: (b, hp, i, 0)
    smem = pl.BlockSpec(memory_space=pltpu.SMEM)
    return pl.pallas_call(
        _gdn_prep_kernel,
        out_shape=(jax.ShapeDtypeStruct(hs, F32), jax.ShapeDtypeStruct(hs, BF16),
                   jax.ShapeDtypeStruct(hs, BF16),
                   jax.ShapeDtypeStruct((bsz, B_HEADS, B_HEAD_DIM, seq), BF16),
                   jax.ShapeDtypeStruct((bsz, B_HEADS, seq // gw, B_CHUNK, gw), BF16),
                   jax.ShapeDtypeStruct((bsz, B_HEADS, nt * nrow, 128), F32)),
        grid=(bsz, nt, B_HEADS // hg),
        in_specs=[smem, smem,
                  pl.BlockSpec((tm, hw), cur(EV_BQ)), pl.BlockSpec((tm, hw), cur(EV_BK)),
                  pl.BlockSpec((tm, hw), cur(EV_BV)),
                  pl.BlockSpec((8, hw), halo(EV_BQ)), pl.BlockSpec((8, hw), halo(EV_BK)),
                  pl.BlockSpec((8, hw), halo(EV_BV)),
                  pl.BlockSpec((tm, 128), lambda b, i, hp: (b * nt + i, EV_MISC // 128)),
                  pl.BlockSpec((CONV_K, hw), cw(0)), pl.BlockSpec((CONV_K, hw), cw(1024)),
                  pl.BlockSpec((CONV_K, hw), cw(2048))],
        out_specs=(pl.BlockSpec((None, hg, tm, 128), out), pl.BlockSpec((None, hg, tm, 128), out),
                   pl.BlockSpec((None, hg, tm, 128), out),
                   pl.BlockSpec((None, hg, B_HEAD_DIM, tm), lambda b, i, hp: (b, hp, 0, i)),
                   pl.BlockSpec((None, hg, tm // gw, B_CHUNK, gw), lambda b, i, hp: (b, hp, i, 0, 0)),
                   pl.BlockSpec((None, hg, nrow, 128), out)),
        compiler_params=_cparams(("parallel", "parallel", "parallel")),
        name="gdn_prep",
    )(b_a_log, b_dt_bias, h, h, h, h, h, h, h, b_conv, b_conv, b_conv)


def _gdn_scan_kernel(u_ref, w_ref, qd_ref, kdt_ref, attn_ref, egl_ref, z_ref, gn_ref, o_ref, s_ref):
    @pl.when(pl.program_id(2) == 0)
    def _():
        s_ref[...] = jnp.zeros(s_ref.shape, F32)

    tm = u_ref.shape[1]
    gw = GDN_GROUP * B_CHUNK
    gain = gn_ref[...]
    zero = jnp.zeros((B_CHUNK, B_HEAD_DIM), BF16)
    states = [s_ref[hh] for hh in range(GDN_SCAN_HEADS)]
    for c in range(tm // B_CHUNK):
        grp, ci = divmod(c, GDN_GROUP)
        rs = slice(c * B_CHUNK, (c + 1) * B_CHUNK)
        for hh in range(GDN_SCAN_HEADS):
            sb = states[hh].astype(BF16)
            v_new = u_ref[hh, rs, :] - _dot(w_ref[hh, rs, :], sb)
            vpad = jnp.concatenate([zero] * ci + [v_new.astype(BF16)] + [zero] * (GDN_GROUP - 1 - ci), axis=0)
            o = _dot(qd_ref[hh, rs, :], sb) + _dot(attn_ref[hh, grp], vpad)
            states[hh] = (states[hh] * egl_ref[hh, c:c + 1, :]
                          + _dot(kdt_ref[hh, :, grp * gw:(grp + 1) * gw], vpad))
            on = o * lax.rsqrt(jnp.mean(o * o, axis=-1, keepdims=True) + EPS) * gain
            lanes = slice(hh * B_HEAD_DIM, (hh + 1) * B_HEAD_DIM)
            o_ref[rs, lanes] = (on * _silu(z_ref[rs, lanes])).astype(BF16)
    for hh in range(GDN_SCAN_HEADS):
        s_ref[hh] = states[hh]


def _gdn_scan(h, bsz, seq, u, w, qd, kd, attn, egl, b_out_norm):
    tm = min(GDN_TM, seq)
    nt = seq // tm
    nrow = max(tm // B_CHUNK, 8)
    gw = GDN_GROUP * B_CHUNK
    hg = GDN_SCAN_HEADS
    blk = lambda b, hp, i: (b, hp, i, 0)
    return pl.pallas_call(
        _gdn_scan_kernel,
        out_shape=jax.ShapeDtypeStruct((bsz * seq, B_HEADS * B_HEAD_DIM), BF16),
        grid=(bsz, B_HEADS // hg, nt),
        in_specs=[pl.BlockSpec((None, hg, tm, 128), blk), pl.BlockSpec((None, hg, tm, 128), blk),
                  pl.BlockSpec((None, hg, tm, 128), blk),
                  pl.BlockSpec((None, hg, B_HEAD_DIM, tm), lambda b, hp, i: (b, hp, 0, i)),
                  pl.BlockSpec((None, hg, tm // gw, B_CHUNK, gw), lambda b, hp, i: (b, hp, i, 0, 0)),
                  pl.BlockSpec((None, hg, nrow, 128), blk),
                  pl.BlockSpec((tm, hg * 128), lambda b, hp, i: (b * nt + i, EV_BZ // (hg * 128) + hp)),
                  pl.BlockSpec((1, 128), lambda b, hp, i: (0, 0))],
        out_specs=pl.BlockSpec((tm, hg * 128), lambda b, hp, i: (b * nt + i, hp)),
        scratch_shapes=[pltpu.VMEM((hg, B_HEAD_DIM, B_HEAD_DIM), F32)],
        compiler_params=_cparams(("parallel", "parallel", "arbitrary")),
        name="gdn_scan",
    )(u, w, qd, kd, attn, egl, h, b_out_norm.reshape(1, -1))


def _rope_kernel(inv_ref, cos_ref, sin_ref):
    tm = cos_ref.shape[0]
    pos = (pl.program_id(0) * tm + lax.broadcasted_iota(I32, (tm, 128), 0)).astype(F32)
    ang = pos * inv_ref[...]
    lane = lax.broadcasted_iota(I32, (tm, 128), 1)
    cos_ref[...] = jnp.cos(ang)
    sin_ref[...] = jnp.where(lane < 64, -jnp.sin(ang), jnp.sin(ang))


def _rope_tables(seq):
    half = C_QK_DIM // 2
    inv = 1.0 / (ROPE_BASE ** jnp.linspace(0.0, 1.0, half, dtype=F32))
    inv2 = jnp.concatenate([inv, inv]).reshape(1, C_QK_DIM)
    tm = min(1024, seq)
    return pl.pallas_call(
        _rope_kernel,
        out_shape=(jax.ShapeDtypeStruct((seq, C_QK_DIM), F32),) * 2,
        grid=(seq // tm,),
        in_specs=[pl.BlockSpec((1, C_QK_DIM), lambda i: (0, 0))],
        out_specs=(pl.BlockSpec((tm, C_QK_DIM), lambda i: (i, 0)),) * 2,
        compiler_params=_cparams(("parallel",)),
        name="rope_tables",
    )(inv2)


RET_TM = 4096


def _retention_kernel(lg_ref, q_ref, k_ref, v_ref, z_ref, cos_ref, sin_ref, gn_ref, o_ref,
                      r_ref, dm_ref, zeta_ref, xi_ref):
    hd = pl.program_id(1)
    lg = lg_ref[0, hd]

    @pl.when(pl.program_id(2) == 0)
    def _():
        r_ref[...] = jnp.zeros(r_ref.shape, F32)
        ri = lax.broadcasted_iota(I32, (C_CHUNK, C_CHUNK), 0)
        ci = lax.broadcasted_iota(I32, (C_CHUNK, C_CHUNK), 1)
        diff = (ri - ci).astype(F32)
        dm_ref[...] = jnp.where(diff >= 0, jnp.exp(jnp.maximum(diff, 0.0) * lg), 0.0)
        zeta_ref[...] = jnp.exp((C_CHUNK - 1 - ri).astype(F32) * lg)
        rv = lax.broadcasted_iota(I32, (C_CHUNK, C_V_DIM), 0).astype(F32)
        xi_ref[...] = jnp.exp((rv + 1.0) * lg)

    g_chunk = lg_ref[1, hd]
    tm = q_ref.shape[0]
    state = r_ref[...]
    dmask = dm_ref[...]
    gain = gn_ref[...]
    for c in range(tm // C_CHUNK):
        rs = slice(c * C_CHUNK, (c + 1) * C_CHUNK)
        cos = cos_ref[rs, :]
        sin = sin_ref[rs, :]
        q = q_ref[rs, :]
        k = k_ref[rs, :]
        qr = q * cos + pltpu.roll(q, C_QK_DIM // 2, 1) * sin
        kr = (k * cos + pltpu.roll(k, C_QK_DIM // 2, 1) * sin) * (C_QK_DIM ** -0.5)
        vb = v_ref[rs, :].astype(BF16)
        qb = qr.astype(BF16)
        s = _dot_nt(qb, kr.astype(BF16)) * dmask
        o = _dot(s.astype(BF16), vb) + _dot(qb, state.astype(BF16)) * xi_ref[...]
        state = state * g_chunk + _dot((kr * zeta_ref[...]).T.astype(BF16), vb)
        mu = jnp.mean(o, axis=-1, keepdims=True)
        oc = o - mu
        var = jnp.mean(oc * oc, axis=-1, keepdims=True)
        y = oc * lax.rsqrt(var + EPS) * gain
        o_ref[rs, :] = (y * _silu(z_ref[rs, :])).astype(BF16)
    r_ref[...] = state


def _retention(h, bsz, seq, cos2, sin2, c_out_norm):
    tm = min(RET_TM, seq)
    nt = seq // tm
    lg = np.log1p(-np.exp2(-5.0 - np.arange(C_HEADS, dtype=np.float32))).astype(np.float32)
    lg = np.stack([lg, np.exp(np.float32(C_CHUNK) * lg).astype(np.float32)])
    return pl.pallas_call(
        _retention_kernel,
        out_shape=jax.ShapeDtypeStruct((bsz * seq, C_HEADS * C_V_DIM), BF16),
        grid=(bsz, C_HEADS, nt),
        in_specs=[pl.BlockSpec(memory_space=pltpu.SMEM),
                  pl.BlockSpec((tm, 128), lambda b, hd, i: (b * nt + i, OD_CQ // 128 + hd)),
                  pl.BlockSpec((tm, 128), lambda b, hd, i: (b * nt + i, OD_CK // 128 + hd)),
                  pl.BlockSpec((tm, 256), lambda b, hd, i: (b * nt + i, OD_CV // 256 + hd)),
                  pl.BlockSpec((tm, 256), lambda b, hd, i: (b * nt + i, OD_CZ // 256 + hd)),
                  pl.BlockSpec((tm, 128), lambda b, hd, i: (i, 0)),
                  pl.BlockSpec((tm, 128), lambda b, hd, i: (i, 0)),
                  pl.BlockSpec((1, C_V_DIM), lambda b, hd, i: (0, hd))],
        out_specs=pl.BlockSpec((tm, C_V_DIM), lambda b, hd, i: (b * nt + i, hd)),
        scratch_shapes=[pltpu.VMEM((C_QK_DIM, C_V_DIM), F32),
                        pltpu.VMEM((C_CHUNK, C_CHUNK), F32),
                        pltpu.VMEM((C_CHUNK, C_QK_DIM), F32),
                        pltpu.VMEM((C_CHUNK, C_V_DIM), F32)],
        compiler_params=_cparams(("parallel", "parallel", "arbitrary")),
        name="retention",
    )(jnp.asarray(lg), h, h, h, h, cos2, sin2, c_out_norm.reshape(1, -1))


def _s5_param_kernel(lre_ref, lim_ref, ldt_ref, lrex_ref, limx_ref, ldtx_ref, bre_ref, bim_ref,
                     are_ref, aim_ref, bbre_ref, bbim_ref):
    def disc(lre, lim, ldt):
        lr = jnp.minimum(lre, -1e-4)
        dt = jnp.exp(ldt)
        mag = jnp.exp(lr * dt)
        return lr, lim, mag * jnp.cos(lim * dt), mag * jnp.sin(lim * dt)

    _, _, a_re, a_im = disc(lre_ref[...], lim_ref[...], ldt_ref[...])
    are_ref[...] = a_re
    aim_ref[...] = a_im
    lr, li, ax_re, ax_im = disc(lrex_ref[...], limx_ref[...], ldtx_ref[...])
    den = lr * lr + li * li
    f_re = ((ax_re - 1.0) * lr + ax_im * li) / den
    f_im = (ax_im * lr - (ax_re - 1.0) * li) / den
    bbre_ref[...] = f_re * bre_ref[...] - f_im * bim_ref[...]
    bbim_ref[...] = f_re * bim_ref[...] + f_im * bre_ref[...]


def _s5_params(lam_re, lam_im, log_dt, b_re, b_im, c_re, c_im):
    g, p, ch = D_GROUPS, D_STATE, D_GROUP
    ldt = jnp.broadcast_to(log_dt[:, None], (g, p))
    rep = lambda a: jnp.repeat(a, ch, axis=1)
    vm = pl.BlockSpec(memory_space=pltpu.VMEM)
    a_re, a_im, bb_re, bb_im = pl.pallas_call(
        _s5_param_kernel,
        out_shape=(jax.ShapeDtypeStruct((g, p), F32), jax.ShapeDtypeStruct((g, p), F32),
                   jax.ShapeDtypeStruct((g, p * ch), F32), jax.ShapeDtypeStruct((g, p * ch), F32)),
        in_specs=[vm] * 8, out_specs=(vm,) * 4,
        name="s5_params",
    )(lam_re, lam_im, ldt, rep(lam_re), rep(lam_im), rep(ldt),
      b_re.reshape(g, p * ch), b_im.reshape(g, p * ch))
    eye = jnp.eye(D_SETS * 4, dtype=F32)

    def pack_b(bb):
        bb = bb.reshape(D_SETS, 16, p, ch)
        return jnp.einsum('sgpi,gh->sgihp', bb, eye).reshape(D_SETS, D_SET_CH, D_SET_ST)

    def pack_c(c):
        c = c.reshape(D_SETS, 16, ch, p)
        return jnp.einsum('sgjp,gh->sgphj', c, eye).reshape(D_SETS, D_SET_ST, D_SET_CH)

    bd = jnp.concatenate([pack_b(bb_re), pack_b(bb_im)], axis=-1).astype(BF16)
    slab = (D_SETS, S5_SLAB, 128)
    return (a_re.reshape(slab), a_im.reshape(slab), bd,
            pack_c(c_re).astype(BF16), pack_c(c_im).astype(BF16))


S5_TM = 512
S5_SLAB = D_SET_ST // 128


def _s5_kernel(u_ref, z_ref, are_ref, aim_ref, bd_ref, cre_ref, cim_ref, dskip_ref, wglu_ref, bglu_ref,
               o_ref, hre_ref, him_ref, y_ref, *x_refs):
    @pl.when(pl.program_id(1) == 0)
    def _():
        hre_ref[...] = jnp.zeros(hre_ref.shape, F32)
        him_ref[...] = jnp.zeros(him_ref.shape, F32)

    xre, xim = x_refs[:D_SETS], x_refs[D_SETS:]
    tm = u_ref.shape[0]
    u = u_ref[...]
    ub = u.astype(BF16)
    for s in range(D_SETS):
        bu = _dot(ub[:, s * D_SET_CH:(s + 1) * D_SET_CH], bd_ref[s])
        for k in range(S5_SLAB):
            xre[s][pl.ds(k, tm, stride=S5_SLAB), :] = bu[:, k * 128:(k + 1) * 128]
            xim[s][pl.ds(k, tm, stride=S5_SLAB), :] = bu[:, D_SET_ST + k * 128:D_SET_ST + (k + 1) * 128]
    a_re = [are_ref[s] for s in range(D_SETS)]
    a_im = [aim_ref[s] for s in range(D_SETS)]

    def step(t, carry):
        rows = pl.ds(pl.multiple_of(t * S5_SLAB, S5_SLAB), S5_SLAB)
        out = []
        for s in range(D_SETS):
            hr, hi = carry[2 * s], carry[2 * s + 1]
            xr = xre[s][rows, :] + a_re[s] * hr - a_im[s] * hi
            xi = xim[s][rows, :] + a_re[s] * hi + a_im[s] * hr
            xre[s][rows, :] = xr
            xim[s][rows, :] = xi
            out += [xr, xi]
        return tuple(out)

    carry = []
    for s in range(D_SETS):
        carry += [hre_ref[s], him_ref[s]]
    carry = lax.fori_loop(0, tm, step, tuple(carry), unroll=8)
    for s in range(D_SETS):
        hre_ref[s] = carry[2 * s]
        him_ref[s] = carry[2 * s + 1]
        x_r = jnp.concatenate([xre[s][pl.ds(k, tm, stride=S5_SLAB), :] for k in range(S5_SLAB)], axis=1)
        x_i = jnp.concatenate([xim[s][pl.ds(k, tm, stride=S5_SLAB), :] for k in range(S5_SLAB)], axis=1)
        y_ref[:, s * D_SET_CH:(s + 1) * D_SET_CH] = (
            _dot(x_r.astype(BF16), cre_ref[s]) - _dot(x_i.astype(BF16), cim_ref[s]))
    y = y_ref[...] + dskip_ref[...] * u
    y = 0.5 * y * (1.0 + jnp.tanh(math.sqrt(2.0 / math.pi) * (y + 0.044715 * (y * y * y))))
    gate = _sigmoid(_dot(y.astype(BF16), wglu_ref[...]) + bglu_ref[...])
    o_ref[...] = (y * gate * _silu(z_ref[...])).astype(BF16)


def _s5(h, bsz, seq, a_re, a_im, bd, cd_re, cd_im, d_skip, w_glu, b_glu):
    tm = min(S5_TM, seq)
    nt = seq // tm
    full = lambda *shape: pl.BlockSpec(shape, lambda b, i: (0,) * len(shape))
    width = D_GROUPS * D_GROUP
    return pl.pallas_call(
        _s5_kernel,
        out_shape=jax.ShapeDtypeStruct((bsz * seq, width), BF16),
        grid=(bsz, nt),
        in_specs=[pl.BlockSpec((tm, width), lambda b, i: (b * nt + i, OD_DU // width)),
                  pl.BlockSpec((tm, width), lambda b, i: (b * nt + i, OD_DZ // width)),
                  full(D_SETS, S5_SLAB, 128), full(D_SETS, S5_SLAB, 128),
                  full(D_SETS, D_SET_CH, 2 * D_SET_ST),
                  full(D_SETS, D_SET_ST, D_SET_CH), full(D_SETS, D_SET_ST, D_SET_CH),
                  full(1, width), full(width, width), full(1, width)],
        out_specs=pl.BlockSpec((tm, width), lambda b, i: (b * nt + i, 0)),
        scratch_shapes=([pltpu.VMEM((D_SETS, S5_SLAB, 128), F32), pltpu.VMEM((D_SETS, S5_SLAB, 128), F32),
                         pltpu.VMEM((tm, width), F32)]
                        + [pltpu.VMEM((tm * S5_SLAB, 128), F32)] * (2 * D_SETS)),
        compiler_params=_cparams(("parallel", "arbitrary")),
        name="s5",
    )(h, h, a_re, a_im, bd, cd_re, cd_im, d_skip.reshape(1, -1), w_glu.astype(BF16),
      b_glu.reshape(1, -1))


def _pack_even_w(w_in):
    sizes = (1024, A_KV_RANK, A_IDX_HEADS * A_IDX_DIM, A_IDX_DIM, A_IDX_HEADS, 1024,
             3 * 1024, B_HEADS, B_HEADS, 1024)
    parts, start = [], 0
    for s in sizes:
        parts.append(w_in[:, start:start + s])
        start += s
    aq, ckv, qi, ki, wi, az, bqkv, ba, bb, bz = parts
    pad = jnp.zeros((w_in.shape[0], EV_WIDTH - start), w_in.dtype)
    return jnp.concatenate([aq, az, bqkv, bz, qi, ckv, ki, wi, ba, bb, pad], axis=1).astype(BF16)


def _even_layer(x2d, bsz, seq, bias, norm_g, w_in, a_q_norm, a_kv_norm, w_kv_up, a_k_norm,
                b_conv, b_a_log, b_dt_bias, b_out_norm, w_out):
    h = _norm_proj(x2d, norm_g, _pack_even_w(w_in), tn=768)
    qta, qit, wt, kidx, katt, vt = _dsa_prep(h, bsz * seq // QB, a_q_norm, a_kv_norm, w_kv_up, a_k_norm)
    mix_a = _dsa_attend(h, bsz, seq, qta, qit, wt, kidx, katt, vt, bias)
    u, w, qd, kd, attn, egl = _gdn_prep(h, bsz, seq, b_conv, b_a_log, b_dt_bias)
    mix_b = _gdn_scan(h, bsz, seq, u, w, qd, kd, attn, egl, b_out_norm)
    return _out_proj(x2d, mix_a, mix_b, w_out)


def _odd_layer(x2d, bsz, seq, cos2, sin2, norm_g, w_in, c_out_norm, lam_re, lam_im, log_dt,
               b_re, b_im, c_re, c_im, d_skip, w_glu, b_glu, w_out):
    h = _norm_proj(x2d, norm_g, w_in.astype(BF16), tn=1024)
    mix_c = _retention(h, bsz, seq, cos2, sin2, c_out_norm)
    s5p = _s5_params(lam_re, lam_im, log_dt, b_re, b_im, c_re, c_im)
    mix_d = _s5(h, bsz, seq, *s5p, d_skip, w_glu, b_glu)
    return _out_proj(x2d, mix_c, mix_d, w_out)


def kernel(x, rel_bias, ev_norm, ev_w_in, ev_a_q_norm, ev_a_kv_norm, ev_w_kv_up, ev_a_k_norm,
           ev_b_conv, ev_b_a_log, ev_b_dt_bias, ev_b_out_norm, ev_w_out,
           od_norm, od_w_in, od_c_out_norm, od_lam_re, od_lam_im, od_log_dt,
           od_b_re, od_b_im, od_c_re, od_c_im, od_d_skip, od_w_glu, od_b_glu, od_w_out):
    bsz, seq, d = x.shape
    depth = ev_norm.shape[0] + od_norm.shape[0]
    x2d = x.reshape(bsz * seq, d)
    bias = _bias_table(rel_bias)
    cos2, sin2 = _rope_tables(seq)
    for layer in range(depth):
        i = layer // 2
        if layer % 2 == 0:
            x2d = _even_layer(x2d, bsz, seq, bias, ev_norm[i], ev_w_in[i], ev_a_q_norm[i],
                              ev_a_kv_norm[i], ev_w_kv_up[i], ev_a_k_norm[i], ev_b_conv[i],
                              ev_b_a_log[i], ev_b_dt_bias[i], ev_b_out_norm[i], ev_w_out[i])
        else:
            x2d = _odd_layer(x2d, bsz, seq, cos2, sin2, od_norm[i], od_w_in[i], od_c_out_norm[i],
                             od_lam_re[i], od_lam_im[i], od_log_dt[i], od_b_re[i], od_b_im[i],
                             od_c_re[i], od_c_im[i], od_d_skip[i], od_w_glu[i], od_b_glu[i], od_w_out[i])
    return x2d.reshape(bsz, seq, d)
```

```python
import functools
import math

import numpy as np
import jax
import jax.numpy as jnp
from jax import lax
from jax.experimental import pallas as pl
from jax.experimental.pallas import tpu as pltpu

F32 = jnp.float32
BF16 = jnp.bfloat16
I32 = jnp.int32

D_MODEL = 1024
EPS = 1e-6
LOG2E = 1.4426950408889634
NEG_INF = float("-inf")
INT_MIN = -(2 ** 31)

A_HEADS, A_HEAD_DIM, A_KV_RANK = 16, 64, 128
A_IDX_HEADS, A_IDX_DIM, A_TOPK_MAX = 8, 64, 256
QB = 128
REL_BUCKETS, REL_MAX_DIST = 32, 128
VT_ROWS = 80
SB = 2 * QB
FAR_BLOCKS = 4
DSA_PREP_TILES = 2
SCORE_ROWS = 4 * QB
BIAS_ROWS = 5 * QB
MASKED = -(2.0 ** 100)
M_INIT = -(2.0 ** 60)
TIE_WALK_MAX = 8
SCORE_UNROLL = 4
COUNT_UNROLL = 4
B_HEADS, B_HEAD_DIM, CONV_K, B_CHUNK = 8, 128, 4, 64
C_HEADS, C_QK_DIM, C_V_DIM, C_CHUNK = 4, 128, 256, 128
ROPE_BASE = 10000.0
D_GROUP, D_STATE, D_GROUPS = 16, 64, 64
D_SETS, D_SET_CH, D_SET_ST = 4, 256, 1024

EV_AQ, EV_AZ, EV_BQ, EV_BK, EV_BV, EV_BZ, EV_QI, EV_CKV, EV_MISC = (
    0, 1024, 2048, 3072, 4096, 5120, 6144, 6656, 6784)
EV_WIDTH = 6912
MISC_KI, MISC_WI, MISC_BA, MISC_BB = 0, 64, 72, 80
OD_CQ, OD_CK, OD_CV, OD_CZ, OD_DU, OD_DZ = 0, 512, 1024, 2048, 3072, 4096
OD_WIDTH = 5120

VMEM_LIMIT = 48 * 1024 * 1024


def _cparams(sem):
    return pltpu.CompilerParams(dimension_semantics=sem, vmem_limit_bytes=VMEM_LIMIT)


def _dot(a, b):
    return jnp.dot(a, b, preferred_element_type=F32)


def _dot_nt(a, b):
    return lax.dot_general(a, b, (((1,), (1,)), ((), ())), preferred_element_type=F32)


def _split_bf16(x, n):
    parts = []
    for term in range(n):
        p = x.astype(BF16)
        parts.append(p)
        if term + 1 < n:
            x = x - p.astype(F32)
    return parts


def _dot_sel(sel, x):
    sel = sel.astype(BF16)
    hi, mid, lo = _split_bf16(x, 3)
    return _dot(sel, hi) + (_dot(sel, mid) + _dot(sel, lo))


def _dot_xsel(x, sel):
    sel = sel.astype(BF16)
    hi, mid, lo = _split_bf16(x, 3)
    return _dot(hi, sel) + (_dot(mid, sel) + _dot(lo, sel))


def _dot_x3(a, b, expand_a=None, expand_b=None):
    ah, al = _split_bf16(a, 2)
    bh, bl = _split_bf16(b, 2)
    if expand_a is not None:
        ah, al = expand_a(ah), expand_a(al)
    if expand_b is not None:
        bh, bl = expand_b(bh), expand_b(bl)
    return _dot(ah, bh) + (_dot(ah, bl) + _dot(al, bh))


def _sigmoid(x):
    return 1.0 / (1.0 + jnp.exp(-x))


def _silu(x):
    return x * _sigmoid(x)


def _norm_proj_kernel(x_ref, g_ref, w_ref, o_ref, xn_ref):
    @pl.when(pl.program_id(1) == 0)
    def _():
        x = x_ref[...]
        ms = jnp.mean(x * x, axis=-1, keepdims=True)
        xn_ref[...] = (x * lax.rsqrt(ms + EPS) * g_ref[...]).astype(BF16)

    o_ref[...] = _dot(xn_ref[...], w_ref[...])


def _norm_proj(x2d, gain, w_bf16, tn, tm=2048):
    t, d = x2d.shape
    n = w_bf16.shape[1]
    tm = min(tm, t)
    return pl.pallas_call(
        _norm_proj_kernel,
        out_shape=jax.ShapeDtypeStruct((t, n), F32),
        grid=(t // tm, n // tn),
        in_specs=[pl.BlockSpec((tm, d), lambda i, j: (i, 0)),
                  pl.BlockSpec((1, d), lambda i, j: (0, 0)),
                  pl.BlockSpec((d, tn), lambda i, j: (0, j))],
        out_specs=pl.BlockSpec((tm, tn), lambda i, j: (i, j)),
        scratch_shapes=[pltpu.VMEM((tm, d), BF16)],
        compiler_params=_cparams(("parallel", "arbitrary")),
        name="norm_proj",
    )(x2d, gain.reshape(1, d), w_bf16)


def _out_proj_kernel(x_ref, a_ref, b_ref, wa_ref, wb_ref, o_ref):
    o_ref[...] = x_ref[...] + _dot(a_ref[...], wa_ref[...]) + _dot(b_ref[...], wb_ref[...])


def _out_proj(x2d, mix_a, mix_b, w_out, tm=1024):
    t, d = x2d.shape
    half = mix_a.shape[1]
    tm = min(tm, t)
    wa = w_out[:half].astype(BF16)
    wb = w_out[half:].astype(BF16)
    return pl.pallas_call(
        _out_proj_kernel,
        out_shape=jax.ShapeDtypeStruct((t, d), F32),
        grid=(t // tm,),
        in_specs=[pl.BlockSpec((tm, d), lambda i: (i, 0)),
                  pl.BlockSpec((tm, half), lambda i: (i, 0)),
                  pl.BlockSpec((tm, half), lambda i: (i, 0)),
                  pl.BlockSpec((half, d), lambda i: (0, 0)),
                  pl.BlockSpec((half, d), lambda i: (0, 0))],
        out_specs=pl.BlockSpec((tm, d), lambda i: (i, 0)),
        compiler_params=_cparams(("parallel",)),
        name="out_proj",
    )(x2d, mix_a, mix_b, wa, wb)


def _t5_bucket_starts():
    exact = REL_BUCKETS // 2
    n = np.arange(0, 4 * REL_MAX_DIST, dtype=np.int64)
    ratio = np.maximum(n, 1).astype(np.float32) / np.float32(exact)
    large = exact + (np.log(ratio).astype(np.float32) / np.float32(math.log(REL_MAX_DIST / exact))
                     * np.float32(REL_BUCKETS - exact)).astype(np.int32)
    bucket = np.where(n < exact, n, np.minimum(large, REL_BUCKETS - 1))
    starts = [int(np.argmax(bucket >= b)) for b in range(REL_BUCKETS)]
    assert all(bucket[s] == b for b, s in enumerate(starts)) and np.all(np.diff(bucket) >= 0)
    assert starts[-1] <= QB, "distances beyond one key block must share the last bucket"
    return starts


_BUCKET_STARTS = _t5_bucket_starts()


def _bias_table_kernel(rb_ref, o_ref):
    row = lax.broadcasted_iota(I32, (BIAS_ROWS, QB), 0)
    lane = lax.broadcasted_iota(I32, (BIAS_ROWS, QB), 1)
    dist = lane + 3 * QB - row
    for h in range(A_HEADS):
        val = jnp.full((BIAS_ROWS, QB), rb_ref[0, h], F32)
        for b in range(1, REL_BUCKETS):
            val = jnp.where(dist >= _BUCKET_STARTS[b], rb_ref[b, h], val)
        val = (val - rb_ref[REL_BUCKETS - 1, h]) * LOG2E
        o_ref[:, h * QB:(h + 1) * QB] = jnp.where(dist >= 0, val, 0.0)


def _bias_table(rel_bias):
    return pl.pallas_call(
        _bias_table_kernel,
        out_shape=jax.ShapeDtypeStruct((BIAS_ROWS, A_HEADS * QB), F32),
        in_specs=[pl.BlockSpec(memory_space=pltpu.SMEM)],
        out_specs=pl.BlockSpec(memory_space=pltpu.VMEM),
        name="dsa_bias_table",
    )(rel_bias)


def _dsa_prep_kernel(aq_ref, qi_ref, ckv_ref, misc_ref, gq_ref, hsum_ref, gkv_ref, wkv_ref, gk_ref,
                     qta_ref, qit_ref, wt_ref, kidx_ref, katt_ref, vt_ref):
    zeros = jnp.zeros((QB, A_HEAD_DIM), F32)
    gq = gq_ref[...]
    ident = jnp.where(lax.broadcasted_iota(I32, (QB, QB), 0) == lax.broadcasted_iota(I32, (QB, QB), 1),
                      1.0, 0.0).astype(BF16)
    ones_row = lax.broadcasted_iota(I32, (VT_ROWS - A_HEAD_DIM, QB), 0) == 0

    def transpose_b(x):
        return _dot_nt(ident, x.astype(BF16)).astype(BF16)

    for t in range(DSA_PREP_TILES):
        rs = slice(t * QB, (t + 1) * QB)
        aq = aq_ref[rs, :]
        sq_hi, sq_lo = _split_bf16(aq * aq, 2)
        ssq = _dot(sq_hi, hsum_ref[...]) + _dot(sq_lo, hsum_ref[...])
        qn_all = aq * lax.rsqrt(ssq * (1.0 / A_HEAD_DIM) + EPS) * gq * (A_HEAD_DIM ** -0.5 * LOG2E)
        for h in range(A_HEADS):
            qn = qn_all[:, h * A_HEAD_DIM:(h + 1) * A_HEAD_DIM]
            qta_ref[t, :, h * QB:(h + 1) * QB] = transpose_b(jnp.concatenate([qn, zeros], axis=1))
        qi = qi_ref[rs, :]
        for h in range(A_IDX_HEADS):
            qih = qi[:, h * A_IDX_DIM:(h + 1) * A_IDX_DIM]
            qit_ref[t, :, h * QB:(h + 1) * QB] = transpose_b(jnp.concatenate([qih, zeros], axis=1))
        misc = misc_ref[rs, :]
        wt_ref[t] = misc.T[MISC_WI:MISC_WI + A_IDX_HEADS, :] * (A_IDX_HEADS ** -0.5 * A_IDX_DIM ** -0.5)
        c = ckv_ref[rs, :]
        cn = c * lax.rsqrt(jnp.mean(c * c, axis=-1, keepdims=True) + EPS) * gkv_ref[...]
        kv = _dot(cn.astype(BF16), wkv_ref[...])
        k = kv[:, :A_HEAD_DIM]
        kn = k * lax.rsqrt(jnp.mean(k * k, axis=-1, keepdims=True) + EPS) * gk_ref[...]
        kidx_ref[t] = jnp.concatenate([misc[:, MISC_KI:MISC_KI + A_IDX_DIM], zeros], axis=1).astype(BF16)
        katt_ref[t] = jnp.concatenate([kn, zeros], axis=1).astype(BF16)
        vt_ref[t, 0:A_HEAD_DIM, :] = transpose_b(kv)[A_HEAD_DIM:, :]
        vt_ref[t, A_HEAD_DIM:, :] = jnp.where(ones_row, 1.0, 0.0).astype(BF16)


def _dsa_prep(h, nblk, a_q_norm, a_kv_norm, w_kv_up, a_k_norm):
    cb = lambda width, off: off // width
    nt = DSA_PREP_TILES
    width = A_HEADS * A_HEAD_DIM
    head_of = np.arange(width) // A_HEAD_DIM
    same_head = (head_of[:, None] == head_of[None, :]).astype(np.float32)
    return pl.pallas_call(
        _dsa_prep_kernel,
        out_shape=(jax.ShapeDtypeStruct((nblk, QB, A_HEADS * QB), BF16),
                   jax.ShapeDtypeStruct((nblk, QB, A_IDX_HEADS * QB), BF16),
                   jax.ShapeDtypeStruct((nblk, A_IDX_HEADS, QB), F32),
                   jax.ShapeDtypeStruct((nblk, QB, 128), BF16),
                   jax.ShapeDtypeStruct((nblk, QB, 128), BF16),
                   jax.ShapeDtypeStruct((nblk, VT_ROWS, QB), BF16)),
        grid=(nblk // nt,),
        in_specs=[pl.BlockSpec((nt * QB, 1024), lambda i: (i, cb(1024, EV_AQ))),
                  pl.BlockSpec((nt * QB, 512), lambda i: (i, cb(512, EV_QI))),
                  pl.BlockSpec((nt * QB, 128), lambda i: (i, cb(128, EV_CKV))),
                  pl.BlockSpec((nt * QB, 128), lambda i: (i, cb(128, EV_MISC))),
                  pl.BlockSpec((1, width), lambda i: (0, 0)),
                  pl.BlockSpec((width, width), lambda i: (0, 0)),
                  pl.BlockSpec((1, A_KV_RANK), lambda i: (0, 0)),
                  pl.BlockSpec((A_KV_RANK, 2 * A_HEAD_DIM), lambda i: (0, 0)),
                  pl.BlockSpec((1, A_HEAD_DIM), lambda i: (0, 0))],
        out_specs=(pl.BlockSpec((nt, QB, A_HEADS * QB), lambda i: (i, 0, 0)),
                   pl.BlockSpec((nt, QB, A_IDX_HEADS * QB), lambda i: (i, 0, 0)),
                   pl.BlockSpec((nt, A_IDX_HEADS, QB), lambda i: (i, 0, 0)),
                   pl.BlockSpec((nt, QB, 128), lambda i: (i, 0, 0)),
                   pl.BlockSpec((nt, QB, 128), lambda i: (i, 0, 0)),
                   pl.BlockSpec((nt, VT_ROWS, QB), lambda i: (i, 0, 0))),
        compiler_params=_cparams(("parallel",)),
        name="dsa_prep",
    )(h, h, h, h, jnp.tile(a_q_norm, A_HEADS).reshape(1, width), jnp.asarray(same_head, BF16),
      a_kv_norm.reshape(1, -1), w_kv_up.astype(BF16), a_k_norm.reshape(1, -1))


def _dsa_kernel(topk, qit_ref, wt_ref, qta_ref, kidx_ref, katt_ref, vt_ref, bias_ref, az_ref, o_ref,
                strip_ref, x_ref, m_ref, acc_ref, lg_ref, mx_ref, p_ref, rhs_ref):
    qt = pl.program_id(1)
    t0 = qt * QB
    rhs_ref[0:QB, :] = qta_ref[...]

    @pl.when((pl.program_id(0) == 0) & (qt == 0))
    def _():
        ident = jnp.where(lax.broadcasted_iota(I32, (QB, QB), 0) == lax.broadcasted_iota(I32, (QB, QB), 1),
                          1.0, 0.0).astype(BF16)
        for h in range(A_HEADS):
            rhs_ref[QB:2 * QB, h * QB:(h + 1) * QB] = ident

    n_sc = qt // 4 + 1
    n_sb = qt // 2 + 1
    lane_s = lax.broadcasted_iota(I32, (SCORE_ROWS, QB), 1)
    row_s = lax.broadcasted_iota(I32, (SCORE_ROWS, QB), 0)

    w = wt_ref[...]

    def score_body(j, carry):
        kblk = kidx_ref[pl.ds(j * 4, 4)].reshape(SCORE_ROWS, 128)
        tot = None
        for hp in range(A_IDX_HEADS // 2):
            s = _dot(kblk, qit_ref[:, hp * SB:(hp + 1) * SB])
            for e in range(2):
                h = 2 * hp + e
                term = jnp.maximum(s[:, e * QB:(e + 1) * QB], 0.0) * w[h:h + 1, :]
                tot = term if tot is None else tot + term
        tot = jnp.where(j * SCORE_ROWS + row_s <= t0 + lane_s, tot, NEG_INF)
        bits = pltpu.bitcast(tot, I32)
        strip_ref[pl.ds(pl.multiple_of(j * SCORE_ROWS, SCORE_ROWS), SCORE_ROWS), :] = (
            bits ^ ((bits >> 31) & 0x7FFFFFFF))
        return carry

    def score_group(jj, carry):
        for sub in range(SCORE_UNROLL):
            score_body(SCORE_UNROLL * jj + sub, carry)
        return carry

    lax.fori_loop(0, n_sc // SCORE_UNROLL, score_group, 0)
    lax.fori_loop(SCORE_UNROLL * (n_sc // SCORE_UNROLL), n_sc, score_body, 0)

    def count(pred):
        def one(j, acc):
            r0 = pl.multiple_of(j * SCORE_ROWS, SCORE_ROWS)
            m = jnp.where(pred(strip_ref[pl.ds(r0, SCORE_ROWS), :], r0), 1, 0)
            return acc + m.reshape(SCORE_ROWS // 32, 32, QB).sum(axis=0)

        def body(j, acc):
            for sub in range(COUNT_UNROLL):
                acc = one(COUNT_UNROLL * j + sub, acc)
            return acc
        n_main = n_sc // COUNT_UNROLL
        acc = lax.fori_loop(0, n_main, body, jnp.zeros((32, QB), I32))
        acc = lax.fori_loop(n_main * COUNT_UNROLL, n_sc, one, acc)
        return acc.reshape(4, 8, QB).sum(axis=0).sum(axis=0, keepdims=True)

    count_ge = lambda cand: count(lambda key, r0: key >= cand)
    c0 = count_ge(jnp.zeros((1, QB), I32))
    nonneg = c0 >= topk
    thr0 = jnp.where(nonneg, 0, INT_MIN)
    cnt0 = jnp.where(nonneg, c0, n_sc * SCORE_ROWS)

    def bit_steps(counter, n):
        def one(_, carry):
            i, thr, cnt = carry
            cand = thr | lax.shift_left(jnp.int32(1), 30 - i)
            c = counter(cand)
            ok = c >= topk
            return i + 1, jnp.where(ok, cand, thr), jnp.where(ok, c, cnt)
        return lambda carry: lax.fori_loop(0, n, one, carry)

    state = bit_steps(count_ge, 15)((jnp.int32(0), thr0, cnt0))
    _, thr, cnt = lax.while_loop(lambda c: (c[0] < 31) & (jnp.max(c[2]) > topk),
                                 bit_steps(count_ge, 4), state)

    x_ref[...] = jnp.full((1, QB), 2 ** 30, I32)
    tied = cnt > topk

    drop = jnp.where(tied, cnt - topk, 0)
    max_drop = jnp.max(drop)

    @pl.when((max_drop > 0) & (max_drop <= TIE_WALK_MAX))
    def _():
        def prev_tie(bound):
            def body(j, acc):
                r0 = pl.multiple_of(j * SCORE_ROWS, SCORE_ROWS)
                idx = r0 + row_s
                hit = (strip_ref[pl.ds(r0, SCORE_ROWS), :] == thr) & (idx < bound)
                return jnp.maximum(acc, jnp.where(hit, idx, -1)
                                   .reshape(SCORE_ROWS // 32, 32, QB).max(axis=0))
            acc = lax.fori_loop(0, n_sc, body, jnp.full((32, QB), -1, I32))
            return acc.reshape(4, 8, QB).max(axis=0).max(axis=0, keepdims=True)

        x_ref[...] = lax.fori_loop(0, max_drop, lambda r, bound: jnp.where(r < drop, prev_tie(bound), bound),
                                   jnp.full((1, QB), 2 ** 30, I32))

    @pl.when(max_drop > TIE_WALK_MAX)
    def _():
        above = count(lambda key, r0: key > thr)
        need = jnp.where(tied, topk - above, 0)

        def tie_body(i, xb):
            cand = xb | lax.shift_left(jnp.int32(1), 14 - i)
            c = count(lambda key, r0: (key == thr) & (r0 + row_s < cand))
            return jnp.where(c <= need, cand, xb)

        xb = lax.fori_loop(0, 15, tie_body, jnp.zeros((1, QB), I32))
        x_ref[...] = jnp.where(tied, xb, 2 ** 30)

    xb = x_ref[...]

    m_ref[...] = jnp.full(m_ref.shape, M_INIT, F32)
    acc_ref[...] = jnp.zeros(acc_ref.shape, F32)
    n_lt = A_HEADS * QB // SB
    row_a = {nk: lax.broadcasted_iota(I32, (nk * QB, QB), 0) for nk in (2, FAR_BLOCKS)}
    lane_a = lax.broadcasted_iota(I32, (SB, QB), 1)

    def key_operand(j, nk, near):
        rows = nk * QB
        r0 = pl.multiple_of(j * rows, rows)
        key = strip_ref[pl.ds(r0, rows), :]
        idx = r0 + row_a[nk]
        sel = (key > thr) | ((key == thr) & (idx < xb))
        if near:
            sel = sel & (idx <= t0 + lane_a)
        pen = jnp.where(sel, 0.0, MASKED).astype(BF16)
        return jnp.concatenate([katt_ref[pl.ds(nk * j, nk)].reshape(rows, 128), pen], axis=1)

    def values(j, nk):
        return jnp.concatenate([vt_ref[nk * j + i] for i in range(nk)], axis=1)

    def new_max(slot):
        m_old = m_ref[...]
        m_new = jnp.maximum(m_old, mx_ref[slot])
        m_ref[...] = m_new
        return m_new.astype(BF16), jnp.exp2(m_old - m_new)

    def logits(j, slot, near, nk=2):
        rows = nk * QB
        kaug = key_operand(j, nk, near)
        if near:
            b0 = pl.multiple_of((2 * j - qt + 3) * QB, QB)
        for lt in range(n_lt):
            cols = slice(lt * SB, (lt + 1) * SB)
            lg = _dot(kaug, rhs_ref[:, cols])
            if near:
                lg = lg + bias_ref[pl.ds(b0, SB), cols]
            lg = lg.astype(BF16)
            lg_ref[slot, 0:rows, cols] = lg
            mx_ref[slot, :, cols] = jnp.max(lg, axis=0, keepdims=True).astype(F32)

    def accumulate(j, slot, nk=2):
        rows = nk * QB
        vt = values(j, nk)
        m_b, alpha = new_max(slot)
        for lt in range(n_lt):
            cols = slice(lt * SB, (lt + 1) * SB)
            p_ref[0:rows, cols] = jnp.exp2(lg_ref[slot, 0:rows, cols] - m_b[:, cols])
        acc_ref[...] = alpha * acc_ref[...] + _dot(vt, p_ref[0:rows, :])

    nk = FAR_BLOCKS
    n_far = jnp.maximum(n_sb - 2, 0)
    n_big = n_far // (nk // 2)
    n_pair = jnp.maximum(n_big - 1, 0) // 2
    n_rem = n_big - 2 * n_pair

    @pl.when(n_big > 0)
    def _():
        logits(0, 0, False, nk)

    def fused(j, slot):
        nxt = 1 - slot
        rows = nk * QB
        kaug = key_operand(j + 1, nk, False)
        vt = values(j, nk)
        m_b, alpha = new_max(slot)
        for lt in range(n_lt):
            cols = slice(lt * SB, (lt + 1) * SB)
            lg = _dot(kaug, rhs_ref[:, cols]).astype(BF16)
            lg_ref[nxt, 0:rows, cols] = lg
            mx_ref[nxt, :, cols] = jnp.max(lg, axis=0, keepdims=True).astype(F32)
            p = jnp.exp2(lg_ref[slot, 0:rows, cols] - m_b[:, cols])
            acc_ref[:, cols] = alpha[:, cols] * acc_ref[:, cols] + _dot(vt, p)

    def pair_body(jj, carry):
        j = 2 * jj
        fused(j, 0)
        fused(j + 1, 1)
        return carry

    lax.fori_loop(0, n_pair, pair_body, 0)

    @pl.when(n_rem == 1)
    def _():
        accumulate(n_big - 1, 0, nk)

    @pl.when(n_rem == 2)
    def _():
        fused(n_big - 2, 0)
        accumulate(n_big - 1, 1, nk)

    @pl.when(n_far % (nk // 2) == 1)
    def _():
        logits(n_far - 1, 0, False)
        accumulate(n_far - 1, 0)

    @pl.when(n_sb >= 2)
    def _():
        logits(n_sb - 2, 0, True)
        accumulate(n_sb - 2, 0)

    logits(n_sb - 1, 0, True)
    accumulate(n_sb - 1, 0)

    acc = acc_ref[...]
    o_t = acc[0:A_HEAD_DIM, :] / acc[A_HEAD_DIM:A_HEAD_DIM + 1, :]
    pad = jnp.zeros((QB - A_HEAD_DIM, QB), F32)
    pieces = []
    for h in range(A_HEADS):
        blk = jnp.concatenate([o_t[:, h * QB:(h + 1) * QB], pad], axis=0)
        pieces.append(blk.T[:, 0:A_HEAD_DIM])
    att = jnp.concatenate(pieces, axis=1)
    o_ref[...] = (att * _silu(az_ref[...])).astype(BF16)


def _dsa_attend(h, bsz, seq, qta, qit, wt, kidx, katt, vt, bias):
    nq = seq // QB
    topk = min(A_TOPK_MAX, seq // 4)
    blk = lambda b, q: (b * nq + q, 0, 0)
    per_batch = lambda b, q: (b, 0, 0)
    once = pl.Buffered(1)
    return pl.pallas_call(
        functools.partial(_dsa_kernel, topk),
        out_shape=jax.ShapeDtypeStruct((bsz * seq, A_HEADS * A_HEAD_DIM), BF16),
        grid=(bsz, nq),
        in_specs=[pl.BlockSpec((None, QB, A_IDX_HEADS * QB), blk),
                  pl.BlockSpec((None, A_IDX_HEADS, QB), blk),
                  pl.BlockSpec((None, QB, A_HEADS * QB), blk),
                  pl.BlockSpec((nq, QB, 128), per_batch, pipeline_mode=once),
                  pl.BlockSpec((nq, QB, 128), per_batch, pipeline_mode=once),
                  pl.BlockSpec((nq, VT_ROWS, QB), per_batch, pipeline_mode=once),
                  pl.BlockSpec((BIAS_ROWS, A_HEADS * QB), lambda b, q: (0, 0), pipeline_mode=once),
                  pl.BlockSpec((QB, 1024), lambda b, q: (b * nq + q, EV_AZ // 1024))],
        out_specs=pl.BlockSpec((QB, 1024), lambda b, q: (b * nq + q, 0)),
        scratch_shapes=[pltpu.VMEM((seq, QB), I32),
                        pltpu.VMEM((1, QB), I32),
                        pltpu.VMEM((1, A_HEADS * QB), F32),
                        pltpu.VMEM((VT_ROWS, A_HEADS * QB), F32),
                        pltpu.VMEM((2, FAR_BLOCKS * QB, A_HEADS * QB), BF16),
                        pltpu.VMEM((2, 1, A_HEADS * QB), F32),
                        pltpu.VMEM((FAR_BLOCKS * QB, A_HEADS * QB), BF16),
                        pltpu.VMEM((2 * QB, A_HEADS * QB), BF16)],
        compiler_params=_cparams(("arbitrary", "arbitrary")),
        name="dsa_attend",
    )(qit, wt, qta, kidx, katt, vt, bias, h)


GDN_TM = 1024
GDN_GROUP = 4
GDN_SCAN_HEADS = 8
GDN_PREP_HEADS = 2


def _gdn_prep_kernel(alog_ref, dtb_ref, q_ref, k_ref, v_ref, qh_ref, kh_ref, vh_ref, misc_ref,
                     cq_ref, ck_ref, cv_ref,
                     u_ref, w_ref, qd_ref, kdt_ref, attn_ref, egl_ref):
    i = pl.program_id(1)
    hp = pl.program_id(2)
    tm = q_ref.shape[0]
    nchunk = tm // B_CHUNK
    heads = range(GDN_PREP_HEADS)
    lanes = [slice(hh * 128, (hh + 1) * 128) for hh in heads]

    def conv_silu(x_ref, halo_ref, c_ref, ln):
        halo = jnp.where(i > 0, halo_ref[:, ln], 0.0)
        ext = jnp.concatenate([halo, x_ref[:, ln]], axis=0)
        cw = c_ref[:, ln]
        y = ext[8:, :] * cw[CONV_K - 1:CONV_K, :]
        for d in range(1, CONV_K):
            y = y + pltpu.roll(ext, d, 0)[8:, :] * cw[CONV_K - 1 - d:CONV_K - d, :]
        return _silu(y)

    q = [conv_silu(q_ref, qh_ref, cq_ref, ln) for ln in lanes]
    k = [conv_silu(k_ref, kh_ref, ck_ref, ln) for ln in lanes]
    v = [conv_silu(v_ref, vh_ref, cv_ref, ln) for ln in lanes]
    qn = [x * lax.rsqrt(jnp.sum(x * x, axis=-1, keepdims=True) + EPS) * (B_HEAD_DIM ** -0.5) for x in q]
    kn = [x * lax.rsqrt(jnp.sum(x * x, axis=-1, keepdims=True) + EPS) for x in k]

    misc = misc_ref[...]
    sel_r = lax.broadcasted_iota(I32, (128, 256), 0)
    sel_c = lax.broadcasted_iota(I32, (128, 256), 1)
    beta, g = [], []
    for hh in heads:
        hd = hp * GDN_PREP_HEADS + hh
        pick = jnp.where(sel_r == jnp.where(sel_c < 128, MISC_BA + hd, MISC_BB + hd), 1.0, 0.0)
        bab = _dot_xsel(misc, pick)
        beta.append(_sigmoid(bab[:, 128:256]))
        xg = bab[:, 0:128] + dtb_ref[hd]
        softplus = jnp.maximum(xg, 0.0) + jnp.log(1.0 + jnp.exp(-jnp.abs(xg)))
        g.append(-jnp.exp(jnp.full((1, 128), alog_ref[hd], F32)) * softplus)

    sl = 256
    r2 = lax.broadcasted_iota(I32, (sl, sl), 0)
    c2 = lax.broadcasted_iota(I32, (sl, sl), 1)
    same = (r2 // B_CHUNK) == (c2 // B_CHUNK)
    tri_blk = jnp.concatenate([jnp.where(same & (c2 <= r2), 1.0, 0.0), jnp.where(same, 1.0, 0.0)], axis=0)
    nsl = tm // sl
    gc, gl = [], []
    for hh in heads:
        both = _dot_sel(tri_blk, jnp.concatenate([g[hh][s * sl:(s + 1) * sl, :] for s in range(nsl)], axis=1))
        gc.append(jnp.concatenate([both[0:sl, s * 128:(s + 1) * 128] for s in range(nsl)], axis=0))
        gl.append(jnp.concatenate([both[sl:2 * sl, s * 128:(s + 1) * 128] for s in range(nsl)], axis=0))

    kb = [kn[hh] * beta[hh] for hh in heads]
    vb = [v[hh] * beta[hh] for hh in heads]
    egc = [jnp.exp(x) for x in gc]
    kbg = [kb[hh] * egc[hh] for hh in heads]
    for hh in heads:
        qd_ref[hh] = (qn[hh] * egc[hh]).astype(BF16)
        kd = kn[hh] * jnp.exp(gl[hh] - gc[hh])
        for m in range(tm // 128):
            kdt_ref[hh, :, m * 128:(m + 1) * 128] = kd[m * 128:(m + 1) * 128, :].T.astype(BF16)

    gw = GDN_GROUP * B_CHUNK
    ri = lax.broadcasted_iota(I32, (B_CHUNK, gw), 0)
    lj = lax.broadcasted_iota(I32, (B_CHUNK, gw), 1)
    lb = lj // B_CHUNK
    lj = lj % B_CHUNK
    bmask = (lax.broadcasted_iota(I32, (gw, gw), 0) // B_CHUNK
             == lax.broadcasted_iota(I32, (gw, gw), 1) // B_CHUNK)

    def fold(x):
        xm = jnp.where(bmask, x, 0.0)
        out = xm[0:B_CHUNK, :]
        for c in range(1, GDN_GROUP):
            out = out + xm[c * B_CHUNK:(c + 1) * B_CHUNK, :]
        return out

    bmask_b = jnp.where(bmask, 1.0, 0.0).astype(BF16)

    def bdiag(cat):
        return jnp.concatenate([cat] * GDN_GROUP, axis=0) * bmask_b

    probs = [(hh, grp) for grp in range(tm // gw) for hh in heads]
    a_cat = []
    for hh, grp in probs:
        rs = slice(grp * gw, (grp + 1) * gw)
        gcs = gc[hh][rs, :]
        col = jnp.concatenate([gcs[0:B_CHUNK, :]] * 2, axis=1)
        for c in range(1, GDN_GROUP):
            col = jnp.where(lb == c, jnp.concatenate([gcs[c * B_CHUNK:(c + 1) * B_CHUNK, :]] * 2, axis=1), col)
        rowf = jnp.concatenate([gcs[m * 128:(m + 1) * 128, :].T[0:B_CHUNK, :] for m in range(gw // 128)],
                               axis=1)
        decay = jnp.exp(jnp.where(ri >= lj, col - rowf, NEG_INF))
        kg = kn[hh][rs, :]
        a_cat.append(jnp.where(ri > lj, fold(_dot_nt(kb[hh][rs, :], kg)) * decay, 0.0))
        attn_ref[hh, grp] = jnp.where(ri >= lj, fold(_dot_nt(qn[hh][rs, :], kg)) * decay, 0.0).astype(BF16)
    eye = jnp.where(ri == lj, 1.0, 0.0)
    inv = [eye - a for a in a_cat]
    pw = [_dot_x3(a, a, expand_b=bdiag) for a in a_cat]
    for step in range(5):
        for p in range(len(probs)):
            if step < 4:
                both = _dot_x3(jnp.concatenate([inv[p], pw[p]], axis=0), pw[p], expand_b=bdiag)
                inv[p] = inv[p] + both[0:B_CHUNK, :]
                pw[p] = both[B_CHUNK:, :]
            else:
                inv[p] = inv[p] + _dot_x3(inv[p], pw[p], expand_b=bdiag)
    for p, (hh, grp) in enumerate(probs):
        rs = slice(grp * gw, (grp + 1) * gw)
        sol = _dot_x3(inv[p], jnp.concatenate([vb[hh][rs, :], kbg[hh][rs, :]], axis=1), expand_a=bdiag)
        u_ref[hh, rs, :] = sol[:, 0:B_HEAD_DIM]
        w_ref[hh, rs, :] = sol[:, B_HEAD_DIM:].astype(BF16)
    for hh in heads:
        for c in range(nchunk):
            egl_ref[hh, c:c + 1, :] = jnp.exp(gl[hh][c * B_CHUNK:c * B_CHUNK + 1, :])
        if nchunk < 8:
            egl_ref[hh, nchunk:, :] = jnp.zeros((8 - nchunk, 128), F32)


def _gdn_prep(h, bsz, seq, b_conv, b_a_log, b_dt_bias):
    tm = min(GDN_TM, seq)
    nt = seq // tm
    nrow = max(tm // B_CHUNK, 8)
    gw = GDN_GROUP * B_CHUNK
    hs = (bsz, B_HEADS, seq, B_HEAD_DIM)
    hg = GDN_PREP_HEADS
    hw = hg * 128
    cur = lambda off: (lambda b, i, hp: (b * nt + i, off // hw + hp))
    halo = lambda off: (lambda b, i, hp: (jnp.maximum((b * nt + i) * (tm // 8) - 1, 0), off // hw + hp))
    cw = lambda off: (lambda b, i, hp: (0, off // hw + hp))
    out = lambda b, i, hp: (b, hp, i, 0)
    smem = pl.BlockSpec(memory_space=pltpu.SMEM)
    return pl.pallas_call(
        _gdn_prep_kernel,
        out_shape=(jax.ShapeDtypeStruct(hs, F32), jax.ShapeDtypeStruct(hs, BF16),
                   jax.ShapeDtypeStruct(hs, BF16),
                   jax.ShapeDtypeStruct((bsz, B_HEADS, B_HEAD_DIM, seq), BF16),
                   jax.ShapeDtypeStruct((bsz, B_HEADS, seq // gw, B_CHUNK, gw), BF16),
                   jax.ShapeDtypeStruct((bsz, B_HEADS, nt * nrow, 128), F32)),
        grid=(bsz, nt, B_HEADS // hg),
        in_specs=[smem, smem,
                  pl.BlockSpec((tm, hw), cur(EV_BQ)), pl.BlockSpec((tm, hw), cur(EV_BK)),
                  pl.BlockSpec((tm, hw), cur(EV_BV)),
                  pl.BlockSpec((8, hw), halo(EV_BQ)), pl.BlockSpec((8, hw), halo(EV_BK)),
                  pl.BlockSpec((8, hw), halo(EV_BV)),
                  pl.BlockSpec((tm, 128), lambda b, i, hp: (b * nt + i, EV_MISC // 128)),
                  pl.BlockSpec((CONV_K, hw), cw(0)), pl.BlockSpec((CONV_K, hw), cw(1024)),
                  pl.BlockSpec((CONV_K, hw), cw(2048))],
        out_specs=(pl.BlockSpec((None, hg, tm, 128), out), pl.BlockSpec((None, hg, tm, 128), out),
                   pl.BlockSpec((None, hg, tm, 128), out),
                   pl.BlockSpec((None, hg, B_HEAD_DIM, tm), lambda b, i, hp: (b, hp, 0, i)),
                   pl.BlockSpec((None, hg, tm // gw, B_CHUNK, gw), lambda b, i, hp: (b, hp, i, 0, 0)),
                   pl.BlockSpec((None, hg, nrow, 128), out)),
        compiler_params=_cparams(("parallel", "parallel", "parallel")),
        name="gdn_prep",
    )(b_a_log, b_dt_bias, h, h, h, h, h, h, h, b_conv, b_conv, b_conv)


def _gdn_scan_kernel(u_ref, w_ref, qd_ref, kdt_ref, attn_ref, egl_ref, z_ref, gn_ref, o_ref, s_ref):
    @pl.when(pl.program_id(2) == 0)
    def _():
        s_ref[...] = jnp.zeros(s_ref.shape, F32)

    tm = u_ref.shape[1]
    gw = GDN_GROUP * B_CHUNK
    gain = gn_ref[...]
    zero = jnp.zeros((B_CHUNK, B_HEAD_DIM), BF16)
    states = [s_ref[hh] for hh in range(GDN_SCAN_HEADS)]
    for c in range(tm // B_CHUNK):
        grp, ci = divmod(c, GDN_GROUP)
        rs = slice(c * B_CHUNK, (c + 1) * B_CHUNK)
        for hh in range(GDN_SCAN_HEADS):
            sb = states[hh].astype(BF16)
            v_new = u_ref[hh, rs, :] - _dot(w_ref[hh, rs, :], sb)
            vpad = jnp.concatenate([zero] * ci + [v_new.astype(BF16)] + [zero] * (GDN_GROUP - 1 - ci), axis=0)
            o = _dot(qd_ref[hh, rs, :], sb) + _dot(attn_ref[hh, grp], vpad)
            states[hh] = (states[hh] * egl_ref[hh, c:c + 1, :]
                          + _dot(kdt_ref[hh, :, grp * gw:(grp + 1) * gw], vpad))
            on = o * lax.rsqrt(jnp.mean(o * o, axis=-1, keepdims=True) + EPS) * gain
            lanes = slice(hh * B_HEAD_DIM, (hh + 1) * B_HEAD_DIM)
            o_ref[rs, lanes] = (on * _silu(z_ref[rs, lanes])).astype(BF16)
    for hh in range(GDN_SCAN_HEADS):
        s_ref[hh] = states[hh]


def _gdn_scan(h, bsz, seq, u, w, qd, kd, attn, egl, b_out_norm):
    tm = min(GDN_TM, seq)
    nt = seq // tm
    nrow = max(tm // B_CHUNK, 8)
    gw = GDN_GROUP * B_CHUNK
    hg = GDN_SCAN_HEADS
    blk = lambda b, hp, i: (b, hp, i, 0)
    return pl.pallas_call(
        _gdn_scan_kernel,
        out_shape=jax.ShapeDtypeStruct((bsz * seq, B_HEADS * B_HEAD_DIM), BF16),
        grid=(bsz, B_HEADS // hg, nt),
        in_specs=[pl.BlockSpec((None, hg, tm, 128), blk), pl.BlockSpec((None, hg, tm, 128), blk),
                  pl.BlockSpec((None, hg, tm, 128), blk),
                  pl.BlockSpec((None, hg, B_HEAD_DIM, tm), lambda b, hp, i: (b, hp, 0, i)),
                  pl.BlockSpec((None, hg, tm // gw, B_CHUNK, gw), lambda b, hp, i: (b, hp, i, 0, 0)),
                  pl.BlockSpec((None, hg, nrow, 128), blk),
                  pl.BlockSpec((tm, hg * 128), lambda b, hp, i: (b * nt + i, EV_BZ // (hg * 128) + hp)),
                  pl.BlockSpec((1, 128), lambda b, hp, i: (0, 0))],
        out_specs=pl.BlockSpec((tm, hg * 128), lambda b, hp, i: (b * nt + i, hp)),
        scratch_shapes=[pltpu.VMEM((hg, B_HEAD_DIM, B_HEAD_DIM), F32)],
        compiler_params=_cparams(("parallel", "parallel", "arbitrary")),
        name="gdn_scan",
    )(u, w, qd, kd, attn, egl, h, b_out_norm.reshape(1, -1))


def _rope_kernel(inv_ref, cos_ref, sin_ref):
    tm = cos_ref.shape[0]
    pos = (pl.program_id(0) * tm + lax.broadcasted_iota(I32, (tm, 128), 0)).astype(F32)
    ang = pos * inv_ref[...]
    lane = lax.broadcasted_iota(I32, (tm, 128), 1)
    cos_ref[...] = jnp.cos(ang)
    sin_ref[...] = jnp.where(lane < 64, -jnp.sin(ang), jnp.sin(ang))


def _rope_tables(seq):
    half = C_QK_DIM // 2
    inv = 1.0 / (ROPE_BASE ** jnp.linspace(0.0, 1.0, half, dtype=F32))
    inv2 = jnp.concatenate([inv, inv]).reshape(1, C_QK_DIM)
    tm = min(1024, seq)
    return pl.pallas_call(
        _rope_kernel,
        out_shape=(jax.ShapeDtypeStruct((seq, C_QK_DIM), F32),) * 2,
        grid=(seq // tm,),
        in_specs=[pl.BlockSpec((1, C_QK_DIM), lambda i: (0, 0))],
        out_specs=(pl.BlockSpec((tm, C_QK_DIM), lambda i: (i, 0)),) * 2,
        compiler_params=_cparams(("parallel",)),
        name="rope_tables",
    )(inv2)


RET_TM = 4096


def _retention_kernel(lg_ref, q_ref, k_ref, v_ref, z_ref, cos_ref, sin_ref, gn_ref, o_ref,
                      r_ref, dm_ref, zeta_ref, xi_ref):
    hd = pl.program_id(1)
    lg = lg_ref[0, hd]

    @pl.when(pl.program_id(2) == 0)
    def _():
        r_ref[...] = jnp.zeros(r_ref.shape, F32)
        ri = lax.broadcasted_iota(I32, (C_CHUNK, C_CHUNK), 0)
        ci = lax.broadcasted_iota(I32, (C_CHUNK, C_CHUNK), 1)
        diff = (ri - ci).astype(F32)
        dm_ref[...] = jnp.where(diff >= 0, jnp.exp(jnp.maximum(diff, 0.0) * lg), 0.0)
        zeta_ref[...] = jnp.exp((C_CHUNK - 1 - ri).astype(F32) * lg)
        rv = lax.broadcasted_iota(I32, (C_CHUNK, C_V_DIM), 0).astype(F32)
        xi_ref[...] = jnp.exp((rv + 1.0) * lg)

    g_chunk = lg_ref[1, hd]
    tm = q_ref.shape[0]
    state = r_ref[...]
    dmask = dm_ref[...]
    gain = gn_ref[...]
    for c in range(tm // C_CHUNK):
        rs = slice(c * C_CHUNK, (c + 1) * C_CHUNK)
        cos = cos_ref[rs, :]
        sin = sin_ref[rs, :]
        q = q_ref[rs, :]
        k = k_ref[rs, :]
        qr = q * cos + pltpu.roll(q, C_QK_DIM // 2, 1) * sin
        kr = (k * cos + pltpu.roll(k, C_QK_DIM // 2, 1) * sin) * (C_QK_DIM ** -0.5)
        vb = v_ref[rs, :].astype(BF16)
        qb = qr.astype(BF16)
        s = _dot_nt(qb, kr.astype(BF16)) * dmask
        o = _dot(s.astype(BF16), vb) + _dot(qb, state.astype(BF16)) * xi_ref[...]
        state = state * g_chunk + _dot((kr * zeta_ref[...]).T.astype(BF16), vb)
        mu = jnp.mean(o, axis=-1, keepdims=True)
        oc = o - mu
        var = jnp.mean(oc * oc, axis=-1, keepdims=True)
        y = oc * lax.rsqrt(var + EPS) * gain
        o_ref[rs, :] = (y * _silu(z_ref[rs, :])).astype(BF16)
    r_ref[...] = state


def _retention(h, bsz, seq, cos2, sin2, c_out_norm):
    tm = min(RET_TM, seq)
    nt = seq // tm
    lg = np.log1p(-np.exp2(-5.0 - np.arange(C_HEADS, dtype=np.float32))).astype(np.float32)
    lg = np.stack([lg, np.exp(np.float32(C_CHUNK) * lg).astype(np.float32)])
    return pl.pallas_call(
        _retention_kernel,
        out_shape=jax.ShapeDtypeStruct((bsz * seq, C_HEADS * C_V_DIM), BF16),
        grid=(bsz, C_HEADS, nt),
        in_specs=[pl.BlockSpec(memory_space=pltpu.SMEM),
                  pl.BlockSpec((tm, 128), lambda b, hd, i: (b * nt + i, OD_CQ // 128 + hd)),
                  pl.BlockSpec((tm, 128), lambda b, hd, i: (b * nt + i, OD_CK // 128 + hd)),
                  pl.BlockSpec((tm, 256), lambda b, hd, i: (b * nt + i, OD_CV // 256 + hd)),
                  pl.BlockSpec((tm, 256), lambda b, hd, i: (b * nt + i, OD_CZ // 256 + hd)),
                  pl.BlockSpec((tm, 128), lambda b, hd, i: (i, 0)),
                  pl.BlockSpec((tm, 128), lambda b, hd, i: (i, 0)),
                  pl.BlockSpec((1, C_V_DIM), lambda b, hd, i: (0, hd))],
        out_specs=pl.BlockSpec((tm, C_V_DIM), lambda b, hd, i: (b * nt + i, hd)),
        scratch_shapes=[pltpu.VMEM((C_QK_DIM, C_V_DIM), F32),
                        pltpu.VMEM((C_CHUNK, C_CHUNK), F32),
                        pltpu.VMEM((C_CHUNK, C_QK_DIM), F32),
                        pltpu.VMEM((C_CHUNK, C_V_DIM), F32)],
        compiler_params=_cparams(("parallel", "parallel", "arbitrary")),
        name="retention",
    )(jnp.asarray(lg), h, h, h, h, cos2, sin2, c_out_norm.reshape(1, -1))


def _s5_param_kernel(lre_ref, lim_ref, ldt_ref, lrex_ref, limx_ref, ldtx_ref, bre_ref, bim_ref,
                     are_ref, aim_ref, bbre_ref, bbim_ref):
    def disc(lre, lim, ldt):
        lr = jnp.minimum(lre, -1e-4)
        dt = jnp.exp(ldt)
        mag = jnp.exp(lr * dt)
        return lr, lim, mag * jnp.cos(lim * dt), mag * jnp.sin(lim * dt)

    _, _, a_re, a_im = disc(lre_ref[...], lim_ref[...], ldt_ref[...])
    are_ref[...] = a_re
    aim_ref[...] = a_im
    lr, li, ax_re, ax_im = disc(lrex_ref[...], limx_ref[...], ldtx_ref[...])
    den = lr * lr + li * li
    f_re = ((ax_re - 1.0) * lr + ax_im * li) / den
    f_im = (ax_im * lr - (ax_re - 1.0) * li) / den
    bbre_ref[...] = f_re * bre_ref[...] - f_im * bim_ref[...]
    bbim_ref[...] = f_re * bim_ref[...] + f_im * bre_ref[...]


def _s5_params(lam_re, lam_im, log_dt, b_re, b_im, c_re, c_im):
    g, p, ch = D_GROUPS, D_STATE, D_GROUP
    ldt = jnp.broadcast_to(log_dt[:, None], (g, p))
    rep = lambda a: jnp.repeat(a, ch, axis=1)
    vm = pl.BlockSpec(memory_space=pltpu.VMEM)
    a_re, a_im, bb_re, bb_im = pl.pallas_call(
        _s5_param_kernel,
        out_shape=(jax.ShapeDtypeStruct((g, p), F32), jax.ShapeDtypeStruct((g, p), F32),
                   jax.ShapeDtypeStruct((g, p * ch), F32), jax.ShapeDtypeStruct((g, p * ch), F32)),
        in_specs=[vm] * 8, out_specs=(vm,) * 4,
        name="s5_params",
    )(lam_re, lam_im, ldt, rep(lam_re), rep(lam_im), rep(ldt),
      b_re.reshape(g, p * ch), b_im.reshape(g, p * ch))
    eye = jnp.eye(D_SETS * 4, dtype=F32)

    def pack_b(bb):
        bb = bb.reshape(D_SETS, 16, p, ch)
        return jnp.einsum('sgpi,gh->sgihp', bb, eye).reshape(D_SETS, D_SET_CH, D_SET_ST)

    def pack_c(c):
        c = c.reshape(D_SETS, 16, ch, p)
        return jnp.einsum('sgjp,gh->sgphj', c, eye).reshape(D_SETS, D_SET_ST, D_SET_CH)

    bd = jnp.concatenate([pack_b(bb_re), pack_b(bb_im)], axis=-1).astype(BF16)
    slab = (D_SETS, S5_SLAB, 128)
    return (a_re.reshape(slab), a_im.reshape(slab), bd,
            pack_c(c_re).astype(BF16), pack_c(c_im).astype(BF16))


S5_TM = 512
S5_SLAB = D_SET_ST // 128


def _s5_kernel(u_ref, z_ref, are_ref, aim_ref, bd_ref, cre_ref, cim_ref, dskip_ref, wglu_ref, bglu_ref,
               o_ref, hre_ref, him_ref, y_ref, *x_refs):
    @pl.when(pl.program_id(1) == 0)
    def _():
        hre_ref[...] = jnp.zeros(hre_ref.shape, F32)
        him_ref[...] = jnp.zeros(him_ref.shape, F32)

    xre, xim = x_refs[:D_SETS], x_refs[D_SETS:]
    tm = u_ref.shape[0]
    u = u_ref[...]
    ub = u.astype(BF16)
    for s in range(D_SETS):
        bu = _dot(ub[:, s * D_SET_CH:(s + 1) * D_SET_CH], bd_ref[s])
        for k in range(S5_SLAB):
            xre[s][pl.ds(k, tm, stride=S5_SLAB), :] = bu[:, k * 128:(k + 1) * 128]
            xim[s][pl.ds(k, tm, stride=S5_SLAB), :] = bu[:, D_SET_ST + k * 128:D_SET_ST + (k + 1) * 128]
    a_re = [are_ref[s] for s in range(D_SETS)]
    a_im = [aim_ref[s] for s in range(D_SETS)]

    def step(t, carry):
        rows = pl.ds(pl.multiple_of(t * S5_SLAB, S5_SLAB), S5_SLAB)
        out = []
        for s in range(D_SETS):
            hr, hi = carry[2 * s], carry[2 * s + 1]
            xr = xre[s][rows, :] + a_re[s] * hr - a_im[s] * hi
            xi = xim[s][rows, :] + a_re[s] * hi + a_im[s] * hr
            xre[s][rows, :] = xr
            xim[s][rows, :] = xi
            out += [xr, xi]
        return tuple(out)

    carry = []
    for s in range(D_SETS):
        carry += [hre_ref[s], him_ref[s]]
    carry = lax.fori_loop(0, tm, step, tuple(carry), unroll=8)
    for s in range(D_SETS):
        hre_ref[s] = carry[2 * s]
        him_ref[s] = carry[2 * s + 1]
        x_r = jnp.concatenate([xre[s][pl.ds(k, tm, stride=S5_SLAB), :] for k in range(S5_SLAB)], axis=1)
        x_i = jnp.concatenate([xim[s][pl.ds(k, tm, stride=S5_SLAB), :] for k in range(S5_SLAB)], axis=1)
        y_ref[:, s * D_SET_CH:(s + 1) * D_SET_CH] = (
            _dot(x_r.astype(BF16), cre_ref[s]) - _dot(x_i.astype(BF16), cim_ref[s]))
    y = y_ref[...] + dskip_ref[...] * u
    y = 0.5 * y * (1.0 + jnp.tanh(math.sqrt(2.0 / math.pi) * (y + 0.044715 * (y * y * y))))
    gate = _sigmoid(_dot(y.astype(BF16), wglu_ref[...]) + bglu_ref[...])
    o_ref[...] = (y * gate * _silu(z_ref[...])).astype(BF16)


def _s5(h, bsz, seq, a_re, a_im, bd, cd_re, cd_im, d_skip, w_glu, b_glu):
    tm = min(S5_TM, seq)
    nt = seq // tm
    full = lambda *shape: pl.BlockSpec(shape, lambda b, i: (0,) * len(shape))
    width = D_GROUPS * D_GROUP
    return pl.pallas_call(
        _s5_kernel,
        out_shape=jax.ShapeDtypeStruct((bsz * seq, width), BF16),
        grid=(bsz, nt),
        in_specs=[pl.BlockSpec((tm, width), lambda b, i: (b * nt + i, OD_DU // width)),
                  pl.BlockSpec((tm, width), lambda b, i: (b * nt + i, OD_DZ // width)),
                  full(D_SETS, S5_SLAB, 128), full(D_SETS, S5_SLAB, 128),
                  full(D_SETS, D_SET_CH, 2 * D_SET_ST),
                  full(D_SETS, D_SET_ST, D_SET_CH), full(D_SETS, D_SET_ST, D_SET_CH),
                  full(1, width), full(width, width), full(1, width)],
        out_specs=pl.BlockSpec((tm, width), lambda b, i: (b * nt + i, 0)),
        scratch_shapes=([pltpu.VMEM((D_SETS, S5_SLAB, 128), F32), pltpu.VMEM((D_SETS, S5_SLAB, 128), F32),
                         pltpu.VMEM((tm, width), F32)]
                        + [pltpu.VMEM((tm * S5_SLAB, 128), F32)] * (2 * D_SETS)),
        compiler_params=_cparams(("parallel", "arbitrary")),
        name="s5",
    )(h, h, a_re, a_im, bd, cd_re, cd_im, d_skip.reshape(1, -1), w_glu.astype(BF16),
      b_glu.reshape(1, -1))


def _pack_even_w(w_in):
    sizes = (1024, A_KV_RANK, A_IDX_HEADS * A_IDX_DIM, A_IDX_DIM, A_IDX_HEADS, 1024,
             3 * 1024, B_HEADS, B_HEADS, 1024)
    parts, start = [], 0
    for s in sizes:
        parts.append(w_in[:, start:start + s])
        start += s
    aq, ckv, qi, ki, wi, az, bqkv, ba, bb, bz = parts
    pad = jnp.zeros((w_in.shape[0], EV_WIDTH - start), w_in.dtype)
    return jnp.concatenate([aq, az, bqkv, bz, qi, ckv, ki, wi, ba, bb, pad], axis=1).astype(BF16)


def _even_layer(x2d, bsz, seq, bias, norm_g, w_in, a_q_norm, a_kv_norm, w_kv_up, a_k_norm,
                b_conv, b_a_log, b_dt_bias, b_out_norm, w_out):
    h = _norm_proj(x2d, norm_g, _pack_even_w(w_in), tn=768)
    qta, qit, wt, kidx, katt, vt = _dsa_prep(h, bsz * seq // QB, a_q_norm, a_kv_norm, w_kv_up, a_k_norm)
    mix_a = _dsa_attend(h, bsz, seq, qta, qit, wt, kidx, katt, vt, bias)
    u, w, qd, kd, attn, egl = _gdn_prep(h, bsz, seq, b_conv, b_a_log, b_dt_bias)
    mix_b = _gdn_scan(h, bsz, seq, u, w, qd, kd, attn, egl, b_out_norm)
    return _out_proj(x2d, mix_a, mix_b, w_out)


def _odd_layer(x2d, bsz, seq, cos2, sin2, norm_g, w_in, c_out_norm, lam_re, lam_im, log_dt,
               b_re, b_im, c_re, c_im, d_skip, w_glu, b_glu, w_out):
    h = _norm_proj(x2d, norm_g, w_in.astype(BF16), tn=1024)
    mix_c = _retention(h, bsz, seq, cos2, sin2, c_out_norm)
    s5p = _s5_params(lam_re, lam_im, log_dt, b_re, b_im, c_re, c_im)
    mix_d = _s5(h, bsz, seq, *s5p, d_skip, w_glu, b_glu)
    return _out_proj(x2d, mix_c, mix_d, w_out)


def kernel(x, rel_bias, ev_norm, ev_w_in, ev_a_q_norm, ev_a_kv_norm, ev_w_kv_up, ev_a_k_norm,
           ev_b_conv, ev_b_a_log, ev_b_dt_bias, ev_b_out_norm, ev_w_out,
           od_norm, od_w_in, od_c_out_norm, od_lam_re, od_lam_im, od_log_dt,
           od_b_re, od_b_im, od_c_re, od_c_im, od_d_skip, od_w_glu, od_b_glu, od_w_out):
    bsz, seq, d = x.shape
    depth = ev_norm.shape[0] + od_norm.shape[0]
    x2d = x.reshape(bsz * seq, d)
    bias = _bias_table(rel_bias)
    cos2, sin2 = _rope_tables(seq)
    for layer in range(depth):
        i = layer // 2
        if layer % 2 == 0:
            x2d = _even_layer(x2d, bsz, seq, bias, ev_norm[i], ev_w_in[i], ev_a_q_norm[i],
                              ev_a_kv_norm[i], ev_w_kv_up[i], ev_a_k_norm[i], ev_b_conv[i],
                              ev_b_a_log[i], ev_b_dt_bias[i], ev_b_out_norm[i], ev_w_out[i])
        else:
            x2d = _odd_layer(x2d, bsz, seq, cos2, sin2, od_norm[i], od_w_in[i], od_c_out_norm[i],
                             od_lam_re[i], od_lam_im[i], od_log_dt[i], od_b_re[i], od_b_im[i],
                             od_c_re[i], od_c_im[i], od_d_skip[i], od_w_glu[i], od_b_glu[i], od_w_out[i])
    return x2d.reshape(bsz, seq, d)
```

```python
import functools
import math

import numpy as np
import jax
import jax.numpy as jnp
from jax import lax
from jax.experimental import pallas as pl
from jax.experimental.pallas import tpu as pltpu

F32 = jnp.float32
BF16 = jnp.bfloat16
I32 = jnp.int32

EPS = 1e-6
LOG2E = 1.4426950408889634
NEG_INF = float("-inf")
INT_MIN = -(2 ** 31)

A_HEADS, A_HEAD_DIM, A_KV_RANK = 16, 64, 128
A_IDX_HEADS, A_IDX_DIM, A_TOPK_MAX = 8, 64, 256
QB = 128
REL_BUCKETS, REL_MAX_DIST = 32, 128
VT_ROWS = 80
SB = 2 * QB
FAR_BLOCKS = 4
DSA_PREP_TILES = 2
SCORE_ROWS = 4 * QB
BIAS_ROWS = 5 * QB
MASKED = -(2.0 ** 100)
M_INIT = -(2.0 ** 60)
TIE_WALK_MAX = 8
SCORE_UNROLL = 4
COUNT_UNROLL = 4
B_HEADS, B_HEAD_DIM, CONV_K, B_CHUNK = 8, 128, 4, 64
C_HEADS, C_QK_DIM, C_V_DIM, C_CHUNK = 4, 128, 256, 128
ROPE_BASE = 10000.0
D_GROUP, D_STATE, D_GROUPS = 16, 64, 64
D_SETS, D_SET_CH, D_SET_ST = 4, 256, 1024

EV_AQ, EV_AZ, EV_BQ, EV_BK, EV_BV, EV_BZ, EV_QI, EV_CKV, EV_MISC = (
    0, 1024, 2048, 3072, 4096, 5120, 6144, 6656, 6784)
EV_WIDTH = 6912
MISC_KI, MISC_WI, MISC_BA, MISC_BB = 0, 64, 72, 80
OD_CQ, OD_CK, OD_CV, OD_CZ, OD_DU, OD_DZ = 0, 512, 1024, 2048, 3072, 4096

VMEM_LIMIT = 48 * 1024 * 1024


def _cparams(sem):
    return pltpu.CompilerParams(dimension_semantics=sem, vmem_limit_bytes=VMEM_LIMIT)


def _dot(a, b):
    return jnp.dot(a, b, preferred_element_type=F32)


def _dot_nt(a, b):
    return lax.dot_general(a, b, (((1,), (1,)), ((), ())), preferred_element_type=F32)


def _split_bf16(x, n):
    parts = []
    for term in range(n):
        p = x.astype(BF16)
        parts.append(p)
        if term + 1 < n:
            x = x - p.astype(F32)
    return parts


def _dot_sel(sel, x):
    sel = sel.astype(BF16)
    hi, mid, lo = _split_bf16(x, 3)
    return _dot(sel, hi) + (_dot(sel, mid) + _dot(sel, lo))


def _dot_xsel(x, sel):
    sel = sel.astype(BF16)
    hi, mid, lo = _split_bf16(x, 3)
    return _dot(hi, sel) + (_dot(mid, sel) + _dot(lo, sel))


def _dot_x3(a, b, expand_a=None, expand_b=None):
    ah, al = _split_bf16(a, 2)
    bh, bl = _split_bf16(b, 2)
    if expand_a is not None:
        ah, al = expand_a(ah), expand_a(al)
    if expand_b is not None:
        bh, bl = expand_b(bh), expand_b(bl)
    return _dot(ah, bh) + (_dot(ah, bl) + _dot(al, bh))


def _sigmoid(x):
    return 1.0 / (1.0 + jnp.exp(-x))


def _silu(x):
    return x * _sigmoid(x)


def _norm_proj_kernel(x_ref, g_ref, w_ref, o_ref, xn_ref):
    @pl.when(pl.program_id(1) == 0)
    def _():
        x = x_ref[...]
        ms = jnp.mean(x * x, axis=-1, keepdims=True)
        xn_ref[...] = (x * lax.rsqrt(ms + EPS) * g_ref[...]).astype(BF16)

    o_ref[...] = _dot(xn_ref[...], w_ref[...])


def _norm_proj(x2d, gain, w_bf16, tn, tm=2048):
    t, d = x2d.shape
    n = w_bf16.shape[1]
    tm = min(tm, t)
    return pl.pallas_call(
        _norm_proj_kernel,
        out_shape=jax.ShapeDtypeStruct((t, n), F32),
        grid=(t // tm, n // tn),
        in_specs=[pl.BlockSpec((tm, d), lambda i, j: (i, 0)),
                  pl.BlockSpec((1, d), lambda i, j: (0, 0)),
                  pl.BlockSpec((d, tn), lambda i, j: (0, j))],
        out_specs=pl.BlockSpec((tm, tn), lambda i, j: (i, j)),
        scratch_shapes=[pltpu.VMEM((tm, d), BF16)],
        compiler_params=_cparams(("parallel", "arbitrary")),
        name="norm_proj",
    )(x2d, gain.reshape(1, d), w_bf16)


def _out_proj_kernel(x_ref, a_ref, b_ref, wa_ref, wb_ref, o_ref):
    o_ref[...] = x_ref[...] + _dot(a_ref[...], wa_ref[...]) + _dot(b_ref[...], wb_ref[...])


def _out_proj(x2d, mix_a, mix_b, w_out, tm=1024):
    t, d = x2d.shape
    half = mix_a.shape[1]
    tm = min(tm, t)
    wa = w_out[:half].astype(BF16)
    wb = w_out[half:].astype(BF16)
    return pl.pallas_call(
        _out_proj_kernel,
        out_shape=jax.ShapeDtypeStruct((t, d), F32),
        grid=(t // tm,),
        in_specs=[pl.BlockSpec((tm, d), lambda i: (i, 0)),
                  pl.BlockSpec((tm, half), lambda i: (i, 0)),
                  pl.BlockSpec((tm, half), lambda i: (i, 0)),
                  pl.BlockSpec((half, d), lambda i: (0, 0)),
                  pl.BlockSpec((half, d), lambda i: (0, 0))],
        out_specs=pl.BlockSpec((tm, d), lambda i: (i, 0)),
        compiler_params=_cparams(("parallel",)),
        name="out_proj",
    )(x2d, mix_a, mix_b, wa, wb)


def _t5_bucket_starts():
    exact = REL_BUCKETS // 2
    n = np.arange(0, 4 * REL_MAX_DIST, dtype=np.int64)
    ratio = np.maximum(n, 1).astype(np.float32) / np.float32(exact)
    large = exact + (np.log(ratio).astype(np.float32) / np.float32(math.log(REL_MAX_DIST / exact))
                     * np.float32(REL_BUCKETS - exact)).astype(np.int32)
    bucket = np.where(n < exact, n, np.minimum(large, REL_BUCKETS - 1))
    starts = [int(np.argmax(bucket >= b)) for b in range(REL_BUCKETS)]
    assert all(bucket[s] == b for b, s in enumerate(starts)) and np.all(np.diff(bucket) >= 0)
    assert starts[-1] <= QB, "distances beyond one key block must share the last bucket"
    return starts


_BUCKET_STARTS = _t5_bucket_starts()


def _bias_table_kernel(rb_ref, o_ref):
    row = lax.broadcasted_iota(I32, (BIAS_ROWS, QB), 0)
    lane = lax.broadcasted_iota(I32, (BIAS_ROWS, QB), 1)
    dist = lane + 3 * QB - row
    for h in range(A_HEADS):
        val = jnp.full((BIAS_ROWS, QB), rb_ref[0, h], F32)
        for b in range(1, REL_BUCKETS):
            val = jnp.where(dist >= _BUCKET_STARTS[b], rb_ref[b, h], val)
        val = (val - rb_ref[REL_BUCKETS - 1, h]) * LOG2E
        o_ref[:, h * QB:(h + 1) * QB] = jnp.where(dist >= 0, val, 0.0)


def _bias_table(rel_bias):
    return pl.pallas_call(
        _bias_table_kernel,
        out_shape=jax.ShapeDtypeStruct((BIAS_ROWS, A_HEADS * QB), F32),
        in_specs=[pl.BlockSpec(memory_space=pltpu.SMEM)],
        out_specs=pl.BlockSpec(memory_space=pltpu.VMEM),
        name="dsa_bias_table",
    )(rel_bias)


def _dsa_prep_kernel(aq_ref, qi_ref, ckv_ref, misc_ref, gq_ref, hsum_ref, gkv_ref, wkv_ref, gk_ref,
                     qta_ref, qit_ref, wt_ref, kidx_ref, katt_ref, vt_ref):
    zeros = jnp.zeros((QB, A_HEAD_DIM), F32)
    gq = gq_ref[...]
    ident = jnp.where(lax.broadcasted_iota(I32, (QB, QB), 0) == lax.broadcasted_iota(I32, (QB, QB), 1),
                      1.0, 0.0).astype(BF16)
    ones_row = lax.broadcasted_iota(I32, (VT_ROWS - A_HEAD_DIM, QB), 0) == 0

    def transpose_b(x):
        return _dot_nt(ident, x.astype(BF16)).astype(BF16)

    for t in range(DSA_PREP_TILES):
        rs = slice(t * QB, (t + 1) * QB)
        aq = aq_ref[rs, :]
        sq_hi, sq_lo = _split_bf16(aq * aq, 2)
        ssq = _dot(sq_hi, hsum_ref[...]) + _dot(sq_lo, hsum_ref[...])
        qn_all = aq * lax.rsqrt(ssq * (1.0 / A_HEAD_DIM) + EPS) * gq * (A_HEAD_DIM ** -0.5 * LOG2E)
        for h in range(A_HEADS):
            qn = qn_all[:, h * A_HEAD_DIM:(h + 1) * A_HEAD_DIM]
            qta_ref[t, :, h * QB:(h + 1) * QB] = transpose_b(jnp.concatenate([qn, zeros], axis=1))
        qi = qi_ref[rs, :]
        for h in range(A_IDX_HEADS):
            qih = qi[:, h * A_IDX_DIM:(h + 1) * A_IDX_DIM]
            qit_ref[t, :, h * QB:(h + 1) * QB] = transpose_b(jnp.concatenate([qih, zeros], axis=1))
        misc = misc_ref[rs, :]
        wt_ref[t] = misc.T[MISC_WI:MISC_WI + A_IDX_HEADS, :] * (A_IDX_HEADS ** -0.5 * A_IDX_DIM ** -0.5)
        c = ckv_ref[rs, :]
        cn = c * lax.rsqrt(jnp.mean(c * c, axis=-1, keepdims=True) + EPS) * gkv_ref[...]
        kv = _dot(cn.astype(BF16), wkv_ref[...])
        k = kv[:, :A_HEAD_DIM]
        kn = k * lax.rsqrt(jnp.mean(k * k, axis=-1, keepdims=True) + EPS) * gk_ref[...]
        kidx_ref[t] = jnp.concatenate([misc[:, MISC_KI:MISC_KI + A_IDX_DIM], zeros], axis=1).astype(BF16)
        katt_ref[t] = jnp.concatenate([kn, zeros], axis=1).astype(BF16)
        vt_ref[t, 0:A_HEAD_DIM, :] = transpose_b(kv)[A_HEAD_DIM:, :]
        vt_ref[t, A_HEAD_DIM:, :] = jnp.where(ones_row, 1.0, 0.0).astype(BF16)


def _dsa_prep(h, nblk, a_q_norm, a_kv_norm, w_kv_up, a_k_norm):
    cb = lambda width, off: off // width
    nt = DSA_PREP_TILES
    width = A_HEADS * A_HEAD_DIM
    head_of = np.arange(width) // A_HEAD_DIM
    same_head = (head_of[:, None] == head_of[None, :]).astype(np.float32)
    return pl.pallas_call(
        _dsa_prep_kernel,
        out_shape=(jax.ShapeDtypeStruct((nblk, QB, A_HEADS * QB), BF16),
                   jax.ShapeDtypeStruct((nblk, QB, A_IDX_HEADS * QB), BF16),
                   jax.ShapeDtypeStruct((nblk, A_IDX_HEADS, QB), F32),
                   jax.ShapeDtypeStruct((nblk, QB, 128), BF16),
                   jax.ShapeDtypeStruct((nblk, QB, 128), BF16),
                   jax.ShapeDtypeStruct((nblk, VT_ROWS, QB), BF16)),
        grid=(nblk // nt,),
        in_specs=[pl.BlockSpec((nt * QB, 1024), lambda i: (i, cb(1024, EV_AQ))),
                  pl.BlockSpec((nt * QB, 512), lambda i: (i, cb(512, EV_QI))),
                  pl.BlockSpec((nt * QB, 128), lambda i: (i, cb(128, EV_CKV))),
                  pl.BlockSpec((nt * QB, 128), lambda i: (i, cb(128, EV_MISC))),
                  pl.BlockSpec((1, width), lambda i: (0, 0)),
                  pl.BlockSpec((width, width), lambda i: (0, 0)),
                  pl.BlockSpec((1, A_KV_RANK), lambda i: (0, 0)),
                  pl.BlockSpec((A_KV_RANK, 2 * A_HEAD_DIM), lambda i: (0, 0)),
                  pl.BlockSpec((1, A_HEAD_DIM), lambda i: (0, 0))],
        out_specs=(pl.BlockSpec((nt, QB, A_HEADS * QB), lambda i: (i, 0, 0)),
                   pl.BlockSpec((nt, QB, A_IDX_HEADS * QB), lambda i: (i, 0, 0)),
                   pl.BlockSpec((nt, A_IDX_HEADS, QB), lambda i: (i, 0, 0)),
                   pl.BlockSpec((nt, QB, 128), lambda i: (i, 0, 0)),
                   pl.BlockSpec((nt, QB, 128), lambda i: (i, 0, 0)),
                   pl.BlockSpec((nt, VT_ROWS, QB), lambda i: (i, 0, 0))),
        compiler_params=_cparams(("parallel",)),
        name="dsa_prep",
    )(h, h, h, h, jnp.tile(a_q_norm, A_HEADS).reshape(1, width), jnp.asarray(same_head, BF16),
      a_kv_norm.reshape(1, -1), w_kv_up.astype(BF16), a_k_norm.reshape(1, -1))


def _dsa_kernel(topk, qit_ref, wt_ref, qta_ref, kidx_ref, katt_ref, vt_ref, bias_ref, az_ref, o_ref,
                strip_ref, x_ref, m_ref, acc_ref, lg_ref, mx_ref, p_ref, rhs_ref):
    qt = pl.program_id(1)
    t0 = qt * QB
    rhs_ref[0:QB, :] = qta_ref[...]

    @pl.when((pl.program_id(0) == 0) & (qt == 0))
    def _():
        ident = jnp.where(lax.broadcasted_iota(I32, (QB, QB), 0) == lax.broadcasted_iota(I32, (QB, QB), 1),
                          1.0, 0.0).astype(BF16)
        for h in range(A_HEADS):
            rhs_ref[QB:2 * QB, h * QB:(h + 1) * QB] = ident

    n_sc = qt // 4 + 1
    n_sb = qt // 2 + 1
    lane_s = lax.broadcasted_iota(I32, (SCORE_ROWS, QB), 1)
    row_s = lax.broadcasted_iota(I32, (SCORE_ROWS, QB), 0)

    w = wt_ref[...]

    def score_body(j, carry):
        kblk = kidx_ref[pl.ds(j * 4, 4)].reshape(SCORE_ROWS, 128)
        tot = None
        for hp in range(A_IDX_HEADS // 2):
            s = _dot(kblk, qit_ref[:, hp * SB:(hp + 1) * SB])
            for e in range(2):
                h = 2 * hp + e
                term = jnp.maximum(s[:, e * QB:(e + 1) * QB], 0.0) * w[h:h + 1, :]
                tot = term if tot is None else tot + term
        tot = jnp.where(j * SCORE_ROWS + row_s <= t0 + lane_s, tot, NEG_INF)
        bits = pltpu.bitcast(tot, I32)
        strip_ref[pl.ds(pl.multiple_of(j * SCORE_ROWS, SCORE_ROWS), SCORE_ROWS), :] = (
            bits ^ ((bits >> 31) & 0x7FFFFFFF))
        return carry

    def score_group(jj, carry):
        for sub in range(SCORE_UNROLL):
            score_body(SCORE_UNROLL * jj + sub, carry)
        return carry

    lax.fori_loop(0, n_sc // SCORE_UNROLL, score_group, 0)
    lax.fori_loop(SCORE_UNROLL * (n_sc // SCORE_UNROLL), n_sc, score_body, 0)

    def count(pred):
        def one(j, acc):
            r0 = pl.multiple_of(j * SCORE_ROWS, SCORE_ROWS)
            m = jnp.where(pred(strip_ref[pl.ds(r0, SCORE_ROWS), :], r0), 1, 0)
            return acc + m.reshape(SCORE_ROWS // 32, 32, QB).sum(axis=0)

        def body(j, acc):
            for sub in range(COUNT_UNROLL):
                acc = one(COUNT_UNROLL * j + sub, acc)
            return acc
        n_main = n_sc // COUNT_UNROLL
        acc = lax.fori_loop(0, n_main, body, jnp.zeros((32, QB), I32))
        acc = lax.fori_loop(n_main * COUNT_UNROLL, n_sc, one, acc)
        return acc.reshape(4, 8, QB).sum(axis=0).sum(axis=0, keepdims=True)

    count_ge = lambda cand: count(lambda key, r0: key >= cand)
    c0 = count_ge(jnp.zeros((1, QB), I32))
    nonneg = c0 >= topk
    thr0 = jnp.where(nonneg, 0, INT_MIN)
    cnt0 = jnp.where(nonneg, c0, n_sc * SCORE_ROWS)

    def bit_steps(counter, n):
        def one(_, carry):
            i, thr, cnt = carry
            cand = thr | lax.shift_left(jnp.int32(1), 30 - i)
            c = counter(cand)
            ok = c >= topk
            return i + 1, jnp.where(ok, cand, thr), jnp.where(ok, c, cnt)
        return lambda carry: lax.fori_loop(0, n, one, carry)

    state = bit_steps(count_ge, 15)((jnp.int32(0), thr0, cnt0))
    _, thr, cnt = lax.while_loop(lambda c: (c[0] < 31) & (jnp.max(c[2]) > topk),
                                 bit_steps(count_ge, 4), state)

    x_ref[...] = jnp.full((1, QB), 2 ** 30, I32)
    tied = cnt > topk

    drop = jnp.where(tied, cnt - topk, 0)
    max_drop = jnp.max(drop)

    @pl.when((max_drop > 0) & (max_drop <= TIE_WALK_MAX))
    def _():
        def prev_tie(bound):
            def body(j, acc):
                r0 = pl.multiple_of(j * SCORE_ROWS, SCORE_ROWS)
                idx = r0 + row_s
                hit = (strip_ref[pl.ds(r0, SCORE_ROWS), :] == thr) & (idx < bound)
                return jnp.maximum(acc, jnp.where(hit, idx, -1)
                                   .reshape(SCORE_ROWS // 32, 32, QB).max(axis=0))
            acc = lax.fori_loop(0, n_sc, body, jnp.full((32, QB), -1, I32))
            return acc.reshape(4, 8, QB).max(axis=0).max(axis=0, keepdims=True)

        x_ref[...] = lax.fori_loop(0, max_drop, lambda r, bound: jnp.where(r < drop, prev_tie(bound), bound),
                                   jnp.full((1, QB), 2 ** 30, I32))

    @pl.when(max_drop > TIE_WALK_MAX)
    def _():
        above = count(lambda key, r0: key > thr)
        need = jnp.where(tied, topk - above, 0)

        def tie_body(i, xb):
            cand = xb | lax.shift_left(jnp.int32(1), 14 - i)
            c = count(lambda key, r0: (key == thr) & (r0 + row_s < cand))
            return jnp.where(c <= need, cand, xb)

        xb = lax.fori_loop(0, 15, tie_body, jnp.zeros((1, QB), I32))
        x_ref[...] = jnp.where(tied, xb, 2 ** 30)

    xb = x_ref[...]

    m_ref[...] = jnp.full(m_ref.shape, M_INIT, F32)
    acc_ref[...] = jnp.zeros(acc_ref.shape, F32)
    n_lt = A_HEADS * QB // SB
    row_a = {nk: lax.broadcasted_iota(I32, (nk * QB, QB), 0) for nk in (2, FAR_BLOCKS)}
    lane_a = lax.broadcasted_iota(I32, (SB, QB), 1)

    def key_operand(j, nk, near):
        rows = nk * QB
        r0 = pl.multiple_of(j * rows, rows)
        key = strip_ref[pl.ds(r0, rows), :]
        idx = r0 + row_a[nk]
        sel = (key > thr) | ((key == thr) & (idx < xb))
        if near:
            sel = sel & (idx <= t0 + lane_a)
        pen = jnp.where(sel, 0.0, MASKED).astype(BF16)
        return jnp.concatenate([katt_ref[pl.ds(nk * j, nk)].reshape(rows, 128), pen], axis=1)

    def values(j, nk):
        return jnp.concatenate([vt_ref[nk * j + i] for i in range(nk)], axis=1)

    def new_max(slot):
        m_old = m_ref[...]
        m_new = jnp.maximum(m_old, mx_ref[slot])
        m_ref[...] = m_new
        return m_new.astype(BF16), jnp.exp2(m_old - m_new)

    def logits(j, slot, near, nk=2):
        rows = nk * QB
        kaug = key_operand(j, nk, near)
        if near:
            b0 = pl.multiple_of((2 * j - qt + 3) * QB, QB)
        for lt in range(n_lt):
            cols = slice(lt * SB, (lt + 1) * SB)
            lg = _dot(kaug, rhs_ref[:, cols])
            if near:
                lg = lg + bias_ref[pl.ds(b0, SB), cols]
            lg = lg.astype(BF16)
            lg_ref[slot, 0:rows, cols] = lg
            mx_ref[slot, :, cols] = jnp.max(lg, axis=0, keepdims=True).astype(F32)

    def accumulate(j, slot, nk=2):
        rows = nk * QB
        vt = values(j, nk)
        m_b, alpha = new_max(slot)
        for lt in range(n_lt):
            cols = slice(lt * SB, (lt + 1) * SB)
            p_ref[0:rows, cols] = jnp.exp2(lg_ref[slot, 0:rows, cols] - m_b[:, cols])
        acc_ref[...] = alpha * acc_ref[...] + _dot(vt, p_ref[0:rows, :])

    nk = FAR_BLOCKS
    n_far = jnp.maximum(n_sb - 2, 0)
    n_big = n_far // (nk // 2)
    n_pair = jnp.maximum(n_big - 1, 0) // 2
    n_rem = n_big - 2 * n_pair

    @pl.when(n_big > 0)
    def _():
        logits(0, 0, False, nk)

    def fused(j, slot):
        nxt = 1 - slot
        rows = nk * QB
        kaug = key_operand(j + 1, nk, False)
        vt = values(j, nk)
        m_b, alpha = new_max(slot)
        for lt in range(n_lt):
            cols = slice(lt * SB, (lt + 1) * SB)
            lg = _dot(kaug, rhs_ref[:, cols]).astype(BF16)
            lg_ref[nxt, 0:rows, cols] = lg
            mx_ref[nxt, :, cols] = jnp.max(lg, axis=0, keepdims=True).astype(F32)
            p = jnp.exp2(lg_ref[slot, 0:rows, cols] - m_b[:, cols])
            acc_ref[:, cols] = alpha[:, cols] * acc_ref[:, cols] + _dot(vt, p)

    def pair_body(jj, carry):
        j = 2 * jj
        fused(j, 0)
        fused(j + 1, 1)
        return carry

    lax.fori_loop(0, n_pair, pair_body, 0)

    @pl.when(n_rem == 1)
    def _():
        accumulate(n_big - 1, 0, nk)

    @pl.when(n_rem == 2)
    def _():
        fused(n_big - 2, 0)
        accumulate(n_big - 1, 1, nk)

    def fused_near(j, slot):
        nxt = 1 - slot
        kaug = key_operand(j + 1, 2, True)
        b0 = pl.multiple_of((2 * (j + 1) - qt + 3) * QB, QB)
        vt = values(j, 2)
        m_b, alpha = new_max(slot)
        for lt in range(n_lt):
            cols = slice(lt * SB, (lt + 1) * SB)
            lg = (_dot(kaug, rhs_ref[:, cols]) + bias_ref[pl.ds(b0, SB), cols]).astype(BF16)
            lg_ref[nxt, 0:SB, cols] = lg
            mx_ref[nxt, :, cols] = jnp.max(lg, axis=0, keepdims=True).astype(F32)
            p = jnp.exp2(lg_ref[slot, 0:SB, cols] - m_b[:, cols])
            acc_ref[:, cols] = alpha[:, cols] * acc_ref[:, cols] + _dot(vt, p)

    has_left = n_far % (nk // 2) == 1

    @pl.when(n_sb == 1)
    def _():
        logits(0, 0, True)
        accumulate(0, 0)

    @pl.when((n_sb >= 2) & jnp.logical_not(has_left))
    def _():
        logits(n_sb - 2, 0, True)
        fused_near(n_sb - 2, 0)
        accumulate(n_sb - 1, 1)

    @pl.when(has_left)
    def _():
        logits(n_sb - 3, 0, False)
        fused_near(n_sb - 3, 0)
        fused_near(n_sb - 2, 1)
        accumulate(n_sb - 1, 0)

    acc = acc_ref[...]
    o_t = acc[0:A_HEAD_DIM, :] / acc[A_HEAD_DIM:A_HEAD_DIM + 1, :]
    pad = jnp.zeros((QB - A_HEAD_DIM, QB), F32)
    pieces = []
    for h in range(A_HEADS):
        blk = jnp.concatenate([o_t[:, h * QB:(h + 1) * QB], pad], axis=0)
        pieces.append(blk.T[:, 0:A_HEAD_DIM])
    att = jnp.concatenate(pieces, axis=1)
    o_ref[...] = (att * _silu(az_ref[...])).astype(BF16)


def _dsa_attend(h, bsz, seq, qta, qit, wt, kidx, katt, vt, bias):
    nq = seq // QB
    topk = min(A_TOPK_MAX, seq // 4)
    blk = lambda b, q: (b * nq + q, 0, 0)
    per_batch = lambda b, q: (b, 0, 0)
    once = pl.Buffered(1)
    return pl.pallas_call(
        functools.partial(_dsa_kernel, topk),
        out_shape=jax.ShapeDtypeStruct((bsz * seq, A_HEADS * A_HEAD_DIM), BF16),
        grid=(bsz, nq),
        in_specs=[pl.BlockSpec((None, QB, A_IDX_HEADS * QB), blk),
                  pl.BlockSpec((None, A_IDX_HEADS, QB), blk),
                  pl.BlockSpec((None, QB, A_HEADS * QB), blk),
                  pl.BlockSpec((nq, QB, 128), per_batch, pipeline_mode=once),
                  pl.BlockSpec((nq, QB, 128), per_batch, pipeline_mode=once),
                  pl.BlockSpec((nq, VT_ROWS, QB), per_batch, pipeline_mode=once),
                  pl.BlockSpec((BIAS_ROWS, A_HEADS * QB), lambda b, q: (0, 0), pipeline_mode=once),
                  pl.BlockSpec((QB, 1024), lambda b, q: (b * nq + q, EV_AZ // 1024))],
        out_specs=pl.BlockSpec((QB, 1024), lambda b, q: (b * nq + q, 0)),
        scratch_shapes=[pltpu.VMEM((seq, QB), I32),
                        pltpu.VMEM((1, QB), I32),
                        pltpu.VMEM((1, A_HEADS * QB), F32),
                        pltpu.VMEM((VT_ROWS, A_HEADS * QB), F32),
                        pltpu.VMEM((2, FAR_BLOCKS * QB, A_HEADS * QB), BF16),
                        pltpu.VMEM((2, 1, A_HEADS * QB), F32),
                        pltpu.VMEM((FAR_BLOCKS * QB, A_HEADS * QB), BF16),
                        pltpu.VMEM((2 * QB, A_HEADS * QB), BF16)],
        compiler_params=_cparams(("arbitrary", "arbitrary")),
        name="dsa_attend",
    )(qit, wt, qta, kidx, katt, vt, bias, h)


GDN_TM = 1024
GDN_GROUP = 4
GDN_SCAN_HEADS = 8
GDN_PREP_HEADS = 2


def _gdn_prep_kernel(alog_ref, dtb_ref, q_ref, k_ref, v_ref, qh_ref, kh_ref, vh_ref, misc_ref,
                     cq_ref, ck_ref, cv_ref,
                     u_ref, w_ref, qd_ref, kdt_ref, attn_ref, egl_ref):
    i = pl.program_id(1)
    hp = pl.program_id(2)
    tm = q_ref.shape[0]
    nchunk = tm // B_CHUNK
    heads = range(GDN_PREP_HEADS)
    lanes = [slice(hh * 128, (hh + 1) * 128) for hh in heads]

    def conv_silu(x_ref, halo_ref, c_ref, ln):
        halo = jnp.where(i > 0, halo_ref[:, ln], 0.0)
        ext = jnp.concatenate([halo, x_ref[:, ln]], axis=0)
        cw = c_ref[:, ln]
        y = ext[8:, :] * cw[CONV_K - 1:CONV_K, :]
        for d in range(1, CONV_K):
            y = y + pltpu.roll(ext, d, 0)[8:, :] * cw[CONV_K - 1 - d:CONV_K - d, :]
        return _silu(y)

    q = [conv_silu(q_ref, qh_ref, cq_ref, ln) for ln in lanes]
    k = [conv_silu(k_ref, kh_ref, ck_ref, ln) for ln in lanes]
    v = [conv_silu(v_ref, vh_ref, cv_ref, ln) for ln in lanes]
    qn = [x * lax.rsqrt(jnp.sum(x * x, axis=-1, keepdims=True) + EPS) * (B_HEAD_DIM ** -0.5) for x in q]
    kn = [x * lax.rsqrt(jnp.sum(x * x, axis=-1, keepdims=True) + EPS) for x in k]

    misc = misc_ref[...]
    sel_r = lax.broadcasted_iota(I32, (128, 256), 0)
    sel_c = lax.broadcasted_iota(I32, (128, 256), 1)
    beta, g = [], []
    for hh in heads:
        hd = hp * GDN_PREP_HEADS + hh
        pick = jnp.where(sel_r == jnp.where(sel_c < 128, MISC_BA + hd, MISC_BB + hd), 1.0, 0.0)
        bab = _dot_xsel(misc, pick)
        beta.append(_sigmoid(bab[:, 128:256]))
        xg = bab[:, 0:128] + dtb_ref[hd]
        softplus = jnp.maximum(xg, 0.0) + jnp.log(1.0 + jnp.exp(-jnp.abs(xg)))
        g.append(-jnp.exp(jnp.full((1, 128), alog_ref[hd], F32)) * softplus)

    sl = 256
    r2 = lax.broadcasted_iota(I32, (sl, sl), 0)
    c2 = lax.broadcasted_iota(I32, (sl, sl), 1)
    same = (r2 // B_CHUNK) == (c2 // B_CHUNK)
    tri_blk = jnp.concatenate([jnp.where(same & (c2 <= r2), 1.0, 0.0), jnp.where(same, 1.0, 0.0)], axis=0)
    nsl = tm // sl
    gc, gl = [], []
    for hh in heads:
        both = _dot_sel(tri_blk, jnp.concatenate([g[hh][s * sl:(s + 1) * sl, :] for s in range(nsl)], axis=1))
        gc.append(jnp.concatenate([both[0:sl, s * 128:(s + 1) * 128] for s in range(nsl)], axis=0))
        gl.append(jnp.concatenate([both[sl:2 * sl, s * 128:(s + 1) * 128] for s in range(nsl)], axis=0))

    kb = [kn[hh] * beta[hh] for hh in heads]
    vb = [v[hh] * beta[hh] for hh in heads]
    egc = [jnp.exp(x) for x in gc]
    kbg = [kb[hh] * egc[hh] for hh in heads]
    for hh in heads:
        qd_ref[hh] = (qn[hh] * egc[hh]).astype(BF16)
        kd = kn[hh] * jnp.exp(gl[hh] - gc[hh])
        for m in range(tm // 128):
            kdt_ref[hh, :, m * 128:(m + 1) * 128] = kd[m * 128:(m + 1) * 128, :].T.astype(BF16)

    gw = GDN_GROUP * B_CHUNK
    ri = lax.broadcasted_iota(I32, (B_CHUNK, gw), 0)
    lj = lax.broadcasted_iota(I32, (B_CHUNK, gw), 1)
    lb = lj // B_CHUNK
    lj = lj % B_CHUNK
    bmask = (lax.broadcasted_iota(I32, (gw, gw), 0) // B_CHUNK
             == lax.broadcasted_iota(I32, (gw, gw), 1) // B_CHUNK)

    def fold(x):
        xm = jnp.where(bmask, x, 0.0)
        out = xm[0:B_CHUNK, :]
        for c in range(1, GDN_GROUP):
            out = out + xm[c * B_CHUNK:(c + 1) * B_CHUNK, :]
        return out

    bmask_b = jnp.where(bmask, 1.0, 0.0).astype(BF16)

    def bdiag(cat):
        return jnp.concatenate([cat] * GDN_GROUP, axis=0) * bmask_b

    probs = [(hh, grp) for grp in range(tm // gw) for hh in heads]
    a_cat = []
    for hh, grp in probs:
        rs = slice(grp * gw, (grp + 1) * gw)
        gcs = gc[hh][rs, :]
        col = jnp.concatenate([gcs[0:B_CHUNK, :]] * 2, axis=1)
        for c in range(1, GDN_GROUP):
            col = jnp.where(lb == c, jnp.concatenate([gcs[c * B_CHUNK:(c + 1) * B_CHUNK, :]] * 2, axis=1), col)
        rowf = jnp.concatenate([gcs[m * 128:(m + 1) * 128, :].T[0:B_CHUNK, :] for m in range(gw // 128)],
                               axis=1)
        decay = jnp.exp(jnp.where(ri >= lj, col - rowf, NEG_INF))
        kg = kn[hh][rs, :]
        a_cat.append(jnp.where(ri > lj, fold(_dot_nt(kb[hh][rs, :], kg)) * decay, 0.0))
        attn_ref[hh, grp] = jnp.where(ri >= lj, fold(_dot_nt(qn[hh][rs, :], kg)) * decay, 0.0).astype(BF16)
    eye = jnp.where(ri == lj, 1.0, 0.0)
    inv = [eye - a for a in a_cat]
    pw = [_dot_x3(a, a, expand_b=bdiag) for a in a_cat]
    for step in range(5):
        for p in range(len(probs)):
            if step < 4:
                both = _dot_x3(jnp.concatenate([inv[p], pw[p]], axis=0), pw[p], expand_b=bdiag)
                inv[p] = inv[p] + both[0:B_CHUNK, :]
                pw[p] = both[B_CHUNK:, :]
            else:
                inv[p] = inv[p] + _dot_x3(inv[p], pw[p], expand_b=bdiag)
    for p, (hh, grp) in enumerate(probs):
        rs = slice(grp * gw, (grp + 1) * gw)
        sol = _dot_x3(inv[p], jnp.concatenate([vb[hh][rs, :], kbg[hh][rs, :]], axis=1), expand_a=bdiag)
        u_ref[hh, rs, :] = sol[:, 0:B_HEAD_DIM]
        w_ref[hh, rs, :] = sol[:, B_HEAD_DIM:].astype(BF16)
    for hh in heads:
        for c in range(nchunk):
            egl_ref[hh, c:c + 1, :] = jnp.exp(gl[hh][c * B_CHUNK:c * B_CHUNK + 1, :])
        if nchunk < 8:
            egl_ref[hh, nchunk:, :] = jnp.zeros((8 - nchunk, 128), F32)


def _gdn_prep(h, bsz, seq, b_conv, b_a_log, b_dt_bias):
    tm = min(GDN_TM, seq)
    nt = seq // tm
    nrow = max(tm // B_CHUNK, 8)
    gw = GDN_GROUP * B_CHUNK
    hs = (bsz, B_HEADS, seq, B_HEAD_DIM)
    hg = GDN_PREP_HEADS
    hw = hg * 128
    cur = lambda off: (lambda b, i, hp: (b * nt + i, off // hw + hp))
    halo = lambda off: (lambda b, i, hp: (jnp.maximum((b * nt + i) * (tm // 8) - 1, 0), off // hw + hp))
    cw = lambda off: (lambda b, i, hp: (0, off // hw + hp))
    out = lambda b, i, hp: (b, hp, i, 0)
    smem = pl.BlockSpec(memory_space=pltpu.SMEM)
    return pl.pallas_call(
        _gdn_prep_kernel,
        out_shape=(jax.ShapeDtypeStruct(hs, F32), jax.ShapeDtypeStruct(hs, BF16),
                   jax.ShapeDtypeStruct(hs, BF16),
                   jax.ShapeDtypeStruct((bsz, B_HEADS, B_HEAD_DIM, seq), BF16),
                   jax.ShapeDtypeStruct((bsz, B_HEADS, seq // gw, B_CHUNK, gw), BF16),
                   jax.ShapeDtypeStruct((bsz, B_HEADS, nt * nrow, 128), F32)),
        grid=(bsz, nt, B_HEADS // hg),
        in_specs=[smem, smem,
                  pl.BlockSpec((tm, hw), cur(EV_BQ)), pl.BlockSpec((tm, hw), cur(EV_BK)),
                  pl.BlockSpec((tm, hw), cur(EV_BV)),
                  pl.BlockSpec((8, hw), halo(EV_BQ)), pl.BlockSpec((8, hw), halo(EV_BK)),
                  pl.BlockSpec((8, hw), halo(EV_BV)),
                  pl.BlockSpec((tm, 128), lambda b, i, hp: (b * nt + i, EV_MISC // 128)),
                  pl.BlockSpec((CONV_K, hw), cw(0)), pl.BlockSpec((CONV_K, hw), cw(1024)),
                  pl.BlockSpec((CONV_K, hw), cw(2048))],
        out_specs=(pl.BlockSpec((None, hg, tm, 128), out), pl.BlockSpec((None, hg, tm, 128), out),
                   pl.BlockSpec((None, hg, tm, 128), out),
                   pl.BlockSpec((None, hg, B_HEAD_DIM, tm), lambda b, i, hp: (b, hp, 0, i)),
                   pl.BlockSpec((None, hg, tm // gw, B_CHUNK, gw), lambda b, i, hp: (b, hp, i, 0, 0)),
                   pl.BlockSpec((None, hg, nrow, 128), out)),
        compiler_params=_cparams(("parallel", "parallel", "parallel")),
        name="gdn_prep",
    )(b_a_log, b_dt_bias, h, h, h, h, h, h, h, b_conv, b_conv, b_conv)


def _gdn_scan_kernel(u_ref, w_ref, qd_ref, kdt_ref, attn_ref, egl_ref, z_ref, gn_ref, o_ref, s_ref):
    @pl.when(pl.program_id(2) == 0)
    def _():
        s_ref[...] = jnp.zeros(s_ref.shape, F32)

    tm = u_ref.shape[1]
    gw = GDN_GROUP * B_CHUNK
    gain = gn_ref[...]
    zero = jnp.zeros((B_CHUNK, B_HEAD_DIM), BF16)
    states = [s_ref[hh] for hh in range(GDN_SCAN_HEADS)]
    for c in range(tm // B_CHUNK):
        grp, ci = divmod(c, GDN_GROUP)
        rs = slice(c * B_CHUNK, (c + 1) * B_CHUNK)
        for hh in range(GDN_SCAN_HEADS):
            sb = states[hh].astype(BF16)
            v_new = u_ref[hh, rs, :] - _dot(w_ref[hh, rs, :], sb)
            vpad = jnp.concatenate([zero] * ci + [v_new.astype(BF16)] + [zero] * (GDN_GROUP - 1 - ci), axis=0)
            o = _dot(qd_ref[hh, rs, :], sb) + _dot(attn_ref[hh, grp], vpad)
            states[hh] = (states[hh] * egl_ref[hh, c:c + 1, :]
                          + _dot(kdt_ref[hh, :, grp * gw:(grp + 1) * gw], vpad))
            on = o * lax.rsqrt(jnp.mean(o * o, axis=-1, keepdims=True) + EPS) * gain
            lanes = slice(hh * B_HEAD_DIM, (hh + 1) * B_HEAD_DIM)
            o_ref[rs, lanes] = (on * _silu(z_ref[rs, lanes])).astype(BF16)
    for hh in range(GDN_SCAN_HEADS):
        s_ref[hh] = states[hh]


def _gdn_scan(h, bsz, seq, u, w, qd, kd, attn, egl, b_out_norm):
    tm = min(GDN_TM, seq)
    nt = seq // tm
    nrow = max(tm // B_CHUNK, 8)
    gw = GDN_GROUP * B_CHUNK
    hg = GDN_SCAN_HEADS
    blk = lambda b, hp, i: (b, hp, i, 0)
    return pl.pallas_call(
        _gdn_scan_kernel,
        out_shape=jax.ShapeDtypeStruct((bsz * seq, B_HEADS * B_HEAD_DIM), BF16),
        grid=(bsz, B_HEADS // hg, nt),
        in_specs=[pl.BlockSpec((None, hg, tm, 128), blk), pl.BlockSpec((None, hg, tm, 128), blk),
                  pl.BlockSpec((None, hg, tm, 128), blk),
                  pl.BlockSpec((None, hg, B_HEAD_DIM, tm), lambda b, hp, i: (b, hp, 0, i)),
                  pl.BlockSpec((None, hg, tm // gw, B_CHUNK, gw), lambda b, hp, i: (b, hp, i, 0, 0)),
                  pl.BlockSpec((None, hg, nrow, 128), blk),
                  pl.BlockSpec((tm, hg * 128), lambda b, hp, i: (b * nt + i, EV_BZ // (hg * 128) + hp)),
                  pl.BlockSpec((1, 128), lambda b, hp, i: (0, 0))],
        out_specs=pl.BlockSpec((tm, hg * 128), lambda b, hp, i: (b * nt + i, hp)),
        scratch_shapes=[pltpu.VMEM((hg, B_HEAD_DIM, B_HEAD_DIM), F32)],
        compiler_params=_cparams(("parallel", "parallel", "arbitrary")),
        name="gdn_scan",
    )(u, w, qd, kd, attn, egl, h, b_out_norm.reshape(1, -1))


def _rope_kernel(inv_ref, cos_ref, sin_ref):
    tm = cos_ref.shape[0]
    pos = (pl.program_id(0) * tm + lax.broadcasted_iota(I32, (tm, 128), 0)).astype(F32)
    ang = pos * inv_ref[...]
    lane = lax.broadcasted_iota(I32, (tm, 128), 1)
    cos_ref[...] = jnp.cos(ang)
    sin_ref[...] = jnp.where(lane < 64, -jnp.sin(ang), jnp.sin(ang))


def _rope_tables(seq):
    half = C_QK_DIM // 2
    inv = 1.0 / (ROPE_BASE ** jnp.linspace(0.0, 1.0, half, dtype=F32))
    inv2 = jnp.concatenate([inv, inv]).reshape(1, C_QK_DIM)
    tm = min(1024, seq)
    return pl.pallas_call(
        _rope_kernel,
        out_shape=(jax.ShapeDtypeStruct((seq, C_QK_DIM), F32),) * 2,
        grid=(seq // tm,),
        in_specs=[pl.BlockSpec((1, C_QK_DIM), lambda i: (0, 0))],
        out_specs=(pl.BlockSpec((tm, C_QK_DIM), lambda i: (i, 0)),) * 2,
        compiler_params=_cparams(("parallel",)),
        name="rope_tables",
    )(inv2)


RET_TM = 4096


def _retention_kernel(lg_ref, q_ref, k_ref, v_ref, z_ref, cos_ref, sin_ref, gn_ref, o_ref,
                      r_ref, dm_ref, zeta_ref, xi_ref):
    hd = pl.program_id(1)
    lg = lg_ref[0, hd]

    @pl.when(pl.program_id(2) == 0)
    def _():
        r_ref[...] = jnp.zeros(r_ref.shape, F32)
        ri = lax.broadcasted_iota(I32, (C_CHUNK, C_CHUNK), 0)
        ci = lax.broadcasted_iota(I32, (C_CHUNK, C_CHUNK), 1)
        diff = (ri - ci).astype(F32)
        dm_ref[...] = jnp.where(diff >= 0, jnp.exp(jnp.maximum(diff, 0.0) * lg), 0.0)
        zeta_ref[...] = jnp.exp((C_CHUNK - 1 - ri).astype(F32) * lg)
        rv = lax.broadcasted_iota(I32, (C_CHUNK, C_V_DIM), 0).astype(F32)
        xi_ref[...] = jnp.exp((rv + 1.0) * lg)

    g_chunk = lg_ref[1, hd]
    tm = q_ref.shape[0]
    state = r_ref[...]
    dmask = dm_ref[...]
    gain = gn_ref[...]
    for c in range(tm // C_CHUNK):
        rs = slice(c * C_CHUNK, (c + 1) * C_CHUNK)
        cos = cos_ref[rs, :]
        sin = sin_ref[rs, :]
        q = q_ref[rs, :]
        k = k_ref[rs, :]
        qr = q * cos + pltpu.roll(q, C_QK_DIM // 2, 1) * sin
        kr = (k * cos + pltpu.roll(k, C_QK_DIM // 2, 1) * sin) * (C_QK_DIM ** -0.5)
        vb = v_ref[rs, :].astype(BF16)
        qb = qr.astype(BF16)
        s = _dot_nt(qb, kr.astype(BF16)) * dmask
        o = _dot(s.astype(BF16), vb) + _dot(qb, state.astype(BF16)) * xi_ref[...]
        state = state * g_chunk + _dot((kr * zeta_ref[...]).T.astype(BF16), vb)
        mu = jnp.mean(o, axis=-1, keepdims=True)
        oc = o - mu
        var = jnp.mean(oc * oc, axis=-1, keepdims=True)
        y = oc * lax.rsqrt(var + EPS) * gain
        o_ref[rs, :] = (y * _silu(z_ref[rs, :])).astype(BF16)
    r_ref[...] = state


def _retention(h, bsz, seq, cos2, sin2, c_out_norm):
    tm = min(RET_TM, seq)
    nt = seq // tm
    lg = np.log1p(-np.exp2(-5.0 - np.arange(C_HEADS, dtype=np.float32))).astype(np.float32)
    lg = np.stack([lg, np.exp(np.float32(C_CHUNK) * lg).astype(np.float32)])
    return pl.pallas_call(
        _retention_kernel,
        out_shape=jax.ShapeDtypeStruct((bsz * seq, C_HEADS * C_V_DIM), BF16),
        grid=(bsz, C_HEADS, nt),
        in_specs=[pl.BlockSpec(memory_space=pltpu.SMEM),
                  pl.BlockSpec((tm, 128), lambda b, hd, i: (b * nt + i, OD_CQ // 128 + hd)),
                  pl.BlockSpec((tm, 128), lambda b, hd, i: (b * nt + i, OD_CK // 128 + hd)),
                  pl.BlockSpec((tm, 256), lambda b, hd, i: (b * nt + i, OD_CV // 256 + hd)),
                  pl.BlockSpec((tm, 256), lambda b, hd, i: (b * nt + i, OD_CZ // 256 + hd)),
                  pl.BlockSpec((tm, 128), lambda b, hd, i: (i, 0)),
                  pl.BlockSpec((tm, 128), lambda b, hd, i: (i, 0)),
                  pl.BlockSpec((1, C_V_DIM), lambda b, hd, i: (0, hd))],
        out_specs=pl.BlockSpec((tm, C_V_DIM), lambda b, hd, i: (b * nt + i, hd)),
        scratch_shapes=[pltpu.VMEM((C_QK_DIM, C_V_DIM), F32),
                        pltpu.VMEM((C_CHUNK, C_CHUNK), F32),
                        pltpu.VMEM((C_CHUNK, C_QK_DIM), F32),
                        pltpu.VMEM((C_CHUNK, C_V_DIM), F32)],
        compiler_params=_cparams(("parallel", "parallel", "arbitrary")),
        name="retention",
    )(jnp.asarray(lg), h, h, h, h, cos2, sin2, c_out_norm.reshape(1, -1))


def _s5_param_kernel(lre_ref, lim_ref, ldt_ref, lrex_ref, limx_ref, ldtx_ref, bre_ref, bim_ref,
                     are_ref, aim_ref, bbre_ref, bbim_ref):
    def disc(lre, lim, ldt):
        lr = jnp.minimum(lre, -1e-4)
        dt = jnp.exp(ldt)
        mag = jnp.exp(lr * dt)
        return lr, lim, mag * jnp.cos(lim * dt), mag * jnp.sin(lim * dt)

    _, _, a_re, a_im = disc(lre_ref[...], lim_ref[...], ldt_ref[...])
    are_ref[...] = a_re
    aim_ref[...] = a_im
    lr, li, ax_re, ax_im = disc(lrex_ref[...], limx_ref[...], ldtx_ref[...])
    den = lr * lr + li * li
    f_re = ((ax_re - 1.0) * lr + ax_im * li) / den
    f_im = (ax_im * lr - (ax_re - 1.0) * li) / den
    bbre_ref[...] = f_re * bre_ref[...] - f_im * bim_ref[...]
    bbim_ref[...] = f_re * bim_ref[...] + f_im * bre_ref[...]


def _s5_params(lam_re, lam_im, log_dt, b_re, b_im, c_re, c_im):
    g, p, ch = D_GROUPS, D_STATE, D_GROUP
    ldt = jnp.broadcast_to(log_dt[:, None], (g, p))
    rep = lambda a: jnp.repeat(a, ch, axis=1)
    vm = pl.BlockSpec(memory_space=pltpu.VMEM)
    a_re, a_im, bb_re, bb_im = pl.pallas_call(
        _s5_param_kernel,
        out_shape=(jax.ShapeDtypeStruct((g, p), F32), jax.ShapeDtypeStruct((g, p), F32),
                   jax.ShapeDtypeStruct((g, p * ch), F32), jax.ShapeDtypeStruct((g, p * ch), F32)),
        in_specs=[vm] * 8, out_specs=(vm,) * 4,
        name="s5_params",
    )(lam_re, lam_im, ldt, rep(lam_re), rep(lam_im), rep(ldt),
      b_re.reshape(g, p * ch), b_im.reshape(g, p * ch))
    eye = jnp.eye(D_SETS * 4, dtype=F32)

    def pack_b(bb):
        bb = bb.reshape(D_SETS, 16, p, ch)
        return jnp.einsum('sgpi,gh->sgihp', bb, eye).reshape(D_SETS, D_SET_CH, D_SET_ST)

    def pack_c(c):
        c = c.reshape(D_SETS, 16, ch, p)
        return jnp.einsum('sgjp,gh->sgphj', c, eye).reshape(D_SETS, D_SET_ST, D_SET_CH)

    bd = jnp.concatenate([pack_b(bb_re), pack_b(bb_im)], axis=-1).astype(BF16)
    slab = (D_SETS, S5_SLAB, 128)
    return (a_re.reshape(slab), a_im.reshape(slab), bd,
            pack_c(c_re).astype(BF16), pack_c(c_im).astype(BF16))


S5_TM = 512
S5_SLAB = D_SET_ST // 128


def _s5_kernel(u_ref, z_ref, are_ref, aim_ref, bd_ref, cre_ref, cim_ref, dskip_ref, wglu_ref, bglu_ref,
               o_ref, hre_ref, him_ref, y_ref, *x_refs):
    @pl.when(pl.program_id(1) == 0)
    def _():
        hre_ref[...] = jnp.zeros(hre_ref.shape, F32)
        him_ref[...] = jnp.zeros(him_ref.shape, F32)

    xre, xim = x_refs[:D_SETS], x_refs[D_SETS:]
    tm = u_ref.shape[0]
    u = u_ref[...]
    ub = u.astype(BF16)
    for s in range(D_SETS):
        bu = _dot(ub[:, s * D_SET_CH:(s + 1) * D_SET_CH], bd_ref[s])
        for k in range(S5_SLAB):
            xre[s][pl.ds(k, tm, stride=S5_SLAB), :] = bu[:, k * 128:(k + 1) * 128]
            xim[s][pl.ds(k, tm, stride=S5_SLAB), :] = bu[:, D_SET_ST + k * 128:D_SET_ST + (k + 1) * 128]
    a_re = [are_ref[s] for s in range(D_SETS)]
    a_im = [aim_ref[s] for s in range(D_SETS)]

    def step(t, carry):
        rows = pl.ds(pl.multiple_of(t * S5_SLAB, S5_SLAB), S5_SLAB)
        out = []
        for s in range(D_SETS):
            hr, hi = carry[2 * s], carry[2 * s + 1]
            xr = xre[s][rows, :] + a_re[s] * hr - a_im[s] * hi
            xi = xim[s][rows, :] + a_re[s] * hi + a_im[s] * hr
            xre[s][rows, :] = xr
            xim[s][rows, :] = xi
            out += [xr, xi]
        return tuple(out)

    carry = []
    for s in range(D_SETS):
        carry += [hre_ref[s], him_ref[s]]
    carry = lax.fori_loop(0, tm, step, tuple(carry), unroll=8)
    for s in range(D_SETS):
        hre_ref[s] = carry[2 * s]
        him_ref[s] = carry[2 * s + 1]
        x_r = jnp.concatenate([xre[s][pl.ds(k, tm, stride=S5_SLAB), :] for k in range(S5_SLAB)], axis=1)
        x_i = jnp.concatenate([xim[s][pl.ds(k, tm, stride=S5_SLAB), :] for k in range(S5_SLAB)], axis=1)
        y_ref[:, s * D_SET_CH:(s + 1) * D_SET_CH] = (
            _dot(x_r.astype(BF16), cre_ref[s]) - _dot(x_i.astype(BF16), cim_ref[s]))
    y = y_ref[...] + dskip_ref[...] * u
    y = 0.5 * y * (1.0 + jnp.tanh(math.sqrt(2.0 / math.pi) * (y + 0.044715 * (y * y * y))))
    gate = _sigmoid(_dot(y.astype(BF16), wglu_ref[...]) + bglu_ref[...])
    o_ref[...] = (y * gate * _silu(z_ref[...])).astype(BF16)


def _s5(h, bsz, seq, a_re, a_im, bd, cd_re, cd_im, d_skip, w_glu, b_glu):
    tm = min(S5_TM, seq)
    nt = seq // tm
    full = lambda *shape: pl.BlockSpec(shape, lambda b, i: (0,) * len(shape))
    width = D_GROUPS * D_GROUP
    return pl.pallas_call(
        _s5_kernel,
        out_shape=jax.ShapeDtypeStruct((bsz * seq, width), BF16),
        grid=(bsz, nt),
        in_specs=[pl.BlockSpec((tm, width), lambda b, i: (b * nt + i, OD_DU // width)),
                  pl.BlockSpec((tm, width), lambda b, i: (b * nt + i, OD_DZ // width)),
                  full(D_SETS, S5_SLAB, 128), full(D_SETS, S5_SLAB, 128),
                  full(D_SETS, D_SET_CH, 2 * D_SET_ST),
                  full(D_SETS, D_SET_ST, D_SET_CH), full(D_SETS, D_SET_ST, D_SET_CH),
                  full(1, width), full(width, width), full(1, width)],
        out_specs=pl.BlockSpec((tm, width), lambda b, i: (b * nt + i, 0)),
        scratch_shapes=([pltpu.VMEM((D_SETS, S5_SLAB, 128), F32), pltpu.VMEM((D_SETS, S5_SLAB, 128), F32),
                         pltpu.VMEM((tm, width), F32)]
                        + [pltpu.VMEM((tm * S5_SLAB, 128), F32)] * (2 * D_SETS)),
        compiler_params=_cparams(("parallel", "arbitrary")),
        name="s5",
    )(h, h, a_re, a_im, bd, cd_re, cd_im, d_skip.reshape(1, -1), w_glu.astype(BF16),
      b_glu.reshape(1, -1))


def _pack_even_w(w_in):
    sizes = (1024, A_KV_RANK, A_IDX_HEADS * A_IDX_DIM, A_IDX_DIM, A_IDX_HEADS, 1024,
             3 * 1024, B_HEADS, B_HEADS, 1024)
    parts, start = [], 0
    for s in sizes:
        parts.append(w_in[:, start:start + s])
        start += s
    aq, ckv, qi, ki, wi, az, bqkv, ba, bb, bz = parts
    pad = jnp.zeros((w_in.shape[0], EV_WIDTH - start), w_in.dtype)
    return jnp.concatenate([aq, az, bqkv, bz, qi, ckv, ki, wi, ba, bb, pad], axis=1).astype(BF16)


def _even_layer(x2d, bsz, seq, bias, norm_g, w_in, a_q_norm, a_kv_norm, w_kv_up, a_k_norm,
                b_conv, b_a_log, b_dt_bias, b_out_norm, w_out):
    h = _norm_proj(x2d, norm_g, _pack_even_w(w_in), tn=768)
    qta, qit, wt, kidx, katt, vt = _dsa_prep(h, bsz * seq // QB, a_q_norm, a_kv_norm, w_kv_up, a_k_norm)
    mix_a = _dsa_attend(h, bsz, seq, qta, qit, wt, kidx, katt, vt, bias)
    u, w, qd, kd, attn, egl = _gdn_prep(h, bsz, seq, b_conv, b_a_log, b_dt_bias)
    mix_b = _gdn_scan(h, bsz, seq, u, w, qd, kd, attn, egl, b_out_norm)
    return _out_proj(x2d, mix_a, mix_b, w_out)


def _odd_layer(x2d, bsz, seq, cos2, sin2, norm_g, w_in, c_out_norm, lam_re, lam_im, log_dt,
               b_re, b_im, c_re, c_im, d_skip, w_glu, b_glu, w_out):
    h = _norm_proj(x2d, norm_g, w_in.astype(BF16), tn=1024)
    mix_c = _retention(h, bsz, seq, cos2, sin2, c_out_norm)
    s5p = _s5_params(lam_re, lam_im, log_dt, b_re, b_im, c_re, c_im)
    mix_d = _s5(h, bsz, seq, *s5p, d_skip, w_glu, b_glu)
    return _out_proj(x2d, mix_c, mix_d, w_out)


def kernel(x, rel_bias, ev_norm, ev_w_in, ev_a_q_norm, ev_a_kv_norm, ev_w_kv_up, ev_a_k_norm,
           ev_b_conv, ev_b_a_log, ev_b_dt_bias, ev_b_out_norm, ev_w_out,
           od_norm, od_w_in, od_c_out_norm, od_lam_re, od_lam_im, od_log_dt,
           od_b_re, od_b_im, od_c_re, od_c_im, od_d_skip, od_w_glu, od_b_glu, od_w_out):
    bsz, seq, d = x.shape
    depth = ev_norm.shape[0] + od_norm.shape[0]
    x2d = x.reshape(bsz * seq, d)
    bias = _bias_table(rel_bias)
    cos2, sin2 = _rope_tables(seq)
    for layer in range(depth):
        i = layer // 2
        if layer % 2 == 0:
            x2d = _even_layer(x2d, bsz, seq, bias, ev_norm[i], ev_w_in[i], ev_a_q_norm[i],
                              ev_a_kv_norm[i], ev_w_kv_up[i], ev_a_k_norm[i], ev_b_conv[i],
                              ev_b_a_log[i], ev_b_dt_bias[i], ev_b_out_norm[i], ev_w_out[i])
        else:
            x2d = _odd_layer(x2d, bsz, seq, cos2, sin2, od_norm[i], od_w_in[i], od_c_out_norm[i],
                             od_lam_re[i], od_lam_im[i], od_log_dt[i], od_b_re[i], od_b_im[i],
                             od_c_re[i], od_c_im[i], od_d_skip[i], od_w_glu[i], od_b_glu[i], od_w_out[i])
    return x2d.reshape(bsz, seq, d)
```
